```python
import jax, jax.numpy as jnp
from jax import lax
import numpy as np

D_MODEL = 1024
BATCH = 32
SEQ = 256
DEPTH = 2
DEC_BATCH = 4
DEC_SEQ = 4096
PAST_LEN = 256

GRID_W = 64
N_EVEN = (DEPTH + 1) // 2
N_ODD = DEPTH // 2
EPS = 1e-6
W_A = D_MODEL
H_A = 8
BW_A = W_A // H_A
CONV_W = 4
LRU_C = 8.0
W_B = D_MODEL
G_B = 8
GW_B = W_B // G_B
IN_E = 2 * W_A + 2 * W_B
OUT_E = W_A + W_B
H_C = 8
DK_C = D_MODEL // H_C
DV_C = 2 * D_MODEL // H_C
QK_C = H_C * DK_C
W_C = H_C * DV_C
IN_O = 2 * QK_C + 3 * W_C + 4 * H_C
CHUNK = 128

kernel_name = 'hybrid_rglru_fourier_mlstm_step'


def rms_norm(x, g):
    xf = x.astype(jnp.float32)
    y = xf * lax.rsqrt(jnp.mean(xf * xf, axis=-1, keepdims=True) + EPS)
    return (y * g.astype(jnp.float32)).astype(x.dtype)


def ada_mod(cond, w, b):
    m = jax.nn.silu(cond) @ w + b
    return jnp.split(m[:, None, :], 3, axis=-1)


def centred_dwconv(x, w, b):
    y = lax.conv_general_dilated(
        x, w[:, None, :].astype(x.dtype), window_strides=(1,),
        padding=[(CONV_W // 2, CONV_W - 1 - CONV_W // 2)],
        dimension_numbers=('NWC', 'WIO', 'NWC'), feature_group_count=x.shape[-1])
    return y + b


def block_diag(x, w):
    bsz, L, _ = x.shape
    nb, bi, bo = w.shape
    return jnp.einsum('blhi,hij->blhj', x.reshape(bsz, L, nb, bi), w).reshape(bsz, L, nb * bo)


def _lin_combine(e1, e2):
    a1, b1 = e1
    a2, b2 = e2
    return a1 * a2, a2 * b1 + b2


def rglru(x, w_r, b_r, w_i, b_i, lam, h0):
    r = jax.nn.sigmoid(block_diag(x, w_r) + b_r)
    i = jax.nn.sigmoid(block_diag(x, w_i) + b_i)
    log_a = -LRU_C * r * jax.nn.softplus(-lam)
    a = jnp.exp(log_a)
    u = jnp.sqrt(-jnp.expm1(2.0 * log_a)) * (i * x)
    u = u.at[:, 0].add(a[:, 0] * h0)
    _, h = lax.associative_scan(_lin_combine, (a, u), axis=1)
    return h, h[:, -1]


def mixer_ab(h, h0, w_in, conv_w, conv_b, w_r, b_r, w_i, b_i, lam, w_four, b_four, w_out):
    f32 = jnp.float32
    bsz, L, _ = h.shape
    proj = h @ w_in
    xa, za, xb, zb = jnp.split(proj, [W_A, 2 * W_A, 2 * W_A + W_B], axis=-1)
    xa = centred_dwconv(xa, conv_w, conv_b).astype(f32)
    ya = 0.0
    finals = []
    for d in range(2):
        xd = jnp.flip(xa, 1) if d else xa
        hd, hT = rglru(xd, w_r[d].astype(f32), b_r[d].astype(f32), w_i[d].astype(f32),
                       b_i[d].astype(f32), lam[d].astype(f32), h0[:, d].astype(f32))
        ya = ya + (jnp.flip(hd, 1) if d else hd)
        finals.append(hT)
    xg = xb.astype(f32).reshape(bsz, L, G_B, GW_B)
    fr = jnp.fft.fft2(xg, axes=(1, 3), norm='ortho').real
    yb = jnp.einsum('blgi,gij->blgj', fr, w_four.astype(f32)).reshape(bsz, L, W_B) + b_four.astype(f32)
    y = jnp.concatenate([ya.astype(h.dtype) * jax.nn.silu(za),
                         yb.astype(h.dtype) * jax.nn.silu(zb)], axis=-1)
    return y @ w_out, jnp.stack(finals, axis=1)


def mlstm_chunkwise(q, k, v, i_pre, f_pre, C0, n0, m0):
    bsz, L = q.shape[:2]
    nc = L // CHUNK

    def chunks(t):
        t = t.reshape((bsz, nc, CHUNK) + t.shape[2:])
        return jnp.moveaxis(t, (1, 3), (0, 2))

    mask = jnp.tril(jnp.ones((CHUNK, CHUNK), dtype=bool))

    def step(carry, xs):
        C, n, m = carry
        qc, kc, vc, ic, lfc = xs
        b = jnp.cumsum(lfc, axis=-1)
        dmat = jnp.where(mask, b[..., :, None] - b[..., None, :] + ic[..., None, :], -jnp.inf)
        m_inter = b + m[..., None]
        m_row = jnp.maximum(m_inter, jnp.max(dmat, axis=-1))
        s = jnp.einsum('bhtd,bhsd->bhts', qc, kc) * jnp.exp(dmat - m_row[..., None])
        w_inter = jnp.exp(m_inter - m_row)
        num = (jnp.einsum('bhts,bhsv->bhtv', s, vc)
               + w_inter[..., None] * jnp.einsum('bhtd,bhdv->bhtv', qc, C))
        den = jnp.sum(s, axis=-1) + w_inter * jnp.einsum('bhtd,bhd->bht', qc, n)
        hc = num / jnp.maximum(jnp.abs(den), jnp.exp(-m_row))[..., None]
        b_end = b[..., -1]
        g = b_end[..., None] - b + ic
        m_new = jnp.maximum(b_end + m, jnp.max(g, axis=-1))
        wk = jnp.exp(g - m_new[..., None])
        decay = jnp.exp(b_end + m - m_new)
        C_new = decay[..., None, None] * C + jnp.einsum('bhs,bhsd,bhsv->bhdv', wk, kc, vc)
        n_new = decay[..., None] * n + jnp.einsum('bhs,bhsd->bhd', wk, kc)
        return (C_new, n_new, m_new), hc

    xs = (chunks(q), chunks(k), chunks(v), chunks(i_pre), chunks(jax.nn.log_sigmoid(f_pre)))
    (C, n, m), h = lax.scan(step, (C0, n0, m0), xs)
    h = jnp.transpose(h, (1, 0, 3, 2, 4)).reshape(bsz, L, H_C * DV_C)
    return h, C, n, m


def mixer_c(h, C0, n0, m0, w_in, b_if, w_out):
    f32 = jnp.float32
    bsz, L, _ = h.shape
    proj = h @ w_in
    q, k, v, o, z, gp = jnp.split(
        proj, [QK_C, 2 * QK_C, 2 * QK_C + W_C, 2 * QK_C + 2 * W_C, 2 * QK_C + 3 * W_C], axis=-1)
    q = q.astype(f32).reshape(bsz, L, H_C, DK_C) * (DK_C ** -0.5)
    k = k.astype(f32).reshape(bsz, L, H_C, DK_C)
    v = v.astype(f32).reshape(bsz, L, H_C, DV_C)
    gp = gp.astype(f32).reshape(bsz, L, 2, 2, H_C) + b_if.astype(f32)
    hsum = 0.0
    Cs, ns, ms = [], [], []
    for d in range(2):
        fl = (lambda t: jnp.flip(t, 1)) if d else (lambda t: t)
        hd, Cd, nd, md = mlstm_chunkwise(fl(q), fl(k), fl(v), fl(gp[:, :, d, 0]), fl(gp[:, :, d, 1]),
                                         C0[:, d].astype(f32), n0[:, d].astype(f32), m0[:, d].astype(f32))
        hsum = hsum + fl(hd)
        Cs.append(Cd)
        ns.append(nd)
        ms.append(md)
    y = jax.nn.sigmoid(o) * hsum.astype(h.dtype) * jax.nn.silu(z)
    return y @ w_out, jnp.stack(Cs, axis=1), jnp.stack(ns, axis=1), jnp.stack(ms, axis=1)


def run_trunk(x, cond, lru0, C0, n0, m0, w_mod, b_mod, g_pre, g_post, ab, cw):
    lru_f, C_f, n_f, m_f = [], [], [], []
    for l in range(DEPTH):
        shift, scale, gate = ada_mod(cond, w_mod[l], b_mod[l])
        h = rms_norm(x, g_pre[l]) * (1.0 + scale) + shift
        if l % 2 == 0:
            e = l // 2
            y, s = mixer_ab(h, lru0[:, e], *[p[e] for p in ab])
            lru_f.append(s)
        else:
            j = l // 2
            y, Cj, nj, mj = mixer_c(h, C0[:, j], n0[:, j], m0[:, j], *[p[j] for p in cw])
            C_f.append(Cj)
            n_f.append(nj)
            m_f.append(mj)
        x = x + gate * rms_norm(y, g_post[l])
    return x, jnp.stack(lru_f, 1), jnp.stack(C_f, 1), jnp.stack(n_f, 1), jnp.stack(m_f, 1)


def setup_inputs(seed: int = 0) -> dict:
    key = jax.random.key(seed)
    ks = jax.random.split(key, 32)
    f32 = jnp.float32

    def nrm(k, shape, s):
        return jax.random.normal(k, shape, f32) * s

    u = jax.random.uniform(ks[0], (N_EVEN, 2, W_A), f32, 0.9, 0.999)
    a = u ** (1.0 / LRU_C)
    lru_lambda = jnp.log(a) - jnp.log1p(-a)
    b_if = jnp.concatenate([nrm(ks[1], (N_ODD, 2, 1, H_C), 0.1),
                            3.0 + 3.0 * jax.random.uniform(ks[2], (N_ODD, 2, 1, H_C), f32)], axis=2)
    return {
        'x_prompt': nrm(ks[3], (BATCH, SEQ, D_MODEL), 1.0),
        'x_sample': nrm(ks[4], (DEC_BATCH, DEC_SEQ, D_MODEL), 1.0),
        'c': nrm(ks[5], (DEC_BATCH, D_MODEL), 1.0),
        'state_lru': nrm(ks[6], (DEC_BATCH, N_EVEN, 2, W_A), 0.5),
        'state_mlstm_C': nrm(ks[7], (DEC_BATCH, N_ODD, 2, H_C, DK_C, DV_C), 0.05),
        'state_mlstm_n': nrm(ks[8], (DEC_BATCH, N_ODD, 2, H_C, DK_C), 0.5),
        'state_mlstm_m': jax.random.uniform(ks[9], (DEC_BATCH, N_ODD, 2, H_C), f32, 0.0, 2.0),
        'c_ctx': nrm(ks[10], (D_MODEL,), 1.0),
        'w_mod': nrm(ks[11], (DEPTH, D_MODEL, 3 * D_MODEL), 0.5 * D_MODEL ** -0.5),
        'b_mod': nrm(ks[12], (DEPTH, 3 * D_MODEL), 0.02),
        'g_pre': 1.0 + nrm(ks[13], (DEPTH, D_MODEL), 0.02),
        'g_post': 1.0 + nrm(ks[14], (DEPTH, D_MODEL), 0.02),
        'w_in_e': nrm(ks[15], (N_EVEN, D_MODEL, IN_E), D_MODEL ** -0.5),
        'conv_w': nrm(ks[16], (N_EVEN, CONV_W, W_A), CONV_W ** -0.5),
        'conv_b': nrm(ks[17], (N_EVEN, W_A), 0.02),
        'w_rg': nrm(ks[18], (N_EVEN, 2, H_A, BW_A, BW_A), BW_A ** -0.5),
        'b_rg': nrm(ks[19], (N_EVEN, 2, W_A), 0.02),
        'w_ig': nrm(ks[20], (N_EVEN, 2, H_A, BW_A, BW_A), BW_A ** -0.5),
        'b_ig': nrm(ks[21], (N_EVEN, 2, W_A), 0.02),
        'lru_lambda': lru_lambda,
        'w_four': nrm(ks[22], (N_EVEN, G_B, GW_B, GW_B), GW_B ** -0.5),
        'b_four': nrm(ks[23], (N_EVEN, W_B), 0.02),
        'w_out_e': nrm(ks[24], (N_EVEN, OUT_E, D_MODEL), OUT_E ** -0.5),
        'w_in_o': nrm(ks[25], (N_ODD, D_MODEL, IN_O), D_MODEL ** -0.5),
        'b_if': b_if,
        'w_out_o': nrm(ks[26], (N_ODD, W_C, D_MODEL), W_C ** -0.5),
    }


def reference(x_prompt, x_sample, c, state_lru, state_mlstm_C, state_mlstm_n, state_mlstm_m,
              c_ctx, w_mod, b_mod, g_pre, g_post, w_in_e, conv_w, conv_b, w_rg, b_rg, w_ig, b_ig,
              lru_lambda, w_four, b_four, w_out_e, w_in_o, b_if, w_out_o):
    f32 = jnp.float32
    ab = (w_in_e, conv_w, conv_b, w_rg, b_rg, w_ig, b_ig, lru_lambda, w_four, b_four, w_out_e)
    cw = (w_in_o, b_if, w_out_o)
    bp = x_prompt.shape[0]
    cond_ctx = jnp.broadcast_to(c_ctx[None, :], (bp, c_ctx.shape[0]))
    y_prompt, new_lru, new_C, new_n, new_m = run_trunk(
        x_prompt, cond_ctx,
        jnp.zeros((bp, N_EVEN, 2, W_A), f32),
        jnp.zeros((bp, N_ODD, 2, H_C, DK_C, DV_C), f32),
        jnp.zeros((bp, N_ODD, 2, H_C, DK_C), f32),
        jnp.zeros((bp, N_ODD, 2, H_C), f32),
        w_mod, b_mod, g_pre, g_post, ab, cw)
    y_sample, _, _, _, _ = run_trunk(
        x_sample, c, state_lru, state_mlstm_C, state_mlstm_n, state_mlstm_m,
        w_mod, b_mod, g_pre, g_post, ab, cw)
    return (y_prompt, y_sample, new_lru, new_C, new_n, new_m)
```

```python
import functools
import math

import numpy as np
import jax
import jax.numpy as jnp
from jax import lax
from jax.experimental import pallas as pl
from jax.experimental.pallas import tpu as pltpu

F32 = jnp.float32
BF16 = jnp.bfloat16

D_MODEL = 1024
DEPTH = 2
EPS = 1e-6
LRU_C = 8.0
CONV_W = 4
N_HEADS = 8
BLK = D_MODEL // N_HEADS
DV = 2 * BLK
CHUNK = 128
FFT_L2 = 256
MOD_ROWS = 8

LANES = 128
SUBLANES = 8
VMEM_LIMIT_CAP = 56 * 1024 * 1024


def _cparams(sem, vmem_bytes):
    return pltpu.CompilerParams(dimension_semantics=sem,
                                vmem_limit_bytes=int(min(max(vmem_bytes, 16 * 1024 * 1024), VMEM_LIMIT_CAP)))


def _silu(x):
    return x * jax.nn.sigmoid(x)


def _mod_kernel(cond_ref, w_ref, b_ref, o_ref):
    s = _silu(cond_ref[...]).astype(BF16)
    o_ref[0] = jnp.dot(s, w_ref[0].astype(BF16), preferred_element_type=F32) + b_ref[0]


def _modulation(cond, w_mod, b_mod):
    d = D_MODEL
    return pl.pallas_call(
        _mod_kernel,
        grid=(DEPTH, 3),
        in_specs=[pl.BlockSpec((MOD_ROWS, d), lambda l, j: (0, 0)),
                  pl.BlockSpec((1, d, d), lambda l, j: (l, 0, j)),
                  pl.BlockSpec((1, 1, d), lambda l, j: (l, 0, j))],
        out_specs=pl.BlockSpec((1, MOD_ROWS, d), lambda l, j: (l, 0, j)),
        out_shape=jax.ShapeDtypeStruct((DEPTH, MOD_ROWS, 3 * d), F32),
        compiler_params=_cparams(("arbitrary", "arbitrary"), 24 * 1024 * 1024),
        name="adaln_mod",
    )(cond, w_mod, b_mod.reshape(DEPTH, 1, 3 * d))


def _normed_input(x_ref, mod_ref, g_ref):
    d = D_MODEL
    x = x_ref[...]
    y = x * lax.rsqrt(jnp.mean(x * x, axis=-1, keepdims=True) + EPS) * g_ref[...]
    shift = mod_ref[0, :, 0:d]
    scale = mod_ref[0, :, d:2 * d]
    return (y * (1.0 + scale) + shift).astype(BF16)


def _mod_index_map(row_off, tiles_per_mod):
    return lambda i: (row_off + i // tiles_per_mod, 0, 0)


def _inproj_e_kernel(x_ref, mod_ref, g_ref, w_ref, dft_ref, xa_ref, za_ref, zb_ref, xc_ref, xs_ref):
    d = D_MODEL
    hb = _normed_input(x_ref, mod_ref, g_ref)
    xa_ref[...] = jnp.dot(hb, w_ref[:, 0:d], preferred_element_type=F32)
    za_ref[...] = jnp.dot(hb, w_ref[:, d:2 * d], preferred_element_type=F32)
    zb_ref[...] = jnp.dot(hb, w_ref[:, 3 * d:4 * d], preferred_element_type=F32)
    xb = jnp.dot(hb, w_ref[:, 2 * d:3 * d], preferred_element_type=F32).astype(BF16)
    for g in range(N_HEADS):
        cs = jnp.dot(xb[:, g * BLK:(g + 1) * BLK], dft_ref[...], preferred_element_type=F32)
        xc_ref[:, g * BLK:(g + 1) * BLK] = cs[:, 0:BLK]
        xs_ref[:, g * BLK:(g + 1) * BLK] = cs[:, BLK:2 * BLK]


def _inproj_e(x2d, mod_l, g_pre, w_in, dft_c, row_off, rows_per_mod, tm):
    m, d = x2d.shape
    row = lambda i: (i, 0)
    const = lambda i: (0, 0)
    out = jax.ShapeDtypeStruct((m, d), F32)
    vmem = 2 * (w_in.size * 2 + tm * d * 4 * 6) + 8 * tm * d * 4
    return pl.pallas_call(
        _inproj_e_kernel,
        grid=(m // tm,),
        in_specs=[pl.BlockSpec((tm, d), row),
                  pl.BlockSpec((1, 1, 3 * d), _mod_index_map(row_off, rows_per_mod // tm)),
                  pl.BlockSpec((1, d), const),
                  pl.BlockSpec(w_in.shape, const),
                  pl.BlockSpec(dft_c.shape, const)],
        out_specs=[pl.BlockSpec((tm, d), row)] * 5,
        out_shape=[out] * 5,
        compiler_params=_cparams(("parallel",), vmem),
        name="inproj_even",
    )(x2d, mod_l, g_pre, w_in, dft_c)


def _scan_chunk(a, u, reverse):
    t = a.shape[0]
    row = lax.broadcasted_iota(jnp.int32, a.shape, 0)
    d = 1
    while d < t:
        if d < SUBLANES:
            shift = (t - d) if reverse else d
            a_sh = pltpu.roll(a, shift, axis=0)
            u_sh = pltpu.roll(u, shift, axis=0)
            valid = (row < t - d) if reverse else (row >= d)
            a_sh = jnp.where(valid, a_sh, 1.0)
            u_sh = jnp.where(valid, u_sh, 0.0)
        else:
            ones = jnp.ones((d, a.shape[1]), F32)
            zeros = jnp.zeros((d, a.shape[1]), F32)
            if reverse:
                a_sh = jnp.concatenate([a[d:], ones], axis=0)
                u_sh = jnp.concatenate([u[d:], zeros], axis=0)
            else:
                a_sh = jnp.concatenate([ones, a[:t - d]], axis=0)
                u_sh = jnp.concatenate([zeros, u[:t - d]], axis=0)
        u = a * u_sh + u
        a = a * a_sh
        d *= 2
    return a, u


def _rglru_kernel(xa_ref, za_ref, cw_ref, cb_ref, wg_ref, bg_ref, lam_ref, h0_ref,
                  y_ref, hT_ref, hs_ref, *, seq, tc):
    w = xa_ref.shape[2]
    nblk = w // BLK
    nchunks = seq // tc
    pad = SUBLANES

    def conv_chunk(c):
        t0 = pl.multiple_of(c * tc, tc)
        cur = xa_ref[0, pl.ds(t0, tc), :]
        prev = xa_ref[0, pl.ds(pl.multiple_of(jnp.maximum(t0 - pad, 0), pad), pad), :]
        nxt = xa_ref[0, pl.ds(pl.multiple_of(jnp.minimum(t0 + tc, seq - pad), pad), pad), :]
        prev = jnp.where(c > 0, prev, 0.0)
        nxt = jnp.where(c < nchunks - 1, nxt, 0.0)
        ext = jnp.concatenate([prev, cur, nxt], axis=0)
        acc = cb_ref[...] + cw_ref[0:1, :] * ext[pad - 2:pad - 2 + tc]
        for j in range(1, CONV_W):
            acc = acc + cw_ref[j:j + 1, :] * ext[pad - 2 + j:pad - 2 + j + tc]
        return t0, acc

    def gates(xc, direction):
        a_parts, u_parts = [], []
        for kb in range(nblk):
            xk = xc[:, kb * BLK:(kb + 1) * BLK]
            g = jnp.dot(xk.astype(BF16), wg_ref[direction, kb], preferred_element_type=F32) + bg_ref[direction, kb]
            r = jax.nn.sigmoid(g[:, 0:BLK])
            i = jax.nn.sigmoid(g[:, BLK:2 * BLK])
            lam = lam_ref[direction, :, kb * BLK:(kb + 1) * BLK]
            softplus_neg = jnp.maximum(-lam, 0.0) + jnp.log1p(jnp.exp(-jnp.abs(lam)))
            a = jnp.exp(-LRU_C * r * softplus_neg)
            a_parts.append(a)
            u_parts.append(jnp.sqrt(1.0 - a * a) * (i * xk))
        if nblk == 1:
            return a_parts[0], u_parts[0]
        return jnp.concatenate(a_parts, axis=1), jnp.concatenate(u_parts, axis=1)

    def fwd_body(c, carry):
        t0, xc = conv_chunk(c)
        a, u = gates(xc, 0)
        p, s = _scan_chunk(a, u, reverse=False)
        h = p * carry + s
        hs_ref[pl.ds(t0, tc), :] = h
        return h[tc - 1:tc, :]

    h_last = lax.fori_loop(0, nchunks, fwd_body, h0_ref[0, 0:1, :])
    hT_ref[0, 0:1, :] = h_last

    def bwd_body(k, carry):
        c = nchunks - 1 - k
        t0, xc = conv_chunk(c)
        a, u = gates(xc, 1)
        p, s = _scan_chunk(a, u, reverse=True)
        h = p * carry + s
        ya = hs_ref[pl.ds(t0, tc), :] + h
        y_ref[0, pl.ds(t0, tc), :] = (ya * _silu(za_ref[0, pl.ds(t0, tc), :])).astype(y_ref.dtype)
        return h[0:1, :]

    h_first = lax.fori_loop(0, nchunks, bwd_body, h0_ref[0, 1:2, :])
    hT_ref[0, 1:2, :] = h_first


def _rglru(xa, za, h0, conv_w, conv_b, wg, bg, lam, w_blk, tc):
    b, seq, d = xa.shape
    nblk = w_blk // BLK
    blk = lambda i, j: (i, 0, j)
    vmem = 2 * (2 * seq * w_blk * 4 + seq * w_blk * 2) + seq * w_blk * 4 + 64 * tc * w_blk * 4
    return pl.pallas_call(
        functools.partial(_rglru_kernel, seq=seq, tc=tc),
        grid=(b, d // w_blk),
        in_specs=[pl.BlockSpec((1, seq, w_blk), blk),
                  pl.BlockSpec((1, seq, w_blk), blk),
                  pl.BlockSpec((CONV_W, w_blk), lambda i, j: (0, j)),
                  pl.BlockSpec((1, w_blk), lambda i, j: (0, j)),
                  pl.BlockSpec((2, nblk, BLK, 2 * BLK), lambda i, j: (0, j, 0, 0)),
                  pl.BlockSpec((2, nblk, 1, 2 * BLK), lambda i, j: (0, j, 0, 0)),
                  pl.BlockSpec((2, 1, w_blk), lambda i, j: (0, 0, j)),
                  pl.BlockSpec((1, 2, w_blk), blk)],
        out_specs=[pl.BlockSpec((1, seq, w_blk), blk),
                   pl.BlockSpec((1, 2, w_blk), blk)],
        out_shape=[jax.ShapeDtypeStruct((b, seq, d), BF16),
                   jax.ShapeDtypeStruct((b, 2, d), F32)],
        scratch_shapes=[pltpu.VMEM((seq, w_blk), F32)],
        compiler_params=_cparams(("parallel", "parallel"), vmem),
        name="rglru_scan",
    )(xa, za, conv_w, conv_b, wg, bg, lam, h0)


def _fft_list(xs):
    n = len(xs)
    if n == 1:
        return xs
    even = _fft_list(xs[0::2])
    odd = _fft_list(xs[1::2])
    out = [None] * n
    for k in range(n // 2):
        o_re, o_im = odd[k]
        if k == 0:
            t_re, t_im = o_re, o_im
        elif 4 * k == n:
            t_re, t_im = o_im, -o_re
        else:
            ang = -2.0 * math.pi * k / n
            wr, wi = math.cos(ang), math.sin(ang)
            t_re = o_re * wr - o_im * wi
            t_im = o_re * wi + o_im * wr
        e_re, e_im = even[k]
        out[k] = (e_re + t_re, e_im + t_im)
        out[k + n // 2] = (e_re - t_re, e_im - t_im)
    return out


def _fourier_kernel(xc_ref, xs_ref, zb_ref, twc_ref, tws_ref, wpos_ref, wf_ref, bf_ref, y_ref, *scratch,
                    seq, tr):
    w = xc_ref.shape[2]
    nblk = w // BLK
    l1 = seq // FFT_L2

    def epilogue(fr, rows):
        parts = []
        for kb in range(nblk):
            yb = jnp.dot(fr[:, kb * BLK:(kb + 1) * BLK].astype(BF16), wf_ref[kb], preferred_element_type=F32)
            parts.append(yb + bf_ref[:, kb * BLK:(kb + 1) * BLK])
        yb = parts[0] if nblk == 1 else jnp.concatenate(parts, axis=1)
        y_ref[0, rows, :] = (yb * _silu(zb_ref[0, rows, :])).astype(y_ref.dtype)

    if l1 == 1:
        fr = (jnp.dot(wpos_ref[:, 0:FFT_L2], xc_ref[0].astype(BF16), preferred_element_type=F32)
              - jnp.dot(wpos_ref[:, FFT_L2:2 * FFT_L2], xs_ref[0].astype(BF16), preferred_element_type=F32))
        epilogue(fr, pl.ds(0, seq))
        return

    b_ref, fr_ref = scratch
    assert w == LANES

    def butterfly(r, carry):
        r0 = pl.multiple_of(r * SUBLANES, SUBLANES)
        zs = [(xc_ref[0, pl.ds(n1 * FFT_L2 + r0, SUBLANES), :], -xs_ref[0, pl.ds(n1 * FFT_L2 + r0, SUBLANES), :])
              for n1 in range(l1)]
        for k1, (a_re, a_im) in enumerate(_fft_list(zs)):
            if k1 == 0:
                b_re, b_im = a_re, a_im
            else:
                tc_ = twc_ref[k1, pl.ds(r0, SUBLANES), :]
                ts_ = tws_ref[k1, pl.ds(r0, SUBLANES), :]
                b_re = a_re * tc_ + a_im * ts_
                b_im = a_im * tc_ - a_re * ts_
            b_ref[k1, pl.ds(r0, SUBLANES), :] = b_re
            b_ref[k1, pl.ds(FFT_L2 + r0, SUBLANES), :] = b_im
        return carry

    lax.fori_loop(0, FFT_L2 // SUBLANES, butterfly, 0)

    def position_dft(k1, carry):
        fr = jnp.dot(wpos_ref[...], b_ref[k1].astype(BF16), preferred_element_type=F32)
        fr_ref[pl.ds(k1, FFT_L2, stride=l1), :] = fr
        return carry

    lax.fori_loop(0, l1, position_dft, 0)

    def finish(i, carry):
        rows = pl.ds(pl.multiple_of(i * tr, tr), tr)
        epilogue(fr_ref[rows, :], rows)
        return carry

    lax.fori_loop(0, seq // tr, finish, 0)


def _fourier(xc, xs, zb, twc, tws, w_pos, w_four, b_four, w_blk, tr):
    b, seq, d = xc.shape
    nblk = w_blk // BLK
    l1 = seq // FFT_L2
    blk = lambda i, j: (i, 0, j)
    scratch = []
    if l1 > 1:
        scratch = [pltpu.VMEM((l1, 2 * FFT_L2, w_blk), F32), pltpu.VMEM((seq, w_blk), F32)]
    vmem = (2 * (3 * seq * w_blk * 4 + seq * w_blk * 2 + 2 * twc.size * 4) + 3 * seq * w_blk * 4
            + 16 * 1024 * 1024)
    return pl.pallas_call(
        functools.partial(_fourier_kernel, seq=seq, tr=tr),
        grid=(b, d // w_blk),
        in_specs=[pl.BlockSpec((1, seq, w_blk), blk),
                  pl.BlockSpec((1, seq, w_blk), blk),
                  pl.BlockSpec((1, seq, w_blk), blk),
                  pl.BlockSpec(twc.shape, lambda i, j: (0, 0, 0)),
                  pl.BlockSpec(tws.shape, lambda i, j: (0, 0, 0)),
                  pl.BlockSpec(w_pos.shape, lambda i, j: (0, 0)),
                  pl.BlockSpec((nblk, BLK, BLK), lambda i, j: (j, 0, 0)),
                  pl.BlockSpec((1, w_blk), lambda i, j: (0, j))],
        out_specs=pl.BlockSpec((1, seq, w_blk), blk),
        out_shape=jax.ShapeDtypeStruct((b, seq, d), BF16),
        scratch_shapes=scratch,
        compiler_params=_cparams(("parallel", "parallel"), vmem),
        name="fourier_mix",
    )(xc, xs, zb, twc, tws, w_pos, w_four, b_four)


def _outproj_kernel(*refs, n_in):
    y_refs = refs[:n_in]
    w_refs = refs[n_in:2 * n_in]
    x_ref, mod_ref, g_ref, o_ref = refs[2 * n_in:]
    d = D_MODEL
    y = jnp.dot(y_refs[0][...], w_refs[0][...], preferred_element_type=F32)
    for k in range(1, n_in):
        y = y + jnp.dot(y_refs[k][...], w_refs[k][...], preferred_element_type=F32)
    yn = y * lax.rsqrt(jnp.mean(y * y, axis=-1, keepdims=True) + EPS) * g_ref[...]
    o_ref[...] = x_ref[...] + mod_ref[0, :, 2 * d:3 * d] * yn


def _outproj(ys, ws, x2d, mod_l, g_post, row_off, rows_per_mod, tm):
    m, d = x2d.shape
    n_in = len(ys)
    row = lambda i: (i, 0)
    const = lambda i: (0, 0)
    vmem = 2 * (sum(wk.size for wk in ws) * 2 + sum(tm * yk.shape[1] for yk in ys) * 2 + 2 * tm * d * 4) + 4 * tm * d * 4
    return pl.pallas_call(
        functools.partial(_outproj_kernel, n_in=n_in),
        grid=(m // tm,),
        in_specs=([pl.BlockSpec((tm, yk.shape[1]), row) for yk in ys]
                  + [pl.BlockSpec(wk.shape, const) for wk in ws]
                  + [pl.BlockSpec((tm, d), row),
                     pl.BlockSpec((1, 1, 3 * d), _mod_index_map(row_off, rows_per_mod // tm)),
                     pl.BlockSpec((1, d), const)]),
        out_specs=pl.BlockSpec((tm, d), row),
        out_shape=jax.ShapeDtypeStruct((m, d), F32),
        compiler_params=_cparams(("parallel",), vmem),
        name="outproj_residual",
    )(*ys, *ws, x2d, mod_l, g_post)


def _inproj_o_kernel(x_ref, mod_ref, g_ref, w_ref, wgt_ref, q_ref, k_ref, v_ref, o_ref, z_ref, gpt_ref):
    d = D_MODEL
    hb = _normed_input(x_ref, mod_ref, g_ref)
    q = jnp.dot(hb, w_ref[:, 0:d], preferred_element_type=F32)
    q_ref[...] = (q * (BLK ** -0.5)).astype(q_ref.dtype)
    k_ref[...] = jnp.dot(hb, w_ref[:, d:2 * d], preferred_element_type=F32).astype(k_ref.dtype)
    for half in range(2):
        cols = slice(half * d, (half + 1) * d)
        v_ref[:, cols] = jnp.dot(hb, w_ref[:, 2 * d + half * d:3 * d + half * d],
                                 preferred_element_type=F32).astype(v_ref.dtype)
        o_ref[:, cols] = jnp.dot(hb, w_ref[:, 4 * d + half * d:5 * d + half * d], preferred_element_type=F32)
        z_ref[:, cols] = jnp.dot(hb, w_ref[:, 6 * d + half * d:7 * d + half * d], preferred_element_type=F32)
    gpt_ref[...] = lax.dot_general(wgt_ref[...], hb, (((1,), (1,)), ((), ())), preferred_element_type=F32)


def _inproj_o(x2d, mod_l, g_pre, w_main, w_gate_t, row_off, rows_per_mod, tm):
    m, d = x2d.shape
    row = lambda i: (i, 0)
    const = lambda i: (0, 0)
    ng = w_gate_t.shape[0]
    vmem = 2 * (w_main.size * 2 + tm * d * 4 + tm * d * 2 * 4 + tm * 2 * d * 4 * 2) + 6 * tm * d * 4
    return pl.pallas_call(
        _inproj_o_kernel,
        grid=(m // tm,),
        in_specs=[pl.BlockSpec((tm, d), row),
                  pl.BlockSpec((1, 1, 3 * d), _mod_index_map(row_off, rows_per_mod // tm)),
                  pl.BlockSpec((1, d), const),
                  pl.BlockSpec(w_main.shape, const),
                  pl.BlockSpec(w_gate_t.shape, const)],
        out_specs=[pl.BlockSpec((tm, d), row), pl.BlockSpec((tm, d), row),
                   pl.BlockSpec((tm, 2 * d), row), pl.BlockSpec((tm, 2 * d), row), pl.BlockSpec((tm, 2 * d), row),
                   pl.BlockSpec((ng, tm), lambda i: (0, i))],
        out_shape=[jax.ShapeDtypeStruct((m, d), BF16), jax.ShapeDtypeStruct((m, d), BF16),
                   jax.ShapeDtypeStruct((m, 2 * d), BF16), jax.ShapeDtypeStruct((m, 2 * d), F32),
                   jax.ShapeDtypeStruct((m, 2 * d), F32), jax.ShapeDtypeStruct((ng, m), F32)],
        compiler_params=_cparams(("parallel",), vmem),
        name="inproj_odd",
    )(x2d, mod_l, g_pre, w_main, w_gate_t)


def _lane_cumsum(v, reverse):
    t = v.shape[-1]
    lane = lax.broadcasted_iota(jnp.int32, v.shape, v.ndim - 1)
    d = 1
    while d < t:
        if reverse:
            v = v + jnp.where(lane < t - d, pltpu.roll(v, t - d, axis=v.ndim - 1), 0.0)
        else:
            v = v + jnp.where(lane >= d, pltpu.roll(v, d, axis=v.ndim - 1), 0.0)
        d *= 2
    return v


def _mlstm_kernel(q_ref, k_ref, v_ref, o_ref, z_ref, gp_ref, bias_ref, c0_ref, n0_ref, m0_ref,
                  y_ref, c_out_ref, n_out_ref, m_out_ref, hs_ref, *, seq):
    t = CHUNK
    nchunks = seq // t
    row_i = lax.broadcasted_iota(jnp.int32, (t, t), 0)
    col_i = lax.broadcasted_iota(jnp.int32, (t, t), 1)
    sub_i = lax.broadcasted_iota(jnp.int32, (SUBLANES, t), 0)
    pad_rows = jnp.zeros((t - SUBLANES, t), F32)

    def chunk(c, direction, state):
        c_mat, n_row, m_rep = state
        reverse = direction == 1
        rows = pl.ds(pl.multiple_of(c * t, t), t)
        qc = q_ref[0, rows, :]
        kc = k_ref[0, rows, :]
        vc = v_ref[0, rows, :]
        g = gp_ref[0, 0, :, rows] + bias_ref[0]
        i_pre = g[2 * direction:2 * direction + 1, :]
        lf = jax.nn.log_sigmoid(g[2 * direction + 1:2 * direction + 2, :])
        b_row = _lane_cumsum(lf, reverse)
        b_end = b_row[:, 0:1] if reverse else b_row[:, t - 1:t]
        a_row = i_pre - b_row
        g_row = b_end + a_row
        m_prev = m_rep[:, 0:1]
        m_new = jnp.maximum(b_end + m_prev, jnp.max(g_row, axis=-1, keepdims=True))
        decay = jnp.exp(b_end + m_prev - m_new)
        stacked = jnp.where(sub_i == 0, b_row, jnp.where(sub_i == 1, g_row, 0.0))
        cols = jnp.concatenate([stacked, pad_rows], axis=0).T
        b_col = cols[:, 0:1]
        wk_col = jnp.exp(cols[:, 1:2] - m_new)
        mask = (col_i >= row_i) if reverse else (col_i <= row_i)
        dmat = jnp.where(mask, b_col + a_row, -jnp.inf)
        m_inter = b_col + m_prev
        m_row = jnp.maximum(m_inter, jnp.max(dmat, axis=-1, keepdims=True))
        s = lax.dot_general(qc, kc, (((1,), (1,)), ((), ())), preferred_element_type=F32) * jnp.exp(dmat - m_row)
        w_inter = jnp.exp(m_inter - m_row)
        num = (jnp.dot(s.astype(BF16), vc, preferred_element_type=F32)
               + w_inter * jnp.dot(qc, c_mat.astype(BF16), preferred_element_type=F32))
        den = (jnp.sum(s, axis=-1, keepdims=True)
               + w_inter * jnp.sum(qc.astype(F32) * n_row, axis=-1, keepdims=True))
        hc = num / jnp.maximum(jnp.abs(den), jnp.exp(-m_row))
        wkk = wk_col * kc.astype(F32)
        c_new = decay * c_mat + lax.dot_general(wkk.astype(BF16), vc, (((0,), (0,)), ((), ())),
                                                preferred_element_type=F32)
        n_new = decay * n_row + jnp.sum(wkk, axis=0, keepdims=True)
        m_new_rep = jnp.broadcast_to(m_new, (1, t))
        return rows, hc, (c_new, n_new, m_new_rep)

    def fwd_body(c, state):
        rows, hc, state = chunk(c, 0, state)
        hs_ref[rows, :] = hc
        return state

    st = lax.fori_loop(0, nchunks, fwd_body, (c0_ref[0, 0, 0], n0_ref[0, 0, 0], m0_ref[0, 0, 0]))
    c_out_ref[0, 0, 0] = st[0]
    n_out_ref[0, 0, 0] = st[1]
    m_out_ref[0, 0, 0] = st[2]

    def bwd_body(k, state):
        rows, hc, state = chunk(nchunks - 1 - k, 1, state)
        hsum = hs_ref[rows, :] + hc
        y_ref[0, rows, :] = (jax.nn.sigmoid(o_ref[0, rows, :]) * hsum * _silu(z_ref[0, rows, :])).astype(y_ref.dtype)
        return state

    st = lax.fori_loop(0, nchunks, bwd_body, (c0_ref[0, 1, 0], n0_ref[0, 1, 0], m0_ref[0, 1, 0]))
    c_out_ref[0, 1, 0] = st[0]
    n_out_ref[0, 1, 0] = st[1]
    m_out_ref[0, 1, 0] = st[2]


def _mlstm(q, k, v, o, z, gp, bias, c0, n0, m0):
    b, seq, _ = q.shape
    qk_blk = lambda i, h: (i, 0, h)
    st5 = lambda i, h: (i, 0, h, 0, 0)
    vmem = 2 * (2 * seq * BLK * 2 + seq * DV * 2 * 2 + 2 * seq * DV * 4 + 2 * 2 * BLK * DV * 4) + seq * DV * 4 \
        + 16 * 1024 * 1024
    return pl.pallas_call(
        functools.partial(_mlstm_kernel, seq=seq),
        grid=(b, N_HEADS),
        in_specs=[pl.BlockSpec((1, seq, BLK), qk_blk),
                  pl.BlockSpec((1, seq, BLK), qk_blk),
                  pl.BlockSpec((1, seq, DV), qk_blk),
                  pl.BlockSpec((1, seq, DV), qk_blk),
                  pl.BlockSpec((1, seq, DV), qk_blk),
                  pl.BlockSpec((1, 1, SUBLANES, seq), lambda i, h: (i, h, 0, 0)),
                  pl.BlockSpec((1, SUBLANES, CHUNK), lambda i, h: (h, 0, 0)),
                  pl.BlockSpec((1, 2, 1, BLK, DV), st5),
                  pl.BlockSpec((1, 2, 1, 1, BLK), st5),
                  pl.BlockSpec((1, 2, 1, 1, CHUNK), st5)],
        out_specs=[pl.BlockSpec((1, seq, DV), qk_blk),
                   pl.BlockSpec((1, 2, 1, BLK, DV), st5),
                   pl.BlockSpec((1, 2, 1, 1, BLK), st5),
                   pl.BlockSpec((1, 2, 1, 1, CHUNK), st5)],
        out_shape=[jax.ShapeDtypeStruct((b, seq, N_HEADS * DV), BF16),
                   jax.ShapeDtypeStruct((b, 2, N_HEADS, BLK, DV), F32),
                   jax.ShapeDtypeStruct((b, 2, N_HEADS, 1, BLK), F32),
                   jax.ShapeDtypeStruct((b, 2, N_HEADS, 1, CHUNK), F32)],
        scratch_shapes=[pltpu.VMEM((seq, DV), F32)],
        compiler_params=_cparams(("parallel", "parallel"), vmem),
        name="mlstm_chunkwise",
    )(q, k, v, o, z, gp, bias, c0, n0, m0)


def _dft_tables(seq):
    n = np.arange(BLK)
    ang = 2.0 * np.pi * np.outer(n, n) / BLK
    dft_c = np.concatenate([np.cos(ang), np.sin(ang)], axis=1) / np.sqrt(BLK)
    l1 = seq // FFT_L2
    n2 = np.arange(FFT_L2)
    ang2 = 2.0 * np.pi * np.outer(n2, n2) / FFT_L2
    w_pos = np.concatenate([np.cos(ang2), np.sin(ang2)], axis=1) / np.sqrt(seq)
    angt = 2.0 * np.pi * np.outer(np.arange(l1), n2) / seq
    twc = np.broadcast_to(np.cos(angt)[:, :, None], (l1, FFT_L2, LANES))
    tws = np.broadcast_to(np.sin(angt)[:, :, None], (l1, FFT_L2, LANES))
    return (jnp.asarray(dft_c, F32).astype(BF16), jnp.asarray(w_pos, F32).astype(BF16),
            jnp.asarray(twc, F32), jnp.asarray(tws, F32))


def _run_trunk(x, mod, row_off, per_batch_mod, lru0, c0, n0, m0, p, cfg):
    b, seq, d = x.shape
    m = b * seq
    rows_per_mod = seq if per_batch_mod else m
    x2d = x.reshape(m, d)
    dft_c, w_pos, twc, tws = _dft_tables(seq)

    mod0 = mod[0].reshape(MOD_ROWS, 1, 3 * d)
    xa, za, zb, xc, xs = _inproj_e(x2d, mod0, p["g_pre"][0:1], p["w_in_e"], dft_c, row_off, rows_per_mod,
                                   cfg["tm_in"])
    r3 = lambda a: a.reshape(b, seq, a.shape[-1])
    ya, lru_f = _rglru(r3(xa), r3(za), lru0, p["conv_w"], p["conv_b"], p["wg"], p["bg"], p["lam"],
                       cfg["w_lru"], cfg["tc_lru"])
    yb = _fourier(r3(xc), r3(xs), r3(zb), twc, tws, w_pos, p["w_four"], p["b_four"], cfg["w_four"], cfg["tr_four"])
    x1 = _outproj([ya.reshape(m, d), yb.reshape(m, d)], [p["w_out_e"][0:d], p["w_out_e"][d:2 * d]],
                  x2d, mod0, p["g_post"][0:1], row_off, rows_per_mod, cfg["tm_out"])

    mod1 = mod[1].reshape(MOD_ROWS, 1, 3 * d)
    q, k, v, o, z, gpt = _inproj_o(x1, mod1, p["g_pre"][1:2], p["w_in_o"], p["w_gate_t"], row_off, rows_per_mod,
                                   cfg["tm_in_o"])
    gp = gpt[0:4 * N_HEADS].reshape(4, N_HEADS, b, seq).transpose(2, 1, 0, 3)
    gp = jnp.concatenate([gp, jnp.zeros((b, N_HEADS, SUBLANES - 4, seq), F32)], axis=2)
    y, c_f, n_f, m_f = _mlstm(r3(q), r3(k), r3(v), r3(o), r3(z), gp, p["gate_bias"], c0, n0, m0)
    x2 = _outproj([y.reshape(m, 2 * d)], [p["w_out_o"]], x1, mod1, p["g_post"][1:2], row_off, rows_per_mod,
                  cfg["tm_out"])
    return x2.reshape(b, seq, d), lru_f, c_f, n_f, m_f


def _prepare_params(conv_w, conv_b, w_rg, b_rg, w_ig, b_ig, lru_lambda, w_four, b_four, w_in_e, w_out_e,
                    w_in_o, b_if, w_out_o, g_pre, g_post):
    d = D_MODEL
    wg = jnp.concatenate([w_rg[0], w_ig[0]], axis=-1).astype(BF16)
    bg = jnp.concatenate([b_rg[0].reshape(2, N_HEADS, 1, BLK), b_ig[0].reshape(2, N_HEADS, 1, BLK)], axis=-1)
    n_gate = 4 * N_HEADS
    w_gate_t = jnp.zeros((LANES, d), F32).at[0:n_gate].set(w_in_o[0][:, 8 * d:8 * d + n_gate].T).astype(BF16)
    bias = b_if[0].reshape(4, N_HEADS).T
    bias = jnp.concatenate([bias, jnp.zeros((N_HEADS, SUBLANES - 4), F32)], axis=1)
    gate_bias = jnp.broadcast_to(bias[:, :, None], (N_HEADS, SUBLANES, CHUNK))
    return dict(
        g_pre=g_pre, g_post=g_post,
        w_in_e=w_in_e[0].astype(BF16), w_out_e=w_out_e[0].astype(BF16),
        conv_w=conv_w[0], conv_b=conv_b[0].reshape(1, d), wg=wg, bg=bg, lam=lru_lambda[0].reshape(2, 1, d),
        w_four=w_four[0].astype(BF16), b_four=b_four[0].reshape(1, d),
        w_in_o=w_in_o[0][:, 0:8 * d].astype(BF16), w_gate_t=w_gate_t, gate_bias=gate_bias,
        w_out_o=w_out_o[0].astype(BF16))


def kernel(x_prompt, x_sample, c, state_lru, state_mlstm_C, state_mlstm_n, state_mlstm_m, c_ctx, w_mod, b_mod,
           g_pre, g_post, w_in_e, conv_w, conv_b, w_rg, b_rg, w_ig, b_ig, lru_lambda, w_four, b_four, w_out_e,
           w_in_o, b_if, w_out_o):
    d = D_MODEL
    bp = x_prompt.shape[0]
    bs = x_sample.shape[0]
    p = _prepare_params(conv_w, conv_b, w_rg, b_rg, w_ig, b_ig, lru_lambda, w_four, b_four, w_in_e, w_out_e,
                        w_in_o, b_if, w_out_o, g_pre, g_post)
    cond = jnp.concatenate([c_ctx[None, :], c, jnp.zeros((MOD_ROWS - 1 - bs, d), F32)], axis=0)
    mod = _modulation(cond, w_mod, b_mod)

    def states(lru, cm, nv, mv):
        bsz = lru.shape[0]
        return (lru[:, 0], cm[:, 0], nv[:, 0].reshape(bsz, 2, N_HEADS, 1, BLK),
                jnp.broadcast_to(mv[:, 0].reshape(bsz, 2, N_HEADS, 1, 1), (bsz, 2, N_HEADS, 1, CHUNK)))

    zero_states = states(jnp.zeros((bp, 1, 2, d), F32), jnp.zeros((bp, 1, 2, N_HEADS, BLK, DV), F32),
                         jnp.zeros((bp, 1, 2, N_HEADS, BLK), F32), jnp.zeros((bp, 1, 2, N_HEADS), F32))
    cfg_p = dict(tm_in=512, tm_in_o=256, tm_out=512, w_lru=d, tc_lru=128, w_four=d, tr_four=256)
    y_prompt, lru_f, c_f, n_f, m_f = _run_trunk(x_prompt, mod, 0, False, *zero_states, p, cfg_p)

    cfg_s = dict(tm_in=512, tm_in_o=256, tm_out=512, w_lru=2 * BLK, tc_lru=128, w_four=BLK, tr_four=256)
    y_sample, _, _, _, _ = _run_trunk(x_sample, mod, 1, True,
                                      *states(state_lru, state_mlstm_C, state_mlstm_n, state_mlstm_m), p, cfg_s)

    return (y_prompt, y_sample, lru_f[:, None], c_f[:, None], n_f[:, None, :, :, 0, :], m_f[:, None, :, :, 0, 0])
```

```python
import functools
import math

import numpy as np
import jax
import jax.numpy as jnp
from jax import lax
from jax.experimental import pallas as pl
from jax.experimental.pallas import tpu as pltpu

F32 = jnp.float32
BF16 = jnp.bfloat16

D_MODEL = 1024
DEPTH = 2
EPS = 1e-6
LRU_C = 8.0
CONV_W = 4
N_HEADS = 8
BLK = D_MODEL // N_HEADS
DV = 2 * BLK
CHUNK = 128
FFT_L2 = 256
MOD_ROWS = 8

LANES = 128
SUBLANES = 8
VMEM_LIMIT_CAP = 56 * 1024 * 1024


def _cparams(sem, vmem_bytes, flags=None):
    return pltpu.CompilerParams(dimension_semantics=sem, flags=flags,
                                vmem_limit_bytes=int(min(max(vmem_bytes, 16 * 1024 * 1024), VMEM_LIMIT_CAP)))


def _silu(x):
    return x * jax.nn.sigmoid(x)


def _mod_kernel(cond_ref, w_ref, b_ref, o_ref):
    s = _silu(cond_ref[...]).astype(BF16)
    o_ref[0] = jnp.dot(s, w_ref[0].astype(BF16), preferred_element_type=F32) + b_ref[0]


def _modulation(cond, w_mod, b_mod):
    d = D_MODEL
    return pl.pallas_call(
        _mod_kernel,
        grid=(DEPTH, 3),
        in_specs=[pl.BlockSpec((MOD_ROWS, d), lambda l, j: (0, 0)),
                  pl.BlockSpec((1, d, d), lambda l, j: (l, 0, j)),
                  pl.BlockSpec((1, 1, d), lambda l, j: (l, 0, j))],
        out_specs=pl.BlockSpec((1, MOD_ROWS, d), lambda l, j: (l, 0, j)),
        out_shape=jax.ShapeDtypeStruct((DEPTH, MOD_ROWS, 3 * d), F32),
        compiler_params=_cparams(("arbitrary", "arbitrary"), 24 * 1024 * 1024),
        name="adaln_mod",
    )(cond, w_mod, b_mod.reshape(DEPTH, 1, 3 * d))


def _normed_input(x_ref, mod_ref, g_ref):
    d = D_MODEL
    x = x_ref[...]
    y = x * lax.rsqrt(jnp.mean(x * x, axis=-1, keepdims=True) + EPS) * g_ref[...]
    shift = mod_ref[0, :, 0:d]
    scale = mod_ref[0, :, d:2 * d]
    return (y * (1.0 + scale) + shift).astype(BF16)


def _mod_index_map(row_off, tiles_per_mod):
    return lambda i: (row_off + i // tiles_per_mod, 0, 0)


def _inproj_e_kernel(x_ref, mod_ref, g_ref, w_ref, dft_ref, xa_ref, za_ref, zb_ref, xc_ref, xs_ref):
    d = D_MODEL
    hb = _normed_input(x_ref, mod_ref, g_ref)
    xa_ref[...] = jnp.dot(hb, w_ref[:, 0:d], preferred_element_type=F32)
    za_ref[...] = jnp.dot(hb, w_ref[:, d:2 * d], preferred_element_type=F32)
    zb_ref[...] = jnp.dot(hb, w_ref[:, 3 * d:4 * d], preferred_element_type=F32)
    xb = jnp.dot(hb, w_ref[:, 2 * d:3 * d], preferred_element_type=F32).astype(BF16)
    for g in range(N_HEADS):
        cs = jnp.dot(xb[:, g * BLK:(g + 1) * BLK], dft_ref[...], preferred_element_type=F32)
        xc_ref[:, g * BLK:(g + 1) * BLK] = cs[:, 0:BLK]
        xs_ref[:, g * BLK:(g + 1) * BLK] = cs[:, BLK:2 * BLK]


def _inproj_e(x2d, mod_l, g_pre, w_in, dft_c, row_off, rows_per_mod, tm):
    m, d = x2d.shape
    row = lambda i: (i, 0)
    const = lambda i: (0, 0)
    out = jax.ShapeDtypeStruct((m, d), F32)
    vmem = 2 * (w_in.size * 2 + tm * d * 4 * 6) + 8 * tm * d * 4
    return pl.pallas_call(
        _inproj_e_kernel,
        grid=(m // tm,),
        in_specs=[pl.BlockSpec((tm, d), row),
                  pl.BlockSpec((1, 1, 3 * d), _mod_index_map(row_off, rows_per_mod // tm)),
                  pl.BlockSpec((1, d), const),
                  pl.BlockSpec(w_in.shape, const),
                  pl.BlockSpec(dft_c.shape, const)],
        out_specs=[pl.BlockSpec((tm, d), row)] * 5,
        out_shape=[out] * 5,
        compiler_params=_cparams(("parallel",), vmem),
        name="inproj_even",
    )(x2d, mod_l, g_pre, w_in, dft_c)


def _scan_chunk(a, u, reverse):
    t = a.shape[0]
    row = lax.broadcasted_iota(jnp.int32, a.shape, 0)
    d = 1
    while d < t:
        if d < SUBLANES:
            shift = (t - d) if reverse else d
            a_sh = pltpu.roll(a, shift, axis=0)
            u_sh = pltpu.roll(u, shift, axis=0)
            valid = (row < t - d) if reverse else (row >= d)
            a_sh = jnp.where(valid, a_sh, 1.0)
            u_sh = jnp.where(valid, u_sh, 0.0)
        else:
            ones = jnp.ones((d, a.shape[1]), F32)
            zeros = jnp.zeros((d, a.shape[1]), F32)
            if reverse:
                a_sh = jnp.concatenate([a[d:], ones], axis=0)
                u_sh = jnp.concatenate([u[d:], zeros], axis=0)
            else:
                a_sh = jnp.concatenate([ones, a[:t - d]], axis=0)
                u_sh = jnp.concatenate([zeros, u[:t - d]], axis=0)
        u = a * u_sh + u
        a = a * a_sh
        d *= 2
    return a, u


def _rglru_kernel(xa_ref, za_ref, cw_ref, cb_ref, wg_ref, bg_ref, lam_ref, h0_ref,
                  y_ref, hT_ref, hf_ref, hb_ref, *, seq, tc):
    nchunks = seq // tc
    groups = tc // SUBLANES
    sub = lax.broadcasted_iota(jnp.int32, (SUBLANES, LANES), 0)

    def strided(ref, t0, g):
        return ref[0, pl.ds(t0 + g, SUBLANES, stride=groups), :]

    def row_bcast(block, r):
        return jnp.broadcast_to(block[r:r + 1, :], (SUBLANES, LANES))

    def conv_chunk(c):
        t0 = pl.multiple_of(c * tc, tc)
        xs = [strided(xa_ref, t0, g) for g in range(groups)]
        prev = xa_ref[0, pl.ds(pl.multiple_of(jnp.maximum(t0 - SUBLANES, 0), SUBLANES), SUBLANES), :]
        nxt = xa_ref[0, pl.ds(pl.multiple_of(jnp.minimum(t0 + tc, seq - SUBLANES), SUBLANES), SUBLANES), :]
        prev = jnp.where(c > 0, prev, 0.0)
        nxt = jnp.where(c < nchunks - 1, nxt, 0.0)
        before2 = jnp.where(sub == 0, row_bcast(prev, SUBLANES - 2), pltpu.roll(xs[groups - 2], 1, axis=0))
        before1 = jnp.where(sub == 0, row_bcast(prev, SUBLANES - 1), pltpu.roll(xs[groups - 1], 1, axis=0))
        after1 = jnp.where(sub == SUBLANES - 1, row_bcast(nxt, 0), pltpu.roll(xs[0], SUBLANES - 1, axis=0))
        ext = [before2, before1] + xs + [after1]
        out = []
        for g in range(groups):
            acc = cb_ref[...] + cw_ref[0:1, :] * ext[g]
            for j in range(1, CONV_W):
                acc = acc + cw_ref[j:j + 1, :] * ext[g + j]
            out.append(acc)
        return t0, jnp.concatenate(out, axis=0)

    def gates(xc, direction):
        g = jnp.dot(xc.astype(BF16), wg_ref[direction, 0], preferred_element_type=F32) + bg_ref[direction, 0]
        r = jax.nn.sigmoid(g[:, 0:BLK])
        i = jax.nn.sigmoid(g[:, BLK:2 * BLK])
        lam = lam_ref[direction]
        softplus_neg = jnp.maximum(-lam, 0.0) + jnp.log1p(jnp.exp(-jnp.abs(lam)))
        neg_log_a = LRU_C * r * softplus_neg
        a = jnp.exp(-neg_log_a)
        om = jnp.tanh(neg_log_a) * (1.0 + a * a)
        return a, jnp.sqrt(om) * (i * xc)

    def scan_dir(c, direction, carry, out_ref):
        reverse = direction == 1
        t0, xc = conv_chunk(c)
        a, u = gates(xc, direction)
        order = range(groups - 1, -1, -1) if reverse else range(groups)
        p_g, s_g = [None] * groups, [None] * groups
        p_run = s_run = None
        for g in order:
            a_v = a[g * SUBLANES:(g + 1) * SUBLANES]
            u_v = u[g * SUBLANES:(g + 1) * SUBLANES]
            if p_run is None:
                p_run, s_run = a_v, u_v
            else:
                s_run = a_v * s_run + u_v
                p_run = a_v * p_run
            p_g[g], s_g[g] = p_run, s_run
        pp, ss = _scan_chunk(p_run, s_run, reverse)
        carry_b = jnp.broadcast_to(carry, (SUBLANES, LANES))
        seg_out = pp * carry_b + ss
        if reverse:
            h_in = jnp.where(sub == SUBLANES - 1, carry_b, pltpu.roll(seg_out, SUBLANES - 1, axis=0))
            new_carry = seg_out[0:1, :]
        else:
            h_in = jnp.where(sub == 0, carry_b, pltpu.roll(seg_out, 1, axis=0))
            new_carry = seg_out[SUBLANES - 1:SUBLANES, :]
        for g in range(groups):
            out_ref[pl.ds(t0 + g, SUBLANES, stride=groups), :] = p_g[g] * h_in + s_g[g]
        return new_carry

    def body(j, carries):
        return (scan_dir(j, 0, carries[0], hf_ref), scan_dir(nchunks - 1 - j, 1, carries[1], hb_ref))

    h_last, h_first = lax.fori_loop(0, nchunks, body, (h0_ref[0, 0:1, :], h0_ref[0, 1:2, :]), unroll=2)
    hT_ref[0, 0:1, :] = h_last
    hT_ref[0, 1:2, :] = h_first

    def gate_out(c, carry):
        rows = pl.ds(pl.multiple_of(c * tc, tc), tc)
        y_ref[0, rows, :] = ((hf_ref[rows, :] + hb_ref[rows, :]) * _silu(za_ref[0, rows, :])).astype(y_ref.dtype)
        return carry

    lax.fori_loop(0, nchunks, gate_out, 0)


def _rglru(xa, za, h0, conv_w, conv_b, wg, bg, lam, tc):
    b, seq, d = xa.shape
    blk = lambda i, j: (i, 0, j)
    vmem = 2 * (2 * seq * BLK * 4 + seq * BLK * 2) + 2 * seq * BLK * 4 + 16 * 1024 * 1024
    return pl.pallas_call(
        functools.partial(_rglru_kernel, seq=seq, tc=tc),
        grid=(b, d // BLK),
        in_specs=[pl.BlockSpec((1, seq, BLK), blk),
                  pl.BlockSpec((1, seq, BLK), blk),
                  pl.BlockSpec((CONV_W, BLK), lambda i, j: (0, j)),
                  pl.BlockSpec((1, BLK), lambda i, j: (0, j)),
                  pl.BlockSpec((2, 1, BLK, 2 * BLK), lambda i, j: (0, j, 0, 0)),
                  pl.BlockSpec((2, 1, 1, 2 * BLK), lambda i, j: (0, j, 0, 0)),
                  pl.BlockSpec((2, 1, BLK), lambda i, j: (0, 0, j)),
                  pl.BlockSpec((1, 2, BLK), blk)],
        out_specs=[pl.BlockSpec((1, seq, BLK), blk),
                   pl.BlockSpec((1, 2, BLK), blk)],
        out_shape=[jax.ShapeDtypeStruct((b, seq, d), BF16),
                   jax.ShapeDtypeStruct((b, 2, d), F32)],
        scratch_shapes=[pltpu.VMEM((seq, BLK), F32), pltpu.VMEM((seq, BLK), F32)],
        compiler_params=_cparams(("parallel", "parallel"), vmem),
        name="rglru_scan",
    )(xa, za, conv_w, conv_b, wg, bg, lam, h0)


def _fft_list(xs):
    n = len(xs)
    if n == 1:
        return xs
    even = _fft_list(xs[0::2])
    odd = _fft_list(xs[1::2])
    out = [None] * n
    for k in range(n // 2):
        o_re, o_im = odd[k]
        if k == 0:
            t_re, t_im = o_re, o_im
        elif 4 * k == n:
            t_re, t_im = o_im, -o_re
        else:
            ang = -2.0 * math.pi * k / n
            wr, wi = math.cos(ang), math.sin(ang)
            t_re = o_re * wr - o_im * wi
            t_im = o_re * wi + o_im * wr
        e_re, e_im = even[k]
        out[k] = (e_re + t_re, e_im + t_im)
        out[k + n // 2] = (e_re - t_re, e_im - t_im)
    return out


def _fourier_kernel(xc_ref, xs_ref, zb_ref, twc_ref, tws_ref, wpos_ref, wf_ref, bf_ref, y_ref, *scratch,
                    seq, tr):
    w = xc_ref.shape[2]
    nblk = w // BLK
    l1 = seq // FFT_L2

    def epilogue(fr, rows):
        parts = []
        for kb in range(nblk):
            yb = jnp.dot(fr[:, kb * BLK:(kb + 1) * BLK].astype(BF16), wf_ref[kb], preferred_element_type=F32)
            parts.append(yb + bf_ref[:, kb * BLK:(kb + 1) * BLK])
        yb = parts[0] if nblk == 1 else jnp.concatenate(parts, axis=1)
        y_ref[0, rows, :] = (yb * _silu(zb_ref[0, rows, :])).astype(y_ref.dtype)

    if l1 == 1:
        fr = (jnp.dot(wpos_ref[:, 0:FFT_L2], xc_ref[0].astype(BF16), preferred_element_type=F32)
              - jnp.dot(wpos_ref[:, FFT_L2:2 * FFT_L2], xs_ref[0].astype(BF16), preferred_element_type=F32))
        epilogue(fr, pl.ds(0, seq))
        return

    b_ref, fr_ref = scratch
    assert w == LANES

    def butterfly(r, carry):
        r0 = pl.multiple_of(r * SUBLANES, SUBLANES)
        zs = [(xc_ref[0, pl.ds(n1 * FFT_L2 + r0, SUBLANES), :], -xs_ref[0, pl.ds(n1 * FFT_L2 + r0, SUBLANES), :])
              for n1 in range(l1)]
        for k1, (a_re, a_im) in enumerate(_fft_list(zs)):
            if k1 == 0:
                b_re, b_im = a_re, a_im
            else:
                tc_ = twc_ref[k1, pl.ds(r0, SUBLANES), :]
                ts_ = tws_ref[k1, pl.ds(r0, SUBLANES), :]
                b_re = a_re * tc_ + a_im * ts_
                b_im = a_im * tc_ - a_re * ts_
            b_ref[k1, pl.ds(r0, SUBLANES), :] = b_re
            b_ref[k1, pl.ds(FFT_L2 + r0, SUBLANES), :] = b_im
        return carry

    lax.fori_loop(0, FFT_L2 // SUBLANES, butterfly, 0)

    def position_dft(k1, carry):
        fr = jnp.dot(wpos_ref[...], b_ref[k1].astype(BF16), preferred_element_type=F32)
        fr_ref[pl.ds(k1, FFT_L2, stride=l1), :] = fr
        return carry

    lax.fori_loop(0, l1, position_dft, 0)

    def finish(i, carry):
        rows = pl.ds(pl.multiple_of(i * tr, tr), tr)
        epilogue(fr_ref[rows, :], rows)
        return carry

    lax.fori_loop(0, seq // tr, finish, 0)


def _fourier(xc, xs, zb, twc, tws, w_pos, w_four, b_four, w_blk, tr):
    b, seq, d = xc.shape
    nblk = w_blk // BLK
    l1 = seq // FFT_L2
    blk = lambda i, j: (i, 0, j)
    scratch = []
    if l1 > 1:
        scratch = [pltpu.VMEM((l1, 2 * FFT_L2, w_blk), F32), pltpu.VMEM((seq, w_blk), F32)]
    vmem = (2 * (3 * seq * w_blk * 4 + seq * w_blk * 2 + 2 * twc.size * 4) + 3 * seq * w_blk * 4
            + 16 * 1024 * 1024)
    return pl.pallas_call(
        functools.partial(_fourier_kernel, seq=seq, tr=tr),
        grid=(b, d // w_blk),
        in_specs=[pl.BlockSpec((1, seq, w_blk), blk),
                  pl.BlockSpec((1, seq, w_blk), blk),
                  pl.BlockSpec((1, seq, w_blk), blk),
                  pl.BlockSpec(twc.shape, lambda i, j: (0, 0, 0)),
                  pl.BlockSpec(tws.shape, lambda i, j: (0, 0, 0)),
                  pl.BlockSpec(w_pos.shape, lambda i, j: (0, 0)),
                  pl.BlockSpec((nblk, BLK, BLK), lambda i, j: (j, 0, 0)),
                  pl.BlockSpec((1, w_blk), lambda i, j: (0, j))],
        out_specs=pl.BlockSpec((1, seq, w_blk), blk),
        out_shape=jax.ShapeDtypeStruct((b, seq, d), BF16),
        scratch_shapes=scratch,
        compiler_params=_cparams(("parallel", "parallel"), vmem),
        name="fourier_mix",
    )(xc, xs, zb, twc, tws, w_pos, w_four, b_four)


def _outproj_kernel(*refs, n_in):
    y_refs = refs[:n_in]
    w_refs = refs[n_in:2 * n_in]
    x_ref, mod_ref, g_ref, o_ref = refs[2 * n_in:]
    d = D_MODEL
    y = jnp.dot(y_refs[0][...], w_refs[0][...], preferred_element_type=F32)
    for k in range(1, n_in):
        y = y + jnp.dot(y_refs[k][...], w_refs[k][...], preferred_element_type=F32)
    yn = y * lax.rsqrt(jnp.mean(y * y, axis=-1, keepdims=True) + EPS) * g_ref[...]
    o_ref[...] = x_ref[...] + mod_ref[0, :, 2 * d:3 * d] * yn


def _outproj(ys, ws, x2d, mod_l, g_post, row_off, rows_per_mod, tm):
    m, d = x2d.shape
    n_in = len(ys)
    row = lambda i: (i, 0)
    const = lambda i: (0, 0)
    vmem = 2 * (sum(wk.size for wk in ws) * 2 + sum(tm * yk.shape[1] for yk in ys) * 2 + 2 * tm * d * 4) + 4 * tm * d * 4
    return pl.pallas_call(
        functools.partial(_outproj_kernel, n_in=n_in),
        grid=(m // tm,),
        in_specs=([pl.BlockSpec((tm, yk.shape[1]), row) for yk in ys]
                  + [pl.BlockSpec(wk.shape, const) for wk in ws]
                  + [pl.BlockSpec((tm, d), row),
                     pl.BlockSpec((1, 1, 3 * d), _mod_index_map(row_off, rows_per_mod // tm)),
                     pl.BlockSpec((1, d), const)]),
        out_specs=pl.BlockSpec((tm, d), row),
        out_shape=jax.ShapeDtypeStruct((m, d), F32),
        compiler_params=_cparams(("parallel",), vmem),
        name="outproj_residual",
    )(*ys, *ws, x2d, mod_l, g_post)


def _inproj_o_kernel(x_ref, mod_ref, g_ref, w_ref, wgt_ref, gb_ref, q_ref, k_ref, v_ref, o_ref, z_ref, gpt_ref):
    d = D_MODEL
    hb = _normed_input(x_ref, mod_ref, g_ref)
    q = jnp.dot(hb, w_ref[:, 0:d], preferred_element_type=F32)
    q_ref[...] = (q * (BLK ** -0.5)).astype(q_ref.dtype)
    k_ref[...] = jnp.dot(hb, w_ref[:, d:2 * d], preferred_element_type=F32).astype(k_ref.dtype)
    for half in range(2):
        cols = slice(half * d, (half + 1) * d)
        v_ref[:, cols] = jnp.dot(hb, w_ref[:, 2 * d + half * d:3 * d + half * d],
                                 preferred_element_type=F32).astype(v_ref.dtype)
        o_ref[:, cols] = jnp.dot(hb, w_ref[:, 4 * d + half * d:5 * d + half * d], preferred_element_type=F32)
        z_ref[:, cols] = jnp.dot(hb, w_ref[:, 6 * d + half * d:7 * d + half * d], preferred_element_type=F32)
    gpt_ref[...] = (lax.dot_general(wgt_ref[...], hb, (((1,), (1,)), ((), ())), preferred_element_type=F32)
                    + gb_ref[:, 0:1])


def _inproj_o(x2d, mod_l, g_pre, w_main, w_gate_t, gate_bias, row_off, rows_per_mod, tm):
    m, d = x2d.shape
    row = lambda i: (i, 0)
    const = lambda i: (0, 0)
    ng = w_gate_t.shape[0]
    vmem = 2 * (w_main.size * 2 + tm * d * 4 + tm * d * 2 * 4 + tm * 2 * d * 4 * 2) + 6 * tm * d * 4
    return pl.pallas_call(
        _inproj_o_kernel,
        grid=(m // tm,),
        in_specs=[pl.BlockSpec((tm, d), row),
                  pl.BlockSpec((1, 1, 3 * d), _mod_index_map(row_off, rows_per_mod // tm)),
                  pl.BlockSpec((1, d), const),
                  pl.BlockSpec(w_main.shape, const),
                  pl.BlockSpec(w_gate_t.shape, const),
                  pl.BlockSpec(gate_bias.shape, const)],
        out_specs=[pl.BlockSpec((tm, d), row), pl.BlockSpec((tm, d), row),
                   pl.BlockSpec((tm, 2 * d), row), pl.BlockSpec((tm, 2 * d), row), pl.BlockSpec((tm, 2 * d), row),
                   pl.BlockSpec((ng, tm), lambda i: (0, i))],
        out_shape=[jax.ShapeDtypeStruct((m, d), BF16), jax.ShapeDtypeStruct((m, d), BF16),
                   jax.ShapeDtypeStruct((m, 2 * d), BF16), jax.ShapeDtypeStruct((m, 2 * d), F32),
                   jax.ShapeDtypeStruct((m, 2 * d), F32), jax.ShapeDtypeStruct((ng, m), F32)],
        compiler_params=_cparams(("parallel",), vmem),
        name="inproj_odd",
    )(x2d, mod_l, g_pre, w_main, w_gate_t, gate_bias)


def _chunk_scan_lanes(v, op, identity, reverse):
    n = v.shape[-1]
    pos = lax.broadcasted_iota(jnp.int32, v.shape, 1) & (CHUNK - 1)
    d = 1
    while d < CHUNK:
        if reverse:
            shifted = jnp.where(pos < CHUNK - d, pltpu.roll(v, n - d, axis=1), identity)
        else:
            shifted = jnp.where(pos >= d, pltpu.roll(v, d, axis=1), identity)
        v = op(v, shifted)
        d *= 2
    return v


def _gate_prep_kernel(g_ref, o_ref):
    for direction in range(2):
        reverse = direction == 1
        i_pre = g_ref[2 * direction]
        b = _chunk_scan_lanes(jax.nn.log_sigmoid(g_ref[2 * direction + 1]), jnp.add, 0.0, reverse)
        a = i_pre - b
        o_ref[3 * direction] = a
        o_ref[3 * direction + 1] = b
        o_ref[3 * direction + 2] = _chunk_scan_lanes(a, jnp.maximum, -jnp.inf, reverse)
    o_ref[6] = jnp.zeros_like(g_ref[0])
    o_ref[7] = jnp.zeros_like(g_ref[0])


def _gate_prep(gq, rb):
    _, r, seq = gq.shape
    return pl.pallas_call(
        _gate_prep_kernel,
        grid=(r // rb,),
        in_specs=[pl.BlockSpec((4, rb, seq), lambda i: (0, i, 0))],
        out_specs=pl.BlockSpec((SUBLANES, rb, seq), lambda i: (0, i, 0)),
        out_shape=jax.ShapeDtypeStruct((SUBLANES, r, seq), F32),
        compiler_params=_cparams(("parallel",), 24 * 1024 * 1024),
        name="mlstm_gate_prep",
    )(gq)


def _mlstm_kernel(q_ref, k_ref, v_ref, o_ref, z_ref, gate_ref, c0_ref, n0_ref, m0_ref,
                  y_ref, c_out_ref, n_out_ref, m_out_ref, hs_ref, cols_ref, *, seq):
    t = CHUNK
    nchunks = seq // t
    half = nchunks // 2
    assert nchunks == 2 * half
    row_i = lax.broadcasted_iota(jnp.int32, (t, t), 0)
    col_i = lax.broadcasted_iota(jnp.int32, (t, t), 1)
    ones_blk = jnp.ones((t, LANES), BF16)

    sel_r = lax.broadcasted_iota(jnp.int32, (4 * SUBLANES, 4 * LANES), 0)
    sel_c = lax.broadcasted_iota(jnp.int32, (4 * SUBLANES, 4 * LANES), 1) // LANES
    wanted = jnp.where(sel_c == 0, 2, jnp.where(sel_c == 1, 1, jnp.where(sel_c == 2, 5, 4)))
    selector = jnp.where(((sel_r & (SUBLANES - 1)) == wanted) & (sel_r < 3 * SUBLANES), 1.0, 0.0).astype(BF16)

    def to_columns(c, carry):
        rows = pl.ds(pl.multiple_of(c * t, t), t)
        g = gate_ref[0, 0, :, rows]
        hi = g.astype(BF16).astype(F32)
        mid = (g - hi).astype(BF16).astype(F32)
        lo = g - hi - mid
        stack = jnp.concatenate([hi, mid, lo, jnp.zeros_like(g)], axis=0).astype(BF16)
        cols_ref[rows, :] = lax.dot_general(stack, selector, (((0,), (0,)), ((), ())), preferred_element_type=F32)
        return carry

    lax.fori_loop(0, nchunks, to_columns, 0, unroll=min(4, nchunks))

    def chunk(c, direction, state):
        c_mat, n_rep, m_rep = state
        reverse = direction == 1
        rows = pl.ds(pl.multiple_of(c * t, t), t)
        qc = q_ref[0, rows, :]
        kc = k_ref[0, rows, :]
        vc = v_ref[0, rows, :]
        g_rows = gate_ref[0, 0, :, rows]
        a_row = g_rows[3 * direction:3 * direction + 1, :]
        last = 0 if reverse else t - 1
        b_end = g_rows[3 * direction + 1:3 * direction + 2, last:last + 1]
        a_end = g_rows[3 * direction + 2:3 * direction + 3, last:last + 1]
        amax_rep = cols_ref[rows, 2 * direction * LANES:(2 * direction + 1) * LANES]
        b_rep = cols_ref[rows, (2 * direction + 1) * LANES:(2 * direction + 2) * LANES]
        m_prev = m_rep[:, 0:1]
        mx_end = jnp.maximum(m_prev, a_end)
        mx = jnp.maximum(m_prev, amax_rep)
        mask = (col_i >= row_i) if reverse else (col_i <= row_i)
        e = jnp.where(mask, jnp.exp(a_row - mx), 0.0)
        qk = lax.dot_general(qc, kc, (((1,), (1,)), ((), ())), preferred_element_type=F32)
        qcn = jnp.dot(qc, jnp.concatenate([c_mat.astype(BF16), n_rep.astype(BF16)], axis=1),
                      preferred_element_type=F32)
        wkk_t = (kc.astype(F32).T * jnp.exp(a_row - mx_end)).astype(BF16)
        upd = jnp.dot(wkk_t, jnp.concatenate([vc, ones_blk], axis=1), preferred_element_type=F32)
        decay = jnp.exp(m_prev - mx_end)
        c_new = decay * c_mat + upd[:, 0:DV]
        n_new = decay * n_rep + upd[:, DV:DV + LANES]
        m_new_rep = jnp.broadcast_to(b_end + mx_end, (1, t))

        def finish_chunk():
            s = qk * e
            w_inter = jnp.exp(m_prev - mx)
            sv = jnp.dot(s.astype(BF16), vc, preferred_element_type=F32)
            den = jnp.sum(s, axis=-1, keepdims=True) + w_inter * qcn[:, DV:DV + LANES]
            inv = 1.0 / jnp.maximum(jnp.abs(den), jnp.exp(-(b_rep + mx)))
            return jnp.concatenate(
                [(sv[:, kb * LANES:(kb + 1) * LANES] + w_inter * qcn[:, kb * LANES:(kb + 1) * LANES]) * inv
                 for kb in range(DV // LANES)], axis=1)

        return rows, finish_chunk, (c_new, n_new, m_new_rep)

    def n_to_columns(n_row):
        return jnp.broadcast_to(n_row, (LANES, BLK)).T

    def n_to_row(n_rep):
        return n_rep.T[0:1, :]

    def finish(rows, hsum):
        y_ref[0, rows, :] = (jax.nn.sigmoid(o_ref[0, rows, :]) * hsum * _silu(z_ref[0, rows, :])).astype(y_ref.dtype)

    def first_half(j, states):
        rows_f, hc_f, st_f = chunk(j, 0, states[0])
        rows_b, hc_b, st_b = chunk(nchunks - 1 - j, 1, states[1])
        hs_ref[rows_f, :] = hc_f()
        hs_ref[rows_b, :] = hc_b()
        return st_f, st_b

    def second_half(j, states):
        rows_f, hc_f, st_f = chunk(j, 0, states[0])
        rows_b, hc_b, st_b = chunk(nchunks - 1 - j, 1, states[1])
        finish(rows_f, hs_ref[rows_f, :] + hc_f())
        finish(rows_b, hs_ref[rows_b, :] + hc_b())
        return st_f, st_b

    init = tuple((c0_ref[0, dr, 0], n_to_columns(n0_ref[0, dr, 0]), m0_ref[0, dr, 0]) for dr in range(2))
    states = lax.fori_loop(0, half, first_half, init)
    states = lax.fori_loop(half, nchunks, second_half, states)
    for dr in range(2):
        c_out_ref[0, dr, 0] = states[dr][0]
        n_out_ref[0, dr, 0] = n_to_row(states[dr][1])
        m_out_ref[0, dr, 0] = states[dr][2]


def _mlstm(q, k, v, o, z, gates, c0, n0, m0):
    b, seq, _ = q.shape
    qk_blk = lambda i, h: (i, 0, h)
    st5 = lambda i, h: (i, 0, h, 0, 0)
    vmem = 2 * (2 * seq * BLK * 2 + seq * DV * 2 * 2 + 2 * seq * DV * 4 + 2 * 2 * BLK * DV * 4) + seq * DV * 4 \
        + seq * 4 * LANES * 4 + 16 * 1024 * 1024
    return pl.pallas_call(
        functools.partial(_mlstm_kernel, seq=seq),
        grid=(b, N_HEADS),
        in_specs=[pl.BlockSpec((1, seq, BLK), qk_blk),
                  pl.BlockSpec((1, seq, BLK), qk_blk),
                  pl.BlockSpec((1, seq, DV), qk_blk),
                  pl.BlockSpec((1, seq, DV), qk_blk),
                  pl.BlockSpec((1, seq, DV), qk_blk),
                  pl.BlockSpec((1, 1, SUBLANES, seq), lambda i, h: (i, h, 0, 0)),
                  pl.BlockSpec((1, 2, 1, BLK, DV), st5),
                  pl.BlockSpec((1, 2, 1, 1, BLK), st5),
                  pl.BlockSpec((1, 2, 1, 1, CHUNK), st5)],
        out_specs=[pl.BlockSpec((1, seq, DV), qk_blk),
                   pl.BlockSpec((1, 2, 1, BLK, DV), st5),
                   pl.BlockSpec((1, 2, 1, 1, BLK), st5),
                   pl.BlockSpec((1, 2, 1, 1, CHUNK), st5)],
        out_shape=[jax.ShapeDtypeStruct((b, seq, N_HEADS * DV), BF16),
                   jax.ShapeDtypeStruct((b, 2, N_HEADS, BLK, DV), F32),
                   jax.ShapeDtypeStruct((b, 2, N_HEADS, 1, BLK), F32),
                   jax.ShapeDtypeStruct((b, 2, N_HEADS, 1, CHUNK), F32)],
        scratch_shapes=[pltpu.VMEM((seq, DV), F32), pltpu.VMEM((seq, 4 * LANES), F32)],
        compiler_params=_cparams(("parallel", "parallel"), vmem),
        name="mlstm_chunkwise",
    )(q, k, v, o, z, gates, c0, n0, m0)


def _dft_tables(seq):
    n = np.arange(BLK)
    ang = 2.0 * np.pi * np.outer(n, n) / BLK
    dft_c = np.concatenate([np.cos(ang), np.sin(ang)], axis=1) / np.sqrt(BLK)
    l1 = seq // FFT_L2
    n2 = np.arange(FFT_L2)
    ang2 = 2.0 * np.pi * np.outer(n2, n2) / FFT_L2
    w_pos = np.concatenate([np.cos(ang2), np.sin(ang2)], axis=1) / np.sqrt(seq)
    angt = 2.0 * np.pi * np.outer(np.arange(l1), n2) / seq
    twc = np.broadcast_to(np.cos(angt)[:, :, None], (l1, FFT_L2, LANES))
    tws = np.broadcast_to(np.sin(angt)[:, :, None], (l1, FFT_L2, LANES))
    return (jnp.asarray(dft_c, F32).astype(BF16), jnp.asarray(w_pos, F32).astype(BF16),
            jnp.asarray(twc, F32), jnp.asarray(tws, F32))


def _run_trunk(x, mod, row_off, per_batch_mod, lru0, c0, n0, m0, p, cfg):
    b, seq, d = x.shape
    m = b * seq
    rows_per_mod = seq if per_batch_mod else m
    x2d = x.reshape(m, d)
    dft_c, w_pos, twc, tws = _dft_tables(seq)

    mod0 = mod[0].reshape(MOD_ROWS, 1, 3 * d)
    xa, za, zb, xc, xs = _inproj_e(x2d, mod0, p["g_pre"][0:1], p["w_in_e"], dft_c, row_off, rows_per_mod,
                                   cfg["tm_in"])
    r3 = lambda a: a.reshape(b, seq, a.shape[-1])
    ya, lru_f = _rglru(r3(xa), r3(za), lru0, p["conv_w"], p["conv_b"], p["wg"], p["bg"], p["lam"], cfg["tc_lru"])
    yb = _fourier(r3(xc), r3(xs), r3(zb), twc, tws, w_pos, p["w_four"], p["b_four"], cfg["w_four"], cfg["tr_four"])
    x1 = _outproj([ya.reshape(m, d), yb.reshape(m, d)], [p["w_out_e"][0:d], p["w_out_e"][d:2 * d]],
                  x2d, mod0, p["g_post"][0:1], row_off, rows_per_mod, cfg["tm_out"])

    mod1 = mod[1].reshape(MOD_ROWS, 1, 3 * d)
    q, k, v, o, z, gpt = _inproj_o(x1, mod1, p["g_pre"][1:2], p["w_in_o"], p["w_gate_t"], p["gate_bias"],
                                   row_off, rows_per_mod, cfg["tm_in_o"])
    gq = gpt[0:4 * N_HEADS].reshape(4, N_HEADS, b, seq).transpose(0, 2, 1, 3).reshape(4, b * N_HEADS, seq)
    gates = _gate_prep(gq, cfg["rb_gate"]).reshape(SUBLANES, b, N_HEADS, seq).transpose(1, 2, 0, 3)
    y, c_f, n_f, m_f = _mlstm(r3(q), r3(k), r3(v), r3(o), r3(z), gates, c0, n0, m0)
    x2 = _outproj([y.reshape(m, 2 * d)], [p["w_out_o"]], x1, mod1, p["g_post"][1:2], row_off, rows_per_mod,
                  cfg["tm_out"])
    return x2.reshape(b, seq, d), lru_f, c_f, n_f, m_f


def _prepare_params(conv_w, conv_b, w_rg, b_rg, w_ig, b_ig, lru_lambda, w_four, b_four, w_in_e, w_out_e,
                    w_in_o, b_if, w_out_o, g_pre, g_post):
    d = D_MODEL
    wg = jnp.concatenate([w_rg[0], w_ig[0]], axis=-1).astype(BF16)
    bg = jnp.concatenate([b_rg[0].reshape(2, N_HEADS, 1, BLK), b_ig[0].reshape(2, N_HEADS, 1, BLK)], axis=-1)
    n_gate = 4 * N_HEADS
    w_gate_t = jnp.zeros((LANES, d), F32).at[0:n_gate].set(w_in_o[0][:, 8 * d:8 * d + n_gate].T).astype(BF16)
    bias = jnp.zeros((LANES,), F32).at[0:n_gate].set(b_if[0].reshape(n_gate))
    gate_bias = jnp.broadcast_to(bias[:, None], (LANES, LANES))
    return dict(
        g_pre=g_pre, g_post=g_post,
        w_in_e=w_in_e[0].astype(BF16), w_out_e=w_out_e[0].astype(BF16),
        conv_w=conv_w[0], conv_b=conv_b[0].reshape(1, d), wg=wg, bg=bg, lam=lru_lambda[0].reshape(2, 1, d),
        w_four=w_four[0].astype(BF16), b_four=b_four[0].reshape(1, d),
        w_in_o=w_in_o[0][:, 0:8 * d].astype(BF16), w_gate_t=w_gate_t, gate_bias=gate_bias,
        w_out_o=w_out_o[0].astype(BF16))


def kernel(x_prompt, x_sample, c, state_lru, state_mlstm_C, state_mlstm_n, state_mlstm_m, c_ctx, w_mod, b_mod,
           g_pre, g_post, w_in_e, conv_w, conv_b, w_rg, b_rg, w_ig, b_ig, lru_lambda, w_four, b_four, w_out_e,
           w_in_o, b_if, w_out_o):
    d = D_MODEL
    bp = x_prompt.shape[0]
    bs = x_sample.shape[0]
    p = _prepare_params(conv_w, conv_b, w_rg, b_rg, w_ig, b_ig, lru_lambda, w_four, b_four, w_in_e, w_out_e,
                        w_in_o, b_if, w_out_o, g_pre, g_post)
    cond = jnp.concatenate([c_ctx[None, :], c, jnp.zeros((MOD_ROWS - 1 - bs, d), F32)], axis=0)
    mod = _modulation(cond, w_mod, b_mod)

    def states(lru, cm, nv, mv):
        bsz = lru.shape[0]
        return (lru[:, 0], cm[:, 0], nv[:, 0].reshape(bsz, 2, N_HEADS, 1, BLK),
                jnp.broadcast_to(mv[:, 0].reshape(bsz, 2, N_HEADS, 1, 1), (bsz, 2, N_HEADS, 1, CHUNK)))

    zero_states = states(jnp.zeros((bp, 1, 2, d), F32), jnp.zeros((bp, 1, 2, N_HEADS, BLK, DV), F32),
                         jnp.zeros((bp, 1, 2, N_HEADS, BLK), F32), jnp.zeros((bp, 1, 2, N_HEADS), F32))
    cfg_p = dict(tm_in=512, tm_in_o=256, tm_out=512, w_lru=d, tc_lru=128, w_four=d, tr_four=256,
                 rb_gate=min(bp * N_HEADS, 128))
    y_prompt, lru_f, c_f, n_f, m_f = _run_trunk(x_prompt, mod, 0, False, *zero_states, p, cfg_p)

    cfg_s = dict(tm_in=512, tm_in_o=256, tm_out=512, w_lru=2 * BLK, tc_lru=128, w_four=BLK, tr_four=256,
                 rb_gate=SUBLANES)
    y_sample, _, _, _, _ = _run_trunk(x_sample, mod, 1, True,
                                      *states(state_lru, state_mlstm_C, state_mlstm_n, state_mlstm_m), p, cfg_s)

    return (y_prompt, y_sample, lru_f[:, None], c_f[:, None], n_f[:, None, :, :, 0, :], m_f[:, None, :, :, 0, 0])
```

```python
import functools
import math

import numpy as np
import jax
import jax.numpy as jnp
from jax import lax
from jax.experimental import pallas as pl
from jax.experimental.pallas import tpu as pltpu

F32 = jnp.float32
BF16 = jnp.bfloat16

D_MODEL = 1024
DEPTH = 2
EPS = 1e-6
LRU_C = 8.0
CONV_W = 4
N_HEADS = 8
BLK = D_MODEL // N_HEADS
DV = 2 * BLK
CHUNK = 128
FFT_L2 = 256
MOD_ROWS = 8

LANES = 128
SUBLANES = 8
VMEM_LIMIT_CAP = 56 * 1024 * 1024


def _cparams(sem, vmem_bytes, flags=None):
    return pltpu.CompilerParams(dimension_semantics=sem, flags=flags,
                                vmem_limit_bytes=int(min(max(vmem_bytes, 16 * 1024 * 1024), VMEM_LIMIT_CAP)))


def _silu(x):
    return x * jax.nn.sigmoid(x)


def _mod_kernel(cond_ref, w_ref, b_ref, o_ref):
    s = _silu(cond_ref[...]).astype(BF16)
    o_ref[0] = jnp.dot(s, w_ref[0].astype(BF16), preferred_element_type=F32) + b_ref[0]


def _modulation(cond, w_mod, b_mod):
    d = D_MODEL
    return pl.pallas_call(
        _mod_kernel,
        grid=(DEPTH, 3),
        in_specs=[pl.BlockSpec((MOD_ROWS, d), lambda l, j: (0, 0)),
                  pl.BlockSpec((1, d, d), lambda l, j: (l, 0, j)),
                  pl.BlockSpec((1, 1, d), lambda l, j: (l, 0, j))],
        out_specs=pl.BlockSpec((1, MOD_ROWS, d), lambda l, j: (l, 0, j)),
        out_shape=jax.ShapeDtypeStruct((DEPTH, MOD_ROWS, 3 * d), F32),
        compiler_params=_cparams(("arbitrary", "arbitrary"), 24 * 1024 * 1024),
        name="adaln_mod",
    )(cond, w_mod, b_mod.reshape(DEPTH, 1, 3 * d))


def _normed_input(x_ref, mod_ref, g_ref):
    d = D_MODEL
    x = x_ref[...]
    y = x * lax.rsqrt(jnp.mean(x * x, axis=-1, keepdims=True) + EPS) * g_ref[...]
    shift = mod_ref[0, :, 0:d]
    scale = mod_ref[0, :, d:2 * d]
    return (y * (1.0 + scale) + shift).astype(BF16)


def _mod_index_map(row_off, tiles_per_mod):
    return lambda i: (row_off + i // tiles_per_mod, 0, 0)


def _inproj_e_kernel(x_ref, mod_ref, g_ref, w_ref, dft_ref, xa_ref, za_ref, zb_ref, xc_ref, xs_ref):
    d = D_MODEL
    hb = _normed_input(x_ref, mod_ref, g_ref)
    xa_ref[...] = jnp.dot(hb, w_ref[:, 0:d], preferred_element_type=F32)
    za_ref[...] = jnp.dot(hb, w_ref[:, d:2 * d], preferred_element_type=F32)
    zb_ref[...] = jnp.dot(hb, w_ref[:, 3 * d:4 * d], preferred_element_type=F32)
    xb = jnp.dot(hb, w_ref[:, 2 * d:3 * d], preferred_element_type=F32).astype(BF16)
    for g in range(N_HEADS):
        cs = jnp.dot(xb[:, g * BLK:(g + 1) * BLK], dft_ref[...], preferred_element_type=F32)
        xc_ref[:, g * BLK:(g + 1) * BLK] = cs[:, 0:BLK]
        xs_ref[:, g * BLK:(g + 1) * BLK] = cs[:, BLK:2 * BLK]


def _inproj_e(x2d, mod_l, g_pre, w_in, dft_c, row_off, rows_per_mod, tm):
    m, d = x2d.shape
    row = lambda i: (i, 0)
    const = lambda i: (0, 0)
    out = jax.ShapeDtypeStruct((m, d), F32)
    vmem = 2 * (w_in.size * 2 + tm * d * 4 * 6) + 8 * tm * d * 4
    return pl.pallas_call(
        _inproj_e_kernel,
        grid=(m // tm,),
        in_specs=[pl.BlockSpec((tm, d), row),
                  pl.BlockSpec((1, 1, 3 * d), _mod_index_map(row_off, rows_per_mod // tm)),
                  pl.BlockSpec((1, d), const),
                  pl.BlockSpec(w_in.shape, const),
                  pl.BlockSpec(dft_c.shape, const)],
        out_specs=[pl.BlockSpec((tm, d), row)] * 5,
        out_shape=[out] * 5,
        compiler_params=_cparams(("parallel",), vmem),
        name="inproj_even",
    )(x2d, mod_l, g_pre, w_in, dft_c)


def _scan_chunk(a, u, reverse):
    t = a.shape[0]
    row = lax.broadcasted_iota(jnp.int32, a.shape, 0)
    d = 1
    while d < t:
        if d < SUBLANES:
            shift = (t - d) if reverse else d
            a_sh = pltpu.roll(a, shift, axis=0)
            u_sh = pltpu.roll(u, shift, axis=0)
            valid = (row < t - d) if reverse else (row >= d)
            a_sh = jnp.where(valid, a_sh, 1.0)
            u_sh = jnp.where(valid, u_sh, 0.0)
        else:
            ones = jnp.ones((d, a.shape[1]), F32)
            zeros = jnp.zeros((d, a.shape[1]), F32)
            if reverse:
                a_sh = jnp.concatenate([a[d:], ones], axis=0)
                u_sh = jnp.concatenate([u[d:], zeros], axis=0)
            else:
                a_sh = jnp.concatenate([ones, a[:t - d]], axis=0)
                u_sh = jnp.concatenate([zeros, u[:t - d]], axis=0)
        u = a * u_sh + u
        a = a * a_sh
        d *= 2
    return a, u


def _rglru_kernel(xa_ref, za_ref, cw_ref, cb_ref, wg_ref, bg_ref, lam_ref, h0_ref,
                  y_ref, hT_ref, hf_ref, hb_ref, *, seq, tc):
    nchunks = seq // tc
    groups = tc // SUBLANES
    sub = lax.broadcasted_iota(jnp.int32, (SUBLANES, LANES), 0)

    def strided(ref, t0, g):
        return ref[0, pl.ds(t0 + g, SUBLANES, stride=groups), :]

    def row_bcast(block, r):
        return jnp.broadcast_to(block[r:r + 1, :], (SUBLANES, LANES))

    def conv_chunk(c):
        t0 = pl.multiple_of(c * tc, tc)
        xs = [strided(xa_ref, t0, g) for g in range(groups)]
        prev = xa_ref[0, pl.ds(pl.multiple_of(jnp.maximum(t0 - SUBLANES, 0), SUBLANES), SUBLANES), :]
        nxt = xa_ref[0, pl.ds(pl.multiple_of(jnp.minimum(t0 + tc, seq - SUBLANES), SUBLANES), SUBLANES), :]
        prev = jnp.where(c > 0, prev, 0.0)
        nxt = jnp.where(c < nchunks - 1, nxt, 0.0)
        before2 = jnp.where(sub == 0, row_bcast(prev, SUBLANES - 2), pltpu.roll(xs[groups - 2], 1, axis=0))
        before1 = jnp.where(sub == 0, row_bcast(prev, SUBLANES - 1), pltpu.roll(xs[groups - 1], 1, axis=0))
        after1 = jnp.where(sub == SUBLANES - 1, row_bcast(nxt, 0), pltpu.roll(xs[0], SUBLANES - 1, axis=0))
        ext = [before2, before1] + xs + [after1]
        out = []
        for g in range(groups):
            acc = cb_ref[...] + cw_ref[0:1, :] * ext[g]
            for j in range(1, CONV_W):
                acc = acc + cw_ref[j:j + 1, :] * ext[g + j]
            out.append(acc)
        return t0, jnp.concatenate(out, axis=0)

    def gates(xc, direction):
        g = jnp.dot(xc.astype(BF16), wg_ref[direction, 0], preferred_element_type=F32) + bg_ref[direction, 0]
        r = jax.nn.sigmoid(g[:, 0:BLK])
        i = jax.nn.sigmoid(g[:, BLK:2 * BLK])
        lam = lam_ref[direction]
        softplus_neg = jnp.maximum(-lam, 0.0) + jnp.log1p(jnp.exp(-jnp.abs(lam)))
        neg_log_a = LRU_C * r * softplus_neg
        a = jnp.exp(-neg_log_a)
        om = jnp.tanh(neg_log_a) * (1.0 + a * a)
        return a, jnp.sqrt(om) * (i * xc)

    def scan_dir(c, direction, carry, out_ref):
        reverse = direction == 1
        t0, xc = conv_chunk(c)
        a, u = gates(xc, direction)
        order = range(groups - 1, -1, -1) if reverse else range(groups)
        p_g, s_g = [None] * groups, [None] * groups
        p_run = s_run = None
        for g in order:
            a_v = a[g * SUBLANES:(g + 1) * SUBLANES]
            u_v = u[g * SUBLANES:(g + 1) * SUBLANES]
            if p_run is None:
                p_run, s_run = a_v, u_v
            else:
                s_run = a_v * s_run + u_v
                p_run = a_v * p_run
            p_g[g], s_g[g] = p_run, s_run
        pp, ss = _scan_chunk(p_run, s_run, reverse)
        carry_b = jnp.broadcast_to(carry, (SUBLANES, LANES))
        seg_out = pp * carry_b + ss
        if reverse:
            h_in = jnp.where(sub == SUBLANES - 1, carry_b, pltpu.roll(seg_out, SUBLANES - 1, axis=0))
            new_carry = seg_out[0:1, :]
        else:
            h_in = jnp.where(sub == 0, carry_b, pltpu.roll(seg_out, 1, axis=0))
            new_carry = seg_out[SUBLANES - 1:SUBLANES, :]
        for g in range(groups):
            out_ref[pl.ds(t0 + g, SUBLANES, stride=groups), :] = p_g[g] * h_in + s_g[g]
        return new_carry

    def body(j, carries):
        return (scan_dir(j, 0, carries[0], hf_ref), scan_dir(nchunks - 1 - j, 1, carries[1], hb_ref))

    h_last, h_first = lax.fori_loop(0, nchunks, body, (h0_ref[0, 0:1, :], h0_ref[0, 1:2, :]), unroll=2)
    hT_ref[0, 0:1, :] = h_last
    hT_ref[0, 1:2, :] = h_first

    def gate_out(c, carry):
        rows = pl.ds(pl.multiple_of(c * tc, tc), tc)
        y_ref[0, rows, :] = ((hf_ref[rows, :] + hb_ref[rows, :]) * _silu(za_ref[0, rows, :])).astype(y_ref.dtype)
        return carry

    lax.fori_loop(0, nchunks, gate_out, 0)


def _rglru(xa, za, h0, conv_w, conv_b, wg, bg, lam, tc):
    b, seq, d = xa.shape
    blk = lambda i, j: (i, 0, j)
    vmem = 2 * (2 * seq * BLK * 4 + seq * BLK * 2) + 2 * seq * BLK * 4 + 16 * 1024 * 1024
    return pl.pallas_call(
        functools.partial(_rglru_kernel, seq=seq, tc=tc),
        grid=(b, d // BLK),
        in_specs=[pl.BlockSpec((1, seq, BLK), blk),
                  pl.BlockSpec((1, seq, BLK), blk),
                  pl.BlockSpec((CONV_W, BLK), lambda i, j: (0, j)),
                  pl.BlockSpec((1, BLK), lambda i, j: (0, j)),
                  pl.BlockSpec((2, 1, BLK, 2 * BLK), lambda i, j: (0, j, 0, 0)),
                  pl.BlockSpec((2, 1, 1, 2 * BLK), lambda i, j: (0, j, 0, 0)),
                  pl.BlockSpec((2, 1, BLK), lambda i, j: (0, 0, j)),
                  pl.BlockSpec((1, 2, BLK), blk)],
        out_specs=[pl.BlockSpec((1, seq, BLK), blk),
                   pl.BlockSpec((1, 2, BLK), blk)],
        out_shape=[jax.ShapeDtypeStruct((b, seq, d), BF16),
                   jax.ShapeDtypeStruct((b, 2, d), F32)],
        scratch_shapes=[pltpu.VMEM((seq, BLK), F32), pltpu.VMEM((seq, BLK), F32)],
        compiler_params=_cparams(("parallel", "parallel"), vmem),
        name="rglru_scan",
    )(xa, za, conv_w, conv_b, wg, bg, lam, h0)


def _fft_list(xs):
    n = len(xs)
    if n == 1:
        return xs
    even = _fft_list(xs[0::2])
    odd = _fft_list(xs[1::2])
    out = [None] * n
    for k in range(n // 2):
        o_re, o_im = odd[k]
        if k == 0:
            t_re, t_im = o_re, o_im
        elif 4 * k == n:
            t_re, t_im = o_im, -o_re
        else:
            ang = -2.0 * math.pi * k / n
            wr, wi = math.cos(ang), math.sin(ang)
            t_re = o_re * wr - o_im * wi
            t_im = o_re * wi + o_im * wr
        e_re, e_im = even[k]
        out[k] = (e_re + t_re, e_im + t_im)
        out[k + n // 2] = (e_re - t_re, e_im - t_im)
    return out


def _fourier_kernel(xc_ref, xs_ref, zb_ref, twc_ref, tws_ref, wpos_ref, wf_ref, bf_ref, y_ref, *scratch,
                    seq, tr):
    w = xc_ref.shape[2]
    nblk = w // BLK
    l1 = seq // FFT_L2

    def epilogue(fr, rows):
        parts = []
        for kb in range(nblk):
            yb = jnp.dot(fr[:, kb * BLK:(kb + 1) * BLK].astype(BF16), wf_ref[kb], preferred_element_type=F32)
            parts.append(yb + bf_ref[:, kb * BLK:(kb + 1) * BLK])
        yb = parts[0] if nblk == 1 else jnp.concatenate(parts, axis=1)
        y_ref[0, rows, :] = (yb * _silu(zb_ref[0, rows, :])).astype(y_ref.dtype)

    if l1 == 1:
        fr = (jnp.dot(wpos_ref[:, 0:FFT_L2], xc_ref[0].astype(BF16), preferred_element_type=F32)
              - jnp.dot(wpos_ref[:, FFT_L2:2 * FFT_L2], xs_ref[0].astype(BF16), preferred_element_type=F32))
        epilogue(fr, pl.ds(0, seq))
        return

    b_ref, fr_ref = scratch
    assert w == LANES

    def butterfly(r, carry):
        r0 = pl.multiple_of(r * SUBLANES, SUBLANES)
        zs = [(xc_ref[0, pl.ds(n1 * FFT_L2 + r0, SUBLANES), :], -xs_ref[0, pl.ds(n1 * FFT_L2 + r0, SUBLANES), :])
              for n1 in range(l1)]
        for k1, (a_re, a_im) in enumerate(_fft_list(zs)):
            if k1 == 0:
                b_re, b_im = a_re, a_im
            else:
                tc_ = twc_ref[k1, pl.ds(r0, SUBLANES), :]
                ts_ = tws_ref[k1, pl.ds(r0, SUBLANES), :]
                b_re = a_re * tc_ + a_im * ts_
                b_im = a_im * tc_ - a_re * ts_
            b_ref[pl.ds(r0, SUBLANES), k1 * LANES:(k1 + 1) * LANES] = b_re
            b_ref[pl.ds(FFT_L2 + r0, SUBLANES), k1 * LANES:(k1 + 1) * LANES] = b_im
        return carry

    lax.fori_loop(0, FFT_L2 // SUBLANES, butterfly, 0)

    per_dot = min(4, l1)
    for nb in range(l1 // per_dot):
        cols = slice(nb * per_dot * LANES, (nb + 1) * per_dot * LANES)
        fr = jnp.dot(wpos_ref[...], b_ref[:, cols].astype(BF16), preferred_element_type=F32)
        for kk in range(per_dot):
            fr_ref[pl.ds(nb * per_dot + kk, FFT_L2, stride=l1), :] = fr[:, kk * LANES:(kk + 1) * LANES]

    def finish(i, carry):
        rows = pl.ds(pl.multiple_of(i * tr, tr), tr)
        epilogue(fr_ref[rows, :], rows)
        return carry

    lax.fori_loop(0, seq // tr, finish, 0, unroll=2)


def _fourier(xc, xs, zb, twc, tws, w_pos, w_four, b_four, w_blk, tr):
    b, seq, d = xc.shape
    nblk = w_blk // BLK
    l1 = seq // FFT_L2
    blk = lambda i, j: (i, 0, j)
    scratch = []
    if l1 > 1:
        scratch = [pltpu.VMEM((2 * FFT_L2, l1 * w_blk), F32), pltpu.VMEM((seq, w_blk), F32)]
    vmem = (2 * (3 * seq * w_blk * 4 + seq * w_blk * 2 + 2 * twc.size * 4) + 3 * seq * w_blk * 4
            + 16 * 1024 * 1024)
    return pl.pallas_call(
        functools.partial(_fourier_kernel, seq=seq, tr=tr),
        grid=(b, d // w_blk),
        in_specs=[pl.BlockSpec((1, seq, w_blk), blk),
                  pl.BlockSpec((1, seq, w_blk), blk),
                  pl.BlockSpec((1, seq, w_blk), blk),
                  pl.BlockSpec(twc.shape, lambda i, j: (0, 0, 0)),
                  pl.BlockSpec(tws.shape, lambda i, j: (0, 0, 0)),
                  pl.BlockSpec(w_pos.shape, lambda i, j: (0, 0)),
                  pl.BlockSpec((nblk, BLK, BLK), lambda i, j: (j, 0, 0)),
                  pl.BlockSpec((1, w_blk), lambda i, j: (0, j))],
        out_specs=pl.BlockSpec((1, seq, w_blk), blk),
        out_shape=jax.ShapeDtypeStruct((b, seq, d), BF16),
        scratch_shapes=scratch,
        compiler_params=_cparams(("parallel", "parallel"), vmem),
        name="fourier_mix",
    )(xc, xs, zb, twc, tws, w_pos, w_four, b_four)


def _outproj_kernel(*refs, n_in):
    y_refs = refs[:n_in]
    w_refs = refs[n_in:2 * n_in]
    x_ref, mod_ref, g_ref, o_ref = refs[2 * n_in:]
    d = D_MODEL
    y = jnp.dot(y_refs[0][...], w_refs[0][...], preferred_element_type=F32)
    for k in range(1, n_in):
        y = y + jnp.dot(y_refs[k][...], w_refs[k][...], preferred_element_type=F32)
    yn = y * lax.rsqrt(jnp.mean(y * y, axis=-1, keepdims=True) + EPS) * g_ref[...]
    o_ref[...] = x_ref[...] + mod_ref[0, :, 2 * d:3 * d] * yn


def _outproj(ys, ws, x2d, mod_l, g_post, row_off, rows_per_mod, tm):
    m, d = x2d.shape
    n_in = len(ys)
    row = lambda i: (i, 0)
    const = lambda i: (0, 0)
    vmem = 2 * (sum(wk.size for wk in ws) * 2 + sum(tm * yk.shape[1] for yk in ys) * 2 + 2 * tm * d * 4) + 4 * tm * d * 4
    return pl.pallas_call(
        functools.partial(_outproj_kernel, n_in=n_in),
        grid=(m // tm,),
        in_specs=([pl.BlockSpec((tm, yk.shape[1]), row) for yk in ys]
                  + [pl.BlockSpec(wk.shape, const) for wk in ws]
                  + [pl.BlockSpec((tm, d), row),
                     pl.BlockSpec((1, 1, 3 * d), _mod_index_map(row_off, rows_per_mod // tm)),
                     pl.BlockSpec((1, d), const)]),
        out_specs=pl.BlockSpec((tm, d), row),
        out_shape=jax.ShapeDtypeStruct((m, d), F32),
        compiler_params=_cparams(("parallel",), vmem),
        name="outproj_residual",
    )(*ys, *ws, x2d, mod_l, g_post)


def _inproj_o_kernel(x_ref, mod_ref, g_ref, w_ref, wgt_ref, gb_ref, q_ref, k_ref, v_ref, o_ref, z_ref, gpt_ref):
    d = D_MODEL
    hb = _normed_input(x_ref, mod_ref, g_ref)
    q = jnp.dot(hb, w_ref[:, 0:d], preferred_element_type=F32)
    q_ref[...] = (q * (BLK ** -0.5)).astype(q_ref.dtype)
    k_ref[...] = jnp.dot(hb, w_ref[:, d:2 * d], preferred_element_type=F32).astype(k_ref.dtype)
    for half in range(2):
        cols = slice(half * d, (half + 1) * d)
        v_ref[:, cols] = jnp.dot(hb, w_ref[:, 2 * d + half * d:3 * d + half * d],
                                 preferred_element_type=F32).astype(v_ref.dtype)
        o_ref[:, cols] = jnp.dot(hb, w_ref[:, 4 * d + half * d:5 * d + half * d], preferred_element_type=F32)
        z_ref[:, cols] = jnp.dot(hb, w_ref[:, 6 * d + half * d:7 * d + half * d], preferred_element_type=F32)
    gpt_ref[...] = (lax.dot_general(wgt_ref[...], hb, (((1,), (1,)), ((), ())), preferred_element_type=F32)
                    + gb_ref[:, 0:1])


def _inproj_o(x2d, mod_l, g_pre, w_main, w_gate_t, gate_bias, row_off, rows_per_mod, tm):
    m, d = x2d.shape
    row = lambda i: (i, 0)
    const = lambda i: (0, 0)
    ng = w_gate_t.shape[0]
    vmem = 2 * (w_main.size * 2 + tm * d * 4 + tm * d * 2 * 4 + tm * 2 * d * 4 * 2) + 6 * tm * d * 4
    return pl.pallas_call(
        _inproj_o_kernel,
        grid=(m // tm,),
        in_specs=[pl.BlockSpec((tm, d), row),
                  pl.BlockSpec((1, 1, 3 * d), _mod_index_map(row_off, rows_per_mod // tm)),
                  pl.BlockSpec((1, d), const),
                  pl.BlockSpec(w_main.shape, const),
                  pl.BlockSpec(w_gate_t.shape, const),
                  pl.BlockSpec(gate_bias.shape, const)],
        out_specs=[pl.BlockSpec((tm, d), row), pl.BlockSpec((tm, d), row),
                   pl.BlockSpec((tm, 2 * d), row), pl.BlockSpec((tm, 2 * d), row), pl.BlockSpec((tm, 2 * d), row),
                   pl.BlockSpec((ng, tm), lambda i: (0, i))],
        out_shape=[jax.ShapeDtypeStruct((m, d), BF16), jax.ShapeDtypeStruct((m, d), BF16),
                   jax.ShapeDtypeStruct((m, 2 * d), BF16), jax.ShapeDtypeStruct((m, 2 * d), F32),
                   jax.ShapeDtypeStruct((m, 2 * d), F32), jax.ShapeDtypeStruct((ng, m), F32)],
        compiler_params=_cparams(("parallel",), vmem),
        name="inproj_odd",
    )(x2d, mod_l, g_pre, w_main, w_gate_t, gate_bias)


def _chunk_scan_lanes(v, op, identity, reverse):
    n = v.shape[-1]
    pos = lax.broadcasted_iota(jnp.int32, v.shape, 1) & (CHUNK - 1)
    d = 1
    while d < CHUNK:
        if reverse:
            shifted = jnp.where(pos < CHUNK - d, pltpu.roll(v, n - d, axis=1), identity)
        else:
            shifted = jnp.where(pos >= d, pltpu.roll(v, d, axis=1), identity)
        v = op(v, shifted)
        d *= 2
    return v


def _gate_prep_kernel(g_ref, o_ref):
    for direction in range(2):
        reverse = direction == 1
        i_pre = g_ref[2 * direction]
        b = _chunk_scan_lanes(jax.nn.log_sigmoid(g_ref[2 * direction + 1]), jnp.add, 0.0, reverse)
        a = i_pre - b
        o_ref[3 * direction] = a
        o_ref[3 * direction + 1] = b
        o_ref[3 * direction + 2] = _chunk_scan_lanes(a, jnp.maximum, -jnp.inf, reverse)
    o_ref[6] = jnp.zeros_like(g_ref[0])
    o_ref[7] = jnp.zeros_like(g_ref[0])


def _gate_prep(gq, rb):
    _, r, seq = gq.shape
    return pl.pallas_call(
        _gate_prep_kernel,
        grid=(r // rb,),
        in_specs=[pl.BlockSpec((4, rb, seq), lambda i: (0, i, 0))],
        out_specs=pl.BlockSpec((SUBLANES, rb, seq), lambda i: (0, i, 0)),
        out_shape=jax.ShapeDtypeStruct((SUBLANES, r, seq), F32),
        compiler_params=_cparams(("parallel",), 24 * 1024 * 1024),
        name="mlstm_gate_prep",
    )(gq)


def _mlstm_kernel(q_ref, k_ref, v_ref, o_ref, z_ref, gate_ref, c0_ref, n0_ref, m0_ref,
                  y_ref, c_out_ref, n_out_ref, m_out_ref, hs_ref, cols_ref, *, seq):
    t = CHUNK
    nchunks = seq // t
    half = nchunks // 2
    assert nchunks == 2 * half
    row_i = lax.broadcasted_iota(jnp.int32, (t, t), 0)
    col_i = lax.broadcasted_iota(jnp.int32, (t, t), 1)
    ones_blk = jnp.ones((t, LANES), BF16)

    sel_r = lax.broadcasted_iota(jnp.int32, (4 * SUBLANES, 4 * LANES), 0)
    sel_c = lax.broadcasted_iota(jnp.int32, (4 * SUBLANES, 4 * LANES), 1) // LANES
    wanted = jnp.where(sel_c == 0, 2, jnp.where(sel_c == 1, 1, jnp.where(sel_c == 2, 5, 4)))
    selector = jnp.where(((sel_r & (SUBLANES - 1)) == wanted) & (sel_r < 3 * SUBLANES), 1.0, 0.0).astype(BF16)

    def to_columns(c, carry):
        rows = pl.ds(pl.multiple_of(c * t, t), t)
        g = gate_ref[0, 0, :, rows]
        hi = g.astype(BF16).astype(F32)
        mid = (g - hi).astype(BF16).astype(F32)
        lo = g - hi - mid
        stack = jnp.concatenate([hi, mid, lo, jnp.zeros_like(g)], axis=0).astype(BF16)
        cols_ref[rows, :] = lax.dot_general(stack, selector, (((0,), (0,)), ((), ())), preferred_element_type=F32)
        return carry

    lax.fori_loop(0, nchunks, to_columns, 0, unroll=min(4, nchunks))

    def chunk_of(step, direction):
        return (nchunks - 1 - step) if direction == 1 else step

    def gate_terms(rows, direction):
        g_rows = gate_ref[0, 0, :, rows]
        last = 0 if direction == 1 else t - 1
        a_row = g_rows[3 * direction:3 * direction + 1, :]
        b_end = g_rows[3 * direction + 1:3 * direction + 2, last:last + 1]
        a_end = g_rows[3 * direction + 2:3 * direction + 3, last:last + 1]
        amax_rep = cols_ref[rows, 2 * direction * LANES:(2 * direction + 1) * LANES]
        return a_row, b_end, a_end, amax_rep

    def scores(step, direction, m_rep):
        rows = pl.ds(pl.multiple_of(chunk_of(step, direction) * t, t), t)
        a_row, b_end, a_end, amax_rep = gate_terms(rows, direction)
        m_prev = m_rep[:, 0:1]
        mx = jnp.maximum(m_prev, amax_rep)
        mask = (col_i >= row_i) if direction == 1 else (col_i <= row_i)
        e = jnp.where(mask, jnp.exp(a_row - mx), 0.0)
        s = lax.dot_general(q_ref[0, rows, :], k_ref[0, rows, :], (((1,), (1,)), ((), ())),
                            preferred_element_type=F32) * e
        m_next = jnp.broadcast_to(b_end + jnp.maximum(m_prev, a_end), (1, t))
        return (s.astype(BF16), jnp.sum(s, axis=-1, keepdims=True), m_rep), m_next

    def outputs(step, direction, memory, pending):
        c_mat, n_rep = memory
        s_b, s_sum, m_rep = pending
        rows = pl.ds(pl.multiple_of(chunk_of(step, direction) * t, t), t)
        qc = q_ref[0, rows, :]
        kc = k_ref[0, rows, :]
        vc = v_ref[0, rows, :]
        a_row, _, a_end, amax_rep = gate_terms(rows, direction)
        b_rep = cols_ref[rows, (2 * direction + 1) * LANES:(2 * direction + 2) * LANES]
        m_prev = m_rep[:, 0:1]
        mx_end = jnp.maximum(m_prev, a_end)
        mx = jnp.maximum(m_prev, amax_rep)
        sv = jnp.dot(s_b, vc, preferred_element_type=F32)
        qcn = jnp.dot(qc, jnp.concatenate([c_mat.astype(BF16), n_rep.astype(BF16)], axis=1),
                      preferred_element_type=F32)
        wkk_t = (kc.astype(F32).T * jnp.exp(a_row - mx_end)).astype(BF16)
        upd = jnp.dot(wkk_t, jnp.concatenate([vc, ones_blk], axis=1), preferred_element_type=F32)
        decay = jnp.exp(m_prev - mx_end)
        w_inter = jnp.exp(m_prev - mx)
        den = s_sum + w_inter * qcn[:, DV:DV + LANES]
        inv = 1.0 / jnp.maximum(jnp.abs(den), jnp.exp(-(b_rep + mx)))
        hc = jnp.concatenate(
            [(sv[:, kb * LANES:(kb + 1) * LANES] + w_inter * qcn[:, kb * LANES:(kb + 1) * LANES]) * inv
             for kb in range(DV // LANES)], axis=1)
        return rows, hc, (decay * c_mat + upd[:, 0:DV], decay * n_rep + upd[:, DV:DV + LANES])

    def n_to_columns(n_row):
        return jnp.broadcast_to(n_row, (LANES, BLK)).T

    def n_to_row(n_rep):
        return n_rep.T[0:1, :]

    def emit(rows, hc, first_visit):
        if first_visit:
            hs_ref[rows, :] = hc
        else:
            hsum = hs_ref[rows, :] + hc
            y_ref[0, rows, :] = (jax.nn.sigmoid(o_ref[0, rows, :]) * hsum
                                 * _silu(z_ref[0, rows, :])).astype(y_ref.dtype)

    def pipelined(step, carry, first_visit, with_next):
        memories, pendings, m_reps = carry
        nxt = [scores(step + 1, dr, m_reps[dr]) for dr in range(2)] if with_next else None
        done = [outputs(step, dr, memories[dr], pendings[dr]) for dr in range(2)]
        for rows, hc, _ in done:
            emit(rows, hc, first_visit)
        new_mem = tuple(dn[2] for dn in done)
        if with_next:
            return new_mem, tuple(n[0] for n in nxt), tuple(n[1] for n in nxt)
        return new_mem, pendings, m_reps

    memories = tuple((c0_ref[0, dr, 0], n_to_columns(n0_ref[0, dr, 0])) for dr in range(2))
    first = [scores(0, dr, m0_ref[0, dr, 0]) for dr in range(2)]
    carry = (memories, tuple(f[0] for f in first), tuple(f[1] for f in first))
    unroll = 2 if half % 2 == 0 else 1
    carry = lax.fori_loop(0, half, functools.partial(pipelined, first_visit=True, with_next=True), carry,
                          unroll=unroll)
    carry = lax.fori_loop(half, nchunks - 1, functools.partial(pipelined, first_visit=False, with_next=True), carry,
                          unroll=unroll)
    memories, _, m_reps = pipelined(nchunks - 1, carry, first_visit=False, with_next=False)
    for dr in range(2):
        c_out_ref[0, dr, 0] = memories[dr][0]
        n_out_ref[0, dr, 0] = n_to_row(memories[dr][1])
        m_out_ref[0, dr, 0] = m_reps[dr]


def _mlstm(q, k, v, o, z, gates, c0, n0, m0):
    b, seq, _ = q.shape
    qk_blk = lambda i, h: (i, 0, h)
    st5 = lambda i, h: (i, 0, h, 0, 0)
    vmem = 2 * (2 * seq * BLK * 2 + seq * DV * 2 * 2 + 2 * seq * DV * 4 + 2 * 2 * BLK * DV * 4) + seq * DV * 4 \
        + seq * 4 * LANES * 4 + 16 * 1024 * 1024
    return pl.pallas_call(
        functools.partial(_mlstm_kernel, seq=seq),
        grid=(b, N_HEADS),
        in_specs=[pl.BlockSpec((1, seq, BLK), qk_blk),
                  pl.BlockSpec((1, seq, BLK), qk_blk),
                  pl.BlockSpec((1, seq, DV), qk_blk),
                  pl.BlockSpec((1, seq, DV), qk_blk),
                  pl.BlockSpec((1, seq, DV), qk_blk),
                  pl.BlockSpec((1, 1, SUBLANES, seq), lambda i, h: (i, h, 0, 0)),
                  pl.BlockSpec((1, 2, 1, BLK, DV), st5),
                  pl.BlockSpec((1, 2, 1, 1, BLK), st5),
                  pl.BlockSpec((1, 2, 1, 1, CHUNK), st5)],
        out_specs=[pl.BlockSpec((1, seq, DV), qk_blk),
                   pl.BlockSpec((1, 2, 1, BLK, DV), st5),
                   pl.BlockSpec((1, 2, 1, 1, BLK), st5),
                   pl.BlockSpec((1, 2, 1, 1, CHUNK), st5)],
        out_shape=[jax.ShapeDtypeStruct((b, seq, N_HEADS * DV), BF16),
                   jax.ShapeDtypeStruct((b, 2, N_HEADS, BLK, DV), F32),
                   jax.ShapeDtypeStruct((b, 2, N_HEADS, 1, BLK), F32),
                   jax.ShapeDtypeStruct((b, 2, N_HEADS, 1, CHUNK), F32)],
        scratch_shapes=[pltpu.VMEM((seq, DV), F32), pltpu.VMEM((seq, 4 * LANES), F32)],
        compiler_params=_cparams(("parallel", "parallel"), vmem),
        name="mlstm_chunkwise",
    )(q, k, v, o, z, gates, c0, n0, m0)


def _dft_tables(seq):
    n = np.arange(BLK)
    ang = 2.0 * np.pi * np.outer(n, n) / BLK
    dft_c = np.concatenate([np.cos(ang), np.sin(ang)], axis=1) / np.sqrt(BLK)
    l1 = seq // FFT_L2
    n2 = np.arange(FFT_L2)
    ang2 = 2.0 * np.pi * np.outer(n2, n2) / FFT_L2
    w_pos = np.concatenate([np.cos(ang2), np.sin(ang2)], axis=1) / np.sqrt(seq)
    angt = 2.0 * np.pi * np.outer(np.arange(l1), n2) / seq
    twc = np.broadcast_to(np.cos(angt)[:, :, None], (l1, FFT_L2, LANES))
    tws = np.broadcast_to(np.sin(angt)[:, :, None], (l1, FFT_L2, LANES))
    return (jnp.asarray(dft_c, F32).astype(BF16), jnp.asarray(w_pos, F32).astype(BF16),
            jnp.asarray(twc, F32), jnp.asarray(tws, F32))


def _run_trunk(x, mod, row_off, per_batch_mod, lru0, c0, n0, m0, p, cfg):
    b, seq, d = x.shape
    m = b * seq
    rows_per_mod = seq if per_batch_mod else m
    x2d = x.reshape(m, d)
    dft_c, w_pos, twc, tws = _dft_tables(seq)

    mod0 = mod[0].reshape(MOD_ROWS, 1, 3 * d)
    xa, za, zb, xc, xs = _inproj_e(x2d, mod0, p["g_pre"][0:1], p["w_in_e"], dft_c, row_off, rows_per_mod,
                                   cfg["tm_in"])
    r3 = lambda a: a.reshape(b, seq, a.shape[-1])
    ya, lru_f = _rglru(r3(xa), r3(za), lru0, p["conv_w"], p["conv_b"], p["wg"], p["bg"], p["lam"], cfg["tc_lru"])
    yb = _fourier(r3(xc), r3(xs), r3(zb), twc, tws, w_pos, p["w_four"], p["b_four"], cfg["w_four"], cfg["tr_four"])
    x1 = _outproj([ya.reshape(m, d), yb.reshape(m, d)], [p["w_out_e"][0:d], p["w_out_e"][d:2 * d]],
                  x2d, mod0, p["g_post"][0:1], row_off, rows_per_mod, cfg["tm_out"])

    mod1 = mod[1].reshape(MOD_ROWS, 1, 3 * d)
    q, k, v, o, z, gpt = _inproj_o(x1, mod1, p["g_pre"][1:2], p["w_in_o"], p["w_gate_t"], p["gate_bias"],
                                   row_off, rows_per_mod, cfg["tm_in_o"])
    gq = gpt[0:4 * N_HEADS].reshape(4, N_HEADS, b, seq).transpose(0, 2, 1, 3).reshape(4, b * N_HEADS, seq)
    gates = _gate_prep(gq, cfg["rb_gate"]).reshape(SUBLANES, b, N_HEADS, seq).transpose(1, 2, 0, 3)
    y, c_f, n_f, m_f = _mlstm(r3(q), r3(k), r3(v), r3(o), r3(z), gates, c0, n0, m0)
    x2 = _outproj([y.reshape(m, 2 * d)], [p["w_out_o"]], x1, mod1, p["g_post"][1:2], row_off, rows_per_mod,
                  cfg["tm_out"])
    return x2.reshape(b, seq, d), lru_f, c_f, n_f, m_f


def _prepare_params(conv_w, conv_b, w_rg, b_rg, w_ig, b_ig, lru_lambda, w_four, b_four, w_in_e, w_out_e,
                    w_in_o, b_if, w_out_o, g_pre, g_post):
    d = D_MODEL
    wg = jnp.concatenate([w_rg[0], w_ig[0]], axis=-1).astype(BF16)
    bg = jnp.concatenate([b_rg[0].reshape(2, N_HEADS, 1, BLK), b_ig[0].reshape(2, N_HEADS, 1, BLK)], axis=-1)
    n_gate = 4 * N_HEADS
    w_gate_t = jnp.zeros((LANES, d), F32).at[0:n_gate].set(w_in_o[0][:, 8 * d:8 * d + n_gate].T).astype(BF16)
    bias = jnp.zeros((LANES,), F32).at[0:n_gate].set(b_if[0].reshape(n_gate))
    gate_bias = jnp.broadcast_to(bias[:, None], (LANES, LANES))
    return dict(
        g_pre=g_pre, g_post=g_post,
        w_in_e=w_in_e[0].astype(BF16), w_out_e=w_out_e[0].astype(BF16),
        conv_w=conv_w[0], conv_b=conv_b[0].reshape(1, d), wg=wg, bg=bg, lam=lru_lambda[0].reshape(2, 1, d),
        w_four=w_four[0].astype(BF16), b_four=b_four[0].reshape(1, d),
        w_in_o=w_in_o[0][:, 0:8 * d].astype(BF16), w_gate_t=w_gate_t, gate_bias=gate_bias,
        w_out_o=w_out_o[0].astype(BF16))


def kernel(x_prompt, x_sample, c, state_lru, state_mlstm_C, state_mlstm_n, state_mlstm_m, c_ctx, w_mod, b_mod,
           g_pre, g_post, w_in_e, conv_w, conv_b, w_rg, b_rg, w_ig, b_ig, lru_lambda, w_four, b_four, w_out_e,
           w_in_o, b_if, w_out_o):
    d = D_MODEL
    bp = x_prompt.shape[0]
    bs = x_sample.shape[0]
    p = _prepare_params(conv_w, conv_b, w_rg, b_rg, w_ig, b_ig, lru_lambda, w_four, b_four, w_in_e, w_out_e,
                        w_in_o, b_if, w_out_o, g_pre, g_post)
    cond = jnp.concatenate([c_ctx[None, :], c, jnp.zeros((MOD_ROWS - 1 - bs, d), F32)], axis=0)
    mod = _modulation(cond, w_mod, b_mod)

    def states(lru, cm, nv, mv):
        bsz = lru.shape[0]
        return (lru[:, 0], cm[:, 0], nv[:, 0].reshape(bsz, 2, N_HEADS, 1, BLK),
                jnp.broadcast_to(mv[:, 0].reshape(bsz, 2, N_HEADS, 1, 1), (bsz, 2, N_HEADS, 1, CHUNK)))

    zero_states = states(jnp.zeros((bp, 1, 2, d), F32), jnp.zeros((bp, 1, 2, N_HEADS, BLK, DV), F32),
                         jnp.zeros((bp, 1, 2, N_HEADS, BLK), F32), jnp.zeros((bp, 1, 2, N_HEADS), F32))
    cfg_p = dict(tm_in=512, tm_in_o=256, tm_out=512, w_lru=d, tc_lru=128, w_four=d, tr_four=256,
                 rb_gate=min(bp * N_HEADS, 128))
    y_prompt, lru_f, c_f, n_f, m_f = _run_trunk(x_prompt, mod, 0, False, *zero_states, p, cfg_p)

    cfg_s = dict(tm_in=512, tm_in_o=256, tm_out=512, w_lru=2 * BLK, tc_lru=128, w_four=BLK, tr_four=256,
                 rb_gate=SUBLANES)
    y_sample, _, _, _, _ = _run_trunk(x_sample, mod, 1, True,
                                      *states(state_lru, state_mlstm_C, state_mlstm_n, state_mlstm_m), p, cfg_s)

    return (y_prompt, y_sample, lru_f[:, None], c_f[:, None], n_f[:, None, :, :, 0, :], m_f[:, None, :, :, 0, 0])
```

```python
import functools
import math

import numpy as np
import jax
import jax.numpy as jnp
from jax import lax
from jax.experimental import pallas as pl
from jax.experimental.pallas import tpu as pltpu

F32 = jnp.float32
BF16 = jnp.bfloat16

D_MODEL = 1024
DEPTH = 2
EPS = 1e-6
LRU_C = 8.0
CONV_W = 4
N_HEADS = 8
BLK = D_MODEL // N_HEADS
DV = 2 * BLK
CHUNK = 128
FFT_L2 = 256
MOD_ROWS = 8

LANES = 128
SUBLANES = 8
VMEM_LIMIT_CAP = 56 * 1024 * 1024


def _cparams(sem, vmem_bytes, flags=None):
    return pltpu.CompilerParams(dimension_semantics=sem, flags=flags,
                                vmem_limit_bytes=int(min(max(vmem_bytes, 16 * 1024 * 1024), VMEM_LIMIT_CAP)))


def _silu(x):
    return x * jax.nn.sigmoid(x)


def _mod_kernel(cond_ref, w_ref, b_ref, o_ref):
    s = _silu(cond_ref[...]).astype(BF16)
    o_ref[0] = jnp.dot(s, w_ref[0].astype(BF16), preferred_element_type=F32) + b_ref[0]


def _modulation(cond, w_mod, b_mod):
    d = D_MODEL
    return pl.pallas_call(
        _mod_kernel,
        grid=(DEPTH, 3),
        in_specs=[pl.BlockSpec((MOD_ROWS, d), lambda l, j: (0, 0)),
                  pl.BlockSpec((1, d, d), lambda l, j: (l, 0, j)),
                  pl.BlockSpec((1, 1, d), lambda l, j: (l, 0, j))],
        out_specs=pl.BlockSpec((1, MOD_ROWS, d), lambda l, j: (l, 0, j)),
        out_shape=jax.ShapeDtypeStruct((DEPTH, MOD_ROWS, 3 * d), F32),
        compiler_params=_cparams(("arbitrary", "arbitrary"), 24 * 1024 * 1024),
        name="adaln_mod",
    )(cond, w_mod, b_mod.reshape(DEPTH, 1, 3 * d))


def _normed_input(x_ref, mod_ref, g_ref):
    d = D_MODEL
    x = x_ref[...]
    y = x * lax.rsqrt(jnp.mean(x * x, axis=-1, keepdims=True) + EPS) * g_ref[...]
    shift = mod_ref[0, :, 0:d]
    scale = mod_ref[0, :, d:2 * d]
    return (y * (1.0 + scale) + shift).astype(BF16)


def _mod_index_map(row_off, tiles_per_mod):
    return lambda i: (row_off + i // tiles_per_mod, 0, 0)


def _inproj_e_kernel(x_ref, mod_ref, g_ref, w_ref, dft_ref, xa_ref, za_ref, zb_ref, xc_ref, xs_ref):
    d = D_MODEL
    hb = _normed_input(x_ref, mod_ref, g_ref)
    xa_ref[...] = jnp.dot(hb, w_ref[:, 0:d], preferred_element_type=F32)
    za_ref[...] = jnp.dot(hb, w_ref[:, d:2 * d], preferred_element_type=F32)
    zb_ref[...] = jnp.dot(hb, w_ref[:, 3 * d:4 * d], preferred_element_type=F32)
    xb = jnp.dot(hb, w_ref[:, 2 * d:3 * d], preferred_element_type=F32).astype(BF16)
    for g in range(N_HEADS):
        cs = jnp.dot(xb[:, g * BLK:(g + 1) * BLK], dft_ref[...], preferred_element_type=F32)
        xc_ref[:, g * BLK:(g + 1) * BLK] = cs[:, 0:BLK]
        xs_ref[:, g * BLK:(g + 1) * BLK] = cs[:, BLK:2 * BLK]


def _inproj_e(x2d, mod_l, g_pre, w_in, dft_c, row_off, rows_per_mod, tm):
    m, d = x2d.shape
    row = lambda i: (i, 0)
    const = lambda i: (0, 0)
    out = jax.ShapeDtypeStruct((m, d), F32)
    vmem = 2 * (w_in.size * 2 + tm * d * 4 * 6) + 8 * tm * d * 4
    return pl.pallas_call(
        _inproj_e_kernel,
        grid=(m // tm,),
        in_specs=[pl.BlockSpec((tm, d), row),
                  pl.BlockSpec((1, 1, 3 * d), _mod_index_map(row_off, rows_per_mod // tm)),
                  pl.BlockSpec((1, d), const),
                  pl.BlockSpec(w_in.shape, const),
                  pl.BlockSpec(dft_c.shape, const)],
        out_specs=[pl.BlockSpec((tm, d), row)] * 5,
        out_shape=[out] * 5,
        compiler_params=_cparams(("parallel",), vmem),
        name="inproj_even",
    )(x2d, mod_l, g_pre, w_in, dft_c)


def _scan_chunk(a, u, reverse):
    t = a.shape[0]
    row = lax.broadcasted_iota(jnp.int32, a.shape, 0)
    d = 1
    while d < t:
        if d < SUBLANES:
            shift = (t - d) if reverse else d
            a_sh = pltpu.roll(a, shift, axis=0)
            u_sh = pltpu.roll(u, shift, axis=0)
            valid = (row < t - d) if reverse else (row >= d)
            a_sh = jnp.where(valid, a_sh, 1.0)
            u_sh = jnp.where(valid, u_sh, 0.0)
        else:
            ones = jnp.ones((d, a.shape[1]), F32)
            zeros = jnp.zeros((d, a.shape[1]), F32)
            if reverse:
                a_sh = jnp.concatenate([a[d:], ones], axis=0)
                u_sh = jnp.concatenate([u[d:], zeros], axis=0)
            else:
                a_sh = jnp.concatenate([ones, a[:t - d]], axis=0)
                u_sh = jnp.concatenate([zeros, u[:t - d]], axis=0)
        u = a * u_sh + u
        a = a * a_sh
        d *= 2
    return a, u


def _rglru_kernel(xa_ref, za_ref, cw_ref, cb_ref, wg_ref, bg_ref, lam_ref, h0_ref,
                  y_ref, hT_ref, hf_ref, hb_ref, *, seq, tc, bb):
    nchunks = seq // tc
    groups = tc // SUBLANES
    sub = lax.broadcasted_iota(jnp.int32, (SUBLANES, LANES), 0)

    def row_bcast(block, r):
        return jnp.broadcast_to(block[r:r + 1, :], (SUBLANES, LANES))

    def conv_chunk(bi, c):
        t0 = pl.multiple_of(c * tc, tc)
        xs = [xa_ref[bi, pl.ds(t0 + g, SUBLANES, stride=groups), :] for g in range(groups)]
        prev = xa_ref[bi, pl.ds(pl.multiple_of(jnp.maximum(t0 - SUBLANES, 0), SUBLANES), SUBLANES), :]
        nxt = xa_ref[bi, pl.ds(pl.multiple_of(jnp.minimum(t0 + tc, seq - SUBLANES), SUBLANES), SUBLANES), :]
        prev = jnp.where(c > 0, prev, 0.0)
        nxt = jnp.where(c < nchunks - 1, nxt, 0.0)
        before2 = jnp.where(sub == 0, row_bcast(prev, SUBLANES - 2), pltpu.roll(xs[groups - 2], 1, axis=0))
        before1 = jnp.where(sub == 0, row_bcast(prev, SUBLANES - 1), pltpu.roll(xs[groups - 1], 1, axis=0))
        after1 = jnp.where(sub == SUBLANES - 1, row_bcast(nxt, 0), pltpu.roll(xs[0], SUBLANES - 1, axis=0))
        ext = [before2, before1] + xs + [after1]
        out = []
        for g in range(groups):
            acc = cb_ref[...] + cw_ref[0:1, :] * ext[g]
            for j in range(1, CONV_W):
                acc = acc + cw_ref[j:j + 1, :] * ext[g + j]
            out.append(acc)
        return t0, jnp.concatenate(out, axis=0)

    def gates(xc, direction):
        g = jnp.dot(xc.astype(BF16), wg_ref[direction, 0], preferred_element_type=F32) + bg_ref[direction, 0]
        r = jax.nn.sigmoid(g[:, 0:BLK])
        i = jax.nn.sigmoid(g[:, BLK:2 * BLK])
        lam = lam_ref[direction]
        softplus_neg = jnp.maximum(-lam, 0.0) + jnp.log1p(jnp.exp(-jnp.abs(lam)))
        neg_log_a = LRU_C * r * softplus_neg
        a = jnp.exp(-neg_log_a)
        om = jnp.tanh(neg_log_a) * (1.0 + a * a)
        root = jnp.where(om > 0.0, om * lax.rsqrt(om), 0.0)
        return a, root * (i * xc)

    def scan_dir(bi, c, direction, carry, out_ref):
        reverse = direction == 1
        t0, xc = conv_chunk(bi, c)
        a, u = gates(xc, direction)
        order = range(groups - 1, -1, -1) if reverse else range(groups)
        p_g, s_g = [None] * groups, [None] * groups
        p_run = s_run = None
        for g in order:
            a_v = a[g * SUBLANES:(g + 1) * SUBLANES]
            u_v = u[g * SUBLANES:(g + 1) * SUBLANES]
            if p_run is None:
                p_run, s_run = a_v, u_v
            else:
                s_run = a_v * s_run + u_v
                p_run = a_v * p_run
            p_g[g], s_g[g] = p_run, s_run
        pp, ss = _scan_chunk(p_run, s_run, reverse)
        carry_b = jnp.broadcast_to(carry, (SUBLANES, LANES))
        seg_out = pp * carry_b + ss
        if reverse:
            h_in = jnp.where(sub == SUBLANES - 1, carry_b, pltpu.roll(seg_out, SUBLANES - 1, axis=0))
            new_carry = seg_out[0:1, :]
        else:
            h_in = jnp.where(sub == 0, carry_b, pltpu.roll(seg_out, 1, axis=0))
            new_carry = seg_out[SUBLANES - 1:SUBLANES, :]
        for g in range(groups):
            out_ref[bi, pl.ds(t0 + g, SUBLANES, stride=groups), :] = p_g[g] * h_in + s_g[g]
        return new_carry

    def body(j, carries):
        return tuple((scan_dir(bi, j, 0, carries[bi][0], hf_ref),
                      scan_dir(bi, nchunks - 1 - j, 1, carries[bi][1], hb_ref)) for bi in range(bb))

    init = tuple((h0_ref[bi, 0:1, :], h0_ref[bi, 1:2, :]) for bi in range(bb))
    if nchunks <= 2:
        finals = init
        for j in range(nchunks):
            finals = body(j, finals)
    else:
        finals = lax.fori_loop(0, nchunks, body, init, unroll=2)
    for bi in range(bb):
        hT_ref[bi, 0:1, :] = finals[bi][0]
        hT_ref[bi, 1:2, :] = finals[bi][1]

    def gate_out(c, carry):
        rows = pl.ds(pl.multiple_of(c * tc, tc), tc)
        for bi in range(bb):
            y_ref[bi, rows, :] = ((hf_ref[bi, rows, :] + hb_ref[bi, rows, :])
                                  * _silu(za_ref[bi, rows, :])).astype(y_ref.dtype)
        return carry

    lax.fori_loop(0, nchunks, gate_out, 0)


def _rglru(xa, za, h0, conv_w, conv_b, wg, bg, lam, tc, bb):
    b, seq, d = xa.shape
    blk = lambda i, j: (i, 0, j)
    vmem = bb * (2 * (2 * seq * BLK * 4 + seq * BLK * 2) + 2 * seq * BLK * 4) + 16 * 1024 * 1024
    return pl.pallas_call(
        functools.partial(_rglru_kernel, seq=seq, tc=tc, bb=bb),
        grid=(b // bb, d // BLK),
        in_specs=[pl.BlockSpec((bb, seq, BLK), blk),
                  pl.BlockSpec((bb, seq, BLK), blk),
                  pl.BlockSpec((CONV_W, BLK), lambda i, j: (0, j)),
                  pl.BlockSpec((1, BLK), lambda i, j: (0, j)),
                  pl.BlockSpec((2, 1, BLK, 2 * BLK), lambda i, j: (0, j, 0, 0)),
                  pl.BlockSpec((2, 1, 1, 2 * BLK), lambda i, j: (0, j, 0, 0)),
                  pl.BlockSpec((2, 1, BLK), lambda i, j: (0, 0, j)),
                  pl.BlockSpec((bb, 2, BLK), blk)],
        out_specs=[pl.BlockSpec((bb, seq, BLK), blk),
                   pl.BlockSpec((bb, 2, BLK), blk)],
        out_shape=[jax.ShapeDtypeStruct((b, seq, d), BF16),
                   jax.ShapeDtypeStruct((b, 2, d), F32)],
        scratch_shapes=[pltpu.VMEM((bb, seq, BLK), F32), pltpu.VMEM((bb, seq, BLK), F32)],
        compiler_params=_cparams(("parallel", "parallel"), vmem),
        name="rglru_scan",
    )(xa, za, conv_w, conv_b, wg, bg, lam, h0)


def _fft_list(xs):
    n = len(xs)
    if n == 1:
        return xs
    even = _fft_list(xs[0::2])
    odd = _fft_list(xs[1::2])
    out = [None] * n
    for k in range(n // 2):
        o_re, o_im = odd[k]
        if k == 0:
            t_re, t_im = o_re, o_im
        elif 4 * k == n:
            t_re, t_im = o_im, -o_re
        else:
            ang = -2.0 * math.pi * k / n
            wr, wi = math.cos(ang), math.sin(ang)
            t_re = o_re * wr - o_im * wi
            t_im = o_re * wi + o_im * wr
        e_re, e_im = even[k]
        out[k] = (e_re + t_re, e_im + t_im)
        out[k + n // 2] = (e_re - t_re, e_im - t_im)
    return out


def _fourier_kernel(xc_ref, xs_ref, zb_ref, twc_ref, tws_ref, wpos_ref, wf_ref, bf_ref, y_ref, *scratch,
                    seq, tr):
    w = xc_ref.shape[2]
    nblk = w // BLK
    l1 = seq // FFT_L2

    def epilogue(fr, rows):
        parts = []
        for kb in range(nblk):
            yb = jnp.dot(fr[:, kb * BLK:(kb + 1) * BLK].astype(BF16), wf_ref[kb], preferred_element_type=F32)
            parts.append(yb + bf_ref[:, kb * BLK:(kb + 1) * BLK])
        yb = parts[0] if nblk == 1 else jnp.concatenate(parts, axis=1)
        y_ref[0, rows, :] = (yb * _silu(zb_ref[0, rows, :])).astype(y_ref.dtype)

    if l1 == 1:
        fr = (jnp.dot(wpos_ref[:, 0:FFT_L2], xc_ref[0].astype(BF16), preferred_element_type=F32)
              - jnp.dot(wpos_ref[:, FFT_L2:2 * FFT_L2], xs_ref[0].astype(BF16), preferred_element_type=F32))
        epilogue(fr, pl.ds(0, seq))
        return

    b_ref, fr_ref = scratch
    assert w == LANES

    def butterfly(r, carry):
        r0 = pl.multiple_of(r * SUBLANES, SUBLANES)
        zs = [(xc_ref[0, pl.ds(n1 * FFT_L2 + r0, SUBLANES), :], -xs_ref[0, pl.ds(n1 * FFT_L2 + r0, SUBLANES), :])
              for n1 in range(l1)]
        for k1, (a_re, a_im) in enumerate(_fft_list(zs)):
            if k1 == 0:
                b_re, b_im = a_re, a_im
            else:
                tc_ = twc_ref[k1, pl.ds(r0, SUBLANES), :]
                ts_ = tws_ref[k1, pl.ds(r0, SUBLANES), :]
                b_re = a_re * tc_ + a_im * ts_
                b_im = a_im * tc_ - a_re * ts_
            b_ref[pl.ds(r0, SUBLANES), k1 * LANES:(k1 + 1) * LANES] = b_re
            b_ref[pl.ds(FFT_L2 + r0, SUBLANES), k1 * LANES:(k1 + 1) * LANES] = b_im
        return carry

    lax.fori_loop(0, FFT_L2 // SUBLANES, butterfly, 0)

    per_dot = min(4, l1)
    for nb in range(l1 // per_dot):
        cols = slice(nb * per_dot * LANES, (nb + 1) * per_dot * LANES)
        fr = jnp.dot(wpos_ref[...], b_ref[:, cols].astype(BF16), preferred_element_type=F32)
        for kk in range(per_dot):
            fr_ref[pl.ds(nb * per_dot + kk, FFT_L2, stride=l1), :] = fr[:, kk * LANES:(kk + 1) * LANES]

    def finish(i, carry):
        rows = pl.ds(pl.multiple_of(i * tr, tr), tr)
        epilogue(fr_ref[rows, :], rows)
        return carry

    lax.fori_loop(0, seq // tr, finish, 0, unroll=2)


def _fourier(xc, xs, zb, twc, tws, w_pos, w_four, b_four, w_blk, tr):
    b, seq, d = xc.shape
    nblk = w_blk // BLK
    l1 = seq // FFT_L2
    blk = lambda i, j: (i, 0, j)
    scratch = []
    if l1 > 1:
        scratch = [pltpu.VMEM((2 * FFT_L2, l1 * w_blk), F32), pltpu.VMEM((seq, w_blk), F32)]
    vmem = (2 * (3 * seq * w_blk * 4 + seq * w_blk * 2 + 2 * twc.size * 4) + 3 * seq * w_blk * 4
            + 16 * 1024 * 1024)
    return pl.pallas_call(
        functools.partial(_fourier_kernel, seq=seq, tr=tr),
        grid=(b, d // w_blk),
        in_specs=[pl.BlockSpec((1, seq, w_blk), blk),
                  pl.BlockSpec((1, seq, w_blk), blk),
                  pl.BlockSpec((1, seq, w_blk), blk),
                  pl.BlockSpec(twc.shape, lambda i, j: (0, 0, 0)),
                  pl.BlockSpec(tws.shape, lambda i, j: (0, 0, 0)),
                  pl.BlockSpec(w_pos.shape, lambda i, j: (0, 0)),
                  pl.BlockSpec((nblk, BLK, BLK), lambda i, j: (j, 0, 0)),
                  pl.BlockSpec((1, w_blk), lambda i, j: (0, j))],
        out_specs=pl.BlockSpec((1, seq, w_blk), blk),
        out_shape=jax.ShapeDtypeStruct((b, seq, d), BF16),
        scratch_shapes=scratch,
        compiler_params=_cparams(("parallel", "parallel"), vmem),
        name="fourier_mix",
    )(xc, xs, zb, twc, tws, w_pos, w_four, b_four)


def _outproj_kernel(*refs, n_in):
    y_refs = refs[:n_in]
    w_refs = refs[n_in:2 * n_in]
    x_ref, mod_ref, g_ref, o_ref = refs[2 * n_in:]
    d = D_MODEL
    y = jnp.dot(y_refs[0][...], w_refs[0][...], preferred_element_type=F32)
    for k in range(1, n_in):
        y = y + jnp.dot(y_refs[k][...], w_refs[k][...], preferred_element_type=F32)
    yn = y * lax.rsqrt(jnp.mean(y * y, axis=-1, keepdims=True) + EPS) * g_ref[...]
    o_ref[...] = x_ref[...] + mod_ref[0, :, 2 * d:3 * d] * yn


def _outproj(ys, ws, x2d, mod_l, g_post, row_off, rows_per_mod, tm):
    m, d = x2d.shape
    n_in = len(ys)
    row = lambda i: (i, 0)
    const = lambda i: (0, 0)
    vmem = 2 * (sum(wk.size for wk in ws) * 2 + sum(tm * yk.shape[1] for yk in ys) * 2 + 2 * tm * d * 4) + 4 * tm * d * 4
    return pl.pallas_call(
        functools.partial(_outproj_kernel, n_in=n_in),
        grid=(m // tm,),
        in_specs=([pl.BlockSpec((tm, yk.shape[1]), row) for yk in ys]
                  + [pl.BlockSpec(wk.shape, const) for wk in ws]
                  + [pl.BlockSpec((tm, d), row),
                     pl.BlockSpec((1, 1, 3 * d), _mod_index_map(row_off, rows_per_mod // tm)),
                     pl.BlockSpec((1, d), const)]),
        out_specs=pl.BlockSpec((tm, d), row),
        out_shape=jax.ShapeDtypeStruct((m, d), F32),
        compiler_params=_cparams(("parallel",), vmem),
        name="outproj_residual",
    )(*ys, *ws, x2d, mod_l, g_post)


def _inproj_o_kernel(x_ref, mod_ref, g_ref, w_ref, wgt_ref, gb_ref, q_ref, k_ref, v_ref, o_ref, z_ref, gpt_ref,
                     gp_ref):
    d = D_MODEL
    hb = _normed_input(x_ref, mod_ref, g_ref)
    q = jnp.dot(hb, w_ref[:, 0:d], preferred_element_type=F32)
    q_ref[...] = (q * (BLK ** -0.5)).astype(q_ref.dtype)
    k_ref[...] = jnp.dot(hb, w_ref[:, d:2 * d], preferred_element_type=F32).astype(k_ref.dtype)
    for half in range(2):
        cols = slice(half * d, (half + 1) * d)
        v_ref[:, cols] = jnp.dot(hb, w_ref[:, 2 * d + half * d:3 * d + half * d],
                                 preferred_element_type=F32).astype(v_ref.dtype)
        o_ref[:, cols] = jnp.dot(hb, w_ref[:, 4 * d + half * d:5 * d + half * d], preferred_element_type=F32)
        z_ref[:, cols] = jnp.dot(hb, w_ref[:, 6 * d + half * d:7 * d + half * d], preferred_element_type=F32)
    gp_ref[...] = jnp.dot(hb, wgt_ref[...], preferred_element_type=F32)
    gpt_ref[...] = gp_ref[...].T + gb_ref[:, 0:1]


def _inproj_o(x2d, mod_l, g_pre, w_main, w_gate, gate_bias, row_off, rows_per_mod, tm):
    m, d = x2d.shape
    row = lambda i: (i, 0)
    const = lambda i: (0, 0)
    ng = w_gate.shape[1]
    vmem = 2 * (w_main.size * 2 + tm * d * 4 + tm * d * 2 * 4 + tm * 2 * d * 4 * 2) + 6 * tm * d * 4
    return pl.pallas_call(
        _inproj_o_kernel,
        grid=(m // tm,),
        in_specs=[pl.BlockSpec((tm, d), row),
                  pl.BlockSpec((1, 1, 3 * d), _mod_index_map(row_off, rows_per_mod // tm)),
                  pl.BlockSpec((1, d), const),
                  pl.BlockSpec(w_main.shape, const),
                  pl.BlockSpec(w_gate.shape, const),
                  pl.BlockSpec(gate_bias.shape, const)],
        out_specs=[pl.BlockSpec((tm, d), row), pl.BlockSpec((tm, d), row),
                   pl.BlockSpec((tm, 2 * d), row), pl.BlockSpec((tm, 2 * d), row), pl.BlockSpec((tm, 2 * d), row),
                   pl.BlockSpec((ng, tm), lambda i: (0, i))],
        out_shape=[jax.ShapeDtypeStruct((m, d), BF16), jax.ShapeDtypeStruct((m, d), BF16),
                   jax.ShapeDtypeStruct((m, 2 * d), BF16), jax.ShapeDtypeStruct((m, 2 * d), F32),
                   jax.ShapeDtypeStruct((m, 2 * d), F32), jax.ShapeDtypeStruct((ng, m), F32)],
        scratch_shapes=[pltpu.VMEM((tm, ng), F32)],
        compiler_params=_cparams(("parallel",), vmem),
        name="inproj_odd",
    )(x2d, mod_l, g_pre, w_main, w_gate, gate_bias)


def _chunk_scan_lanes(v, op, identity, reverse):
    n = v.shape[-1]
    pos = lax.broadcasted_iota(jnp.int32, v.shape, 1) & (CHUNK - 1)
    d = 1
    while d < CHUNK:
        if reverse:
            shifted = jnp.where(pos < CHUNK - d, pltpu.roll(v, n - d, axis=1), identity)
        else:
            shifted = jnp.where(pos >= d, pltpu.roll(v, d, axis=1), identity)
        v = op(v, shifted)
        d *= 2
    return v


def _gate_prep_kernel(g_ref, o_ref):
    for direction in range(2):
        reverse = direction == 1
        i_pre = g_ref[2 * direction]
        b = _chunk_scan_lanes(jax.nn.log_sigmoid(g_ref[2 * direction + 1]), jnp.add, 0.0, reverse)
        a = i_pre - b
        o_ref[3 * direction] = a
        o_ref[3 * direction + 1] = b
        o_ref[3 * direction + 2] = _chunk_scan_lanes(a, jnp.maximum, -jnp.inf, reverse)
    o_ref[6] = jnp.zeros_like(g_ref[0])
    o_ref[7] = jnp.zeros_like(g_ref[0])


def _gate_prep(gq, rb):
    _, r, seq = gq.shape
    return pl.pallas_call(
        _gate_prep_kernel,
        grid=(r // rb,),
        in_specs=[pl.BlockSpec((4, rb, seq), lambda i: (0, i, 0))],
        out_specs=pl.BlockSpec((SUBLANES, rb, seq), lambda i: (0, i, 0)),
        out_shape=jax.ShapeDtypeStruct((SUBLANES, r, seq), F32),
        compiler_params=_cparams(("parallel",), 24 * 1024 * 1024),
        name="mlstm_gate_prep",
    )(gq)


def _mlstm_kernel(*refs, seq, heads, zero_init):
    q_ref, k_ref, v_ref, o_ref, z_ref, gate_ref = refs[0:6]
    refs = refs[6:]
    if not zero_init:
        c0_ref, n0_ref, m0_ref = refs[0:3]
        refs = refs[3:]
    y_ref, c_out_ref, n_out_ref, m_out_ref, hs_ref, cols_ref = refs
    t = CHUNK
    nchunks = seq // t
    half = nchunks // 2
    assert nchunks == 2 * half
    streams = [(hh, dr) for hh in range(heads) for dr in range(2)]

    def head_lanes(hh, width):
        return slice(hh * width, (hh + 1) * width)
    row_i = lax.broadcasted_iota(jnp.int32, (t, t), 0)
    col_i = lax.broadcasted_iota(jnp.int32, (t, t), 1)
    ones_blk = jnp.ones((t, LANES), BF16)

    sel_r = lax.broadcasted_iota(jnp.int32, (4 * SUBLANES, 4 * LANES), 0)
    sel_c = lax.broadcasted_iota(jnp.int32, (4 * SUBLANES, 4 * LANES), 1) // LANES
    wanted = jnp.where(sel_c == 0, 2, jnp.where(sel_c == 1, 1, jnp.where(sel_c == 2, 5, 4)))
    selector = jnp.where(((sel_r & (SUBLANES - 1)) == wanted) & (sel_r < 3 * SUBLANES), 1.0, 0.0).astype(BF16)

    def to_columns(c, carry):
        rows = pl.ds(pl.multiple_of(c * t, t), t)
        for hh in range(heads):
            g = gate_ref[0, hh, :, rows]
            hi = g.astype(BF16).astype(F32)
            mid = (g - hi).astype(BF16).astype(F32)
            lo = g - hi - mid
            stack = jnp.concatenate([hi, mid, lo, jnp.zeros_like(g)], axis=0).astype(BF16)
            cols_ref[hh, rows, :] = lax.dot_general(stack, selector, (((0,), (0,)), ((), ())),
                                                    preferred_element_type=F32)
        return carry

    if nchunks <= 2:
        for c in range(nchunks):
            to_columns(c, 0)
    else:
        lax.fori_loop(0, nchunks, to_columns, 0, unroll=4)

    def chunk_of(step, direction):
        return (nchunks - 1 - step) if direction == 1 else step

    def gate_terms(rows, hh, direction):
        g_rows = gate_ref[0, hh, :, rows]
        last = 0 if direction == 1 else t - 1
        a_row = g_rows[3 * direction:3 * direction + 1, :]
        b_end = g_rows[3 * direction + 1:3 * direction + 2, last:last + 1]
        a_end = g_rows[3 * direction + 2:3 * direction + 3, last:last + 1]
        amax_rep = cols_ref[hh, rows, 2 * direction * LANES:(2 * direction + 1) * LANES]
        return a_row, b_end, a_end, amax_rep

    def scores(step, stream, m_rep):
        hh, direction = stream
        rows = pl.ds(pl.multiple_of(chunk_of(step, direction) * t, t), t)
        a_row, b_end, a_end, amax_rep = gate_terms(rows, hh, direction)
        m_prev = m_rep[:, 0:1]
        mx = jnp.maximum(m_prev, amax_rep)
        mask = (col_i >= row_i) if direction == 1 else (col_i <= row_i)
        e = jnp.where(mask, jnp.exp(a_row - mx), 0.0)
        s = lax.dot_general(q_ref[0, rows, head_lanes(hh, BLK)], k_ref[0, rows, head_lanes(hh, BLK)],
                            (((1,), (1,)), ((), ())), preferred_element_type=F32) * e
        m_next = jnp.broadcast_to(b_end + jnp.maximum(m_prev, a_end), (1, t))
        return (s.astype(BF16), jnp.sum(s, axis=-1, keepdims=True), m_rep), m_next

    def outputs(step, stream, memory, pending):
        hh, direction = stream
        c_mat, n_rep = memory
        s_b, s_sum, m_rep = pending
        rows = pl.ds(pl.multiple_of(chunk_of(step, direction) * t, t), t)
        qc = q_ref[0, rows, head_lanes(hh, BLK)]
        kc = k_ref[0, rows, head_lanes(hh, BLK)]
        vc = v_ref[0, rows, head_lanes(hh, DV)]
        a_row, _, a_end, amax_rep = gate_terms(rows, hh, direction)
        b_rep = cols_ref[hh, rows, (2 * direction + 1) * LANES:(2 * direction + 2) * LANES]
        m_prev = m_rep[:, 0:1]
        mx_end = jnp.maximum(m_prev, a_end)
        mx = jnp.maximum(m_prev, amax_rep)
        sv = jnp.dot(s_b, vc, preferred_element_type=F32)
        qcn = jnp.dot(qc, jnp.concatenate([c_mat.astype(BF16), n_rep.astype(BF16)], axis=1),
                      preferred_element_type=F32)
        wkk_t = (kc.astype(F32).T * jnp.exp(a_row - mx_end)).astype(BF16)
        upd = jnp.dot(wkk_t, jnp.concatenate([vc, ones_blk], axis=1), preferred_element_type=F32)
        decay = jnp.exp(m_prev - mx_end)
        w_inter = jnp.exp(m_prev - mx)
        den = s_sum + w_inter * qcn[:, DV:DV + LANES]
        inv = 1.0 / jnp.maximum(jnp.abs(den), jnp.exp(-(b_rep + mx)))
        hc = jnp.concatenate(
            [(sv[:, kb * LANES:(kb + 1) * LANES] + w_inter * qcn[:, kb * LANES:(kb + 1) * LANES]) * inv
             for kb in range(DV // LANES)], axis=1)
        return rows, hc, (decay * c_mat + upd[:, 0:DV], decay * n_rep + upd[:, DV:DV + LANES])

    def n_to_columns(n_row):
        return jnp.broadcast_to(n_row, (LANES, BLK)).T

    def n_to_row(n_rep):
        return n_rep.T[0:1, :]

    def emit(hh, rows, hc, first_visit):
        if first_visit:
            hs_ref[hh, rows, :] = hc
        else:
            hsum = hs_ref[hh, rows, :] + hc
            lanes = head_lanes(hh, DV)
            y_ref[0, rows, lanes] = (jax.nn.sigmoid(o_ref[0, rows, lanes]) * hsum
                                     * _silu(z_ref[0, rows, lanes])).astype(y_ref.dtype)

    def pipelined(step, carry, first_visit, with_next):
        memories, pendings, m_reps = carry
        nxt = [scores(step + 1, sm, m_reps[i]) for i, sm in enumerate(streams)] if with_next else None
        done = [outputs(step, sm, memories[i], pendings[i]) for i, sm in enumerate(streams)]
        for (hh, _), (rows, hc, _) in zip(streams, done):
            emit(hh, rows, hc, first_visit)
        new_mem = tuple(dn[2] for dn in done)
        if with_next:
            return new_mem, tuple(n[0] for n in nxt), tuple(n[1] for n in nxt)
        return new_mem, pendings, m_reps

    if zero_init:
        memories = tuple((jnp.zeros((BLK, DV), F32), jnp.zeros((BLK, LANES), F32)) for _ in streams)
        first = [scores(0, sm, jnp.zeros((1, t), F32)) for sm in streams]
    else:
        memories = tuple((c0_ref[0, dr, hh], n_to_columns(n0_ref[0, dr, hh])) for hh, dr in streams)
        first = [scores(0, (hh, dr), m0_ref[0, dr, hh]) for hh, dr in streams]
    carry = (memories, tuple(f[0] for f in first), tuple(f[1] for f in first))
    if nchunks <= 2:
        for step in range(nchunks - 1):
            carry = pipelined(step, carry, first_visit=step < half, with_next=True)
    else:
        unroll = 2 if half % 2 == 0 else 1
        carry = lax.fori_loop(0, half, functools.partial(pipelined, first_visit=True, with_next=True), carry,
                              unroll=unroll)
        carry = lax.fori_loop(half, nchunks - 1, functools.partial(pipelined, first_visit=False, with_next=True),
                              carry, unroll=unroll)
    memories, _, m_reps = pipelined(nchunks - 1, carry, first_visit=False, with_next=False)
    for i, (hh, dr) in enumerate(streams):
        c_out_ref[0, dr, hh] = memories[i][0]
        n_out_ref[0, dr, hh] = n_to_row(memories[i][1])
        m_out_ref[0, dr, hh] = m_reps[i]


def _mlstm(q, k, v, o, z, gates, states, heads):
    b, seq, _ = q.shape
    qk_blk = lambda i, h: (i, 0, h)
    st5 = lambda i, h: (i, 0, h, 0, 0)
    state_specs = [pl.BlockSpec((1, 2, heads, BLK, DV), st5),
                   pl.BlockSpec((1, 2, heads, 1, BLK), st5),
                   pl.BlockSpec((1, 2, heads, 1, CHUNK), st5)]
    vmem = heads * (2 * (2 * seq * BLK * 2 + seq * DV * 2 * 2 + 2 * seq * DV * 4 + 2 * 2 * BLK * DV * 4)
                    + seq * DV * 4 + seq * 4 * LANES * 4) + 16 * 1024 * 1024
    return pl.pallas_call(
        functools.partial(_mlstm_kernel, seq=seq, heads=heads, zero_init=states is None),
        grid=(b, N_HEADS // heads),
        in_specs=[pl.BlockSpec((1, seq, heads * BLK), qk_blk),
                  pl.BlockSpec((1, seq, heads * BLK), qk_blk),
                  pl.BlockSpec((1, seq, heads * DV), qk_blk),
                  pl.BlockSpec((1, seq, heads * DV), qk_blk),
                  pl.BlockSpec((1, seq, heads * DV), qk_blk),
                  pl.BlockSpec((1, heads, SUBLANES, seq), lambda i, h: (i, h, 0, 0))]
        + ([] if states is None else state_specs),
        out_specs=[pl.BlockSpec((1, seq, heads * DV), qk_blk)] + state_specs,
        out_shape=[jax.ShapeDtypeStruct((b, seq, N_HEADS * DV), BF16),
                   jax.ShapeDtypeStruct((b, 2, N_HEADS, BLK, DV), F32),
                   jax.ShapeDtypeStruct((b, 2, N_HEADS, 1, BLK), F32),
                   jax.ShapeDtypeStruct((b, 2, N_HEADS, 1, CHUNK), F32)],
        scratch_shapes=[pltpu.VMEM((heads, seq, DV), F32), pltpu.VMEM((heads, seq, 4 * LANES), F32)],
        compiler_params=_cparams(("parallel", "parallel"), vmem),
        name="mlstm_chunkwise",
    )(q, k, v, o, z, gates, *(() if states is None else states))


def _dft_tables(seq):
    n = np.arange(BLK)
    ang = 2.0 * np.pi * np.outer(n, n) / BLK
    dft_c = np.concatenate([np.cos(ang), np.sin(ang)], axis=1) / np.sqrt(BLK)
    l1 = seq // FFT_L2
    n2 = np.arange(FFT_L2)
    ang2 = 2.0 * np.pi * np.outer(n2, n2) / FFT_L2
    w_pos = np.concatenate([np.cos(ang2), np.sin(ang2)], axis=1) / np.sqrt(seq)
    angt = 2.0 * np.pi * np.outer(np.arange(l1), n2) / seq
    twc = np.broadcast_to(np.cos(angt)[:, :, None], (l1, FFT_L2, LANES))
    tws = np.broadcast_to(np.sin(angt)[:, :, None], (l1, FFT_L2, LANES))
    return (jnp.asarray(dft_c, F32).astype(BF16), jnp.asarray(w_pos, F32).astype(BF16),
            jnp.asarray(twc, F32), jnp.asarray(tws, F32))


def _run_trunk(x, mod, row_off, per_batch_mod, lru0, mlstm0, p, cfg):
    b, seq, d = x.shape
    m = b * seq
    rows_per_mod = seq if per_batch_mod else m
    x2d = x.reshape(m, d)
    dft_c, w_pos, twc, tws = _dft_tables(seq)

    mod0 = mod[0].reshape(MOD_ROWS, 1, 3 * d)
    xa, za, zb, xc, xs = _inproj_e(x2d, mod0, p["g_pre"][0:1], p["w_in_e"], dft_c, row_off, rows_per_mod,
                                   cfg["tm_in"])
    r3 = lambda a: a.reshape(b, seq, a.shape[-1])
    ya, lru_f = _rglru(r3(xa), r3(za), lru0, p["conv_w"], p["conv_b"], p["wg"], p["bg"], p["lam"], cfg["tc_lru"],
                       cfg["bb_lru"])
    yb = _fourier(r3(xc), r3(xs), r3(zb), twc, tws, w_pos, p["w_four"], p["b_four"], cfg["w_four"], cfg["tr_four"])
    x1 = _outproj([ya.reshape(m, d), yb.reshape(m, d)], [p["w_out_e"][0:d], p["w_out_e"][d:2 * d]],
                  x2d, mod0, p["g_post"][0:1], row_off, rows_per_mod, cfg["tm_out"])

    mod1 = mod[1].reshape(MOD_ROWS, 1, 3 * d)
    q, k, v, o, z, gpt = _inproj_o(x1, mod1, p["g_pre"][1:2], p["w_in_o"], p["w_gate"], p["gate_bias"],
                                   row_off, rows_per_mod, cfg["tm_in_o"])
    gq = gpt[0:4 * N_HEADS].reshape(4, N_HEADS, b, seq).transpose(0, 2, 1, 3).reshape(4, b * N_HEADS, seq)
    gates = _gate_prep(gq, cfg["rb_gate"]).reshape(SUBLANES, b, N_HEADS, seq).transpose(1, 2, 0, 3)
    y, c_f, n_f, m_f = _mlstm(r3(q), r3(k), r3(v), r3(o), r3(z), gates, mlstm0, cfg["heads_mlstm"])
    x2 = _outproj([y.reshape(m, 2 * d)], [p["w_out_o"]], x1, mod1, p["g_post"][1:2], row_off, rows_per_mod,
                  cfg["tm_out"])
    return x2.reshape(b, seq, d), lru_f, c_f, n_f, m_f


def _prepare_params(conv_w, conv_b, w_rg, b_rg, w_ig, b_ig, lru_lambda, w_four, b_four, w_in_e, w_out_e,
                    w_in_o, b_if, w_out_o, g_pre, g_post):
    d = D_MODEL
    wg = jnp.concatenate([w_rg[0], w_ig[0]], axis=-1).astype(BF16)
    bg = jnp.concatenate([b_rg[0].reshape(2, N_HEADS, 1, BLK), b_ig[0].reshape(2, N_HEADS, 1, BLK)], axis=-1)
    n_gate = 4 * N_HEADS
    w_gate = jnp.zeros((d, LANES), F32).at[:, 0:n_gate].set(w_in_o[0][:, 8 * d:8 * d + n_gate]).astype(BF16)
    bias = jnp.zeros((LANES,), F32).at[0:n_gate].set(b_if[0].reshape(n_gate))
    gate_bias = jnp.broadcast_to(bias[:, None], (LANES, LANES))
    return dict(
        g_pre=g_pre, g_post=g_post,
        w_in_e=w_in_e[0].astype(BF16), w_out_e=w_out_e[0].astype(BF16),
        conv_w=conv_w[0], conv_b=conv_b[0].reshape(1, d), wg=wg, bg=bg, lam=lru_lambda[0].reshape(2, 1, d),
        w_four=w_four[0].astype(BF16), b_four=b_four[0].reshape(1, d),
        w_in_o=w_in_o[0][:, 0:8 * d].astype(BF16), w_gate=w_gate, gate_bias=gate_bias,
        w_out_o=w_out_o[0].astype(BF16))


def kernel(x_prompt, x_sample, c, state_lru, state_mlstm_C, state_mlstm_n, state_mlstm_m, c_ctx, w_mod, b_mod,
           g_pre, g_post, w_in_e, conv_w, conv_b, w_rg, b_rg, w_ig, b_ig, lru_lambda, w_four, b_four, w_out_e,
           w_in_o, b_if, w_out_o):
    d = D_MODEL
    bp = x_prompt.shape[0]
    bs = x_sample.shape[0]
    p = _prepare_params(conv_w, conv_b, w_rg, b_rg, w_ig, b_ig, lru_lambda, w_four, b_four, w_in_e, w_out_e,
                        w_in_o, b_if, w_out_o, g_pre, g_post)
    cond = jnp.concatenate([c_ctx[None, :], c, jnp.zeros((MOD_ROWS - 1 - bs, d), F32)], axis=0)
    mod = _modulation(cond, w_mod, b_mod)

    cfg_p = dict(tm_in=512, tm_in_o=256, tm_out=512, tc_lru=128, bb_lru=4 if bp % 4 == 0 else 1, w_four=d,
                 tr_four=256, rb_gate=min(bp * N_HEADS, 128), heads_mlstm=2)
    y_prompt, lru_f, c_f, n_f, m_f = _run_trunk(x_prompt, mod, 0, False, jnp.zeros((bp, 2, d), F32), None, p, cfg_p)

    cfg_s = dict(tm_in=512, tm_in_o=256, tm_out=512, tc_lru=128, bb_lru=1, w_four=BLK, tr_four=256,
                 rb_gate=SUBLANES, heads_mlstm=1)
    mlstm0 = (state_mlstm_C[:, 0], state_mlstm_n[:, 0].reshape(bs, 2, N_HEADS, 1, BLK),
              jnp.broadcast_to(state_mlstm_m[:, 0].reshape(bs, 2, N_HEADS, 1, 1), (bs, 2, N_HEADS, 1, CHUNK)))
    y_sample, _, _, _, _ = _run_trunk(x_sample, mod, 1, True, state_lru[:, 0], mlstm0, p, cfg_s)

    return (y_prompt, y_sample, lru_f[:, None], c_f[:, None], n_f[:, None, :, :, 0, :], m_f[:, None, :, :, 0, 0])
```

```python
import functools
import math

import numpy as np
import jax
import jax.numpy as jnp
from jax import lax
from jax.experimental import pallas as pl
from jax.experimental.pallas import tpu as pltpu

F32 = jnp.float32
BF16 = jnp.bfloat16

D_MODEL = 1024
DEPTH = 2
EPS = 1e-6
LRU_C = 8.0
CONV_W = 4
N_HEADS = 8
BLK = D_MODEL // N_HEADS
DV = 2 * BLK
CHUNK = 128
FFT_L2 = 256
MOD_ROWS = 8

LANES = 128
SUBLANES = 8
VMEM_LIMIT_CAP = 56 * 1024 * 1024


def _cparams(sem, vmem_bytes, flags=None):
    return pltpu.CompilerParams(dimension_semantics=sem, flags=flags,
                                vmem_limit_bytes=int(min(max(vmem_bytes, 16 * 1024 * 1024), VMEM_LIMIT_CAP)))


def _silu(x):
    return x * jax.nn.sigmoid(x)


def _mod_kernel(cond_ref, w_ref, b_ref, o_ref):
    s = _silu(cond_ref[...]).astype(BF16)
    o_ref[0] = jnp.dot(s, w_ref[0].astype(BF16), preferred_element_type=F32) + b_ref[0]


def _modulation(cond, w_mod, b_mod):
    d = D_MODEL
    return pl.pallas_call(
        _mod_kernel,
        grid=(DEPTH, 3),
        in_specs=[pl.BlockSpec((MOD_ROWS, d), lambda l, j: (0, 0)),
                  pl.BlockSpec((1, d, d), lambda l, j: (l, 0, j)),
                  pl.BlockSpec((1, 1, d), lambda l, j: (l, 0, j))],
        out_specs=pl.BlockSpec((1, MOD_ROWS, d), lambda l, j: (l, 0, j)),
        out_shape=jax.ShapeDtypeStruct((DEPTH, MOD_ROWS, 3 * d), F32),
        compiler_params=_cparams(("arbitrary", "arbitrary"), 24 * 1024 * 1024),
        name="adaln_mod",
    )(cond, w_mod, b_mod.reshape(DEPTH, 1, 3 * d))


def _normed_input(x_ref, mod_ref, g_ref):
    d = D_MODEL
    x = x_ref[...]
    y = x * lax.rsqrt(jnp.mean(x * x, axis=-1, keepdims=True) + EPS) * g_ref[...]
    shift = mod_ref[0, :, 0:d]
    scale = mod_ref[0, :, d:2 * d]
    return (y * (1.0 + scale) + shift).astype(BF16)


def _mod_index_map(row_off, tiles_per_mod):
    return lambda i: (row_off + i // tiles_per_mod, 0, 0)


def _inproj_e_kernel(x_ref, mod_ref, g_ref, w_ref, dft_ref, xa_ref, za_ref, zb_ref, xc_ref, xs_ref):
    d = D_MODEL
    hb = _normed_input(x_ref, mod_ref, g_ref)
    xa_ref[...] = jnp.dot(hb, w_ref[:, 0:d], preferred_element_type=F32)
    za_ref[...] = jnp.dot(hb, w_ref[:, d:2 * d], preferred_element_type=F32)
    zb_ref[...] = jnp.dot(hb, w_ref[:, 3 * d:4 * d], preferred_element_type=F32)
    xb = jnp.dot(hb, w_ref[:, 2 * d:3 * d], preferred_element_type=F32).astype(BF16)
    for g in range(N_HEADS):
        cs = jnp.dot(xb[:, g * BLK:(g + 1) * BLK], dft_ref[...], preferred_element_type=F32)
        xc_ref[:, g * BLK:(g + 1) * BLK] = cs[:, 0:BLK]
        xs_ref[:, g * BLK:(g + 1) * BLK] = cs[:, BLK:2 * BLK]


def _inproj_e(x2d, mod_l, g_pre, w_in, dft_c, row_off, rows_per_mod, tm):
    m, d = x2d.shape
    row = lambda i: (i, 0)
    const = lambda i: (0, 0)
    out = jax.ShapeDtypeStruct((m, d), F32)
    vmem = 2 * (w_in.size * 2 + tm * d * 4 * 6) + 8 * tm * d * 4
    return pl.pallas_call(
        _inproj_e_kernel,
        grid=(m // tm,),
        in_specs=[pl.BlockSpec((tm, d), row),
                  pl.BlockSpec((1, 1, 3 * d), _mod_index_map(row_off, rows_per_mod // tm)),
                  pl.BlockSpec((1, d), const),
                  pl.BlockSpec(w_in.shape, const),
                  pl.BlockSpec(dft_c.shape, const)],
        out_specs=[pl.BlockSpec((tm, d), row)] * 5,
        out_shape=[out] * 5,
        compiler_params=_cparams(("parallel",), vmem),
        name="inproj_even",
    )(x2d, mod_l, g_pre, w_in, dft_c)


def _scan_chunk(a, u, reverse):
    t = a.shape[0]
    row = lax.broadcasted_iota(jnp.int32, a.shape, 0)
    d = 1
    while d < t:
        if d < SUBLANES:
            shift = (t - d) if reverse else d
            a_sh = pltpu.roll(a, shift, axis=0)
            u_sh = pltpu.roll(u, shift, axis=0)
            valid = (row < t - d) if reverse else (row >= d)
            a_sh = jnp.where(valid, a_sh, 1.0)
            u_sh = jnp.where(valid, u_sh, 0.0)
        else:
            ones = jnp.ones((d, a.shape[1]), F32)
            zeros = jnp.zeros((d, a.shape[1]), F32)
            if reverse:
                a_sh = jnp.concatenate([a[d:], ones], axis=0)
                u_sh = jnp.concatenate([u[d:], zeros], axis=0)
            else:
                a_sh = jnp.concatenate([ones, a[:t - d]], axis=0)
                u_sh = jnp.concatenate([zeros, u[:t - d]], axis=0)
        u = a * u_sh + u
        a = a * a_sh
        d *= 2
    return a, u


def _rglru_kernel(xa_ref, za_ref, cw_ref, cb_ref, wg_ref, bg_ref, lam_ref, h0_ref,
                  y_ref, hT_ref, hf_ref, hb_ref, *, seq, tc, bb):
    nchunks = seq // tc
    groups = tc // SUBLANES
    sub = lax.broadcasted_iota(jnp.int32, (SUBLANES, LANES), 0)

    def row_bcast(block, r):
        return jnp.broadcast_to(block[r:r + 1, :], (SUBLANES, LANES))

    def conv_chunk(bi, c):
        t0 = pl.multiple_of(c * tc, tc)
        xs = [xa_ref[bi, pl.ds(t0 + g, SUBLANES, stride=groups), :] for g in range(groups)]
        prev = xa_ref[bi, pl.ds(pl.multiple_of(jnp.maximum(t0 - SUBLANES, 0), SUBLANES), SUBLANES), :]
        nxt = xa_ref[bi, pl.ds(pl.multiple_of(jnp.minimum(t0 + tc, seq - SUBLANES), SUBLANES), SUBLANES), :]
        prev = jnp.where(c > 0, prev, 0.0)
        nxt = jnp.where(c < nchunks - 1, nxt, 0.0)
        before2 = jnp.where(sub == 0, row_bcast(prev, SUBLANES - 2), pltpu.roll(xs[groups - 2], 1, axis=0))
        before1 = jnp.where(sub == 0, row_bcast(prev, SUBLANES - 1), pltpu.roll(xs[groups - 1], 1, axis=0))
        after1 = jnp.where(sub == SUBLANES - 1, row_bcast(nxt, 0), pltpu.roll(xs[0], SUBLANES - 1, axis=0))
        ext = [before2, before1] + xs + [after1]
        out = []
        for g in range(groups):
            acc = cb_ref[...] + cw_ref[0:1, :] * ext[g]
            for j in range(1, CONV_W):
                acc = acc + cw_ref[j:j + 1, :] * ext[g + j]
            out.append(acc)
        return t0, jnp.concatenate(out, axis=0)

    def gates(xc, direction):
        g = jnp.dot(xc.astype(BF16), wg_ref[direction, 0], preferred_element_type=F32) + bg_ref[direction, 0]
        r = jax.nn.sigmoid(g[:, 0:BLK])
        i = jax.nn.sigmoid(g[:, BLK:2 * BLK])
        lam = lam_ref[direction]
        softplus_neg = jnp.maximum(-lam, 0.0) + jnp.log1p(jnp.exp(-jnp.abs(lam)))
        neg_log_a = LRU_C * r * softplus_neg
        a = jnp.exp(-neg_log_a)
        om = jnp.tanh(neg_log_a) * (1.0 + a * a)
        root = jnp.where(om > 0.0, om * lax.rsqrt(om), 0.0)
        return a, root * (i * xc)

    def scan_dir(bi, c, direction, carry, out_ref):
        reverse = direction == 1
        t0, xc = conv_chunk(bi, c)
        a, u = gates(xc, direction)
        order = range(groups - 1, -1, -1) if reverse else range(groups)
        p_g, s_g = [None] * groups, [None] * groups
        p_run = s_run = None
        for g in order:
            a_v = a[g * SUBLANES:(g + 1) * SUBLANES]
            u_v = u[g * SUBLANES:(g + 1) * SUBLANES]
            if p_run is None:
                p_run, s_run = a_v, u_v
            else:
                s_run = a_v * s_run + u_v
                p_run = a_v * p_run
            p_g[g], s_g[g] = p_run, s_run
        pp, ss = _scan_chunk(p_run, s_run, reverse)
        carry_b = jnp.broadcast_to(carry, (SUBLANES, LANES))
        seg_out = pp * carry_b + ss
        if reverse:
            h_in = jnp.where(sub == SUBLANES - 1, carry_b, pltpu.roll(seg_out, SUBLANES - 1, axis=0))
            new_carry = seg_out[0:1, :]
        else:
            h_in = jnp.where(sub == 0, carry_b, pltpu.roll(seg_out, 1, axis=0))
            new_carry = seg_out[SUBLANES - 1:SUBLANES, :]
        for g in range(groups):
            out_ref[bi, pl.ds(t0 + g, SUBLANES, stride=groups), :] = p_g[g] * h_in + s_g[g]
        return new_carry

    def body(j, carries):
        return tuple((scan_dir(bi, j, 0, carries[bi][0], hf_ref),
                      scan_dir(bi, nchunks - 1 - j, 1, carries[bi][1], hb_ref)) for bi in range(bb))

    init = tuple((h0_ref[bi, 0:1, :], h0_ref[bi, 1:2, :]) for bi in range(bb))
    if nchunks <= 2:
        finals = init
        for j in range(nchunks):
            finals = body(j, finals)
    else:
        finals = lax.fori_loop(0, nchunks, body, init, unroll=2)
    for bi in range(bb):
        hT_ref[bi, 0:1, :] = finals[bi][0]
        hT_ref[bi, 1:2, :] = finals[bi][1]

    def gate_out(c, carry):
        rows = pl.ds(pl.multiple_of(c * tc, tc), tc)
        for bi in range(bb):
            y_ref[bi, rows, :] = ((hf_ref[bi, rows, :] + hb_ref[bi, rows, :])
                                  * _silu(za_ref[bi, rows, :])).astype(y_ref.dtype)
        return carry

    lax.fori_loop(0, nchunks, gate_out, 0)


def _rglru(xa, za, h0, conv_w, conv_b, wg, bg, lam, tc, bb):
    b, seq, d = xa.shape
    blk = lambda i, j: (i, 0, j)
    vmem = bb * (2 * (2 * seq * BLK * 4 + seq * BLK * 2) + 2 * seq * BLK * 4) + 16 * 1024 * 1024
    return pl.pallas_call(
        functools.partial(_rglru_kernel, seq=seq, tc=tc, bb=bb),
        grid=(b // bb, d // BLK),
        in_specs=[pl.BlockSpec((bb, seq, BLK), blk),
                  pl.BlockSpec((bb, seq, BLK), blk),
                  pl.BlockSpec((CONV_W, BLK), lambda i, j: (0, j)),
                  pl.BlockSpec((1, BLK), lambda i, j: (0, j)),
                  pl.BlockSpec((2, 1, BLK, 2 * BLK), lambda i, j: (0, j, 0, 0)),
                  pl.BlockSpec((2, 1, 1, 2 * BLK), lambda i, j: (0, j, 0, 0)),
                  pl.BlockSpec((2, 1, BLK), lambda i, j: (0, 0, j)),
                  pl.BlockSpec((bb, 2, BLK), blk)],
        out_specs=[pl.BlockSpec((bb, seq, BLK), blk),
                   pl.BlockSpec((bb, 2, BLK), blk)],
        out_shape=[jax.ShapeDtypeStruct((b, seq, d), BF16),
                   jax.ShapeDtypeStruct((b, 2, d), F32)],
        scratch_shapes=[pltpu.VMEM((bb, seq, BLK), F32), pltpu.VMEM((bb, seq, BLK), F32)],
        compiler_params=_cparams(("parallel", "parallel"), vmem),
        name="rglru_scan",
    )(xa, za, conv_w, conv_b, wg, bg, lam, h0)


def _fft_list(xs):
    n = len(xs)
    if n == 1:
        return xs
    even = _fft_list(xs[0::2])
    odd = _fft_list(xs[1::2])
    out = [None] * n
    for k in range(n // 2):
        o_re, o_im = odd[k]
        if k == 0:
            t_re, t_im = o_re, o_im
        elif 4 * k == n:
            t_re, t_im = o_im, -o_re
        else:
            ang = -2.0 * math.pi * k / n
            wr, wi = math.cos(ang), math.sin(ang)
            t_re = o_re * wr - o_im * wi
            t_im = o_re * wi + o_im * wr
        e_re, e_im = even[k]
        out[k] = (e_re + t_re, e_im + t_im)
        out[k + n // 2] = (e_re - t_re, e_im - t_im)
    return out


def _fourier_kernel(xc_ref, xs_ref, zb_ref, twc_ref, tws_ref, wpos_ref, wf_ref, bf_ref, y_ref, *scratch,
                    seq, tr):
    w = xc_ref.shape[2]
    nblk = w // BLK
    l1 = seq // FFT_L2

    def epilogue(fr, rows):
        parts = []
        for kb in range(nblk):
            yb = jnp.dot(fr[:, kb * BLK:(kb + 1) * BLK].astype(BF16), wf_ref[kb], preferred_element_type=F32)
            parts.append(yb + bf_ref[:, kb * BLK:(kb + 1) * BLK])
        yb = parts[0] if nblk == 1 else jnp.concatenate(parts, axis=1)
        y_ref[0, rows, :] = (yb * _silu(zb_ref[0, rows, :])).astype(y_ref.dtype)

    if l1 == 1:
        fr = (jnp.dot(wpos_ref[:, 0:FFT_L2], xc_ref[0].astype(BF16), preferred_element_type=F32)
              + jnp.dot(wpos_ref[:, FFT_L2:2 * FFT_L2], xs_ref[0].astype(BF16), preferred_element_type=F32))
        epilogue(fr, pl.ds(0, seq))
        return

    b_ref, fr_ref = scratch
    assert w == LANES

    def butterfly(r, carry):
        r0 = pl.multiple_of(r * SUBLANES, SUBLANES)
        zs = [(xc_ref[0, pl.ds(n1 * FFT_L2 + r0, SUBLANES), :], xs_ref[0, pl.ds(n1 * FFT_L2 + r0, SUBLANES), :])
              for n1 in range(l1)]
        for k1, (a_re, a_im) in enumerate(_fft_list(zs)):
            if k1 == 0:
                b_re, b_im = a_re, a_im
            else:
                tc_ = twc_ref[k1, pl.ds(r0, SUBLANES), :]
                ts_ = tws_ref[k1, pl.ds(r0, SUBLANES), :]
                b_re = a_re * tc_ + a_im * ts_
                b_im = a_im * tc_ - a_re * ts_
            b_ref[pl.ds(r0, SUBLANES), k1 * LANES:(k1 + 1) * LANES] = b_re
            b_ref[pl.ds(FFT_L2 + r0, SUBLANES), k1 * LANES:(k1 + 1) * LANES] = b_im
        return carry

    lax.fori_loop(0, FFT_L2 // SUBLANES, butterfly, 0)

    per_dot = min(4, l1)
    for nb in range(l1 // per_dot):
        cols = slice(nb * per_dot * LANES, (nb + 1) * per_dot * LANES)
        fr = jnp.dot(wpos_ref[...], b_ref[:, cols].astype(BF16), preferred_element_type=F32)
        for kk in range(per_dot):
            fr_ref[pl.ds(nb * per_dot + kk, FFT_L2, stride=l1), :] = fr[:, kk * LANES:(kk + 1) * LANES]

    def finish(i, carry):
        rows = pl.ds(pl.multiple_of(i * tr, tr), tr)
        epilogue(fr_ref[rows, :], rows)
        return carry

    lax.fori_loop(0, seq // tr, finish, 0, unroll=2)


def _fourier(xc, xs, zb, twc, tws, w_pos, w_four, b_four, w_blk, tr):
    b, seq, d = xc.shape
    nblk = w_blk // BLK
    l1 = seq // FFT_L2
    blk = lambda i, j: (i, 0, j)
    scratch = []
    if l1 > 1:
        scratch = [pltpu.VMEM((2 * FFT_L2, l1 * w_blk), F32), pltpu.VMEM((seq, w_blk), F32)]
    vmem = (2 * (3 * seq * w_blk * 4 + seq * w_blk * 2 + 2 * twc.size * 4) + 3 * seq * w_blk * 4
            + 16 * 1024 * 1024)
    return pl.pallas_call(
        functools.partial(_fourier_kernel, seq=seq, tr=tr),
        grid=(b, d // w_blk),
        in_specs=[pl.BlockSpec((1, seq, w_blk), blk),
                  pl.BlockSpec((1, seq, w_blk), blk),
                  pl.BlockSpec((1, seq, w_blk), blk),
                  pl.BlockSpec(twc.shape, lambda i, j: (0, 0, 0)),
                  pl.BlockSpec(tws.shape, lambda i, j: (0, 0, 0)),
                  pl.BlockSpec(w_pos.shape, lambda i, j: (0, 0)),
                  pl.BlockSpec((nblk, BLK, BLK), lambda i, j: (j, 0, 0)),
                  pl.BlockSpec((1, w_blk), lambda i, j: (0, j))],
        out_specs=pl.BlockSpec((1, seq, w_blk), blk),
        out_shape=jax.ShapeDtypeStruct((b, seq, d), BF16),
        scratch_shapes=scratch,
        compiler_params=_cparams(("parallel", "parallel"), vmem),
        name="fourier_mix",
    )(xc, xs, zb, twc, tws, w_pos, w_four, b_four)


def _outproj_kernel(*refs, n_in):
    y_refs = refs[:n_in]
    w_refs = refs[n_in:2 * n_in]
    x_ref, mod_ref, g_ref, o_ref = refs[2 * n_in:]
    d = D_MODEL
    y = jnp.dot(y_refs[0][...], w_refs[0][...], preferred_element_type=F32)
    for k in range(1, n_in):
        y = y + jnp.dot(y_refs[k][...], w_refs[k][...], preferred_element_type=F32)
    yn = y * lax.rsqrt(jnp.mean(y * y, axis=-1, keepdims=True) + EPS) * g_ref[...]
    o_ref[...] = x_ref[...] + mod_ref[0, :, 2 * d:3 * d] * yn


def _outproj(ys, ws, x2d, mod_l, g_post, row_off, rows_per_mod, tm):
    m, d = x2d.shape
    n_in = len(ys)
    row = lambda i: (i, 0)
    const = lambda i: (0, 0)
    vmem = 2 * (sum(wk.size for wk in ws) * 2 + sum(tm * yk.shape[1] for yk in ys) * 2 + 2 * tm * d * 4) + 4 * tm * d * 4
    return pl.pallas_call(
        functools.partial(_outproj_kernel, n_in=n_in),
        grid=(m // tm,),
        in_specs=([pl.BlockSpec((tm, yk.shape[1]), row) for yk in ys]
                  + [pl.BlockSpec(wk.shape, const) for wk in ws]
                  + [pl.BlockSpec((tm, d), row),
                     pl.BlockSpec((1, 1, 3 * d), _mod_index_map(row_off, rows_per_mod // tm)),
                     pl.BlockSpec((1, d), const)]),
        out_specs=pl.BlockSpec((tm, d), row),
        out_shape=jax.ShapeDtypeStruct((m, d), F32),
        compiler_params=_cparams(("parallel",), vmem),
        name="outproj_residual",
    )(*ys, *ws, x2d, mod_l, g_post)


def _inproj_o_kernel(x_ref, mod_ref, g_ref, w_ref, wgt_ref, gb_ref, q_ref, k_ref, v_ref, o_ref, z_ref, gpt_ref,
                     gp_ref):
    d = D_MODEL
    hb = _normed_input(x_ref, mod_ref, g_ref)
    q = jnp.dot(hb, w_ref[:, 0:d], preferred_element_type=F32)
    q_ref[...] = (q * (BLK ** -0.5)).astype(q_ref.dtype)
    k_ref[...] = jnp.dot(hb, w_ref[:, d:2 * d], preferred_element_type=F32).astype(k_ref.dtype)
    for half in range(2):
        cols = slice(half * d, (half + 1) * d)
        v_ref[:, cols] = jnp.dot(hb, w_ref[:, 2 * d + half * d:3 * d + half * d],
                                 preferred_element_type=F32).astype(v_ref.dtype)
        o_ref[:, cols] = jnp.dot(hb, w_ref[:, 4 * d + half * d:5 * d + half * d], preferred_element_type=F32)
        z_ref[:, cols] = jnp.dot(hb, w_ref[:, 6 * d + half * d:7 * d + half * d], preferred_element_type=F32)
    gp_ref[...] = jnp.dot(hb, wgt_ref[...], preferred_element_type=F32)
    gpt_ref[...] = gp_ref[...].T + gb_ref[:, 0:1]


def _inproj_o(x2d, mod_l, g_pre, w_main, w_gate, gate_bias, row_off, rows_per_mod, tm):
    m, d = x2d.shape
    row = lambda i: (i, 0)
    const = lambda i: (0, 0)
    ng = w_gate.shape[1]
    vmem = 2 * (w_main.size * 2 + tm * d * 4 + tm * d * 2 * 4 + tm * 2 * d * 4 * 2) + 6 * tm * d * 4
    return pl.pallas_call(
        _inproj_o_kernel,
        grid=(m // tm,),
        in_specs=[pl.BlockSpec((tm, d), row),
                  pl.BlockSpec((1, 1, 3 * d), _mod_index_map(row_off, rows_per_mod // tm)),
                  pl.BlockSpec((1, d), const),
                  pl.BlockSpec(w_main.shape, const),
                  pl.BlockSpec(w_gate.shape, const),
                  pl.BlockSpec(gate_bias.shape, const)],
        out_specs=[pl.BlockSpec((tm, d), row), pl.BlockSpec((tm, d), row),
                   pl.BlockSpec((tm, 2 * d), row), pl.BlockSpec((tm, 2 * d), row), pl.BlockSpec((tm, 2 * d), row),
                   pl.BlockSpec((ng, tm), lambda i: (0, i))],
        out_shape=[jax.ShapeDtypeStruct((m, d), BF16), jax.ShapeDtypeStruct((m, d), BF16),
                   jax.ShapeDtypeStruct((m, 2 * d), BF16), jax.ShapeDtypeStruct((m, 2 * d), F32),
                   jax.ShapeDtypeStruct((m, 2 * d), F32), jax.ShapeDtypeStruct((ng, m), F32)],
        scratch_shapes=[pltpu.VMEM((tm, ng), F32)],
        compiler_params=_cparams(("parallel",), vmem),
        name="inproj_odd",
    )(x2d, mod_l, g_pre, w_main, w_gate, gate_bias)


def _chunk_scan_lanes(v, op, identity, reverse):
    n = v.shape[-1]
    pos = lax.broadcasted_iota(jnp.int32, v.shape, 1) & (CHUNK - 1)
    d = 1
    while d < CHUNK:
        if reverse:
            shifted = jnp.where(pos < CHUNK - d, pltpu.roll(v, n - d, axis=1), identity)
        else:
            shifted = jnp.where(pos >= d, pltpu.roll(v, d, axis=1), identity)
        v = op(v, shifted)
        d *= 2
    return v


def _gate_prep_kernel(g_ref, o_ref):
    for direction in range(2):
        reverse = direction == 1
        i_pre = g_ref[2 * direction]
        b = _chunk_scan_lanes(jax.nn.log_sigmoid(g_ref[2 * direction + 1]), jnp.add, 0.0, reverse)
        a = i_pre - b
        o_ref[3 * direction] = a
        o_ref[3 * direction + 1] = b
        o_ref[3 * direction + 2] = _chunk_scan_lanes(a, jnp.maximum, -jnp.inf, reverse)
    o_ref[6] = jnp.zeros_like(g_ref[0])
    o_ref[7] = jnp.zeros_like(g_ref[0])


def _gate_prep(gq, rb):
    _, r, seq = gq.shape
    return pl.pallas_call(
        _gate_prep_kernel,
        grid=(r // rb,),
        in_specs=[pl.BlockSpec((4, rb, seq), lambda i: (0, i, 0))],
        out_specs=pl.BlockSpec((SUBLANES, rb, seq), lambda i: (0, i, 0)),
        out_shape=jax.ShapeDtypeStruct((SUBLANES, r, seq), F32),
        compiler_params=_cparams(("parallel",), 24 * 1024 * 1024),
        name="mlstm_gate_prep",
    )(gq)


def _mlstm_kernel(*refs, seq, heads, zero_init):
    q_ref, k_ref, v_ref, o_ref, z_ref, gate_ref = refs[0:6]
    refs = refs[6:]
    if not zero_init:
        c0_ref, n0_ref, m0_ref = refs[0:3]
        refs = refs[3:]
    y_ref, c_out_ref, n_out_ref, m_out_ref, hs_ref, cols_ref = refs
    t = CHUNK
    nchunks = seq // t
    half = nchunks // 2
    assert nchunks == 2 * half
    streams = [(hh, dr) for hh in range(heads) for dr in range(2)]

    def head_lanes(hh, width):
        return slice(hh * width, (hh + 1) * width)
    row_i = lax.broadcasted_iota(jnp.int32, (t, t), 0)
    col_i = lax.broadcasted_iota(jnp.int32, (t, t), 1)
    ones_blk = jnp.ones((t, LANES), BF16)

    sel_r = lax.broadcasted_iota(jnp.int32, (4 * SUBLANES, 4 * LANES), 0)
    sel_c = lax.broadcasted_iota(jnp.int32, (4 * SUBLANES, 4 * LANES), 1) // LANES
    wanted = jnp.where(sel_c == 0, 2, jnp.where(sel_c == 1, 1, jnp.where(sel_c == 2, 5, 4)))
    selector = jnp.where(((sel_r & (SUBLANES - 1)) == wanted) & (sel_r < 3 * SUBLANES), 1.0, 0.0).astype(BF16)

    piece = min(seq, 4 * t)
    for hh in range(heads):
        for c in range(seq // piece):
            g = gate_ref[0, hh, :, c * piece:(c + 1) * piece]
            hi = g.astype(BF16).astype(F32)
            mid = (g - hi).astype(BF16).astype(F32)
            lo = g - hi - mid
            stack = jnp.concatenate([hi, mid, lo, jnp.zeros_like(g)], axis=0).astype(BF16)
            cols_ref[hh, c * piece:(c + 1) * piece, :] = lax.dot_general(
                stack, selector, (((0,), (0,)), ((), ())), preferred_element_type=F32)

    def chunk_of(step, direction):
        return (nchunks - 1 - step) if direction == 1 else step

    def gate_terms(rows, hh, direction):
        g_rows = gate_ref[0, hh, :, rows]
        last = 0 if direction == 1 else t - 1
        a_row = g_rows[3 * direction:3 * direction + 1, :]
        b_end = g_rows[3 * direction + 1:3 * direction + 2, last:last + 1]
        a_end = g_rows[3 * direction + 2:3 * direction + 3, last:last + 1]
        amax_rep = cols_ref[hh, rows, 2 * direction * LANES:(2 * direction + 1) * LANES]
        return a_row, b_end, a_end, amax_rep

    def scores(step, stream, m_rep):
        hh, direction = stream
        rows = pl.ds(pl.multiple_of(chunk_of(step, direction) * t, t), t)
        a_row, b_end, a_end, amax_rep = gate_terms(rows, hh, direction)
        m_prev = m_rep[:, 0:1]
        mx = jnp.maximum(m_prev, amax_rep)
        mask = (col_i >= row_i) if direction == 1 else (col_i <= row_i)
        e = jnp.where(mask, jnp.exp(a_row - mx), 0.0)
        s = lax.dot_general(q_ref[0, rows, head_lanes(hh, BLK)], k_ref[0, rows, head_lanes(hh, BLK)],
                            (((1,), (1,)), ((), ())), preferred_element_type=F32) * e
        m_next = jnp.broadcast_to(b_end + jnp.maximum(m_prev, a_end), (1, t))
        return (s.astype(BF16), jnp.sum(s, axis=-1, keepdims=True), m_rep), m_next

    def outputs(step, stream, memory, pending):
        hh, direction = stream
        c_mat, n_rep = memory
        s_b, s_sum, m_rep = pending
        rows = pl.ds(pl.multiple_of(chunk_of(step, direction) * t, t), t)
        qc = q_ref[0, rows, head_lanes(hh, BLK)]
        kc = k_ref[0, rows, head_lanes(hh, BLK)]
        vc = v_ref[0, rows, head_lanes(hh, DV)]
        a_row, _, a_end, amax_rep = gate_terms(rows, hh, direction)
        b_rep = cols_ref[hh, rows, (2 * direction + 1) * LANES:(2 * direction + 2) * LANES]
        m_prev = m_rep[:, 0:1]
        mx_end = jnp.maximum(m_prev, a_end)
        mx = jnp.maximum(m_prev, amax_rep)
        sv = jnp.dot(s_b, vc, preferred_element_type=F32)
        qcn = jnp.dot(qc, jnp.concatenate([c_mat.astype(BF16), n_rep.astype(BF16)], axis=1),
                      preferred_element_type=F32)
        wkk_t = (kc.astype(F32).T * jnp.exp(a_row - mx_end)).astype(BF16)
        upd = jnp.dot(wkk_t, jnp.concatenate([vc, ones_blk], axis=1), preferred_element_type=F32)
        decay = jnp.exp(m_prev - mx_end)
        w_inter = jnp.exp(m_prev - mx)
        den = s_sum + w_inter * qcn[:, DV:DV + LANES]
        inv = 1.0 / jnp.maximum(jnp.abs(den), jnp.exp(-(b_rep + mx)))
        hc = jnp.concatenate(
            [(sv[:, kb * LANES:(kb + 1) * LANES] + w_inter * qcn[:, kb * LANES:(kb + 1) * LANES]) * inv
             for kb in range(DV // LANES)], axis=1)
        return rows, hc, (decay * c_mat + upd[:, 0:DV], decay * n_rep + upd[:, DV:DV + LANES])

    def n_to_columns(n_row):
        return jnp.broadcast_to(n_row, (LANES, BLK)).T

    def n_to_row(n_rep):
        return n_rep.T[0:1, :]

    def emit(hh, rows, hc, first_visit):
        if first_visit:
            hs_ref[hh, rows, :] = hc
        else:
            hsum = hs_ref[hh, rows, :] + hc
            lanes = head_lanes(hh, DV)
            y_ref[0, rows, lanes] = (jax.nn.sigmoid(o_ref[0, rows, lanes]) * hsum
                                     * _silu(z_ref[0, rows, lanes])).astype(y_ref.dtype)

    def pipelined(step, carry, first_visit, with_next):
        memories, pendings, m_reps = carry
        nxt = [scores(step + 1, sm, m_reps[i]) for i, sm in enumerate(streams)] if with_next else None
        done = [outputs(step, sm, memories[i], pendings[i]) for i, sm in enumerate(streams)]
        for (hh, _), (rows, hc, _) in zip(streams, done):
            emit(hh, rows, hc, first_visit)
        new_mem = tuple(dn[2] for dn in done)
        if with_next:
            return new_mem, tuple(n[0] for n in nxt), tuple(n[1] for n in nxt)
        return new_mem, pendings, m_reps

    if zero_init:
        memories = tuple((jnp.zeros((BLK, DV), F32), jnp.zeros((BLK, LANES), F32)) for _ in streams)
        first = [scores(0, sm, jnp.zeros((1, t), F32)) for sm in streams]
    else:
        memories = tuple((c0_ref[0, dr, hh], n_to_columns(n0_ref[0, dr, hh])) for hh, dr in streams)
        first = [scores(0, (hh, dr), m0_ref[0, dr, hh]) for hh, dr in streams]
    carry = (memories, tuple(f[0] for f in first), tuple(f[1] for f in first))
    if nchunks <= 2:
        for step in range(nchunks - 1):
            carry = pipelined(step, carry, first_visit=step < half, with_next=True)
    else:
        unroll = 2 if half % 2 == 0 else 1
        carry = lax.fori_loop(0, half, functools.partial(pipelined, first_visit=True, with_next=True), carry,
                              unroll=unroll)
        carry = lax.fori_loop(half, nchunks - 1, functools.partial(pipelined, first_visit=False, with_next=True),
                              carry, unroll=unroll)
    memories, _, m_reps = pipelined(nchunks - 1, carry, first_visit=False, with_next=False)
    for i, (hh, dr) in enumerate(streams):
        c_out_ref[0, dr, hh] = memories[i][0]
        n_out_ref[0, dr, hh] = n_to_row(memories[i][1])
        m_out_ref[0, dr, hh] = m_reps[i]


def _mlstm(q, k, v, o, z, gates, states, heads):
    b, seq, _ = q.shape
    qk_blk = lambda i, h: (i, 0, h)
    st5 = lambda i, h: (i, 0, h, 0, 0)
    state_specs = [pl.BlockSpec((1, 2, heads, BLK, DV), st5),
                   pl.BlockSpec((1, 2, heads, 1, BLK), st5),
                   pl.BlockSpec((1, 2, heads, 1, CHUNK), st5)]
    vmem = heads * (2 * (2 * seq * BLK * 2 + seq * DV * 2 * 2 + 2 * seq * DV * 4 + 2 * 2 * BLK * DV * 4)
                    + seq * DV * 4 + seq * 4 * LANES * 4) + 16 * 1024 * 1024
    return pl.pallas_call(
        functools.partial(_mlstm_kernel, seq=seq, heads=heads, zero_init=states is None),
        grid=(b, N_HEADS // heads),
        in_specs=[pl.BlockSpec((1, seq, heads * BLK), qk_blk),
                  pl.BlockSpec((1, seq, heads * BLK), qk_blk),
                  pl.BlockSpec((1, seq, heads * DV), qk_blk),
                  pl.BlockSpec((1, seq, heads * DV), qk_blk),
                  pl.BlockSpec((1, seq, heads * DV), qk_blk),
                  pl.BlockSpec((1, heads, SUBLANES, seq), lambda i, h: (i, h, 0, 0))]
        + ([] if states is None else state_specs),
        out_specs=[pl.BlockSpec((1, seq, heads * DV), qk_blk)] + state_specs,
        out_shape=[jax.ShapeDtypeStruct((b, seq, N_HEADS * DV), BF16),
                   jax.ShapeDtypeStruct((b, 2, N_HEADS, BLK, DV), F32),
                   jax.ShapeDtypeStruct((b, 2, N_HEADS, 1, BLK), F32),
                   jax.ShapeDtypeStruct((b, 2, N_HEADS, 1, CHUNK), F32)],
        scratch_shapes=[pltpu.VMEM((heads, seq, DV), F32), pltpu.VMEM((heads, seq, 4 * LANES), F32)],
        compiler_params=_cparams(("parallel", "parallel"), vmem),
        name="mlstm_chunkwise",
    )(q, k, v, o, z, gates, *(() if states is None else states))


def _dft_tables(seq):
    n = np.arange(BLK)
    ang = 2.0 * np.pi * np.outer(n, n) / BLK
    dft_c = np.concatenate([np.cos(ang), -np.sin(ang)], axis=1) / np.sqrt(BLK)
    l1 = seq // FFT_L2
    n2 = np.arange(FFT_L2)
    ang2 = 2.0 * np.pi * np.outer(n2, n2) / FFT_L2
    w_pos = np.concatenate([np.cos(ang2), np.sin(ang2)], axis=1) / np.sqrt(seq)
    angt = 2.0 * np.pi * np.outer(np.arange(l1), n2) / seq
    twc = np.broadcast_to(np.cos(angt)[:, :, None], (l1, FFT_L2, LANES))
    tws = np.broadcast_to(np.sin(angt)[:, :, None], (l1, FFT_L2, LANES))
    return (jnp.asarray(dft_c, F32).astype(BF16), jnp.asarray(w_pos, F32).astype(BF16),
            jnp.asarray(twc, F32), jnp.asarray(tws, F32))


def _run_trunk(x, mod, row_off, per_batch_mod, lru0, mlstm0, p, cfg):
    b, seq, d = x.shape
    m = b * seq
    rows_per_mod = seq if per_batch_mod else m
    x2d = x.reshape(m, d)
    dft_c, w_pos, twc, tws = _dft_tables(seq)

    mod0 = mod[0].reshape(MOD_ROWS, 1, 3 * d)
    xa, za, zb, xc, xs = _inproj_e(x2d, mod0, p["g_pre"][0:1], p["w_in_e"], dft_c, row_off, rows_per_mod,
                                   cfg["tm_in"])
    r3 = lambda a: a.reshape(b, seq, a.shape[-1])
    ya, lru_f = _rglru(r3(xa), r3(za), lru0, p["conv_w"], p["conv_b"], p["wg"], p["bg"], p["lam"], cfg["tc_lru"],
                       cfg["bb_lru"])
    yb = _fourier(r3(xc), r3(xs), r3(zb), twc, tws, w_pos, p["w_four"], p["b_four"], cfg["w_four"], cfg["tr_four"])
    x1 = _outproj([ya.reshape(m, d), yb.reshape(m, d)], [p["w_out_e"][0:d], p["w_out_e"][d:2 * d]],
                  x2d, mod0, p["g_post"][0:1], row_off, rows_per_mod, cfg["tm_out"])

    mod1 = mod[1].reshape(MOD_ROWS, 1, 3 * d)
    q, k, v, o, z, gpt = _inproj_o(x1, mod1, p["g_pre"][1:2], p["w_in_o"], p["w_gate"], p["gate_bias"],
                                   row_off, rows_per_mod, cfg["tm_in_o"])
    gq = gpt[0:4 * N_HEADS].reshape(4, N_HEADS, b, seq).transpose(0, 2, 1, 3).reshape(4, b * N_HEADS, seq)
    gates = _gate_prep(gq, cfg["rb_gate"]).reshape(SUBLANES, b, N_HEADS, seq).transpose(1, 2, 0, 3)
    y, c_f, n_f, m_f = _mlstm(r3(q), r3(k), r3(v), r3(o), r3(z), gates, mlstm0, cfg["heads_mlstm"])
    x2 = _outproj([y.reshape(m, 2 * d)], [p["w_out_o"]], x1, mod1, p["g_post"][1:2], row_off, rows_per_mod,
                  cfg["tm_out"])
    return x2.reshape(b, seq, d), lru_f, c_f, n_f, m_f


def _prepare_params(conv_w, conv_b, w_rg, b_rg, w_ig, b_ig, lru_lambda, w_four, b_four, w_in_e, w_out_e,
                    w_in_o, b_if, w_out_o, g_pre, g_post):
    d = D_MODEL
    wg = jnp.concatenate([w_rg[0], w_ig[0]], axis=-1).astype(BF16)
    bg = jnp.concatenate([b_rg[0].reshape(2, N_HEADS, 1, BLK), b_ig[0].reshape(2, N_HEADS, 1, BLK)], axis=-1)
    n_gate = 4 * N_HEADS
    w_gate = jnp.zeros((d, LANES), F32).at[:, 0:n_gate].set(w_in_o[0][:, 8 * d:8 * d + n_gate]).astype(BF16)
    bias = jnp.zeros((LANES,), F32).at[0:n_gate].set(b_if[0].reshape(n_gate))
    gate_bias = jnp.broadcast_to(bias[:, None], (LANES, LANES))
    return dict(
        g_pre=g_pre, g_post=g_post,
        w_in_e=w_in_e[0].astype(BF16), w_out_e=w_out_e[0].astype(BF16),
        conv_w=conv_w[0], conv_b=conv_b[0].reshape(1, d), wg=wg, bg=bg, lam=lru_lambda[0].reshape(2, 1, d),
        w_four=w_four[0].astype(BF16), b_four=b_four[0].reshape(1, d),
        w_in_o=w_in_o[0].astype(BF16), w_gate=w_gate, gate_bias=gate_bias,
        w_out_o=w_out_o[0].astype(BF16))


def kernel(x_prompt, x_sample, c, state_lru, state_mlstm_C, state_mlstm_n, state_mlstm_m, c_ctx, w_mod, b_mod,
           g_pre, g_post, w_in_e, conv_w, conv_b, w_rg, b_rg, w_ig, b_ig, lru_lambda, w_four, b_four, w_out_e,
           w_in_o, b_if, w_out_o):
    d = D_MODEL
    bp = x_prompt.shape[0]
    bs = x_sample.shape[0]
    p = _prepare_params(conv_w, conv_b, w_rg, b_rg, w_ig, b_ig, lru_lambda, w_four, b_four, w_in_e, w_out_e,
                        w_in_o, b_if, w_out_o, g_pre, g_post)
    cond = jnp.concatenate([c_ctx[None, :], c, jnp.zeros((MOD_ROWS - 1 - bs, d), F32)], axis=0)
    mod = _modulation(cond, w_mod, b_mod)

    cfg_p = dict(tm_in=512, tm_in_o=256, tm_out=512, tc_lru=128, bb_lru=4 if bp % 4 == 0 else 1, w_four=d,
                 tr_four=256, rb_gate=min(bp * N_HEADS, 128), heads_mlstm=2)
    y_prompt, lru_f, c_f, n_f, m_f = _run_trunk(x_prompt, mod, 0, False, jnp.zeros((bp, 2, d), F32), None, p, cfg_p)

    cfg_s = dict(tm_in=512, tm_in_o=256, tm_out=512, tc_lru=128, bb_lru=1, w_four=BLK, tr_four=256,
                 rb_gate=SUBLANES, heads_mlstm=1)
    mlstm0 = (state_mlstm_C[:, 0], state_mlstm_n[:, 0].reshape(bs, 2, N_HEADS, 1, BLK),
              jnp.broadcast_to(state_mlstm_m[:, 0].reshape(bs, 2, N_HEADS, 1, 1), (bs, 2, N_HEADS, 1, CHUNK)))
    y_sample, _, _, _, _ = _run_trunk(x_sample, mod, 1, True, state_lru[:, 0], mlstm0, p, cfg_s)

    return (y_prompt, y_sample, lru_f[:, None], c_f[:, None], n_f[:, None, :, :, 0, :], m_f[:, None, :, :, 0, 0])
```

```python
import functools
import math

import numpy as np
import jax
import jax.numpy as jnp
from jax import lax
from jax.experimental import pallas as pl
from jax.experimental.pallas import tpu as pltpu

F32 = jnp.float32
BF16 = jnp.bfloat16

D_MODEL = 1024
DEPTH = 2
EPS = 1e-6
LRU_C = 8.0
CONV_W = 4
N_HEADS = 8
BLK = D_MODEL // N_HEADS
DV = 2 * BLK
CHUNK = 128
FFT_L2 = 256
MOD_ROWS = 8

LANES = 128
SUBLANES = 8
VMEM_LIMIT_CAP = 56 * 1024 * 1024


def _cparams(sem, vmem_bytes, flags=None):
    return pltpu.CompilerParams(dimension_semantics=sem, flags=flags,
                                vmem_limit_bytes=int(min(max(vmem_bytes, 16 * 1024 * 1024), VMEM_LIMIT_CAP)))


def _silu(x):
    return x * jax.nn.sigmoid(x)


def _mod_kernel(cond_ref, w_ref, b_ref, o_ref):
    s = _silu(cond_ref[...]).astype(BF16)
    o_ref[0] = jnp.dot(s, w_ref[0].astype(BF16), preferred_element_type=F32) + b_ref[0]


def _modulation(cond, w_mod, b_mod):
    d = D_MODEL
    return pl.pallas_call(
        _mod_kernel,
        grid=(DEPTH, 3),
        in_specs=[pl.BlockSpec((MOD_ROWS, d), lambda l, j: (0, 0)),
                  pl.BlockSpec((1, d, d), lambda l, j: (l, 0, j)),
                  pl.BlockSpec((1, 1, d), lambda l, j: (l, 0, j))],
        out_specs=pl.BlockSpec((1, MOD_ROWS, d), lambda l, j: (l, 0, j)),
        out_shape=jax.ShapeDtypeStruct((DEPTH, MOD_ROWS, 3 * d), F32),
        compiler_params=_cparams(("arbitrary", "arbitrary"), 24 * 1024 * 1024),
        name="adaln_mod",
    )(cond, w_mod, b_mod.reshape(DEPTH, 1, 3 * d))


def _normed_input(x_ref, mod_ref, g_ref):
    d = D_MODEL
    x = x_ref[...]
    y = x * lax.rsqrt(jnp.mean(x * x, axis=-1, keepdims=True) + EPS) * g_ref[...]
    shift = mod_ref[0, :, 0:d]
    scale = mod_ref[0, :, d:2 * d]
    return (y * (1.0 + scale) + shift).astype(BF16)


def _mod_index_map(row_off, tiles_per_mod):
    return lambda i: (row_off + i // tiles_per_mod, 0, 0)


def _inproj_e_kernel(x_ref, mod_ref, g_ref, w_ref, dft_ref, xa_ref, za_ref, zb_ref, xc_ref, xs_ref):
    d = D_MODEL
    hb = _normed_input(x_ref, mod_ref, g_ref)
    xa_ref[...] = jnp.dot(hb, w_ref[:, 0:d], preferred_element_type=F32)
    za_ref[...] = jnp.dot(hb, w_ref[:, d:2 * d], preferred_element_type=F32)
    zb_ref[...] = jnp.dot(hb, w_ref[:, 3 * d:4 * d], preferred_element_type=F32)
    xb = jnp.dot(hb, w_ref[:, 2 * d:3 * d], preferred_element_type=F32).astype(BF16)
    for g in range(N_HEADS):
        cs = jnp.dot(xb[:, g * BLK:(g + 1) * BLK], dft_ref[...], preferred_element_type=F32)
        xc_ref[:, g * BLK:(g + 1) * BLK] = cs[:, 0:BLK]
        xs_ref[:, g * BLK:(g + 1) * BLK] = cs[:, BLK:2 * BLK]


def _inproj_e(x2d, mod_l, g_pre, w_in, dft_c, row_off, rows_per_mod, tm):
    m, d = x2d.shape
    row = lambda i: (i, 0)
    const = lambda i: (0, 0)
    out = jax.ShapeDtypeStruct((m, d), F32)
    vmem = 2 * (w_in.size * 2 + tm * d * 4 * 6) + 8 * tm * d * 4
    return pl.pallas_call(
        _inproj_e_kernel,
        grid=(m // tm,),
        in_specs=[pl.BlockSpec((tm, d), row),
                  pl.BlockSpec((1, 1, 3 * d), _mod_index_map(row_off, rows_per_mod // tm)),
                  pl.BlockSpec((1, d), const),
                  pl.BlockSpec(w_in.shape, const),
                  pl.BlockSpec(dft_c.shape, const)],
        out_specs=[pl.BlockSpec((tm, d), row)] * 5,
        out_shape=[out] * 5,
        compiler_params=_cparams(("parallel",), vmem),
        name="inproj_even",
    )(x2d, mod_l, g_pre, w_in, dft_c)


def _scan_chunk(a, u, reverse):
    t = a.shape[0]
    row = lax.broadcasted_iota(jnp.int32, a.shape, 0)
    d = 1
    while d < t:
        if d < SUBLANES:
            shift = (t - d) if reverse else d
            a_sh = pltpu.roll(a, shift, axis=0)
            u_sh = pltpu.roll(u, shift, axis=0)
            valid = (row < t - d) if reverse else (row >= d)
            a_sh = jnp.where(valid, a_sh, 1.0)
            u_sh = jnp.where(valid, u_sh, 0.0)
        else:
            ones = jnp.ones((d, a.shape[1]), F32)
            zeros = jnp.zeros((d, a.shape[1]), F32)
            if reverse:
                a_sh = jnp.concatenate([a[d:], ones], axis=0)
                u_sh = jnp.concatenate([u[d:], zeros], axis=0)
            else:
                a_sh = jnp.concatenate([ones, a[:t - d]], axis=0)
                u_sh = jnp.concatenate([zeros, u[:t - d]], axis=0)
        u = a * u_sh + u
        a = a * a_sh
        d *= 2
    return a, u


def _rglru_kernel(xa_ref, za_ref, cw_ref, cb_ref, wg_ref, bg_ref, lam_ref, h0_ref,
                  y_ref, hT_ref, hf_ref, hb_ref, *, seq, tc, bb):
    nchunks = seq // tc
    groups = tc // SUBLANES
    sub = lax.broadcasted_iota(jnp.int32, (SUBLANES, LANES), 0)

    def row_bcast(block, r):
        return jnp.broadcast_to(block[r:r + 1, :], (SUBLANES, LANES))

    def conv_chunk(bi, c):
        t0 = pl.multiple_of(c * tc, tc)
        xs = [xa_ref[bi, pl.ds(t0 + g, SUBLANES, stride=groups), :] for g in range(groups)]
        prev = xa_ref[bi, pl.ds(pl.multiple_of(jnp.maximum(t0 - SUBLANES, 0), SUBLANES), SUBLANES), :]
        nxt = xa_ref[bi, pl.ds(pl.multiple_of(jnp.minimum(t0 + tc, seq - SUBLANES), SUBLANES), SUBLANES), :]
        prev = jnp.where(c > 0, prev, 0.0)
        nxt = jnp.where(c < nchunks - 1, nxt, 0.0)
        before2 = jnp.where(sub == 0, row_bcast(prev, SUBLANES - 2), pltpu.roll(xs[groups - 2], 1, axis=0))
        before1 = jnp.where(sub == 0, row_bcast(prev, SUBLANES - 1), pltpu.roll(xs[groups - 1], 1, axis=0))
        after1 = jnp.where(sub == SUBLANES - 1, row_bcast(nxt, 0), pltpu.roll(xs[0], SUBLANES - 1, axis=0))
        ext = [before2, before1] + xs + [after1]
        out = []
        for g in range(groups):
            acc = cb_ref[...] + cw_ref[0:1, :] * ext[g]
            for j in range(1, CONV_W):
                acc = acc + cw_ref[j:j + 1, :] * ext[g + j]
            out.append(acc)
        return t0, jnp.concatenate(out, axis=0)

    def gates(xc, direction):
        g = jnp.dot(xc.astype(BF16), wg_ref[direction, 0], preferred_element_type=F32) + bg_ref[direction, 0]
        r = jax.nn.sigmoid(g[:, 0:BLK])
        i = jax.nn.sigmoid(g[:, BLK:2 * BLK])
        lam = lam_ref[direction]
        softplus_neg = jnp.maximum(-lam, 0.0) + jnp.log1p(jnp.exp(-jnp.abs(lam)))
        neg_log_a = LRU_C * r * softplus_neg
        a = jnp.exp(-neg_log_a)
        om = jnp.tanh(neg_log_a) * (1.0 + a * a)
        root = jnp.where(om > 0.0, om * lax.rsqrt(om), 0.0)
        return a, root * (i * xc)

    def scan_dir(bi, c, direction, carry, out_ref):
        reverse = direction == 1
        t0, xc = conv_chunk(bi, c)
        a, u = gates(xc, direction)
        order = range(groups - 1, -1, -1) if reverse else range(groups)
        p_g, s_g = [None] * groups, [None] * groups
        p_run = s_run = None
        for g in order:
            a_v = a[g * SUBLANES:(g + 1) * SUBLANES]
            u_v = u[g * SUBLANES:(g + 1) * SUBLANES]
            if p_run is None:
                p_run, s_run = a_v, u_v
            else:
                s_run = a_v * s_run + u_v
                p_run = a_v * p_run
            p_g[g], s_g[g] = p_run, s_run
        pp, ss = _scan_chunk(p_run, s_run, reverse)
        carry_b = jnp.broadcast_to(carry, (SUBLANES, LANES))
        seg_out = pp * carry_b + ss
        if reverse:
            h_in = jnp.where(sub == SUBLANES - 1, carry_b, pltpu.roll(seg_out, SUBLANES - 1, axis=0))
            new_carry = seg_out[0:1, :]
        else:
            h_in = jnp.where(sub == 0, carry_b, pltpu.roll(seg_out, 1, axis=0))
            new_carry = seg_out[SUBLANES - 1:SUBLANES, :]
        for g in range(groups):
            out_ref[bi, pl.ds(t0 + g, SUBLANES, stride=groups), :] = p_g[g] * h_in + s_g[g]
        return new_carry

    def body(j, carries):
        return tuple((scan_dir(bi, j, 0, carries[bi][0], hf_ref),
                      scan_dir(bi, nchunks - 1 - j, 1, carries[bi][1], hb_ref)) for bi in range(bb))

    init = tuple((h0_ref[bi, 0:1, :], h0_ref[bi, 1:2, :]) for bi in range(bb))
    if nchunks <= 2:
        finals = init
        for j in range(nchunks):
            finals = body(j, finals)
    else:
        finals = lax.fori_loop(0, nchunks, body, init, unroll=4)
    for bi in range(bb):
        hT_ref[bi, 0:1, :] = finals[bi][0]
        hT_ref[bi, 1:2, :] = finals[bi][1]

    def gate_out(c, carry):
        rows = pl.ds(pl.multiple_of(c * tc, tc), tc)
        for bi in range(bb):
            y_ref[bi, rows, :] = ((hf_ref[bi, rows, :] + hb_ref[bi, rows, :])
                                  * _silu(za_ref[bi, rows, :])).astype(y_ref.dtype)
        return carry

    lax.fori_loop(0, nchunks, gate_out, 0)


def _rglru(xa, za, h0, conv_w, conv_b, wg, bg, lam, tc, bb):
    b, seq, d = xa.shape
    blk = lambda i, j: (i, 0, j)
    vmem = bb * (2 * (2 * seq * BLK * 4 + seq * BLK * 2) + 2 * seq * BLK * 4) + 16 * 1024 * 1024
    return pl.pallas_call(
        functools.partial(_rglru_kernel, seq=seq, tc=tc, bb=bb),
        grid=(b // bb, d // BLK),
        in_specs=[pl.BlockSpec((bb, seq, BLK), blk),
                  pl.BlockSpec((bb, seq, BLK), blk),
                  pl.BlockSpec((CONV_W, BLK), lambda i, j: (0, j)),
                  pl.BlockSpec((1, BLK), lambda i, j: (0, j)),
                  pl.BlockSpec((2, 1, BLK, 2 * BLK), lambda i, j: (0, j, 0, 0)),
                  pl.BlockSpec((2, 1, 1, 2 * BLK), lambda i, j: (0, j, 0, 0)),
                  pl.BlockSpec((2, 1, BLK), lambda i, j: (0, 0, j)),
                  pl.BlockSpec((bb, 2, BLK), blk)],
        out_specs=[pl.BlockSpec((bb, seq, BLK), blk),
                   pl.BlockSpec((bb, 2, BLK), blk)],
        out_shape=[jax.ShapeDtypeStruct((b, seq, d), BF16),
                   jax.ShapeDtypeStruct((b, 2, d), F32)],
        scratch_shapes=[pltpu.VMEM((bb, seq, BLK), F32), pltpu.VMEM((bb, seq, BLK), F32)],
        compiler_params=_cparams(("parallel", "parallel"), vmem),
        name="rglru_scan",
    )(xa, za, conv_w, conv_b, wg, bg, lam, h0)


def _fft_list(xs):
    n = len(xs)
    if n == 1:
        return xs
    even = _fft_list(xs[0::2])
    odd = _fft_list(xs[1::2])
    out = [None] * n
    for k in range(n // 2):
        o_re, o_im = odd[k]
        if k == 0:
            t_re, t_im = o_re, o_im
        elif 4 * k == n:
            t_re, t_im = o_im, -o_re
        else:
            ang = -2.0 * math.pi * k / n
            wr, wi = math.cos(ang), math.sin(ang)
            t_re = o_re * wr - o_im * wi
            t_im = o_re * wi + o_im * wr
        e_re, e_im = even[k]
        out[k] = (e_re + t_re, e_im + t_im)
        out[k + n // 2] = (e_re - t_re, e_im - t_im)
    return out


def _fourier_kernel(xc_ref, xs_ref, zb_ref, twc_ref, tws_ref, wpos_ref, wf_ref, bf_ref, y_ref, *scratch,
                    seq, tr):
    w = xc_ref.shape[2]
    nblk = w // BLK
    l1 = seq // FFT_L2

    def epilogue(fr, rows):
        parts = []
        for kb in range(nblk):
            yb = jnp.dot(fr[:, kb * BLK:(kb + 1) * BLK].astype(BF16), wf_ref[kb], preferred_element_type=F32)
            parts.append(yb + bf_ref[:, kb * BLK:(kb + 1) * BLK])
        yb = parts[0] if nblk == 1 else jnp.concatenate(parts, axis=1)
        y_ref[0, rows, :] = (yb * _silu(zb_ref[0, rows, :])).astype(y_ref.dtype)

    if l1 == 1:
        fr = (jnp.dot(wpos_ref[:, 0:FFT_L2], xc_ref[0].astype(BF16), preferred_element_type=F32)
              + jnp.dot(wpos_ref[:, FFT_L2:2 * FFT_L2], xs_ref[0].astype(BF16), preferred_element_type=F32))
        epilogue(fr, pl.ds(0, seq))
        return

    b_ref, fr_ref = scratch
    assert w == LANES

    def butterfly(r, carry):
        r0 = pl.multiple_of(r * SUBLANES, SUBLANES)
        zs = [(xc_ref[0, pl.ds(n1 * FFT_L2 + r0, SUBLANES), :], xs_ref[0, pl.ds(n1 * FFT_L2 + r0, SUBLANES), :])
              for n1 in range(l1)]
        for k1, (a_re, a_im) in enumerate(_fft_list(zs)):
            if k1 == 0:
                b_re, b_im = a_re, a_im
            else:
                tc_ = twc_ref[k1, pl.ds(r0, SUBLANES), :]
                ts_ = tws_ref[k1, pl.ds(r0, SUBLANES), :]
                b_re = a_re * tc_ + a_im * ts_
                b_im = a_im * tc_ - a_re * ts_
            b_ref[pl.ds(r0, SUBLANES), k1 * LANES:(k1 + 1) * LANES] = b_re
            b_ref[pl.ds(FFT_L2 + r0, SUBLANES), k1 * LANES:(k1 + 1) * LANES] = b_im
        return carry

    lax.fori_loop(0, FFT_L2 // SUBLANES, butterfly, 0, unroll=2)

    per_dot = min(4, l1)
    for nb in range(l1 // per_dot):
        cols = slice(nb * per_dot * LANES, (nb + 1) * per_dot * LANES)
        fr = jnp.dot(wpos_ref[...], b_ref[:, cols].astype(BF16), preferred_element_type=F32)
        for kk in range(per_dot):
            fr_ref[pl.ds(nb * per_dot + kk, FFT_L2, stride=l1), :] = fr[:, kk * LANES:(kk + 1) * LANES]

    def finish(i, carry):
        rows = pl.ds(pl.multiple_of(i * tr, tr), tr)
        epilogue(fr_ref[rows, :], rows)
        return carry

    lax.fori_loop(0, seq // tr, finish, 0, unroll=4)


def _fourier(xc, xs, zb, twc, tws, w_pos, w_four, b_four, w_blk, tr):
    b, seq, d = xc.shape
    nblk = w_blk // BLK
    l1 = seq // FFT_L2
    blk = lambda i, j: (i, 0, j)
    scratch = []
    if l1 > 1:
        scratch = [pltpu.VMEM((2 * FFT_L2, l1 * w_blk), F32), pltpu.VMEM((seq, w_blk), F32)]
    vmem = (2 * (3 * seq * w_blk * 4 + seq * w_blk * 2 + 2 * twc.size * 4) + 3 * seq * w_blk * 4
            + 16 * 1024 * 1024)
    return pl.pallas_call(
        functools.partial(_fourier_kernel, seq=seq, tr=tr),
        grid=(b, d // w_blk),
        in_specs=[pl.BlockSpec((1, seq, w_blk), blk),
                  pl.BlockSpec((1, seq, w_blk), blk),
                  pl.BlockSpec((1, seq, w_blk), blk),
                  pl.BlockSpec(twc.shape, lambda i, j: (0, 0, 0)),
                  pl.BlockSpec(tws.shape, lambda i, j: (0, 0, 0)),
                  pl.BlockSpec(w_pos.shape, lambda i, j: (0, 0)),
                  pl.BlockSpec((nblk, BLK, BLK), lambda i, j: (j, 0, 0)),
                  pl.BlockSpec((1, w_blk), lambda i, j: (0, j))],
        out_specs=pl.BlockSpec((1, seq, w_blk), blk),
        out_shape=jax.ShapeDtypeStruct((b, seq, d), BF16),
        scratch_shapes=scratch,
        compiler_params=_cparams(("parallel", "parallel"), vmem),
        name="fourier_mix",
    )(xc, xs, zb, twc, tws, w_pos, w_four, b_four)


def _outproj_kernel(*refs, n_in):
    y_refs = refs[:n_in]
    w_refs = refs[n_in:2 * n_in]
    x_ref, mod_ref, g_ref, o_ref = refs[2 * n_in:]
    d = D_MODEL
    y = jnp.dot(y_refs[0][...], w_refs[0][...], preferred_element_type=F32)
    for k in range(1, n_in):
        y = y + jnp.dot(y_refs[k][...], w_refs[k][...], preferred_element_type=F32)
    yn = y * lax.rsqrt(jnp.mean(y * y, axis=-1, keepdims=True) + EPS) * g_ref[...]
    o_ref[...] = x_ref[...] + mod_ref[0, :, 2 * d:3 * d] * yn


def _outproj(ys, ws, x2d, mod_l, g_post, row_off, rows_per_mod, tm):
    m, d = x2d.shape
    n_in = len(ys)
    row = lambda i: (i, 0)
    const = lambda i: (0, 0)
    vmem = 2 * (sum(wk.size for wk in ws) * 2 + sum(tm * yk.shape[1] for yk in ys) * 2 + 2 * tm * d * 4) + 4 * tm * d * 4
    return pl.pallas_call(
        functools.partial(_outproj_kernel, n_in=n_in),
        grid=(m // tm,),
        in_specs=([pl.BlockSpec((tm, yk.shape[1]), row) for yk in ys]
                  + [pl.BlockSpec(wk.shape, const) for wk in ws]
                  + [pl.BlockSpec((tm, d), row),
                     pl.BlockSpec((1, 1, 3 * d), _mod_index_map(row_off, rows_per_mod // tm)),
                     pl.BlockSpec((1, d), const)]),
        out_specs=pl.BlockSpec((tm, d), row),
        out_shape=jax.ShapeDtypeStruct((m, d), F32),
        compiler_params=_cparams(("parallel",), vmem),
        name="outproj_residual",
    )(*ys, *ws, x2d, mod_l, g_post)


def _inproj_o_kernel(x_ref, mod_ref, g_ref, w_ref, wgt_ref, gb_ref, q_ref, k_ref, v_ref, o_ref, z_ref, gpt_ref,
                     gp_ref):
    d = D_MODEL
    hb = _normed_input(x_ref, mod_ref, g_ref)
    q = jnp.dot(hb, w_ref[:, 0:d], preferred_element_type=F32)
    q_ref[...] = (q * (BLK ** -0.5)).astype(q_ref.dtype)
    k_ref[...] = jnp.dot(hb, w_ref[:, d:2 * d], preferred_element_type=F32).astype(k_ref.dtype)
    for half in range(2):
        cols = slice(half * d, (half + 1) * d)
        v_ref[:, cols] = jnp.dot(hb, w_ref[:, 2 * d + half * d:3 * d + half * d],
                                 preferred_element_type=F32).astype(v_ref.dtype)
        o_ref[:, cols] = jnp.dot(hb, w_ref[:, 4 * d + half * d:5 * d + half * d], preferred_element_type=F32)
        z_ref[:, cols] = jnp.dot(hb, w_ref[:, 6 * d + half * d:7 * d + half * d], preferred_element_type=F32)
    gp_ref[...] = jnp.dot(hb, wgt_ref[...], preferred_element_type=F32)
    gpt_ref[...] = gp_ref[...].T + gb_ref[:, 0:1]


def _inproj_o(x2d, mod_l, g_pre, w_main, w_gate, gate_bias, row_off, rows_per_mod, tm):
    m, d = x2d.shape
    row = lambda i: (i, 0)
    const = lambda i: (0, 0)
    ng = w_gate.shape[1]
    vmem = 2 * (w_main.size * 2 + tm * d * 4 + tm * d * 2 * 4 + tm * 2 * d * 4 * 2) + 6 * tm * d * 4
    return pl.pallas_call(
        _inproj_o_kernel,
        grid=(m // tm,),
        in_specs=[pl.BlockSpec((tm, d), row),
                  pl.BlockSpec((1, 1, 3 * d), _mod_index_map(row_off, rows_per_mod // tm)),
                  pl.BlockSpec((1, d), const),
                  pl.BlockSpec(w_main.shape, const),
                  pl.BlockSpec(w_gate.shape, const),
                  pl.BlockSpec(gate_bias.shape, const)],
        out_specs=[pl.BlockSpec((tm, d), row), pl.BlockSpec((tm, d), row),
                   pl.BlockSpec((tm, 2 * d), row), pl.BlockSpec((tm, 2 * d), row), pl.BlockSpec((tm, 2 * d), row),
                   pl.BlockSpec((ng, tm), lambda i: (0, i))],
        out_shape=[jax.ShapeDtypeStruct((m, d), BF16), jax.ShapeDtypeStruct((m, d), BF16),
                   jax.ShapeDtypeStruct((m, 2 * d), BF16), jax.ShapeDtypeStruct((m, 2 * d), F32),
                   jax.ShapeDtypeStruct((m, 2 * d), F32), jax.ShapeDtypeStruct((ng, m), F32)],
        scratch_shapes=[pltpu.VMEM((tm, ng), F32)],
        compiler_params=_cparams(("parallel",), vmem),
        name="inproj_odd",
    )(x2d, mod_l, g_pre, w_main, w_gate, gate_bias)


def _chunk_scan_lanes(v, op, identity, reverse):
    n = v.shape[-1]
    pos = lax.broadcasted_iota(jnp.int32, v.shape, 1) & (CHUNK - 1)
    d = 1
    while d < CHUNK:
        if reverse:
            shifted = jnp.where(pos < CHUNK - d, pltpu.roll(v, n - d, axis=1), identity)
        else:
            shifted = jnp.where(pos >= d, pltpu.roll(v, d, axis=1), identity)
        v = op(v, shifted)
        d *= 2
    return v


def _gate_prep_kernel(g_ref, o_ref):
    for direction in range(2):
        reverse = direction == 1
        i_pre = g_ref[2 * direction]
        b = _chunk_scan_lanes(jax.nn.log_sigmoid(g_ref[2 * direction + 1]), jnp.add, 0.0, reverse)
        a = i_pre - b
        o_ref[3 * direction] = a
        o_ref[3 * direction + 1] = b
        o_ref[3 * direction + 2] = _chunk_scan_lanes(a, jnp.maximum, -jnp.inf, reverse)
    o_ref[6] = jnp.zeros_like(g_ref[0])
    o_ref[7] = jnp.zeros_like(g_ref[0])


def _gate_prep(gq, rb):
    _, r, seq = gq.shape
    return pl.pallas_call(
        _gate_prep_kernel,
        grid=(r // rb,),
        in_specs=[pl.BlockSpec((4, rb, seq), lambda i: (0, i, 0))],
        out_specs=pl.BlockSpec((SUBLANES, rb, seq), lambda i: (0, i, 0)),
        out_shape=jax.ShapeDtypeStruct((SUBLANES, r, seq), F32),
        compiler_params=_cparams(("parallel",), 24 * 1024 * 1024),
        name="mlstm_gate_prep",
    )(gq)


def _mlstm_kernel(*refs, seq, heads, zero_init):
    q_ref, k_ref, v_ref, o_ref, z_ref, gate_ref = refs[0:6]
    refs = refs[6:]
    if not zero_init:
        c0_ref, n0_ref, m0_ref = refs[0:3]
        refs = refs[3:]
    y_ref, c_out_ref, n_out_ref, m_out_ref, hs_ref, cols_ref = refs
    t = CHUNK
    nchunks = seq // t
    half = nchunks // 2
    assert nchunks == 2 * half
    streams = [(hh, dr) for hh in range(heads) for dr in range(2)]

    def head_lanes(hh, width):
        return slice(hh * width, (hh + 1) * width)
    row_i = lax.broadcasted_iota(jnp.int32, (t, t), 0)
    col_i = lax.broadcasted_iota(jnp.int32, (t, t), 1)
    ones_blk = jnp.ones((t, LANES), BF16)

    sel_r = lax.broadcasted_iota(jnp.int32, (4 * SUBLANES, 4 * LANES), 0)
    sel_c = lax.broadcasted_iota(jnp.int32, (4 * SUBLANES, 4 * LANES), 1) // LANES
    wanted = jnp.where(sel_c == 0, 2, jnp.where(sel_c == 1, 1, jnp.where(sel_c == 2, 5, 4)))
    selector = jnp.where(((sel_r & (SUBLANES - 1)) == wanted) & (sel_r < 3 * SUBLANES), 1.0, 0.0).astype(BF16)

    piece = min(seq, 4 * t)
    for hh in range(heads):
        for c in range(seq // piece):
            g = gate_ref[0, hh, :, c * piece:(c + 1) * piece]
            hi = g.astype(BF16).astype(F32)
            mid = (g - hi).astype(BF16).astype(F32)
            lo = g - hi - mid
            stack = jnp.concatenate([hi, mid, lo, jnp.zeros_like(g)], axis=0).astype(BF16)
            cols_ref[hh, c * piece:(c + 1) * piece, :] = lax.dot_general(
                stack, selector, (((0,), (0,)), ((), ())), preferred_element_type=F32)

    def chunk_of(step, direction):
        return (nchunks - 1 - step) if direction == 1 else step

    def gate_terms(rows, hh, direction):
        g_rows = gate_ref[0, hh, :, rows]
        last = 0 if direction == 1 else t - 1
        a_row = g_rows[3 * direction:3 * direction + 1, :]
        b_end = g_rows[3 * direction + 1:3 * direction + 2, last:last + 1]
        a_end = g_rows[3 * direction + 2:3 * direction + 3, last:last + 1]
        amax_rep = cols_ref[hh, rows, 2 * direction * LANES:(2 * direction + 1) * LANES]
        return a_row, b_end, a_end, amax_rep

    def scores(step, stream, m_rep):
        hh, direction = stream
        rows = pl.ds(pl.multiple_of(chunk_of(step, direction) * t, t), t)
        a_row, b_end, a_end, amax_rep = gate_terms(rows, hh, direction)
        m_prev = m_rep[:, 0:1]
        mx = jnp.maximum(m_prev, amax_rep)
        mask = (col_i >= row_i) if direction == 1 else (col_i <= row_i)
        e = jnp.where(mask, jnp.exp(a_row - mx), 0.0)
        s = lax.dot_general(q_ref[0, rows, head_lanes(hh, BLK)], k_ref[0, rows, head_lanes(hh, BLK)],
                            (((1,), (1,)), ((), ())), preferred_element_type=F32) * e
        m_next = jnp.broadcast_to(b_end + jnp.maximum(m_prev, a_end), (1, t))
        return (s.astype(BF16), jnp.sum(s, axis=-1, keepdims=True), m_rep), m_next

    def outputs(step, stream, memory, pending):
        hh, direction = stream
        c_mat, n_rep = memory
        s_b, s_sum, m_rep = pending
        rows = pl.ds(pl.multiple_of(chunk_of(step, direction) * t, t), t)
        qc = q_ref[0, rows, head_lanes(hh, BLK)]
        kc = k_ref[0, rows, head_lanes(hh, BLK)]
        vc = v_ref[0, rows, head_lanes(hh, DV)]
        a_row, _, a_end, amax_rep = gate_terms(rows, hh, direction)
        b_rep = cols_ref[hh, rows, (2 * direction + 1) * LANES:(2 * direction + 2) * LANES]
        m_prev = m_rep[:, 0:1]
        mx_end = jnp.maximum(m_prev, a_end)
        mx = jnp.maximum(m_prev, amax_rep)
        sv = jnp.dot(s_b, vc, preferred_element_type=F32)
        qcn = jnp.dot(qc, jnp.concatenate([c_mat.astype(BF16), n_rep.astype(BF16)], axis=1),
                      preferred_element_type=F32)
        wkk_t = (kc.astype(F32).T * jnp.exp(a_row - mx_end)).astype(BF16)
        upd = jnp.dot(wkk_t, jnp.concatenate([vc, ones_blk], axis=1), preferred_element_type=F32)
        decay = jnp.exp(m_prev - mx_end)
        w_inter = jnp.exp(m_prev - mx)
        den = s_sum + w_inter * qcn[:, DV:DV + LANES]
        inv = 1.0 / jnp.maximum(jnp.abs(den), jnp.exp(-(b_rep + mx)))
        hc = jnp.concatenate(
            [(sv[:, kb * LANES:(kb + 1) * LANES] + w_inter * qcn[:, kb * LANES:(kb + 1) * LANES]) * inv
             for kb in range(DV // LANES)], axis=1)
        return rows, hc, (decay * c_mat + upd[:, 0:DV], decay * n_rep + upd[:, DV:DV + LANES])

    def n_to_columns(n_row):
        return jnp.broadcast_to(n_row, (LANES, BLK)).T

    def n_to_row(n_rep):
        return n_rep.T[0:1, :]

    def emit(hh, rows, hc, first_visit):
        if first_visit:
            hs_ref[hh, rows, :] = hc
        else:
            hsum = hs_ref[hh, rows, :] + hc
            lanes = head_lanes(hh, DV)
            y_ref[0, rows, lanes] = (jax.nn.sigmoid(o_ref[0, rows, lanes]) * hsum
                                     * _silu(z_ref[0, rows, lanes])).astype(y_ref.dtype)

    def pipelined(step, carry, first_visit, with_next):
        memories, pendings, m_reps = carry
        nxt = [scores(step + 1, sm, m_reps[i]) for i, sm in enumerate(streams)] if with_next else None
        done = [outputs(step, sm, memories[i], pendings[i]) for i, sm in enumerate(streams)]
        for (hh, _), (rows, hc, _) in zip(streams, done):
            emit(hh, rows, hc, first_visit)
        new_mem = tuple(dn[2] for dn in done)
        if with_next:
            return new_mem, tuple(n[0] for n in nxt), tuple(n[1] for n in nxt)
        return new_mem, pendings, m_reps

    if zero_init:
        memories = tuple((jnp.zeros((BLK, DV), F32), jnp.zeros((BLK, LANES), F32)) for _ in streams)
        first = [scores(0, sm, jnp.zeros((1, t), F32)) for sm in streams]
    else:
        memories = tuple((c0_ref[0, dr, hh], n_to_columns(n0_ref[0, dr, hh])) for hh, dr in streams)
        first = [scores(0, (hh, dr), m0_ref[0, dr, hh]) for hh, dr in streams]
    carry = (memories, tuple(f[0] for f in first), tuple(f[1] for f in first))
    if nchunks <= 2:
        for step in range(nchunks - 1):
            carry = pipelined(step, carry, first_visit=step < half, with_next=True)
    else:
        unroll = 2 if half % 2 == 0 else 1
        carry = lax.fori_loop(0, half, functools.partial(pipelined, first_visit=True, with_next=True), carry,
                              unroll=unroll)
        carry = lax.fori_loop(half, nchunks - 1, functools.partial(pipelined, first_visit=False, with_next=True),
                              carry, unroll=unroll)
    memories, _, m_reps = pipelined(nchunks - 1, carry, first_visit=False, with_next=False)
    for i, (hh, dr) in enumerate(streams):
        c_out_ref[0, dr, hh] = memories[i][0]
        n_out_ref[0, dr, hh] = n_to_row(memories[i][1])
        m_out_ref[0, dr, hh] = m_reps[i]


def _mlstm(q, k, v, o, z, gates, states, heads):
    b, seq, _ = q.shape
    qk_blk = lambda i, h: (i, 0, h)
    st5 = lambda i, h: (i, 0, h, 0, 0)
    state_specs = [pl.BlockSpec((1, 2, heads, BLK, DV), st5),
                   pl.BlockSpec((1, 2, heads, 1, BLK), st5),
                   pl.BlockSpec((1, 2, heads, 1, CHUNK), st5)]
    vmem = heads * (2 * (2 * seq * BLK * 2 + seq * DV * 2 * 2 + 2 * seq * DV * 4 + 2 * 2 * BLK * DV * 4)
                    + seq * DV * 4 + seq * 4 * LANES * 4) + 16 * 1024 * 1024
    return pl.pallas_call(
        functools.partial(_mlstm_kernel, seq=seq, heads=heads, zero_init=states is None),
        grid=(b, N_HEADS // heads),
        in_specs=[pl.BlockSpec((1, seq, heads * BLK), qk_blk),
                  pl.BlockSpec((1, seq, heads * BLK), qk_blk),
                  pl.BlockSpec((1, seq, heads * DV), qk_blk),
                  pl.BlockSpec((1, seq, heads * DV), qk_blk),
                  pl.BlockSpec((1, seq, heads * DV), qk_blk),
                  pl.BlockSpec((1, heads, SUBLANES, seq), lambda i, h: (i, h, 0, 0))]
        + ([] if states is None else state_specs),
        out_specs=[pl.BlockSpec((1, seq, heads * DV), qk_blk)] + state_specs,
        out_shape=[jax.ShapeDtypeStruct((b, seq, N_HEADS * DV), BF16),
                   jax.ShapeDtypeStruct((b, 2, N_HEADS, BLK, DV), F32),
                   jax.ShapeDtypeStruct((b, 2, N_HEADS, 1, BLK), F32),
                   jax.ShapeDtypeStruct((b, 2, N_HEADS, 1, CHUNK), F32)],
        scratch_shapes=[pltpu.VMEM((heads, seq, DV), F32), pltpu.VMEM((heads, seq, 4 * LANES), F32)],
        compiler_params=_cparams(("parallel", "parallel"), vmem),
        name="mlstm_chunkwise",
    )(q, k, v, o, z, gates, *(() if states is None else states))


def _dft_tables(seq):
    n = np.arange(BLK)
    ang = 2.0 * np.pi * np.outer(n, n) / BLK
    dft_c = np.concatenate([np.cos(ang), -np.sin(ang)], axis=1) / np.sqrt(BLK)
    l1 = seq // FFT_L2
    n2 = np.arange(FFT_L2)
    ang2 = 2.0 * np.pi * np.outer(n2, n2) / FFT_L2
    w_pos = np.concatenate([np.cos(ang2), np.sin(ang2)], axis=1) / np.sqrt(seq)
    angt = 2.0 * np.pi * np.outer(np.arange(l1), n2) / seq
    twc = np.broadcast_to(np.cos(angt)[:, :, None], (l1, FFT_L2, LANES))
    tws = np.broadcast_to(np.sin(angt)[:, :, None], (l1, FFT_L2, LANES))
    return (jnp.asarray(dft_c, F32).astype(BF16), jnp.asarray(w_pos, F32).astype(BF16),
            jnp.asarray(twc, F32), jnp.asarray(tws, F32))


def _run_trunk(x, mod, row_off, per_batch_mod, lru0, mlstm0, p, cfg):
    b, seq, d = x.shape
    m = b * seq
    rows_per_mod = seq if per_batch_mod else m
    x2d = x.reshape(m, d)
    dft_c, w_pos, twc, tws = _dft_tables(seq)

    mod0 = mod[0].reshape(MOD_ROWS, 1, 3 * d)
    xa, za, zb, xc, xs = _inproj_e(x2d, mod0, p["g_pre"][0:1], p["w_in_e"], dft_c, row_off, rows_per_mod,
                                   cfg["tm_in"])
    r3 = lambda a: a.reshape(b, seq, a.shape[-1])
    ya, lru_f = _rglru(r3(xa), r3(za), lru0, p["conv_w"], p["conv_b"], p["wg"], p["bg"], p["lam"], cfg["tc_lru"],
                       cfg["bb_lru"])
    yb = _fourier(r3(xc), r3(xs), r3(zb), twc, tws, w_pos, p["w_four"], p["b_four"], cfg["w_four"], cfg["tr_four"])
    x1 = _outproj([ya.reshape(m, d), yb.reshape(m, d)], [p["w_out_e"][0:d], p["w_out_e"][d:2 * d]],
                  x2d, mod0, p["g_post"][0:1], row_off, rows_per_mod, cfg["tm_out"])

    mod1 = mod[1].reshape(MOD_ROWS, 1, 3 * d)
    q, k, v, o, z, gpt = _inproj_o(x1, mod1, p["g_pre"][1:2], p["w_in_o"], p["w_gate"], p["gate_bias"],
                                   row_off, rows_per_mod, cfg["tm_in_o"])
    gq = gpt[0:4 * N_HEADS].reshape(4, N_HEADS, b, seq).transpose(0, 2, 1, 3).reshape(4, b * N_HEADS, seq)
    gates = _gate_prep(gq, cfg["rb_gate"]).reshape(SUBLANES, b, N_HEADS, seq).transpose(1, 2, 0, 3)
    y, c_f, n_f, m_f = _mlstm(r3(q), r3(k), r3(v), r3(o), r3(z), gates, mlstm0, cfg["heads_mlstm"])
    x2 = _outproj([y.reshape(m, 2 * d)], [p["w_out_o"]], x1, mod1, p["g_post"][1:2], row_off, rows_per_mod,
                  cfg["tm_out"])
    return x2.reshape(b, seq, d), lru_f, c_f, n_f, m_f


def _prepare_params(conv_w, conv_b, w_rg, b_rg, w_ig, b_ig, lru_lambda, w_four, b_four, w_in_e, w_out_e,
                    w_in_o, b_if, w_out_o, g_pre, g_post):
    d = D_MODEL
    wg = jnp.concatenate([w_rg[0], w_ig[0]], axis=-1).astype(BF16)
    bg = jnp.concatenate([b_rg[0].reshape(2, N_HEADS, 1, BLK), b_ig[0].reshape(2, N_HEADS, 1, BLK)], axis=-1)
    n_gate = 4 * N_HEADS
    w_gate = jnp.zeros((d, LANES), F32).at[:, 0:n_gate].set(w_in_o[0][:, 8 * d:8 * d + n_gate]).astype(BF16)
    bias = jnp.zeros((LANES,), F32).at[0:n_gate].set(b_if[0].reshape(n_gate))
    gate_bias = jnp.broadcast_to(bias[:, None], (LANES, LANES))
    return dict(
        g_pre=g_pre, g_post=g_post,
        w_in_e=w_in_e[0].astype(BF16), w_out_e=w_out_e[0].astype(BF16),
        conv_w=conv_w[0], conv_b=conv_b[0].reshape(1, d), wg=wg, bg=bg, lam=lru_lambda[0].reshape(2, 1, d),
        w_four=w_four[0].astype(BF16), b_four=b_four[0].reshape(1, d),
        w_in_o=w_in_o[0].astype(BF16), w_gate=w_gate, gate_bias=gate_bias,
        w_out_o=w_out_o[0].astype(BF16))


def kernel(x_prompt, x_sample, c, state_lru, state_mlstm_C, state_mlstm_n, state_mlstm_m, c_ctx, w_mod, b_mod,
           g_pre, g_post, w_in_e, conv_w, conv_b, w_rg, b_rg, w_ig, b_ig, lru_lambda, w_four, b_four, w_out_e,
           w_in_o, b_if, w_out_o):
    d = D_MODEL
    bp = x_prompt.shape[0]
    bs = x_sample.shape[0]
    p = _prepare_params(conv_w, conv_b, w_rg, b_rg, w_ig, b_ig, lru_lambda, w_four, b_four, w_in_e, w_out_e,
                        w_in_o, b_if, w_out_o, g_pre, g_post)
    cond = jnp.concatenate([c_ctx[None, :], c, jnp.zeros((MOD_ROWS - 1 - bs, d), F32)], axis=0)
    mod = _modulation(cond, w_mod, b_mod)

    cfg_p = dict(tm_in=512, tm_in_o=256, tm_out=512, tc_lru=128, bb_lru=4 if bp % 4 == 0 else 1, w_four=d,
                 tr_four=256, rb_gate=min(bp * N_HEADS, 128), heads_mlstm=4)
    y_prompt, lru_f, c_f, n_f, m_f = _run_trunk(x_prompt, mod, 0, False, jnp.zeros((bp, 2, d), F32), None, p, cfg_p)

    cfg_s = dict(tm_in=512, tm_in_o=256, tm_out=512, tc_lru=128, bb_lru=1, w_four=BLK, tr_four=256,
                 rb_gate=SUBLANES, heads_mlstm=1)
    mlstm0 = (state_mlstm_C[:, 0], state_mlstm_n[:, 0].reshape(bs, 2, N_HEADS, 1, BLK),
              jnp.broadcast_to(state_mlstm_m[:, 0].reshape(bs, 2, N_HEADS, 1, 1), (bs, 2, N_HEADS, 1, CHUNK)))
    y_sample, _, _, _, _ = _run_trunk(x_sample, mod, 1, True, state_lru[:, 0], mlstm0, p, cfg_s)

    return (y_prompt, y_sample, lru_f[:, None], c_f[:, None], n_f[:, None, :, :, 0, :], m_f[:, None, :, :, 0, 0])
```

```python
import functools
import math

import numpy as np
import jax
import jax.numpy as jnp
from jax import lax
from jax.experimental import pallas as pl
from jax.experimental.pallas import tpu as pltpu

F32 = jnp.float32
BF16 = jnp.bfloat16

D_MODEL = 1024
DEPTH = 2
EPS = 1e-6
LRU_C = 8.0
CONV_W = 4
N_HEADS = 8
BLK = D_MODEL // N_HEADS
DV = 2 * BLK
CHUNK = 128
FFT_L2 = 256
MOD_ROWS = 8
GATE_DTYPE = jnp.bfloat16

LANES = 128
SUBLANES = 8
VMEM_LIMIT_CAP = 56 * 1024 * 1024


def _cparams(sem, vmem_bytes, flags=None):
    return pltpu.CompilerParams(dimension_semantics=sem, flags=flags,
                                vmem_limit_bytes=int(min(max(vmem_bytes, 16 * 1024 * 1024), VMEM_LIMIT_CAP)))


def _silu(x):
    return x * jax.nn.sigmoid(x)


def _mod_kernel(cond_ref, w_ref, b_ref, o_ref):
    s = _silu(cond_ref[...]).astype(BF16)
    o_ref[0] = jnp.dot(s, w_ref[0].astype(BF16), preferred_element_type=F32) + b_ref[0]


def _modulation(cond, w_mod, b_mod):
    d = D_MODEL
    return pl.pallas_call(
        _mod_kernel,
        grid=(DEPTH, 3),
        in_specs=[pl.BlockSpec((MOD_ROWS, d), lambda l, j: (0, 0)),
                  pl.BlockSpec((1, d, d), lambda l, j: (l, 0, j)),
                  pl.BlockSpec((1, 1, d), lambda l, j: (l, 0, j))],
        out_specs=pl.BlockSpec((1, MOD_ROWS, d), lambda l, j: (l, 0, j)),
        out_shape=jax.ShapeDtypeStruct((DEPTH, MOD_ROWS, 3 * d), F32),
        compiler_params=_cparams(("arbitrary", "arbitrary"), 24 * 1024 * 1024),
        name="adaln_mod",
    )(cond, w_mod, b_mod.reshape(DEPTH, 1, 3 * d))


def _normed_input(x_ref, mod_ref, g_ref):
    d = D_MODEL
    x = x_ref[...]
    y = x * lax.rsqrt(jnp.mean(x * x, axis=-1, keepdims=True) + EPS) * g_ref[...]
    shift = mod_ref[0, :, 0:d]
    scale = mod_ref[0, :, d:2 * d]
    return (y * (1.0 + scale) + shift).astype(BF16)


def _mod_index_map(row_off, tiles_per_mod):
    return lambda i: (row_off + i // tiles_per_mod, 0, 0)


def _inproj_e_kernel(x_ref, mod_ref, g_ref, w_ref, dft_ref, xa_ref, za_ref, zb_ref, xc_ref, xs_ref):
    d = D_MODEL
    hb = _normed_input(x_ref, mod_ref, g_ref)
    xa_ref[...] = jnp.dot(hb, w_ref[:, 0:d], preferred_element_type=F32)
    za_ref[...] = jnp.dot(hb, w_ref[:, d:2 * d], preferred_element_type=F32).astype(za_ref.dtype)
    zb_ref[...] = jnp.dot(hb, w_ref[:, 3 * d:4 * d], preferred_element_type=F32).astype(zb_ref.dtype)
    xb = jnp.dot(hb, w_ref[:, 2 * d:3 * d], preferred_element_type=F32).astype(BF16)
    for g in range(N_HEADS):
        cs = jnp.dot(xb[:, g * BLK:(g + 1) * BLK], dft_ref[...], preferred_element_type=F32)
        xc_ref[:, g * BLK:(g + 1) * BLK] = cs[:, 0:BLK]
        xs_ref[:, g * BLK:(g + 1) * BLK] = cs[:, BLK:2 * BLK]


def _inproj_e(x2d, mod_l, g_pre, w_in, dft_c, row_off, rows_per_mod, tm):
    m, d = x2d.shape
    row = lambda i: (i, 0)
    const = lambda i: (0, 0)
    out = jax.ShapeDtypeStruct((m, d), F32)
    gate = jax.ShapeDtypeStruct((m, d), GATE_DTYPE)
    vmem = 2 * (w_in.size * 2 + tm * d * 4 * 6) + 8 * tm * d * 4
    return pl.pallas_call(
        _inproj_e_kernel,
        grid=(m // tm,),
        in_specs=[pl.BlockSpec((tm, d), row),
                  pl.BlockSpec((1, 1, 3 * d), _mod_index_map(row_off, rows_per_mod // tm)),
                  pl.BlockSpec((1, d), const),
                  pl.BlockSpec(w_in.shape, const),
                  pl.BlockSpec(dft_c.shape, const)],
        out_specs=[pl.BlockSpec((tm, d), row)] * 5,
        out_shape=[out, gate, gate, out, out],
        compiler_params=_cparams(("parallel",), vmem),
        name="inproj_even",
    )(x2d, mod_l, g_pre, w_in, dft_c)


def _scan_chunk(a, u, reverse):
    t = a.shape[0]
    row = lax.broadcasted_iota(jnp.int32, a.shape, 0)
    d = 1
    while d < t:
        if d < SUBLANES:
            shift = (t - d) if reverse else d
            a_sh = pltpu.roll(a, shift, axis=0)
            u_sh = pltpu.roll(u, shift, axis=0)
            valid = (row < t - d) if reverse else (row >= d)
            a_sh = jnp.where(valid, a_sh, 1.0)
            u_sh = jnp.where(valid, u_sh, 0.0)
        else:
            ones = jnp.ones((d, a.shape[1]), F32)
            zeros = jnp.zeros((d, a.shape[1]), F32)
            if reverse:
                a_sh = jnp.concatenate([a[d:], ones], axis=0)
                u_sh = jnp.concatenate([u[d:], zeros], axis=0)
            else:
                a_sh = jnp.concatenate([ones, a[:t - d]], axis=0)
                u_sh = jnp.concatenate([zeros, u[:t - d]], axis=0)
        u = a * u_sh + u
        a = a * a_sh
        d *= 2
    return a, u


def _rglru_kernel(xa_ref, za_ref, cw_ref, cb_ref, wg_ref, bg_ref, lam_ref, h0_ref,
                  y_ref, hT_ref, hf_ref, hb_ref, *, seq, tc, bb):
    nchunks = seq // tc
    groups = tc // SUBLANES
    sub = lax.broadcasted_iota(jnp.int32, (SUBLANES, LANES), 0)

    def row_bcast(block, r):
        return jnp.broadcast_to(block[r:r + 1, :], (SUBLANES, LANES))

    def conv_chunk(bi, c):
        t0 = pl.multiple_of(c * tc, tc)
        xs = [xa_ref[bi, pl.ds(t0 + g, SUBLANES, stride=groups), :] for g in range(groups)]
        prev = xa_ref[bi, pl.ds(pl.multiple_of(jnp.maximum(t0 - SUBLANES, 0), SUBLANES), SUBLANES), :]
        nxt = xa_ref[bi, pl.ds(pl.multiple_of(jnp.minimum(t0 + tc, seq - SUBLANES), SUBLANES), SUBLANES), :]
        prev = jnp.where(c > 0, prev, 0.0)
        nxt = jnp.where(c < nchunks - 1, nxt, 0.0)
        before2 = jnp.where(sub == 0, row_bcast(prev, SUBLANES - 2), pltpu.roll(xs[groups - 2], 1, axis=0))
        before1 = jnp.where(sub == 0, row_bcast(prev, SUBLANES - 1), pltpu.roll(xs[groups - 1], 1, axis=0))
        after1 = jnp.where(sub == SUBLANES - 1, row_bcast(nxt, 0), pltpu.roll(xs[0], SUBLANES - 1, axis=0))
        ext = [before2, before1] + xs + [after1]
        out = []
        for g in range(groups):
            acc = cb_ref[...] + cw_ref[0:1, :] * ext[g]
            for j in range(1, CONV_W):
                acc = acc + cw_ref[j:j + 1, :] * ext[g + j]
            out.append(acc)
        return t0, jnp.concatenate(out, axis=0)

    def gates(xc, direction):
        g = jnp.dot(xc.astype(BF16), wg_ref[direction, 0], preferred_element_type=F32) + bg_ref[direction, 0]
        r = jax.nn.sigmoid(g[:, 0:BLK])
        i = jax.nn.sigmoid(g[:, BLK:2 * BLK])
        lam = lam_ref[direction]
        softplus_neg = jnp.maximum(-lam, 0.0) + jnp.log1p(jnp.exp(-jnp.abs(lam)))
        neg_log_a = LRU_C * r * softplus_neg
        a = jnp.exp(-neg_log_a)
        om = jnp.tanh(neg_log_a) * (1.0 + a * a)
        root = jnp.where(om > 0.0, om * lax.rsqrt(om), 0.0)
        return a, root * (i * xc)

    def scan_dir(bi, c, direction, carry, out_ref):
        reverse = direction == 1
        t0, xc = conv_chunk(bi, c)
        a, u = gates(xc, direction)
        order = range(groups - 1, -1, -1) if reverse else range(groups)
        p_g, s_g = [None] * groups, [None] * groups
        p_run = s_run = None
        for g in order:
            a_v = a[g * SUBLANES:(g + 1) * SUBLANES]
            u_v = u[g * SUBLANES:(g + 1) * SUBLANES]
            if p_run is None:
                p_run, s_run = a_v, u_v
            else:
                s_run = a_v * s_run + u_v
                p_run = a_v * p_run
            p_g[g], s_g[g] = p_run, s_run
        pp, ss = _scan_chunk(p_run, s_run, reverse)
        carry_b = jnp.broadcast_to(carry, (SUBLANES, LANES))
        seg_out = pp * carry_b + ss
        if reverse:
            h_in = jnp.where(sub == SUBLANES - 1, carry_b, pltpu.roll(seg_out, SUBLANES - 1, axis=0))
            new_carry = seg_out[0:1, :]
        else:
            h_in = jnp.where(sub == 0, carry_b, pltpu.roll(seg_out, 1, axis=0))
            new_carry = seg_out[SUBLANES - 1:SUBLANES, :]
        for g in range(groups):
            out_ref[bi, pl.ds(t0 + g, SUBLANES, stride=groups), :] = p_g[g] * h_in + s_g[g]
        return new_carry

    def body(j, carries):
        return tuple((scan_dir(bi, j, 0, carries[bi][0], hf_ref),
                      scan_dir(bi, nchunks - 1 - j, 1, carries[bi][1], hb_ref)) for bi in range(bb))

    init = tuple((h0_ref[bi, 0:1, :], h0_ref[bi, 1:2, :]) for bi in range(bb))
    if nchunks <= 2:
        finals = init
        for j in range(nchunks):
            finals = body(j, finals)
    else:
        finals = lax.fori_loop(0, nchunks, body, init, unroll=4)
    for bi in range(bb):
        hT_ref[bi, 0:1, :] = finals[bi][0]
        hT_ref[bi, 1:2, :] = finals[bi][1]

    def gate_out(c, carry):
        rows = pl.ds(pl.multiple_of(c * tc, tc), tc)
        for bi in range(bb):
            y_ref[bi, rows, :] = ((hf_ref[bi, rows, :] + hb_ref[bi, rows, :])
                                  * _silu(za_ref[bi, rows, :].astype(F32))).astype(y_ref.dtype)
        return carry

    lax.fori_loop(0, nchunks, gate_out, 0)


def _rglru(xa, za, h0, conv_w, conv_b, wg, bg, lam, tc, bb):
    b, seq, d = xa.shape
    blk = lambda i, j: (i, 0, j)
    vmem = bb * (2 * (2 * seq * BLK * 4 + seq * BLK * 2) + 2 * seq * BLK * 4) + 16 * 1024 * 1024
    return pl.pallas_call(
        functools.partial(_rglru_kernel, seq=seq, tc=tc, bb=bb),
        grid=(b // bb, d // BLK),
        in_specs=[pl.BlockSpec((bb, seq, BLK), blk),
                  pl.BlockSpec((bb, seq, BLK), blk),
                  pl.BlockSpec((CONV_W, BLK), lambda i, j: (0, j)),
                  pl.BlockSpec((1, BLK), lambda i, j: (0, j)),
                  pl.BlockSpec((2, 1, BLK, 2 * BLK), lambda i, j: (0, j, 0, 0)),
                  pl.BlockSpec((2, 1, 1, 2 * BLK), lambda i, j: (0, j, 0, 0)),
                  pl.BlockSpec((2, 1, BLK), lambda i, j: (0, 0, j)),
                  pl.BlockSpec((bb, 2, BLK), blk)],
        out_specs=[pl.BlockSpec((bb, seq, BLK), blk),
                   pl.BlockSpec((bb, 2, BLK), blk)],
        out_shape=[jax.ShapeDtypeStruct((b, seq, d), BF16),
                   jax.ShapeDtypeStruct((b, 2, d), F32)],
        scratch_shapes=[pltpu.VMEM((bb, seq, BLK), F32), pltpu.VMEM((bb, seq, BLK), F32)],
        compiler_params=_cparams(("parallel", "parallel"), vmem),
        name="rglru_scan",
    )(xa, za, conv_w, conv_b, wg, bg, lam, h0)


def _fft_list(xs):
    n = len(xs)
    if n == 1:
        return xs
    even = _fft_list(xs[0::2])
    odd = _fft_list(xs[1::2])
    out = [None] * n
    for k in range(n // 2):
        o_re, o_im = odd[k]
        if k == 0:
            t_re, t_im = o_re, o_im
        elif 4 * k == n:
            t_re, t_im = o_im, -o_re
        else:
            ang = -2.0 * math.pi * k / n
            wr, wi = math.cos(ang), math.sin(ang)
            t_re = o_re * wr - o_im * wi
            t_im = o_re * wi + o_im * wr
        e_re, e_im = even[k]
        out[k] = (e_re + t_re, e_im + t_im)
        out[k + n // 2] = (e_re - t_re, e_im - t_im)
    return out


def _fourier_kernel(xc_ref, xs_ref, zb_ref, twc_ref, tws_ref, wpos_ref, wf_ref, bf_ref, y_ref, *scratch,
                    seq, tr):
    w = xc_ref.shape[2]
    nblk = w // BLK
    l1 = seq // FFT_L2

    def epilogue(fr, rows):
        parts = []
        for kb in range(nblk):
            yb = jnp.dot(fr[:, kb * BLK:(kb + 1) * BLK].astype(BF16), wf_ref[kb], preferred_element_type=F32)
            parts.append(yb + bf_ref[:, kb * BLK:(kb + 1) * BLK])
        yb = parts[0] if nblk == 1 else jnp.concatenate(parts, axis=1)
        y_ref[0, rows, :] = (yb * _silu(zb_ref[0, rows, :].astype(F32))).astype(y_ref.dtype)

    if l1 == 1:
        fr = (jnp.dot(wpos_ref[:, 0:FFT_L2], xc_ref[0].astype(BF16), preferred_element_type=F32)
              + jnp.dot(wpos_ref[:, FFT_L2:2 * FFT_L2], xs_ref[0].astype(BF16), preferred_element_type=F32))
        epilogue(fr, pl.ds(0, seq))
        return

    b_ref, fr_ref = scratch
    assert w == LANES

    def butterfly(r, carry):
        r0 = pl.multiple_of(r * SUBLANES, SUBLANES)
        zs = [(xc_ref[0, pl.ds(n1 * FFT_L2 + r0, SUBLANES), :], xs_ref[0, pl.ds(n1 * FFT_L2 + r0, SUBLANES), :])
              for n1 in range(l1)]
        for k1, (a_re, a_im) in enumerate(_fft_list(zs)):
            if k1 == 0:
                b_re, b_im = a_re, a_im
            else:
                tc_ = twc_ref[k1, pl.ds(r0, SUBLANES), :]
                ts_ = tws_ref[k1, pl.ds(r0, SUBLANES), :]
                b_re = a_re * tc_ + a_im * ts_
                b_im = a_im * tc_ - a_re * ts_
            b_ref[pl.ds(r0, SUBLANES), k1 * LANES:(k1 + 1) * LANES] = b_re
            b_ref[pl.ds(FFT_L2 + r0, SUBLANES), k1 * LANES:(k1 + 1) * LANES] = b_im
        return carry

    lax.fori_loop(0, FFT_L2 // SUBLANES, butterfly, 0, unroll=2)

    per_dot = min(4, l1)
    for nb in range(l1 // per_dot):
        cols = slice(nb * per_dot * LANES, (nb + 1) * per_dot * LANES)
        fr = jnp.dot(wpos_ref[...], b_ref[:, cols].astype(BF16), preferred_element_type=F32)
        for kk in range(per_dot):
            fr_ref[pl.ds(nb * per_dot + kk, FFT_L2, stride=l1), :] = fr[:, kk * LANES:(kk + 1) * LANES]

    def finish(i, carry):
        rows = pl.ds(pl.multiple_of(i * tr, tr), tr)
        epilogue(fr_ref[rows, :], rows)
        return carry

    lax.fori_loop(0, seq // tr, finish, 0, unroll=4)


def _fourier(xc, xs, zb, twc, tws, w_pos, w_four, b_four, w_blk, tr):
    b, seq, d = xc.shape
    nblk = w_blk // BLK
    l1 = seq // FFT_L2
    blk = lambda i, j: (i, 0, j)
    scratch = []
    if l1 > 1:
        scratch = [pltpu.VMEM((2 * FFT_L2, l1 * w_blk), F32), pltpu.VMEM((seq, w_blk), F32)]
    vmem = (2 * (3 * seq * w_blk * 4 + seq * w_blk * 2 + 2 * twc.size * 4) + 3 * seq * w_blk * 4
            + 16 * 1024 * 1024)
    return pl.pallas_call(
        functools.partial(_fourier_kernel, seq=seq, tr=tr),
        grid=(b, d // w_blk),
        in_specs=[pl.BlockSpec((1, seq, w_blk), blk),
                  pl.BlockSpec((1, seq, w_blk), blk),
                  pl.BlockSpec((1, seq, w_blk), blk),
                  pl.BlockSpec(twc.shape, lambda i, j: (0, 0, 0)),
                  pl.BlockSpec(tws.shape, lambda i, j: (0, 0, 0)),
                  pl.BlockSpec(w_pos.shape, lambda i, j: (0, 0)),
                  pl.BlockSpec((nblk, BLK, BLK), lambda i, j: (j, 0, 0)),
                  pl.BlockSpec((1, w_blk), lambda i, j: (0, j))],
        out_specs=pl.BlockSpec((1, seq, w_blk), blk),
        out_shape=jax.ShapeDtypeStruct((b, seq, d), BF16),
        scratch_shapes=scratch,
        compiler_params=_cparams(("parallel", "parallel"), vmem),
        name="fourier_mix",
    )(xc, xs, zb, twc, tws, w_pos, w_four, b_four)


def _outproj_kernel(*refs, n_in):
    y_refs = refs[:n_in]
    w_refs = refs[n_in:2 * n_in]
    x_ref, mod_ref, g_ref, o_ref = refs[2 * n_in:]
    d = D_MODEL
    y = jnp.dot(y_refs[0][...], w_refs[0][...], preferred_element_type=F32)
    for k in range(1, n_in):
        y = y + jnp.dot(y_refs[k][...], w_refs[k][...], preferred_element_type=F32)
    yn = y * lax.rsqrt(jnp.mean(y * y, axis=-1, keepdims=True) + EPS) * g_ref[...]
    o_ref[...] = x_ref[...] + mod_ref[0, :, 2 * d:3 * d] * yn


def _outproj(ys, ws, x2d, mod_l, g_post, row_off, rows_per_mod, tm):
    m, d = x2d.shape
    n_in = len(ys)
    row = lambda i: (i, 0)
    const = lambda i: (0, 0)
    vmem = 2 * (sum(wk.size for wk in ws) * 2 + sum(tm * yk.shape[1] for yk in ys) * 2 + 2 * tm * d * 4) + 4 * tm * d * 4
    return pl.pallas_call(
        functools.partial(_outproj_kernel, n_in=n_in),
        grid=(m // tm,),
        in_specs=([pl.BlockSpec((tm, yk.shape[1]), row) for yk in ys]
                  + [pl.BlockSpec(wk.shape, const) for wk in ws]
                  + [pl.BlockSpec((tm, d), row),
                     pl.BlockSpec((1, 1, 3 * d), _mod_index_map(row_off, rows_per_mod // tm)),
                     pl.BlockSpec((1, d), const)]),
        out_specs=pl.BlockSpec((tm, d), row),
        out_shape=jax.ShapeDtypeStruct((m, d), F32),
        compiler_params=_cparams(("parallel",), vmem),
        name="outproj_residual",
    )(*ys, *ws, x2d, mod_l, g_post)


def _inproj_o_kernel(x_ref, mod_ref, g_ref, w_ref, wgt_ref, gb_ref, q_ref, k_ref, v_ref, o_ref, z_ref, gpt_ref,
                     gp_ref):
    d = D_MODEL
    hb = _normed_input(x_ref, mod_ref, g_ref)
    q = jnp.dot(hb, w_ref[:, 0:d], preferred_element_type=F32)
    q_ref[...] = (q * (BLK ** -0.5)).astype(q_ref.dtype)
    k_ref[...] = jnp.dot(hb, w_ref[:, d:2 * d], preferred_element_type=F32).astype(k_ref.dtype)
    for half in range(2):
        cols = slice(half * d, (half + 1) * d)
        v_ref[:, cols] = jnp.dot(hb, w_ref[:, 2 * d + half * d:3 * d + half * d],
                                 preferred_element_type=F32).astype(v_ref.dtype)
        o_ref[:, cols] = jnp.dot(hb, w_ref[:, 4 * d + half * d:5 * d + half * d],
                                 preferred_element_type=F32).astype(o_ref.dtype)
        z_ref[:, cols] = jnp.dot(hb, w_ref[:, 6 * d + half * d:7 * d + half * d],
                                 preferred_element_type=F32).astype(z_ref.dtype)
    gp_ref[...] = jnp.dot(hb, wgt_ref[...], preferred_element_type=F32)
    gpt_ref[...] = gp_ref[...].T + gb_ref[:, 0:1]


def _inproj_o(x2d, mod_l, g_pre, w_main, w_gate, gate_bias, row_off, rows_per_mod, tm):
    m, d = x2d.shape
    row = lambda i: (i, 0)
    const = lambda i: (0, 0)
    ng = w_gate.shape[1]
    vmem = 2 * (w_main.size * 2 + tm * d * 4 + tm * d * 2 * 4 + tm * 2 * d * 4 * 2) + 6 * tm * d * 4
    return pl.pallas_call(
        _inproj_o_kernel,
        grid=(m // tm,),
        in_specs=[pl.BlockSpec((tm, d), row),
                  pl.BlockSpec((1, 1, 3 * d), _mod_index_map(row_off, rows_per_mod // tm)),
                  pl.BlockSpec((1, d), const),
                  pl.BlockSpec(w_main.shape, const),
                  pl.BlockSpec(w_gate.shape, const),
                  pl.BlockSpec(gate_bias.shape, const)],
        out_specs=[pl.BlockSpec((tm, d), row), pl.BlockSpec((tm, d), row),
                   pl.BlockSpec((tm, 2 * d), row), pl.BlockSpec((tm, 2 * d), row), pl.BlockSpec((tm, 2 * d), row),
                   pl.BlockSpec((ng, tm), lambda i: (0, i))],
        out_shape=[jax.ShapeDtypeStruct((m, d), BF16), jax.ShapeDtypeStruct((m, d), BF16),
                   jax.ShapeDtypeStruct((m, 2 * d), BF16), jax.ShapeDtypeStruct((m, 2 * d), GATE_DTYPE),
                   jax.ShapeDtypeStruct((m, 2 * d), GATE_DTYPE), jax.ShapeDtypeStruct((ng, m), F32)],
        scratch_shapes=[pltpu.VMEM((tm, ng), F32)],
        compiler_params=_cparams(("parallel",), vmem),
        name="inproj_odd",
    )(x2d, mod_l, g_pre, w_main, w_gate, gate_bias)


def _chunk_scan_lanes(v, op, identity, reverse):
    n = v.shape[-1]
    pos = lax.broadcasted_iota(jnp.int32, v.shape, 1) & (CHUNK - 1)
    d = 1
    while d < CHUNK:
        if reverse:
            shifted = jnp.where(pos < CHUNK - d, pltpu.roll(v, n - d, axis=1), identity)
        else:
            shifted = jnp.where(pos >= d, pltpu.roll(v, d, axis=1), identity)
        v = op(v, shifted)
        d *= 2
    return v


def _gate_prep_kernel(g_ref, o_ref):
    for direction in range(2):
        reverse = direction == 1
        i_pre = g_ref[2 * direction]
        b = _chunk_scan_lanes(jax.nn.log_sigmoid(g_ref[2 * direction + 1]), jnp.add, 0.0, reverse)
        a = i_pre - b
        o_ref[3 * direction] = a
        o_ref[3 * direction + 1] = b
        o_ref[3 * direction + 2] = _chunk_scan_lanes(a, jnp.maximum, -jnp.inf, reverse)
    o_ref[6] = jnp.zeros_like(g_ref[0])
    o_ref[7] = jnp.zeros_like(g_ref[0])


def _gate_prep(gq, rb):
    _, r, seq = gq.shape
    return pl.pallas_call(
        _gate_prep_kernel,
        grid=(r // rb,),
        in_specs=[pl.BlockSpec((4, rb, seq), lambda i: (0, i, 0))],
        out_specs=pl.BlockSpec((SUBLANES, rb, seq), lambda i: (0, i, 0)),
        out_shape=jax.ShapeDtypeStruct((SUBLANES, r, seq), F32),
        compiler_params=_cparams(("parallel",), 24 * 1024 * 1024),
        name="mlstm_gate_prep",
    )(gq)


def _mlstm_kernel(*refs, seq, heads, zero_init):
    q_ref, k_ref, v_ref, o_ref, z_ref, gate_ref = refs[0:6]
    refs = refs[6:]
    if not zero_init:
        c0_ref, n0_ref, m0_ref = refs[0:3]
        refs = refs[3:]
    y_ref, c_out_ref, n_out_ref, m_out_ref, hs_ref, cols_ref = refs
    t = CHUNK
    nchunks = seq // t
    half = nchunks // 2
    assert nchunks == 2 * half
    streams = [(hh, dr) for hh in range(heads) for dr in range(2)]

    def head_lanes(hh, width):
        return slice(hh * width, (hh + 1) * width)
    row_i = lax.broadcasted_iota(jnp.int32, (t, t), 0)
    col_i = lax.broadcasted_iota(jnp.int32, (t, t), 1)
    ones_blk = jnp.ones((t, LANES), BF16)

    sel_r = lax.broadcasted_iota(jnp.int32, (4 * SUBLANES, 4 * LANES), 0)
    sel_c = lax.broadcasted_iota(jnp.int32, (4 * SUBLANES, 4 * LANES), 1) // LANES
    wanted = jnp.where(sel_c == 0, 2, jnp.where(sel_c == 1, 1, jnp.where(sel_c == 2, 5, 4)))
    selector = jnp.where(((sel_r & (SUBLANES - 1)) == wanted) & (sel_r < 3 * SUBLANES), 1.0, 0.0).astype(BF16)

    piece = min(seq, 4 * t)
    for hh in range(heads):
        for c in range(seq // piece):
            g = gate_ref[0, hh, :, c * piece:(c + 1) * piece]
            hi = g.astype(BF16).astype(F32)
            mid = (g - hi).astype(BF16).astype(F32)
            lo = g - hi - mid
            stack = jnp.concatenate([hi, mid, lo, jnp.zeros_like(g)], axis=0).astype(BF16)
            cols_ref[hh, c * piece:(c + 1) * piece, :] = lax.dot_general(
                stack, selector, (((0,), (0,)), ((), ())), preferred_element_type=F32)

    def chunk_of(step, direction):
        return (nchunks - 1 - step) if direction == 1 else step

    def gate_terms(rows, hh, direction):
        g_rows = gate_ref[0, hh, :, rows]
        last = 0 if direction == 1 else t - 1
        a_row = g_rows[3 * direction:3 * direction + 1, :]
        b_end = g_rows[3 * direction + 1:3 * direction + 2, last:last + 1]
        a_end = g_rows[3 * direction + 2:3 * direction + 3, last:last + 1]
        amax_rep = cols_ref[hh, rows, 2 * direction * LANES:(2 * direction + 1) * LANES]
        return a_row, b_end, a_end, amax_rep

    def scores(step, stream, m_rep):
        hh, direction = stream
        rows = pl.ds(pl.multiple_of(chunk_of(step, direction) * t, t), t)
        a_row, b_end, a_end, amax_rep = gate_terms(rows, hh, direction)
        m_prev = m_rep[:, 0:1]
        mx = jnp.maximum(m_prev, amax_rep)
        mask = (col_i >= row_i) if direction == 1 else (col_i <= row_i)
        e = jnp.where(mask, jnp.exp(a_row - mx), 0.0)
        s = lax.dot_general(q_ref[0, rows, head_lanes(hh, BLK)], k_ref[0, rows, head_lanes(hh, BLK)],
                            (((1,), (1,)), ((), ())), preferred_element_type=F32) * e
        m_next = jnp.broadcast_to(b_end + jnp.maximum(m_prev, a_end), (1, t))
        return (s.astype(BF16), jnp.sum(s, axis=-1, keepdims=True), m_rep), m_next

    def outputs(step, stream, memory, pending):
        hh, direction = stream
        c_mat, n_rep = memory
        s_b, s_sum, m_rep = pending
        rows = pl.ds(pl.multiple_of(chunk_of(step, direction) * t, t), t)
        qc = q_ref[0, rows, head_lanes(hh, BLK)]
        kc = k_ref[0, rows, head_lanes(hh, BLK)]
        vc = v_ref[0, rows, head_lanes(hh, DV)]
        a_row, _, a_end, amax_rep = gate_terms(rows, hh, direction)
        b_rep = cols_ref[hh, rows, (2 * direction + 1) * LANES:(2 * direction + 2) * LANES]
        m_prev = m_rep[:, 0:1]
        mx_end = jnp.maximum(m_prev, a_end)
        mx = jnp.maximum(m_prev, amax_rep)
        sv = jnp.dot(s_b, vc, preferred_element_type=F32)
        qcn = jnp.dot(qc, jnp.concatenate([c_mat.astype(BF16), n_rep.astype(BF16)], axis=1),
                      preferred_element_type=F32)
        wkk_t = (kc.astype(F32).T * jnp.exp(a_row - mx_end)).astype(BF16)
        upd = jnp.dot(wkk_t, jnp.concatenate([vc, ones_blk], axis=1), preferred_element_type=F32)
        decay = jnp.exp(m_prev - mx_end)
        w_inter = jnp.exp(m_prev - mx)
        den = s_sum + w_inter * qcn[:, DV:DV + LANES]
        inv = 1.0 / jnp.maximum(jnp.abs(den), jnp.exp(-(b_rep + mx)))
        hc = jnp.concatenate(
            [(sv[:, kb * LANES:(kb + 1) * LANES] + w_inter * qcn[:, kb * LANES:(kb + 1) * LANES]) * inv
             for kb in range(DV // LANES)], axis=1)
        return rows, hc, (decay * c_mat + upd[:, 0:DV], decay * n_rep + upd[:, DV:DV + LANES])

    def n_to_columns(n_row):
        return jnp.broadcast_to(n_row, (LANES, BLK)).T

    def n_to_row(n_rep):
        return n_rep.T[0:1, :]

    def emit(hh, rows, hc, first_visit):
        if first_visit:
            hs_ref[hh, rows, :] = hc
        else:
            hsum = hs_ref[hh, rows, :] + hc
            lanes = head_lanes(hh, DV)
            y_ref[0, rows, lanes] = (jax.nn.sigmoid(o_ref[0, rows, lanes].astype(F32)) * hsum
                                     * _silu(z_ref[0, rows, lanes].astype(F32))).astype(y_ref.dtype)

    def pipelined(step, carry, first_visit, with_next):
        memories, pendings, m_reps = carry
        nxt = [scores(step + 1, sm, m_reps[i]) for i, sm in enumerate(streams)] if with_next else None
        done = [outputs(step, sm, memories[i], pendings[i]) for i, sm in enumerate(streams)]
        for (hh, _), (rows, hc, _) in zip(streams, done):
            emit(hh, rows, hc, first_visit)
        new_mem = tuple(dn[2] for dn in done)
        if with_next:
            return new_mem, tuple(n[0] for n in nxt), tuple(n[1] for n in nxt)
        return new_mem, pendings, m_reps

    if zero_init:
        memories = tuple((jnp.zeros((BLK, DV), F32), jnp.zeros((BLK, LANES), F32)) for _ in streams)
        first = [scores(0, sm, jnp.zeros((1, t), F32)) for sm in streams]
    else:
        memories = tuple((c0_ref[0, dr, hh], n_to_columns(n0_ref[0, dr, hh])) for hh, dr in streams)
        first = [scores(0, (hh, dr), m0_ref[0, dr, hh]) for hh, dr in streams]
    carry = (memories, tuple(f[0] for f in first), tuple(f[1] for f in first))
    if nchunks <= 2:
        for step in range(nchunks - 1):
            carry = pipelined(step, carry, first_visit=step < half, with_next=True)
    else:
        unroll = 2 if half % 2 == 0 else 1
        carry = lax.fori_loop(0, half, functools.partial(pipelined, first_visit=True, with_next=True), carry,
                              unroll=unroll)
        carry = lax.fori_loop(half, nchunks - 1, functools.partial(pipelined, first_visit=False, with_next=True),
                              carry, unroll=unroll)
    memories, _, m_reps = pipelined(nchunks - 1, carry, first_visit=False, with_next=False)
    for i, (hh, dr) in enumerate(streams):
        c_out_ref[0, dr, hh] = memories[i][0]
        n_out_ref[0, dr, hh] = n_to_row(memories[i][1])
        m_out_ref[0, dr, hh] = m_reps[i]


def _mlstm(q, k, v, o, z, gates, states, heads):
    b, seq, _ = q.shape
    qk_blk = lambda i, h: (i, 0, h)
    st5 = lambda i, h: (i, 0, h, 0, 0)
    state_specs = [pl.BlockSpec((1, 2, heads, BLK, DV), st5),
                   pl.BlockSpec((1, 2, heads, 1, BLK), st5),
                   pl.BlockSpec((1, 2, heads, 1, CHUNK), st5)]
    vmem = heads * (2 * (2 * seq * BLK * 2 + seq * DV * 2 * 2 + 2 * seq * DV * 4 + 2 * 2 * BLK * DV * 4)
                    + seq * DV * 4 + seq * 4 * LANES * 4) + 16 * 1024 * 1024
    return pl.pallas_call(
        functools.partial(_mlstm_kernel, seq=seq, heads=heads, zero_init=states is None),
        grid=(b, N_HEADS // heads),
        in_specs=[pl.BlockSpec((1, seq, heads * BLK), qk_blk),
                  pl.BlockSpec((1, seq, heads * BLK), qk_blk),
                  pl.BlockSpec((1, seq, heads * DV), qk_blk),
                  pl.BlockSpec((1, seq, heads * DV), qk_blk),
                  pl.BlockSpec((1, seq, heads * DV), qk_blk),
                  pl.BlockSpec((1, heads, SUBLANES, seq), lambda i, h: (i, h, 0, 0))]
        + ([] if states is None else state_specs),
        out_specs=[pl.BlockSpec((1, seq, heads * DV), qk_blk)] + state_specs,
        out_shape=[jax.ShapeDtypeStruct((b, seq, N_HEADS * DV), BF16),
                   jax.ShapeDtypeStruct((b, 2, N_HEADS, BLK, DV), F32),
                   jax.ShapeDtypeStruct((b, 2, N_HEADS, 1, BLK), F32),
                   jax.ShapeDtypeStruct((b, 2, N_HEADS, 1, CHUNK), F32)],
        scratch_shapes=[pltpu.VMEM((heads, seq, DV), F32), pltpu.VMEM((heads, seq, 4 * LANES), F32)],
        compiler_params=_cparams(("parallel", "parallel"), vmem),
        name="mlstm_chunkwise",
    )(q, k, v, o, z, gates, *(() if states is None else states))


def _dft_tables(seq):
    n = np.arange(BLK)
    ang = 2.0 * np.pi * np.outer(n, n) / BLK
    dft_c = np.concatenate([np.cos(ang), -np.sin(ang)], axis=1) / np.sqrt(BLK)
    l1 = seq // FFT_L2
    n2 = np.arange(FFT_L2)
    ang2 = 2.0 * np.pi * np.outer(n2, n2) / FFT_L2
    w_pos = np.concatenate([np.cos(ang2), np.sin(ang2)], axis=1) / np.sqrt(seq)
    angt = 2.0 * np.pi * np.outer(np.arange(l1), n2) / seq
    twc = np.broadcast_to(np.cos(angt)[:, :, None], (l1, FFT_L2, LANES))
    tws = np.broadcast_to(np.sin(angt)[:, :, None], (l1, FFT_L2, LANES))
    return (jnp.asarray(dft_c, F32).astype(BF16), jnp.asarray(w_pos, F32).astype(BF16),
            jnp.asarray(twc, F32), jnp.asarray(tws, F32))


def _run_trunk(x, mod, row_off, per_batch_mod, lru0, mlstm0, p, cfg):
    b, seq, d = x.shape
    m = b * seq
    rows_per_mod = seq if per_batch_mod else m
    x2d = x.reshape(m, d)
    dft_c, w_pos, twc, tws = _dft_tables(seq)

    mod0 = mod[0].reshape(MOD_ROWS, 1, 3 * d)
    xa, za, zb, xc, xs = _inproj_e(x2d, mod0, p["g_pre"][0:1], p["w_in_e"], dft_c, row_off, rows_per_mod,
                                   cfg["tm_in"])
    r3 = lambda a: a.reshape(b, seq, a.shape[-1])
    ya, lru_f = _rglru(r3(xa), r3(za), lru0, p["conv_w"], p["conv_b"], p["wg"], p["bg"], p["lam"], cfg["tc_lru"],
                       cfg["bb_lru"])
    yb = _fourier(r3(xc), r3(xs), r3(zb), twc, tws, w_pos, p["w_four"], p["b_four"], cfg["w_four"], cfg["tr_four"])
    x1 = _outproj([ya.reshape(m, d), yb.reshape(m, d)], [p["w_out_e"][0:d], p["w_out_e"][d:2 * d]],
                  x2d, mod0, p["g_post"][0:1], row_off, rows_per_mod, cfg["tm_out"])

    mod1 = mod[1].reshape(MOD_ROWS, 1, 3 * d)
    q, k, v, o, z, gpt = _inproj_o(x1, mod1, p["g_pre"][1:2], p["w_in_o"], p["w_gate"], p["gate_bias"],
                                   row_off, rows_per_mod, cfg["tm_in_o"])
    gq = gpt[0:4 * N_HEADS].reshape(4, N_HEADS, b, seq).transpose(0, 2, 1, 3).reshape(4, b * N_HEADS, seq)
    gates = _gate_prep(gq, cfg["rb_gate"]).reshape(SUBLANES, b, N_HEADS, seq).transpose(1, 2, 0, 3)
    y, c_f, n_f, m_f = _mlstm(r3(q), r3(k), r3(v), r3(o), r3(z), gates, mlstm0, cfg["heads_mlstm"])
    x2 = _outproj([y.reshape(m, 2 * d)], [p["w_out_o"]], x1, mod1, p["g_post"][1:2], row_off, rows_per_mod,
                  cfg["tm_out"])
    return x2.reshape(b, seq, d), lru_f, c_f, n_f, m_f


def _prepare_params(conv_w, conv_b, w_rg, b_rg, w_ig, b_ig, lru_lambda, w_four, b_four, w_in_e, w_out_e,
                    w_in_o, b_if, w_out_o, g_pre, g_post):
    d = D_MODEL
    wg = jnp.concatenate([w_rg[0], w_ig[0]], axis=-1).astype(BF16)
    bg = jnp.concatenate([b_rg[0].reshape(2, N_HEADS, 1, BLK), b_ig[0].reshape(2, N_HEADS, 1, BLK)], axis=-1)
    n_gate = 4 * N_HEADS
    w_gate = jnp.zeros((d, LANES), F32).at[:, 0:n_gate].set(w_in_o[0][:, 8 * d:8 * d + n_gate]).astype(BF16)
    bias = jnp.zeros((LANES,), F32).at[0:n_gate].set(b_if[0].reshape(n_gate))
    gate_bias = jnp.broadcast_to(bias[:, None], (LANES, LANES))
    return dict(
        g_pre=g_pre, g_post=g_post,
        w_in_e=w_in_e[0].astype(BF16), w_out_e=w_out_e[0].astype(BF16),
        conv_w=conv_w[0], conv_b=conv_b[0].reshape(1, d), wg=wg, bg=bg, lam=lru_lambda[0].reshape(2, 1, d),
        w_four=w_four[0].astype(BF16), b_four=b_four[0].reshape(1, d),
        w_in_o=w_in_o[0].astype(BF16), w_gate=w_gate, gate_bias=gate_bias,
        w_out_o=w_out_o[0].astype(BF16))


def kernel(x_prompt, x_sample, c, state_lru, state_mlstm_C, state_mlstm_n, state_mlstm_m, c_ctx, w_mod, b_mod,
           g_pre, g_post, w_in_e, conv_w, conv_b, w_rg, b_rg, w_ig, b_ig, lru_lambda, w_four, b_four, w_out_e,
           w_in_o, b_if, w_out_o):
    d = D_MODEL
    bp = x_prompt.shape[0]
    bs = x_sample.shape[0]
    p = _prepare_params(conv_w, conv_b, w_rg, b_rg, w_ig, b_ig, lru_lambda, w_four, b_four, w_in_e, w_out_e,
                        w_in_o, b_if, w_out_o, g_pre, g_post)
    cond = jnp.concatenate([c_ctx[None, :], c, jnp.zeros((MOD_ROWS - 1 - bs, d), F32)], axis=0)
    mod = _modulation(cond, w_mod, b_mod)

    cfg_p = dict(tm_in=512, tm_in_o=256, tm_out=512, tc_lru=128, bb_lru=4 if bp % 4 == 0 else 1, w_four=d,
                 tr_four=256, rb_gate=min(bp * N_HEADS, 128), heads_mlstm=4)
    y_prompt, lru_f, c_f, n_f, m_f = _run_trunk(x_prompt, mod, 0, False, jnp.zeros((bp, 2, d), F32), None, p, cfg_p)

    cfg_s = dict(tm_in=512, tm_in_o=256, tm_out=512, tc_lru=128, bb_lru=1, w_four=BLK, tr_four=256,
                 rb_gate=SUBLANES, heads_mlstm=1)
    mlstm0 = (state_mlstm_C[:, 0], state_mlstm_n[:, 0].reshape(bs, 2, N_HEADS, 1, BLK),
              jnp.broadcast_to(state_mlstm_m[:, 0].reshape(bs, 2, N_HEADS, 1, 1), (bs, 2, N_HEADS, 1, CHUNK)))
    y_sample, _, _, _, _ = _run_trunk(x_sample, mod, 1, True, state_lru[:, 0], mlstm0, p, cfg_s)

    return (y_prompt, y_sample, lru_f[:, None], c_f[:, None], n_f[:, None, :, :, 0, :], m_f[:, None, :, :, 0, 0])
```

```python
import functools
import math

import numpy as np
import jax
import jax.numpy as jnp
from jax import lax
from jax.experimental import pallas as pl
from jax.experimental.pallas import tpu as pltpu

F32 = jnp.float32
BF16 = jnp.bfloat16

D_MODEL = 1024
DEPTH = 2
EPS = 1e-6
LRU_C = 8.0
CONV_W = 4
N_HEADS = 8
BLK = D_MODEL // N_HEADS
DV = 2 * BLK
CHUNK = 128
FFT_L2 = 256
MOD_ROWS = 8

LANES = 128
SUBLANES = 8
VMEM_LIMIT_CAP = 56 * 1024 * 1024


def _cparams(sem, vmem_bytes, flags=None):
    return pltpu.CompilerParams(dimension_semantics=sem, flags=flags,
                                vmem_limit_bytes=int(min(max(vmem_bytes, 16 * 1024 * 1024), VMEM_LIMIT_CAP)))


def _silu(x):
    return x * jax.nn.sigmoid(x)


def _mod_kernel(cond_ref, w_ref, b_ref, o_ref):
    s = _silu(cond_ref[...]).astype(BF16)
    o_ref[0] = jnp.dot(s, w_ref[0].astype(BF16), preferred_element_type=F32) + b_ref[0]


def _modulation(cond, w_mod, b_mod):
    d = D_MODEL
    return pl.pallas_call(
        _mod_kernel,
        grid=(DEPTH, 3),
        in_specs=[pl.BlockSpec((MOD_ROWS, d), lambda l, j: (0, 0)),
                  pl.BlockSpec((1, d, d), lambda l, j: (l, 0, j)),
                  pl.BlockSpec((1, 1, d), lambda l, j: (l, 0, j))],
        out_specs=pl.BlockSpec((1, MOD_ROWS, d), lambda l, j: (l, 0, j)),
        out_shape=jax.ShapeDtypeStruct((DEPTH, MOD_ROWS, 3 * d), F32),
        compiler_params=_cparams(("arbitrary", "arbitrary"), 24 * 1024 * 1024),
        name="adaln_mod",
    )(cond, w_mod, b_mod.reshape(DEPTH, 1, 3 * d))


def _normed_input(x_ref, mod_ref, g_ref):
    d = D_MODEL
    x = x_ref[...]
    y = x * lax.rsqrt(jnp.mean(x * x, axis=-1, keepdims=True) + EPS) * g_ref[...]
    shift = mod_ref[0, :, 0:d]
    scale = mod_ref[0, :, d:2 * d]
    return (y * (1.0 + scale) + shift).astype(BF16)


def _mod_index_map(row_off, tiles_per_mod):
    return lambda i: (row_off + i // tiles_per_mod, 0, 0)


def _inproj_e_kernel(x_ref, mod_ref, g_ref, w_ref, dft_ref, xa_ref, za_ref, zb_ref, xc_ref, xs_ref):
    d = D_MODEL
    hb = _normed_input(x_ref, mod_ref, g_ref)
    xa_ref[...] = jnp.dot(hb, w_ref[:, 0:d], preferred_element_type=F32)
    za_ref[...] = jnp.dot(hb, w_ref[:, d:2 * d], preferred_element_type=F32)
    zb_ref[...] = jnp.dot(hb, w_ref[:, 3 * d:4 * d], preferred_element_type=F32)
    xb = jnp.dot(hb, w_ref[:, 2 * d:3 * d], preferred_element_type=F32).astype(BF16)
    for g in range(N_HEADS):
        cs = jnp.dot(xb[:, g * BLK:(g + 1) * BLK], dft_ref[...], preferred_element_type=F32)
        xc_ref[:, g * BLK:(g + 1) * BLK] = cs[:, 0:BLK]
        xs_ref[:, g * BLK:(g + 1) * BLK] = cs[:, BLK:2 * BLK]


def _inproj_e(x2d, mod_l, g_pre, w_in, dft_c, row_off, rows_per_mod, tm):
    m, d = x2d.shape
    row = lambda i: (i, 0)
    const = lambda i: (0, 0)
    out = jax.ShapeDtypeStruct((m, d), F32)
    vmem = 2 * (w_in.size * 2 + tm * d * 4 * 6) + 8 * tm * d * 4
    return pl.pallas_call(
        _inproj_e_kernel,
        grid=(m // tm,),
        in_specs=[pl.BlockSpec((tm, d), row),
                  pl.BlockSpec((1, 1, 3 * d), _mod_index_map(row_off, rows_per_mod // tm)),
                  pl.BlockSpec((1, d), const),
                  pl.BlockSpec(w_in.shape, const),
                  pl.BlockSpec(dft_c.shape, const)],
        out_specs=[pl.BlockSpec((tm, d), row)] * 5,
        out_shape=[out] * 5,
        compiler_params=_cparams(("parallel",), vmem),
        name="inproj_even",
    )(x2d, mod_l, g_pre, w_in, dft_c)


def _scan_chunk(a, u, reverse):
    t = a.shape[0]
    row = lax.broadcasted_iota(jnp.int32, a.shape, 0)
    d = 1
    while d < t:
        if d < SUBLANES:
            shift = (t - d) if reverse else d
            a_sh = pltpu.roll(a, shift, axis=0)
            u_sh = pltpu.roll(u, shift, axis=0)
            valid = (row < t - d) if reverse else (row >= d)
            a_sh = jnp.where(valid, a_sh, 1.0)
            u_sh = jnp.where(valid, u_sh, 0.0)
        else:
            ones = jnp.ones((d, a.shape[1]), F32)
            zeros = jnp.zeros((d, a.shape[1]), F32)
            if reverse:
                a_sh = jnp.concatenate([a[d:], ones], axis=0)
                u_sh = jnp.concatenate([u[d:], zeros], axis=0)
            else:
                a_sh = jnp.concatenate([ones, a[:t - d]], axis=0)
                u_sh = jnp.concatenate([zeros, u[:t - d]], axis=0)
        u = a * u_sh + u
        a = a * a_sh
        d *= 2
    return a, u


def _rglru_kernel(xa_ref, za_ref, cw_ref, cb_ref, wg_ref, bg_ref, lam_ref, h0_ref,
                  y_ref, hT_ref, hf_ref, hb_ref, *, seq, tc, bb):
    nchunks = seq // tc
    groups = tc // SUBLANES
    sub = lax.broadcasted_iota(jnp.int32, (SUBLANES, LANES), 0)

    def row_bcast(block, r):
        return jnp.broadcast_to(block[r:r + 1, :], (SUBLANES, LANES))

    def conv_chunk(bi, c):
        t0 = pl.multiple_of(c * tc, tc)
        xs = [xa_ref[bi, pl.ds(t0 + g, SUBLANES, stride=groups), :] for g in range(groups)]
        prev = xa_ref[bi, pl.ds(pl.multiple_of(jnp.maximum(t0 - SUBLANES, 0), SUBLANES), SUBLANES), :]
        nxt = xa_ref[bi, pl.ds(pl.multiple_of(jnp.minimum(t0 + tc, seq - SUBLANES), SUBLANES), SUBLANES), :]
        prev = jnp.where(c > 0, prev, 0.0)
        nxt = jnp.where(c < nchunks - 1, nxt, 0.0)
        before2 = jnp.where(sub == 0, row_bcast(prev, SUBLANES - 2), pltpu.roll(xs[groups - 2], 1, axis=0))
        before1 = jnp.where(sub == 0, row_bcast(prev, SUBLANES - 1), pltpu.roll(xs[groups - 1], 1, axis=0))
        after1 = jnp.where(sub == SUBLANES - 1, row_bcast(nxt, 0), pltpu.roll(xs[0], SUBLANES - 1, axis=0))
        ext = [before2, before1] + xs + [after1]
        out = []
        for g in range(groups):
            acc = cb_ref[...] + cw_ref[0:1, :] * ext[g]
            for j in range(1, CONV_W):
                acc = acc + cw_ref[j:j + 1, :] * ext[g + j]
            out.append(acc)
        return t0, jnp.concatenate(out, axis=0)

    def gates(xc, direction):
        g = jnp.dot(xc.astype(BF16), wg_ref[direction, 0], preferred_element_type=F32) + bg_ref[direction, 0]
        r = jax.nn.sigmoid(g[:, 0:BLK])
        i = jax.nn.sigmoid(g[:, BLK:2 * BLK])
        lam = lam_ref[direction]
        softplus_neg = jnp.maximum(-lam, 0.0) + jnp.log1p(jnp.exp(-jnp.abs(lam)))
        neg_log_a = LRU_C * r * softplus_neg
        a = jnp.exp(-neg_log_a)
        om = jnp.tanh(neg_log_a) * (1.0 + a * a)
        root = jnp.where(om > 0.0, om * lax.rsqrt(om), 0.0)
        return a, root * (i * xc)

    def scan_dir(bi, c, direction, carry, out_ref):
        reverse = direction == 1
        t0, xc = conv_chunk(bi, c)
        a, u = gates(xc, direction)
        order = range(groups - 1, -1, -1) if reverse else range(groups)
        p_g, s_g = [None] * groups, [None] * groups
        p_run = s_run = None
        for g in order:
            a_v = a[g * SUBLANES:(g + 1) * SUBLANES]
            u_v = u[g * SUBLANES:(g + 1) * SUBLANES]
            if p_run is None:
                p_run, s_run = a_v, u_v
            else:
                s_run = a_v * s_run + u_v
                p_run = a_v * p_run
            p_g[g], s_g[g] = p_run, s_run
        pp, ss = _scan_chunk(p_run, s_run, reverse)
        carry_b = jnp.broadcast_to(carry, (SUBLANES, LANES))
        seg_out = pp * carry_b + ss
        if reverse:
            h_in = jnp.where(sub == SUBLANES - 1, carry_b, pltpu.roll(seg_out, SUBLANES - 1, axis=0))
            new_carry = seg_out[0:1, :]
        else:
            h_in = jnp.where(sub == 0, carry_b, pltpu.roll(seg_out, 1, axis=0))
            new_carry = seg_out[SUBLANES - 1:SUBLANES, :]
        for g in range(groups):
            out_ref[bi, pl.ds(t0 + g, SUBLANES, stride=groups), :] = p_g[g] * h_in + s_g[g]
        return new_carry

    def body(j, carries):
        return tuple((scan_dir(bi, j, 0, carries[bi][0], hf_ref),
                      scan_dir(bi, nchunks - 1 - j, 1, carries[bi][1], hb_ref)) for bi in range(bb))

    init = tuple((h0_ref[bi, 0:1, :], h0_ref[bi, 1:2, :]) for bi in range(bb))
    if nchunks <= 2:
        finals = init
        for j in range(nchunks):
            finals = body(j, finals)
    else:
        finals = lax.fori_loop(0, nchunks, body, init, unroll=4)
    for bi in range(bb):
        hT_ref[bi, 0:1, :] = finals[bi][0]
        hT_ref[bi, 1:2, :] = finals[bi][1]

    def gate_out(c, carry):
        rows = pl.ds(pl.multiple_of(c * tc, tc), tc)
        for bi in range(bb):
            y_ref[bi, rows, :] = ((hf_ref[bi, rows, :] + hb_ref[bi, rows, :])
                                  * _silu(za_ref[bi, rows, :])).astype(y_ref.dtype)
        return carry

    lax.fori_loop(0, nchunks, gate_out, 0)


def _rglru(xa, za, h0, conv_w, conv_b, wg, bg, lam, tc, bb):
    b, seq, d = xa.shape
    blk = lambda i, j: (i, 0, j)
    vmem = bb * (2 * (2 * seq * BLK * 4 + seq * BLK * 2) + 2 * seq * BLK * 4) + 16 * 1024 * 1024
    return pl.pallas_call(
        functools.partial(_rglru_kernel, seq=seq, tc=tc, bb=bb),
        grid=(b // bb, d // BLK),
        in_specs=[pl.BlockSpec((bb, seq, BLK), blk),
                  pl.BlockSpec((bb, seq, BLK), blk),
                  pl.BlockSpec((CONV_W, BLK), lambda i, j: (0, j)),
                  pl.BlockSpec((1, BLK), lambda i, j: (0, j)),
                  pl.BlockSpec((2, 1, BLK, 2 * BLK), lambda i, j: (0, j, 0, 0)),
                  pl.BlockSpec((2, 1, 1, 2 * BLK), lambda i, j: (0, j, 0, 0)),
                  pl.BlockSpec((2, 1, BLK), lambda i, j: (0, 0, j)),
                  pl.BlockSpec((bb, 2, BLK), blk)],
        out_specs=[pl.BlockSpec((bb, seq, BLK), blk),
                   pl.BlockSpec((bb, 2, BLK), blk)],
        out_shape=[jax.ShapeDtypeStruct((b, seq, d), BF16),
                   jax.ShapeDtypeStruct((b, 2, d), F32)],
        scratch_shapes=[pltpu.VMEM((bb, seq, BLK), F32), pltpu.VMEM((bb, seq, BLK), F32)],
        compiler_params=_cparams(("parallel", "parallel"), vmem),
        name="rglru_scan",
    )(xa, za, conv_w, conv_b, wg, bg, lam, h0)


def _fft_list(xs):
    n = len(xs)
    if n == 1:
        return xs
    even = _fft_list(xs[0::2])
    odd = _fft_list(xs[1::2])
    out = [None] * n
    for k in range(n // 2):
        o_re, o_im = odd[k]
        if k == 0:
            t_re, t_im = o_re, o_im
        elif 4 * k == n:
            t_re, t_im = o_im, -o_re
        else:
            ang = -2.0 * math.pi * k / n
            wr, wi = math.cos(ang), math.sin(ang)
            t_re = o_re * wr - o_im * wi
            t_im = o_re * wi + o_im * wr
        e_re, e_im = even[k]
        out[k] = (e_re + t_re, e_im + t_im)
        out[k + n // 2] = (e_re - t_re, e_im - t_im)
    return out


def _fourier_kernel(xc_ref, xs_ref, zb_ref, twc_ref, tws_ref, wpos_ref, wf_ref, bf_ref, y_ref, *scratch,
                    seq, tr):
    w = xc_ref.shape[2]
    nblk = w // BLK
    l1 = seq // FFT_L2

    def epilogue(fr, rows):
        parts = []
        for kb in range(nblk):
            yb = jnp.dot(fr[:, kb * BLK:(kb + 1) * BLK].astype(BF16), wf_ref[kb], preferred_element_type=F32)
            parts.append(yb + bf_ref[:, kb * BLK:(kb + 1) * BLK])
        yb = parts[0] if nblk == 1 else jnp.concatenate(parts, axis=1)
        y_ref[0, rows, :] = (yb * _silu(zb_ref[0, rows, :])).astype(y_ref.dtype)

    if l1 == 1:
        fr = (jnp.dot(wpos_ref[:, 0:FFT_L2], xc_ref[0].astype(BF16), preferred_element_type=F32)
              + jnp.dot(wpos_ref[:, FFT_L2:2 * FFT_L2], xs_ref[0].astype(BF16), preferred_element_type=F32))
        epilogue(fr, pl.ds(0, seq))
        return

    b_ref, fr_ref = scratch
    assert w == LANES

    def butterfly(r, carry):
        r0 = pl.multiple_of(r * SUBLANES, SUBLANES)
        zs = [(xc_ref[0, pl.ds(n1 * FFT_L2 + r0, SUBLANES), :], xs_ref[0, pl.ds(n1 * FFT_L2 + r0, SUBLANES), :])
              for n1 in range(l1)]
        for k1, (a_re, a_im) in enumerate(_fft_list(zs)):
            if k1 == 0:
                b_re, b_im = a_re, a_im
            else:
                tc_ = twc_ref[k1, pl.ds(r0, SUBLANES), :]
                ts_ = tws_ref[k1, pl.ds(r0, SUBLANES), :]
                b_re = a_re * tc_ + a_im * ts_
                b_im = a_im * tc_ - a_re * ts_
            b_ref[pl.ds(r0, SUBLANES), k1 * LANES:(k1 + 1) * LANES] = b_re
            b_ref[pl.ds(FFT_L2 + r0, SUBLANES), k1 * LANES:(k1 + 1) * LANES] = b_im
        return carry

    lax.fori_loop(0, FFT_L2 // SUBLANES, butterfly, 0, unroll=2)

    per_dot = min(4, l1)
    for nb in range(l1 // per_dot):
        cols = slice(nb * per_dot * LANES, (nb + 1) * per_dot * LANES)
        fr = jnp.dot(wpos_ref[...], b_ref[:, cols].astype(BF16), preferred_element_type=F32)
        for kk in range(per_dot):
            fr_ref[pl.ds(nb * per_dot + kk, FFT_L2, stride=l1), :] = fr[:, kk * LANES:(kk + 1) * LANES]

    def finish(i, carry):
        rows = pl.ds(pl.multiple_of(i * tr, tr), tr)
        epilogue(fr_ref[rows, :], rows)
        return carry

    lax.fori_loop(0, seq // tr, finish, 0, unroll=4)


def _fourier(xc, xs, zb, twc, tws, w_pos, w_four, b_four, w_blk, tr):
    b, seq, d = xc.shape
    nblk = w_blk // BLK
    l1 = seq // FFT_L2
    blk = lambda i, j: (i, 0, j)
    scratch = []
    if l1 > 1:
        scratch = [pltpu.VMEM((2 * FFT_L2, l1 * w_blk), F32), pltpu.VMEM((seq, w_blk), F32)]
    vmem = (2 * (3 * seq * w_blk * 4 + seq * w_blk * 2 + 2 * twc.size * 4) + 3 * seq * w_blk * 4
            + 16 * 1024 * 1024)
    return pl.pallas_call(
        functools.partial(_fourier_kernel, seq=seq, tr=tr),
        grid=(b, d // w_blk),
        in_specs=[pl.BlockSpec((1, seq, w_blk), blk),
                  pl.BlockSpec((1, seq, w_blk), blk),
                  pl.BlockSpec((1, seq, w_blk), blk),
                  pl.BlockSpec(twc.shape, lambda i, j: (0, 0, 0)),
                  pl.BlockSpec(tws.shape, lambda i, j: (0, 0, 0)),
                  pl.BlockSpec(w_pos.shape, lambda i, j: (0, 0)),
                  pl.BlockSpec((nblk, BLK, BLK), lambda i, j: (j, 0, 0)),
                  pl.BlockSpec((1, w_blk), lambda i, j: (0, j))],
        out_specs=pl.BlockSpec((1, seq, w_blk), blk),
        out_shape=jax.ShapeDtypeStruct((b, seq, d), BF16),
        scratch_shapes=scratch,
        compiler_params=_cparams(("parallel", "parallel"), vmem),
        name="fourier_mix",
    )(xc, xs, zb, twc, tws, w_pos, w_four, b_four)


def _outproj_kernel(*refs, n_in):
    y_refs = refs[:n_in]
    w_refs = refs[n_in:2 * n_in]
    x_ref, mod_ref, g_ref, o_ref = refs[2 * n_in:]
    d = D_MODEL
    y = jnp.dot(y_refs[0][...], w_refs[0][...], preferred_element_type=F32)
    for k in range(1, n_in):
        y = y + jnp.dot(y_refs[k][...], w_refs[k][...], preferred_element_type=F32)
    yn = y * lax.rsqrt(jnp.mean(y * y, axis=-1, keepdims=True) + EPS) * g_ref[...]
    o_ref[...] = x_ref[...] + mod_ref[0, :, 2 * d:3 * d] * yn


def _outproj(ys, ws, x2d, mod_l, g_post, row_off, rows_per_mod, tm):
    m, d = x2d.shape
    n_in = len(ys)
    row = lambda i: (i, 0)
    const = lambda i: (0, 0)
    vmem = 2 * (sum(wk.size for wk in ws) * 2 + sum(tm * yk.shape[1] for yk in ys) * 2 + 2 * tm * d * 4) + 4 * tm * d * 4
    return pl.pallas_call(
        functools.partial(_outproj_kernel, n_in=n_in),
        grid=(m // tm,),
        in_specs=([pl.BlockSpec((tm, yk.shape[1]), row) for yk in ys]
                  + [pl.BlockSpec(wk.shape, const) for wk in ws]
                  + [pl.BlockSpec((tm, d), row),
                     pl.BlockSpec((1, 1, 3 * d), _mod_index_map(row_off, rows_per_mod // tm)),
                     pl.BlockSpec((1, d), const)]),
        out_specs=pl.BlockSpec((tm, d), row),
        out_shape=jax.ShapeDtypeStruct((m, d), F32),
        compiler_params=_cparams(("parallel",), vmem),
        name="outproj_residual",
    )(*ys, *ws, x2d, mod_l, g_post)


def _inproj_o_kernel(x_ref, mod_ref, g_ref, w_ref, wgt_ref, gb_ref, q_ref, k_ref, v_ref, o_ref, z_ref, gpt_ref,
                     gp_ref):
    d = D_MODEL
    hb = _normed_input(x_ref, mod_ref, g_ref)
    q = jnp.dot(hb, w_ref[:, 0:d], preferred_element_type=F32)
    q_ref[...] = (q * (BLK ** -0.5)).astype(q_ref.dtype)
    k_ref[...] = jnp.dot(hb, w_ref[:, d:2 * d], preferred_element_type=F32).astype(k_ref.dtype)
    for half in range(2):
        cols = slice(half * d, (half + 1) * d)
        v_ref[:, cols] = jnp.dot(hb, w_ref[:, 2 * d + half * d:3 * d + half * d],
                                 preferred_element_type=F32).astype(v_ref.dtype)
        o_ref[:, cols] = jnp.dot(hb, w_ref[:, 4 * d + half * d:5 * d + half * d], preferred_element_type=F32)
        z_ref[:, cols] = jnp.dot(hb, w_ref[:, 6 * d + half * d:7 * d + half * d], preferred_element_type=F32)
    gp_ref[...] = jnp.dot(hb, wgt_ref[...], preferred_element_type=F32)
    gpt_ref[...] = gp_ref[...].T + gb_ref[:, 0:1]


def _inproj_o(x2d, mod_l, g_pre, w_main, w_gate, gate_bias, row_off, rows_per_mod, tm):
    m, d = x2d.shape
    row = lambda i: (i, 0)
    const = lambda i: (0, 0)
    ng = w_gate.shape[1]
    vmem = 2 * (w_main.size * 2 + tm * d * 4 + tm * d * 2 * 4 + tm * 2 * d * 4 * 2) + 6 * tm * d * 4
    return pl.pallas_call(
        _inproj_o_kernel,
        grid=(m // tm,),
        in_specs=[pl.BlockSpec((tm, d), row),
                  pl.BlockSpec((1, 1, 3 * d), _mod_index_map(row_off, rows_per_mod // tm)),
                  pl.BlockSpec((1, d), const),
                  pl.BlockSpec(w_main.shape, const, pipeline_mode=pl.Buffered(1)),
                  pl.BlockSpec(w_gate.shape, const),
                  pl.BlockSpec(gate_bias.shape, const)],
        out_specs=[pl.BlockSpec((tm, d), row), pl.BlockSpec((tm, d), row),
                   pl.BlockSpec((tm, 2 * d), row), pl.BlockSpec((tm, 2 * d), row), pl.BlockSpec((tm, 2 * d), row),
                   pl.BlockSpec((ng, tm), lambda i: (0, i))],
        out_shape=[jax.ShapeDtypeStruct((m, d), BF16), jax.ShapeDtypeStruct((m, d), BF16),
                   jax.ShapeDtypeStruct((m, 2 * d), BF16), jax.ShapeDtypeStruct((m, 2 * d), F32),
                   jax.ShapeDtypeStruct((m, 2 * d), F32), jax.ShapeDtypeStruct((ng, m), F32)],
        scratch_shapes=[pltpu.VMEM((tm, ng), F32)],
        compiler_params=_cparams(("parallel",), vmem),
        name="inproj_odd",
    )(x2d, mod_l, g_pre, w_main, w_gate, gate_bias)


def _chunk_scan_lanes(v, op, identity, reverse):
    n = v.shape[-1]
    pos = lax.broadcasted_iota(jnp.int32, v.shape, 1) & (CHUNK - 1)
    d = 1
    while d < CHUNK:
        if reverse:
            shifted = jnp.where(pos < CHUNK - d, pltpu.roll(v, n - d, axis=1), identity)
        else:
            shifted = jnp.where(pos >= d, pltpu.roll(v, d, axis=1), identity)
        v = op(v, shifted)
        d *= 2
    return v


def _gate_prep_kernel(g_ref, o_ref):
    for direction in range(2):
        reverse = direction == 1
        i_pre = g_ref[2 * direction]
        b = _chunk_scan_lanes(jax.nn.log_sigmoid(g_ref[2 * direction + 1]), jnp.add, 0.0, reverse)
        a = i_pre - b
        o_ref[3 * direction] = a
        o_ref[3 * direction + 1] = b
        o_ref[3 * direction + 2] = _chunk_scan_lanes(a, jnp.maximum, -jnp.inf, reverse)
    o_ref[6] = jnp.zeros_like(g_ref[0])
    o_ref[7] = jnp.zeros_like(g_ref[0])


def _gate_prep(gq, rb):
    _, r, seq = gq.shape
    return pl.pallas_call(
        _gate_prep_kernel,
        grid=(r // rb,),
        in_specs=[pl.BlockSpec((4, rb, seq), lambda i: (0, i, 0))],
        out_specs=pl.BlockSpec((SUBLANES, rb, seq), lambda i: (0, i, 0)),
        out_shape=jax.ShapeDtypeStruct((SUBLANES, r, seq), F32),
        compiler_params=_cparams(("parallel",), 24 * 1024 * 1024),
        name="mlstm_gate_prep",
    )(gq)


def _mlstm_kernel(*refs, seq, heads, zero_init):
    q_ref, k_ref, v_ref, o_ref, z_ref, gate_ref = refs[0:6]
    refs = refs[6:]
    if not zero_init:
        c0_ref, n0_ref, m0_ref = refs[0:3]
        refs = refs[3:]
    y_ref, c_out_ref, n_out_ref, m_out_ref, hs_ref, cols_ref = refs
    t = CHUNK
    nchunks = seq // t
    half = nchunks // 2
    assert nchunks == 2 * half
    streams = [(hh, dr) for hh in range(heads) for dr in range(2)]

    def head_lanes(hh, width):
        return slice(hh * width, (hh + 1) * width)
    row_i = lax.broadcasted_iota(jnp.int32, (t, t), 0)
    col_i = lax.broadcasted_iota(jnp.int32, (t, t), 1)
    ones_blk = jnp.ones((t, LANES), BF16)

    sel_r = lax.broadcasted_iota(jnp.int32, (4 * SUBLANES, 4 * LANES), 0)
    sel_c = lax.broadcasted_iota(jnp.int32, (4 * SUBLANES, 4 * LANES), 1) // LANES
    wanted = jnp.where(sel_c == 0, 2, jnp.where(sel_c == 1, 1, jnp.where(sel_c == 2, 5, 4)))
    selector = jnp.where(((sel_r & (SUBLANES - 1)) == wanted) & (sel_r < 3 * SUBLANES), 1.0, 0.0).astype(BF16)

    piece = min(seq, 4 * t)
    for hh in range(heads):
        for c in range(seq // piece):
            g = gate_ref[0, hh, :, c * piece:(c + 1) * piece]
            hi = g.astype(BF16).astype(F32)
            mid = (g - hi).astype(BF16).astype(F32)
            lo = g - hi - mid
            stack = jnp.concatenate([hi, mid, lo, jnp.zeros_like(g)], axis=0).astype(BF16)
            cols_ref[hh, c * piece:(c + 1) * piece, :] = lax.dot_general(
                stack, selector, (((0,), (0,)), ((), ())), preferred_element_type=F32)

    def chunk_of(step, direction):
        return (nchunks - 1 - step) if direction == 1 else step

    def gate_terms(rows, hh, direction):
        g_rows = gate_ref[0, hh, :, rows]
        last = 0 if direction == 1 else t - 1
        a_row = g_rows[3 * direction:3 * direction + 1, :]
        b_end = g_rows[3 * direction + 1:3 * direction + 2, last:last + 1]
        a_end = g_rows[3 * direction + 2:3 * direction + 3, last:last + 1]
        amax_rep = cols_ref[hh, rows, 2 * direction * LANES:(2 * direction + 1) * LANES]
        return a_row, b_end, a_end, amax_rep

    def scores(step, stream, m_rep):
        hh, direction = stream
        rows = pl.ds(pl.multiple_of(chunk_of(step, direction) * t, t), t)
        a_row, b_end, a_end, amax_rep = gate_terms(rows, hh, direction)
        m_prev = m_rep[:, 0:1]
        mx = jnp.maximum(m_prev, amax_rep)
        mask = (col_i >= row_i) if direction == 1 else (col_i <= row_i)
        e = jnp.where(mask, jnp.exp(a_row - mx), 0.0)
        s = lax.dot_general(q_ref[0, rows, head_lanes(hh, BLK)], k_ref[0, rows, head_lanes(hh, BLK)],
                            (((1,), (1,)), ((), ())), preferred_element_type=F32) * e
        m_next = jnp.broadcast_to(b_end + jnp.maximum(m_prev, a_end), (1, t))
        return (s.astype(BF16), jnp.sum(s, axis=-1, keepdims=True), m_rep), m_next

    def outputs(step, stream, memory, pending):
        hh, direction = stream
        c_mat, n_rep = memory
        s_b, s_sum, m_rep = pending
        rows = pl.ds(pl.multiple_of(chunk_of(step, direction) * t, t), t)
        qc = q_ref[0, rows, head_lanes(hh, BLK)]
        kc = k_ref[0, rows, head_lanes(hh, BLK)]
        vc = v_ref[0, rows, head_lanes(hh, DV)]
        a_row, _, a_end, amax_rep = gate_terms(rows, hh, direction)
        b_rep = cols_ref[hh, rows, (2 * direction + 1) * LANES:(2 * direction + 2) * LANES]
        m_prev = m_rep[:, 0:1]
        mx_end = jnp.maximum(m_prev, a_end)
        mx = jnp.maximum(m_prev, amax_rep)
        sv = jnp.dot(s_b, vc, preferred_element_type=F32)
        qcn = jnp.dot(qc, jnp.concatenate([c_mat.astype(BF16), n_rep.astype(BF16)], axis=1),
                      preferred_element_type=F32)
        wkk_t = (kc.astype(F32).T * jnp.exp(a_row - mx_end)).astype(BF16)
        upd = jnp.dot(wkk_t, jnp.concatenate([vc, ones_blk], axis=1), preferred_element_type=F32)
        decay = jnp.exp(m_prev - mx_end)
        w_inter = jnp.exp(m_prev - mx)
        den = s_sum + w_inter * qcn[:, DV:DV + LANES]
        inv = 1.0 / jnp.maximum(jnp.abs(den), jnp.exp(-(b_rep + mx)))
        hc = jnp.concatenate(
            [(sv[:, kb * LANES:(kb + 1) * LANES] + w_inter * qcn[:, kb * LANES:(kb + 1) * LANES]) * inv
             for kb in range(DV // LANES)], axis=1)
        return rows, hc, (decay * c_mat + upd[:, 0:DV], decay * n_rep + upd[:, DV:DV + LANES])

    def n_to_columns(n_row):
        return jnp.broadcast_to(n_row, (LANES, BLK)).T

    def n_to_row(n_rep):
        return n_rep.T[0:1, :]

    def emit(hh, rows, hc, first_visit):
        if first_visit:
            hs_ref[hh, rows, :] = hc
        else:
            hsum = hs_ref[hh, rows, :] + hc
            lanes = head_lanes(hh, DV)
            y_ref[0, rows, lanes] = (jax.nn.sigmoid(o_ref[0, rows, lanes]) * hsum
                                     * _silu(z_ref[0, rows, lanes])).astype(y_ref.dtype)

    def pipelined(step, carry, first_visit, with_next):
        memories, pendings, m_reps = carry
        nxt = [scores(step + 1, sm, m_reps[i]) for i, sm in enumerate(streams)] if with_next else None
        done = [outputs(step, sm, memories[i], pendings[i]) for i, sm in enumerate(streams)]
        for (hh, _), (rows, hc, _) in zip(streams, done):
            emit(hh, rows, hc, first_visit)
        new_mem = tuple(dn[2] for dn in done)
        if with_next:
            return new_mem, tuple(n[0] for n in nxt), tuple(n[1] for n in nxt)
        return new_mem, pendings, m_reps

    if zero_init:
        memories = tuple((jnp.zeros((BLK, DV), F32), jnp.zeros((BLK, LANES), F32)) for _ in streams)
        first = [scores(0, sm, jnp.zeros((1, t), F32)) for sm in streams]
    else:
        memories = tuple((c0_ref[0, dr, hh], n_to_columns(n0_ref[0, dr, hh])) for hh, dr in streams)
        first = [scores(0, (hh, dr), m0_ref[0, dr, hh]) for hh, dr in streams]
    carry = (memories, tuple(f[0] for f in first), tuple(f[1] for f in first))
    if nchunks <= 2:
        for step in range(nchunks - 1):
            carry = pipelined(step, carry, first_visit=step < half, with_next=True)
    else:
        unroll = 2 if half % 2 == 0 else 1
        carry = lax.fori_loop(0, half, functools.partial(pipelined, first_visit=True, with_next=True), carry,
                              unroll=unroll)
        carry = lax.fori_loop(half, nchunks - 1, functools.partial(pipelined, first_visit=False, with_next=True),
                              carry, unroll=unroll)
    memories, _, m_reps = pipelined(nchunks - 1, carry, first_visit=False, with_next=False)
    for i, (hh, dr) in enumerate(streams):
        c_out_ref[0, dr, hh] = memories[i][0]
        n_out_ref[0, dr, hh] = n_to_row(memories[i][1])
        m_out_ref[0, dr, hh] = m_reps[i]


def _mlstm(q, k, v, o, z, gates, states, heads):
    b, seq, _ = q.shape
    qk_blk = lambda i, h: (i, 0, h)
    st5 = lambda i, h: (i, 0, h, 0, 0)
    state_specs = [pl.BlockSpec((1, 2, heads, BLK, DV), st5),
                   pl.BlockSpec((1, 2, heads, 1, BLK), st5),
                   pl.BlockSpec((1, 2, heads, 1, CHUNK), st5)]
    vmem = heads * (2 * (2 * seq * BLK * 2 + seq * DV * 2 * 2 + 2 * seq * DV * 4 + 2 * 2 * BLK * DV * 4)
                    + seq * DV * 4 + seq * 4 * LANES * 4) + 16 * 1024 * 1024
    return pl.pallas_call(
        functools.partial(_mlstm_kernel, seq=seq, heads=heads, zero_init=states is None),
        grid=(b, N_HEADS // heads),
        in_specs=[pl.BlockSpec((1, seq, heads * BLK), qk_blk),
                  pl.BlockSpec((1, seq, heads * BLK), qk_blk),
                  pl.BlockSpec((1, seq, heads * DV), qk_blk),
                  pl.BlockSpec((1, seq, heads * DV), qk_blk),
                  pl.BlockSpec((1, seq, heads * DV), qk_blk),
                  pl.BlockSpec((1, heads, SUBLANES, seq), lambda i, h: (i, h, 0, 0))]
        + ([] if states is None else state_specs),
        out_specs=[pl.BlockSpec((1, seq, heads * DV), qk_blk)] + state_specs,
        out_shape=[jax.ShapeDtypeStruct((b, seq, N_HEADS * DV), BF16),
                   jax.ShapeDtypeStruct((b, 2, N_HEADS, BLK, DV), F32),
                   jax.ShapeDtypeStruct((b, 2, N_HEADS, 1, BLK), F32),
                   jax.ShapeDtypeStruct((b, 2, N_HEADS, 1, CHUNK), F32)],
        scratch_shapes=[pltpu.VMEM((heads, seq, DV), F32), pltpu.VMEM((heads, seq, 4 * LANES), F32)],
        compiler_params=_cparams(("parallel", "parallel"), vmem),
        name="mlstm_chunkwise",
    )(q, k, v, o, z, gates, *(() if states is None else states))


def _dft_tables(seq):
    n = np.arange(BLK)
    ang = 2.0 * np.pi * np.outer(n, n) / BLK
    dft_c = np.concatenate([np.cos(ang), -np.sin(ang)], axis=1) / np.sqrt(BLK)
    l1 = seq // FFT_L2
    n2 = np.arange(FFT_L2)
    ang2 = 2.0 * np.pi * np.outer(n2, n2) / FFT_L2
    w_pos = np.concatenate([np.cos(ang2), np.sin(ang2)], axis=1) / np.sqrt(seq)
    angt = 2.0 * np.pi * np.outer(np.arange(l1), n2) / seq
    twc = np.broadcast_to(np.cos(angt)[:, :, None], (l1, FFT_L2, LANES))
    tws = np.broadcast_to(np.sin(angt)[:, :, None], (l1, FFT_L2, LANES))
    return (jnp.asarray(dft_c, F32).astype(BF16), jnp.asarray(w_pos, F32).astype(BF16),
            jnp.asarray(twc, F32), jnp.asarray(tws, F32))


def _run_trunk(x, mod, row_off, per_batch_mod, lru0, mlstm0, p, cfg):
    b, seq, d = x.shape
    m = b * seq
    rows_per_mod = seq if per_batch_mod else m
    x2d = x.reshape(m, d)
    dft_c, w_pos, twc, tws = _dft_tables(seq)

    mod0 = mod[0].reshape(MOD_ROWS, 1, 3 * d)
    xa, za, zb, xc, xs = _inproj_e(x2d, mod0, p["g_pre"][0:1], p["w_in_e"], dft_c, row_off, rows_per_mod,
                                   cfg["tm_in"])
    r3 = lambda a: a.reshape(b, seq, a.shape[-1])
    ya, lru_f = _rglru(r3(xa), r3(za), lru0, p["conv_w"], p["conv_b"], p["wg"], p["bg"], p["lam"], cfg["tc_lru"],
                       cfg["bb_lru"])
    yb = _fourier(r3(xc), r3(xs), r3(zb), twc, tws, w_pos, p["w_four"], p["b_four"], cfg["w_four"], cfg["tr_four"])
    x1 = _outproj([ya.reshape(m, d), yb.reshape(m, d)], [p["w_out_e"][0:d], p["w_out_e"][d:2 * d]],
                  x2d, mod0, p["g_post"][0:1], row_off, rows_per_mod, cfg["tm_out"])

    mod1 = mod[1].reshape(MOD_ROWS, 1, 3 * d)
    q, k, v, o, z, gpt = _inproj_o(x1, mod1, p["g_pre"][1:2], p["w_in_o"], p["w_gate"], p["gate_bias"],
                                   row_off, rows_per_mod, cfg["tm_in_o"])
    gq = gpt[0:4 * N_HEADS].reshape(4, N_HEADS, b, seq).transpose(0, 2, 1, 3).reshape(4, b * N_HEADS, seq)
    gates = _gate_prep(gq, cfg["rb_gate"]).reshape(SUBLANES, b, N_HEADS, seq).transpose(1, 2, 0, 3)
    y, c_f, n_f, m_f = _mlstm(r3(q), r3(k), r3(v), r3(o), r3(z), gates, mlstm0, cfg["heads_mlstm"])
    x2 = _outproj([y.reshape(m, 2 * d)], [p["w_out_o"]], x1, mod1, p["g_post"][1:2], row_off, rows_per_mod,
                  cfg["tm_out"])
    return x2.reshape(b, seq, d), lru_f, c_f, n_f, m_f


def _prepare_params(conv_w, conv_b, w_rg, b_rg, w_ig, b_ig, lru_lambda, w_four, b_four, w_in_e, w_out_e,
                    w_in_o, b_if, w_out_o, g_pre, g_post):
    d = D_MODEL
    wg = jnp.concatenate([w_rg[0], w_ig[0]], axis=-1).astype(BF16)
    bg = jnp.concatenate([b_rg[0].reshape(2, N_HEADS, 1, BLK), b_ig[0].reshape(2, N_HEADS, 1, BLK)], axis=-1)
    n_gate = 4 * N_HEADS
    w_gate = jnp.zeros((d, LANES), F32).at[:, 0:n_gate].set(w_in_o[0][:, 8 * d:8 * d + n_gate]).astype(BF16)
    bias = jnp.zeros((LANES,), F32).at[0:n_gate].set(b_if[0].reshape(n_gate))
    gate_bias = jnp.broadcast_to(bias[:, None], (LANES, LANES))
    return dict(
        g_pre=g_pre, g_post=g_post,
        w_in_e=w_in_e[0].astype(BF16), w_out_e=w_out_e[0].astype(BF16),
        conv_w=conv_w[0], conv_b=conv_b[0].reshape(1, d), wg=wg, bg=bg, lam=lru_lambda[0].reshape(2, 1, d),
        w_four=w_four[0].astype(BF16), b_four=b_four[0].reshape(1, d),
        w_in_o=w_in_o[0].astype(BF16), w_gate=w_gate, gate_bias=gate_bias,
        w_out_o=w_out_o[0].astype(BF16))


def kernel(x_prompt, x_sample, c, state_lru, state_mlstm_C, state_mlstm_n, state_mlstm_m, c_ctx, w_mod, b_mod,
           g_pre, g_post, w_in_e, conv_w, conv_b, w_rg, b_rg, w_ig, b_ig, lru_lambda, w_four, b_four, w_out_e,
           w_in_o, b_if, w_out_o):
    d = D_MODEL
    bp = x_prompt.shape[0]
    bs = x_sample.shape[0]
    p = _prepare_params(conv_w, conv_b, w_rg, b_rg, w_ig, b_ig, lru_lambda, w_four, b_four, w_in_e, w_out_e,
                        w_in_o, b_if, w_out_o, g_pre, g_post)
    cond = jnp.concatenate([c_ctx[None, :], c, jnp.zeros((MOD_ROWS - 1 - bs, d), F32)], axis=0)
    mod = _modulation(cond, w_mod, b_mod)

    cfg_p = dict(tm_in=512, tm_in_o=512, tm_out=1024, tc_lru=128, bb_lru=4 if bp % 4 == 0 else 1, w_four=d,
                 tr_four=256, rb_gate=min(bp * N_HEADS, 128), heads_mlstm=4)
    y_prompt, lru_f, c_f, n_f, m_f = _run_trunk(x_prompt, mod, 0, False, jnp.zeros((bp, 2, d), F32), None, p, cfg_p)

    cfg_s = dict(tm_in=512, tm_in_o=512, tm_out=1024, tc_lru=128, bb_lru=1, w_four=BLK, tr_four=256,
                 rb_gate=SUBLANES, heads_mlstm=1)
    mlstm0 = (state_mlstm_C[:, 0], state_mlstm_n[:, 0].reshape(bs, 2, N_HEADS, 1, BLK),
              jnp.broadcast_to(state_mlstm_m[:, 0].reshape(bs, 2, N_HEADS, 1, 1), (bs, 2, N_HEADS, 1, CHUNK)))
    y_sample, _, _, _, _ = _run_trunk(x_sample, mod, 1, True, state_lru[:, 0], mlstm0, p, cfg_s)

    return (y_prompt, y_sample, lru_f[:, None], c_f[:, None], n_f[:, None, :, :, 0, :], m_f[:, None, :, :, 0, 0])
```

```python
import functools
import math

import numpy as np
import jax
import jax.numpy as jnp
from jax import lax
from jax.experimental import pallas as pl
from jax.experimental.pallas import tpu as pltpu

F32 = jnp.float32
BF16 = jnp.bfloat16

D_MODEL = 1024
DEPTH = 2
EPS = 1e-6
LRU_C = 8.0
CONV_W = 4
N_HEADS = 8
BLK = D_MODEL // N_HEADS
DV = 2 * BLK
CHUNK = 128
FFT_L2 = 256
MOD_ROWS = 8

LANES = 128
SUBLANES = 8
VMEM_LIMIT_CAP = 56 * 1024 * 1024


def _cparams(sem, vmem_bytes, flags=None):
    return pltpu.CompilerParams(dimension_semantics=sem, flags=flags,
                                vmem_limit_bytes=int(min(max(vmem_bytes, 16 * 1024 * 1024), VMEM_LIMIT_CAP)))


def _silu(x):
    return x * jax.nn.sigmoid(x)


def _mod_kernel(cond_ref, w_ref, b_ref, o_ref):
    s = _silu(cond_ref[...]).astype(BF16)
    o_ref[0] = jnp.dot(s, w_ref[0].astype(BF16), preferred_element_type=F32) + b_ref[0]


def _modulation(cond, w_mod, b_mod):
    d = D_MODEL
    return pl.pallas_call(
        _mod_kernel,
        grid=(DEPTH, 3),
        in_specs=[pl.BlockSpec((MOD_ROWS, d), lambda l, j: (0, 0)),
                  pl.BlockSpec((1, d, d), lambda l, j: (l, 0, j)),
                  pl.BlockSpec((1, 1, d), lambda l, j: (l, 0, j))],
        out_specs=pl.BlockSpec((1, MOD_ROWS, d), lambda l, j: (l, 0, j)),
        out_shape=jax.ShapeDtypeStruct((DEPTH, MOD_ROWS, 3 * d), F32),
        compiler_params=_cparams(("arbitrary", "arbitrary"), 24 * 1024 * 1024),
        name="adaln_mod",
    )(cond, w_mod, b_mod.reshape(DEPTH, 1, 3 * d))


def _normed_input(x_ref, mod_ref, g_ref):
    d = D_MODEL
    x = x_ref[...]
    y = x * lax.rsqrt(jnp.mean(x * x, axis=-1, keepdims=True) + EPS) * g_ref[...]
    shift = mod_ref[0, :, 0:d]
    scale = mod_ref[0, :, d:2 * d]
    return (y * (1.0 + scale) + shift).astype(BF16)


def _mod_index_map(row_off, tiles_per_mod):
    return lambda i: (row_off + i // tiles_per_mod, 0, 0)


def _inproj_e_kernel(x_ref, mod_ref, g_ref, w_ref, dft_ref, xa_ref, za_ref, zb_ref, xc_ref, xs_ref):
    d = D_MODEL
    hb = _normed_input(x_ref, mod_ref, g_ref)
    xa_ref[...] = jnp.dot(hb, w_ref[:, 0:d], preferred_element_type=F32)
    za_ref[...] = jnp.dot(hb, w_ref[:, d:2 * d], preferred_element_type=F32)
    zb_ref[...] = jnp.dot(hb, w_ref[:, 3 * d:4 * d], preferred_element_type=F32)
    xb = jnp.dot(hb, w_ref[:, 2 * d:3 * d], preferred_element_type=F32).astype(BF16)
    for g in range(N_HEADS):
        cs = jnp.dot(xb[:, g * BLK:(g + 1) * BLK], dft_ref[...], preferred_element_type=F32)
        xc_ref[:, g * BLK:(g + 1) * BLK] = cs[:, 0:BLK]
        xs_ref[:, g * BLK:(g + 1) * BLK] = cs[:, BLK:2 * BLK]


def _inproj_e(x2d, mod_l, g_pre, w_in, dft_c, row_off, rows_per_mod, tm):
    m, d = x2d.shape
    row = lambda i: (i, 0)
    const = lambda i: (0, 0)
    out = jax.ShapeDtypeStruct((m, d), F32)
    vmem = 2 * (w_in.size * 2 + tm * d * 4 * 6) + 8 * tm * d * 4
    return pl.pallas_call(
        _inproj_e_kernel,
        grid=(m // tm,),
        in_specs=[pl.BlockSpec((tm, d), row),
                  pl.BlockSpec((1, 1, 3 * d), _mod_index_map(row_off, rows_per_mod // tm)),
                  pl.BlockSpec((1, d), const),
                  pl.BlockSpec(w_in.shape, const),
                  pl.BlockSpec(dft_c.shape, const)],
        out_specs=[pl.BlockSpec((tm, d), row)] * 5,
        out_shape=[out] * 5,
        compiler_params=_cparams(("parallel",), vmem),
        name="inproj_even",
    )(x2d, mod_l, g_pre, w_in, dft_c)


def _scan_chunk(a, u, reverse):
    t = a.shape[0]
    row = lax.broadcasted_iota(jnp.int32, a.shape, 0)
    d = 1
    while d < t:
        if d < SUBLANES:
            shift = (t - d) if reverse else d
            a_sh = pltpu.roll(a, shift, axis=0)
            u_sh = pltpu.roll(u, shift, axis=0)
            valid = (row < t - d) if reverse else (row >= d)
            a_sh = jnp.where(valid, a_sh, 1.0)
            u_sh = jnp.where(valid, u_sh, 0.0)
        else:
            ones = jnp.ones((d, a.shape[1]), F32)
            zeros = jnp.zeros((d, a.shape[1]), F32)
            if reverse:
                a_sh = jnp.concatenate([a[d:], ones], axis=0)
                u_sh = jnp.concatenate([u[d:], zeros], axis=0)
            else:
                a_sh = jnp.concatenate([ones, a[:t - d]], axis=0)
                u_sh = jnp.concatenate([zeros, u[:t - d]], axis=0)
        u = a * u_sh + u
        a = a * a_sh
        d *= 2
    return a, u


def _rglru_kernel(xa_ref, za_ref, cw_ref, cb_ref, wg_ref, bg_ref, lam_ref, h0_ref,
                  y_ref, hT_ref, hf_ref, hb_ref, xc_ref, *, seq, tc, bb):
    nchunks = seq // tc
    groups = tc // SUBLANES
    sub = lax.broadcasted_iota(jnp.int32, (SUBLANES, LANES), 0)

    def row_bcast(block, r):
        return jnp.broadcast_to(block[r:r + 1, :], (SUBLANES, LANES))

    def conv_chunk(bi, c):
        t0 = pl.multiple_of(c * tc, tc)
        xs = [xa_ref[bi, pl.ds(t0 + g, SUBLANES, stride=groups), :] for g in range(groups)]
        prev = xa_ref[bi, pl.ds(pl.multiple_of(jnp.maximum(t0 - SUBLANES, 0), SUBLANES), SUBLANES), :]
        nxt = xa_ref[bi, pl.ds(pl.multiple_of(jnp.minimum(t0 + tc, seq - SUBLANES), SUBLANES), SUBLANES), :]
        prev = jnp.where(c > 0, prev, 0.0)
        nxt = jnp.where(c < nchunks - 1, nxt, 0.0)
        before2 = jnp.where(sub == 0, row_bcast(prev, SUBLANES - 2), pltpu.roll(xs[groups - 2], 1, axis=0))
        before1 = jnp.where(sub == 0, row_bcast(prev, SUBLANES - 1), pltpu.roll(xs[groups - 1], 1, axis=0))
        after1 = jnp.where(sub == SUBLANES - 1, row_bcast(nxt, 0), pltpu.roll(xs[0], SUBLANES - 1, axis=0))
        ext = [before2, before1] + xs + [after1]
        out = []
        for g in range(groups):
            acc = cb_ref[...] + cw_ref[0:1, :] * ext[g]
            for j in range(1, CONV_W):
                acc = acc + cw_ref[j:j + 1, :] * ext[g + j]
            out.append(acc)
        return t0, jnp.concatenate(out, axis=0)

    def gates(xc, direction):
        g = jnp.dot(xc.astype(BF16), wg_ref[direction, 0], preferred_element_type=F32) + bg_ref[direction, 0]
        r = jax.nn.sigmoid(g[:, 0:BLK])
        i = jax.nn.sigmoid(g[:, BLK:2 * BLK])
        lam = lam_ref[direction]
        softplus_neg = jnp.maximum(-lam, 0.0) + jnp.log1p(jnp.exp(-jnp.abs(lam)))
        neg_log_a = LRU_C * r * softplus_neg
        a = jnp.exp(-neg_log_a)
        om = jnp.tanh(neg_log_a) * (1.0 + a * a)
        root = jnp.where(om > 0.0, om * lax.rsqrt(om), 0.0)
        return a, root * (i * xc)

    def scan_dir(bi, c, direction, carry, out_ref, first_visit):
        reverse = direction == 1
        if first_visit:
            t0, xc = conv_chunk(bi, c)
            xc_ref[bi, pl.ds(t0, tc), :] = xc
        else:
            t0 = pl.multiple_of(c * tc, tc)
            xc = xc_ref[bi, pl.ds(t0, tc), :]
        a, u = gates(xc, direction)
        order = range(groups - 1, -1, -1) if reverse else range(groups)
        p_g, s_g = [None] * groups, [None] * groups
        p_run = s_run = None
        for g in order:
            a_v = a[g * SUBLANES:(g + 1) * SUBLANES]
            u_v = u[g * SUBLANES:(g + 1) * SUBLANES]
            if p_run is None:
                p_run, s_run = a_v, u_v
            else:
                s_run = a_v * s_run + u_v
                p_run = a_v * p_run
            p_g[g], s_g[g] = p_run, s_run
        pp, ss = _scan_chunk(p_run, s_run, reverse)
        carry_b = jnp.broadcast_to(carry, (SUBLANES, LANES))
        seg_out = pp * carry_b + ss
        if reverse:
            h_in = jnp.where(sub == SUBLANES - 1, carry_b, pltpu.roll(seg_out, SUBLANES - 1, axis=0))
            new_carry = seg_out[0:1, :]
        else:
            h_in = jnp.where(sub == 0, carry_b, pltpu.roll(seg_out, 1, axis=0))
            new_carry = seg_out[SUBLANES - 1:SUBLANES, :]
        for g in range(groups):
            out_ref[bi, pl.ds(t0 + g, SUBLANES, stride=groups), :] = p_g[g] * h_in + s_g[g]
        return new_carry

    def body(j, carries, first_visit):
        return tuple((scan_dir(bi, j, 0, carries[bi][0], hf_ref, first_visit),
                      scan_dir(bi, nchunks - 1 - j, 1, carries[bi][1], hb_ref, first_visit)) for bi in range(bb))

    half = nchunks // 2
    assert nchunks == 2 * half
    init = tuple((h0_ref[bi, 0:1, :], h0_ref[bi, 1:2, :]) for bi in range(bb))
    if nchunks <= 2:
        finals = init
        for j in range(nchunks):
            finals = body(j, finals, j < half)
    else:
        unroll = 4 if half % 4 == 0 else 1
        finals = lax.fori_loop(0, half, functools.partial(body, first_visit=True), init, unroll=unroll)
        finals = lax.fori_loop(half, nchunks, functools.partial(body, first_visit=False), finals, unroll=unroll)
    for bi in range(bb):
        hT_ref[bi, 0:1, :] = finals[bi][0]
        hT_ref[bi, 1:2, :] = finals[bi][1]

    def gate_out(c, carry):
        rows = pl.ds(pl.multiple_of(c * tc, tc), tc)
        for bi in range(bb):
            y_ref[bi, rows, :] = ((hf_ref[bi, rows, :] + hb_ref[bi, rows, :])
                                  * _silu(za_ref[bi, rows, :])).astype(y_ref.dtype)
        return carry

    lax.fori_loop(0, nchunks, gate_out, 0, unroll=2 if nchunks > 2 else 1)


def _rglru(xa, za, h0, conv_w, conv_b, wg, bg, lam, tc, bb):
    b, seq, d = xa.shape
    blk = lambda i, j: (i, 0, j)
    vmem = bb * (2 * (2 * seq * BLK * 4 + seq * BLK * 2) + 3 * seq * BLK * 4) + 16 * 1024 * 1024
    return pl.pallas_call(
        functools.partial(_rglru_kernel, seq=seq, tc=tc, bb=bb),
        grid=(b // bb, d // BLK),
        in_specs=[pl.BlockSpec((bb, seq, BLK), blk),
                  pl.BlockSpec((bb, seq, BLK), blk),
                  pl.BlockSpec((CONV_W, BLK), lambda i, j: (0, j)),
                  pl.BlockSpec((1, BLK), lambda i, j: (0, j)),
                  pl.BlockSpec((2, 1, BLK, 2 * BLK), lambda i, j: (0, j, 0, 0)),
                  pl.BlockSpec((2, 1, 1, 2 * BLK), lambda i, j: (0, j, 0, 0)),
                  pl.BlockSpec((2, 1, BLK), lambda i, j: (0, 0, j)),
                  pl.BlockSpec((bb, 2, BLK), blk)],
        out_specs=[pl.BlockSpec((bb, seq, BLK), blk),
                   pl.BlockSpec((bb, 2, BLK), blk)],
        out_shape=[jax.ShapeDtypeStruct((b, seq, d), BF16),
                   jax.ShapeDtypeStruct((b, 2, d), F32)],
        scratch_shapes=[pltpu.VMEM((bb, seq, BLK), F32)] * 3,
        compiler_params=_cparams(("parallel", "parallel"), vmem),
        name="rglru_scan",
    )(xa, za, conv_w, conv_b, wg, bg, lam, h0)


def _fft_list(xs):
    n = len(xs)
    if n == 1:
        return xs
    even = _fft_list(xs[0::2])
    odd = _fft_list(xs[1::2])
    out = [None] * n
    for k in range(n // 2):
        o_re, o_im = odd[k]
        if k == 0:
            t_re, t_im = o_re, o_im
        elif 4 * k == n:
            t_re, t_im = o_im, -o_re
        else:
            ang = -2.0 * math.pi * k / n
            wr, wi = math.cos(ang), math.sin(ang)
            t_re = o_re * wr - o_im * wi
            t_im = o_re * wi + o_im * wr
        e_re, e_im = even[k]
        out[k] = (e_re + t_re, e_im + t_im)
        out[k + n // 2] = (e_re - t_re, e_im - t_im)
    return out


def _fourier_kernel(xc_ref, xs_ref, zb_ref, twc_ref, tws_ref, wpos_ref, wf_ref, bf_ref, y_ref, *scratch,
                    seq, tr):
    w = xc_ref.shape[2]
    nblk = w // BLK
    l1 = seq // FFT_L2

    def epilogue(fr, rows):
        parts = []
        for kb in range(nblk):
            yb = jnp.dot(fr[:, kb * BLK:(kb + 1) * BLK].astype(BF16), wf_ref[kb], preferred_element_type=F32)
            parts.append(yb + bf_ref[:, kb * BLK:(kb + 1) * BLK])
        yb = parts[0] if nblk == 1 else jnp.concatenate(parts, axis=1)
        y_ref[0, rows, :] = (yb * _silu(zb_ref[0, rows, :])).astype(y_ref.dtype)

    if l1 == 1:
        fr = (jnp.dot(wpos_ref[:, 0:FFT_L2], xc_ref[0].astype(BF16), preferred_element_type=F32)
              + jnp.dot(wpos_ref[:, FFT_L2:2 * FFT_L2], xs_ref[0].astype(BF16), preferred_element_type=F32))
        epilogue(fr, pl.ds(0, seq))
        return

    b_ref, fr_ref = scratch
    assert w == LANES

    def butterfly(r, carry):
        r0 = pl.multiple_of(r * SUBLANES, SUBLANES)
        zs = [(xc_ref[0, pl.ds(n1 * FFT_L2 + r0, SUBLANES), :], xs_ref[0, pl.ds(n1 * FFT_L2 + r0, SUBLANES), :])
              for n1 in range(l1)]
        for k1, (a_re, a_im) in enumerate(_fft_list(zs)):
            if k1 == 0:
                b_re, b_im = a_re, a_im
            else:
                tc_ = twc_ref[k1, pl.ds(r0, SUBLANES), :]
                ts_ = tws_ref[k1, pl.ds(r0, SUBLANES), :]
                b_re = a_re * tc_ + a_im * ts_
                b_im = a_im * tc_ - a_re * ts_
            b_ref[pl.ds(r0, SUBLANES), k1 * LANES:(k1 + 1) * LANES] = b_re
            b_ref[pl.ds(FFT_L2 + r0, SUBLANES), k1 * LANES:(k1 + 1) * LANES] = b_im
        return carry

    lax.fori_loop(0, FFT_L2 // SUBLANES, butterfly, 0, unroll=2)

    per_dot = min(4, l1)
    for nb in range(l1 // per_dot):
        cols = slice(nb * per_dot * LANES, (nb + 1) * per_dot * LANES)
        fr = jnp.dot(wpos_ref[...], b_ref[:, cols].astype(BF16), preferred_element_type=F32)
        for kk in range(per_dot):
            fr_ref[pl.ds(nb * per_dot + kk, FFT_L2, stride=l1), :] = fr[:, kk * LANES:(kk + 1) * LANES]

    def finish(i, carry):
        rows = pl.ds(pl.multiple_of(i * tr, tr), tr)
        epilogue(fr_ref[rows, :], rows)
        return carry

    lax.fori_loop(0, seq // tr, finish, 0, unroll=4)


def _fourier(xc, xs, zb, twc, tws, w_pos, w_four, b_four, w_blk, tr):
    b, seq, d = xc.shape
    nblk = w_blk // BLK
    l1 = seq // FFT_L2
    blk = lambda i, j: (i, 0, j)
    scratch = []
    if l1 > 1:
        scratch = [pltpu.VMEM((2 * FFT_L2, l1 * w_blk), F32), pltpu.VMEM((seq, w_blk), F32)]
    vmem = (2 * (3 * seq * w_blk * 4 + seq * w_blk * 2 + 2 * twc.size * 4) + 3 * seq * w_blk * 4
            + 16 * 1024 * 1024)
    return pl.pallas_call(
        functools.partial(_fourier_kernel, seq=seq, tr=tr),
        grid=(b, d // w_blk),
        in_specs=[pl.BlockSpec((1, seq, w_blk), blk),
                  pl.BlockSpec((1, seq, w_blk), blk),
                  pl.BlockSpec((1, seq, w_blk), blk),
                  pl.BlockSpec(twc.shape, lambda i, j: (0, 0, 0)),
                  pl.BlockSpec(tws.shape, lambda i, j: (0, 0, 0)),
                  pl.BlockSpec(w_pos.shape, lambda i, j: (0, 0)),
                  pl.BlockSpec((nblk, BLK, BLK), lambda i, j: (j, 0, 0)),
                  pl.BlockSpec((1, w_blk), lambda i, j: (0, j))],
        out_specs=pl.BlockSpec((1, seq, w_blk), blk),
        out_shape=jax.ShapeDtypeStruct((b, seq, d), BF16),
        scratch_shapes=scratch,
        compiler_params=_cparams(("parallel", "parallel"), vmem),
        name="fourier_mix",
    )(xc, xs, zb, twc, tws, w_pos, w_four, b_four)


def _outproj_kernel(*refs, n_in):
    y_refs = refs[:n_in]
    w_refs = refs[n_in:2 * n_in]
    x_ref, mod_ref, g_ref, o_ref = refs[2 * n_in:]
    d = D_MODEL
    y = jnp.dot(y_refs[0][...], w_refs[0][...], preferred_element_type=F32)
    for k in range(1, n_in):
        y = y + jnp.dot(y_refs[k][...], w_refs[k][...], preferred_element_type=F32)
    yn = y * lax.rsqrt(jnp.mean(y * y, axis=-1, keepdims=True) + EPS) * g_ref[...]
    o_ref[...] = x_ref[...] + mod_ref[0, :, 2 * d:3 * d] * yn


def _outproj(ys, ws, x2d, mod_l, g_post, row_off, rows_per_mod, tm):
    m, d = x2d.shape
    n_in = len(ys)
    row = lambda i: (i, 0)
    const = lambda i: (0, 0)
    vmem = 2 * (sum(wk.size for wk in ws) * 2 + sum(tm * yk.shape[1] for yk in ys) * 2 + 2 * tm * d * 4) + 4 * tm * d * 4
    return pl.pallas_call(
        functools.partial(_outproj_kernel, n_in=n_in),
        grid=(m // tm,),
        in_specs=([pl.BlockSpec((tm, yk.shape[1]), row) for yk in ys]
                  + [pl.BlockSpec(wk.shape, const) for wk in ws]
                  + [pl.BlockSpec((tm, d), row),
                     pl.BlockSpec((1, 1, 3 * d), _mod_index_map(row_off, rows_per_mod // tm)),
                     pl.BlockSpec((1, d), const)]),
        out_specs=pl.BlockSpec((tm, d), row),
        out_shape=jax.ShapeDtypeStruct((m, d), F32),
        compiler_params=_cparams(("parallel",), vmem),
        name="outproj_residual",
    )(*ys, *ws, x2d, mod_l, g_post)


def _inproj_o_kernel(x_ref, mod_ref, g_ref, w_ref, wgt_ref, gb_ref, q_ref, k_ref, v_ref, o_ref, z_ref, gpt_ref,
                     gp_ref):
    d = D_MODEL
    hb = _normed_input(x_ref, mod_ref, g_ref)
    q = jnp.dot(hb, w_ref[:, 0:d], preferred_element_type=F32)
    q_ref[...] = (q * (BLK ** -0.5)).astype(q_ref.dtype)
    k_ref[...] = jnp.dot(hb, w_ref[:, d:2 * d], preferred_element_type=F32).astype(k_ref.dtype)
    for half in range(2):
        cols = slice(half * d, (half + 1) * d)
        v_ref[:, cols] = jnp.dot(hb, w_ref[:, 2 * d + half * d:3 * d + half * d],
                                 preferred_element_type=F32).astype(v_ref.dtype)
        o_ref[:, cols] = jnp.dot(hb, w_ref[:, 4 * d + half * d:5 * d + half * d], preferred_element_type=F32)
        z_ref[:, cols] = jnp.dot(hb, w_ref[:, 6 * d + half * d:7 * d + half * d], preferred_element_type=F32)
    gp_ref[...] = jnp.dot(hb, wgt_ref[...], preferred_element_type=F32)
    gpt_ref[...] = gp_ref[...].T + gb_ref[:, 0:1]


def _inproj_o(x2d, mod_l, g_pre, w_main, w_gate, gate_bias, row_off, rows_per_mod, tm):
    m, d = x2d.shape
    row = lambda i: (i, 0)
    const = lambda i: (0, 0)
    ng = w_gate.shape[1]
    vmem = 2 * (w_main.size * 2 + tm * d * 4 + tm * d * 2 * 4 + tm * 2 * d * 4 * 2) + 6 * tm * d * 4
    return pl.pallas_call(
        _inproj_o_kernel,
        grid=(m // tm,),
        in_specs=[pl.BlockSpec((tm, d), row),
                  pl.BlockSpec((1, 1, 3 * d), _mod_index_map(row_off, rows_per_mod // tm)),
                  pl.BlockSpec((1, d), const),
                  pl.BlockSpec(w_main.shape, const, pipeline_mode=pl.Buffered(1)),
                  pl.BlockSpec(w_gate.shape, const),
                  pl.BlockSpec(gate_bias.shape, const)],
        out_specs=[pl.BlockSpec((tm, d), row), pl.BlockSpec((tm, d), row),
                   pl.BlockSpec((tm, 2 * d), row), pl.BlockSpec((tm, 2 * d), row), pl.BlockSpec((tm, 2 * d), row),
                   pl.BlockSpec((ng, tm), lambda i: (0, i))],
        out_shape=[jax.ShapeDtypeStruct((m, d), BF16), jax.ShapeDtypeStruct((m, d), BF16),
                   jax.ShapeDtypeStruct((m, 2 * d), BF16), jax.ShapeDtypeStruct((m, 2 * d), F32),
                   jax.ShapeDtypeStruct((m, 2 * d), F32), jax.ShapeDtypeStruct((ng, m), F32)],
        scratch_shapes=[pltpu.VMEM((tm, ng), F32)],
        compiler_params=_cparams(("parallel",), vmem),
        name="inproj_odd",
    )(x2d, mod_l, g_pre, w_main, w_gate, gate_bias)


def _chunk_scan_lanes(v, op, identity, reverse):
    n = v.shape[-1]
    pos = lax.broadcasted_iota(jnp.int32, v.shape, 1) & (CHUNK - 1)
    d = 1
    while d < CHUNK:
        if reverse:
            shifted = jnp.where(pos < CHUNK - d, pltpu.roll(v, n - d, axis=1), identity)
        else:
            shifted = jnp.where(pos >= d, pltpu.roll(v, d, axis=1), identity)
        v = op(v, shifted)
        d *= 2
    return v


def _gate_prep_kernel(g_ref, o_ref):
    for direction in range(2):
        reverse = direction == 1
        i_pre = g_ref[2 * direction]
        b = _chunk_scan_lanes(jax.nn.log_sigmoid(g_ref[2 * direction + 1]), jnp.add, 0.0, reverse)
        a = i_pre - b
        o_ref[3 * direction] = a
        o_ref[3 * direction + 1] = b
        o_ref[3 * direction + 2] = _chunk_scan_lanes(a, jnp.maximum, -jnp.inf, reverse)
    o_ref[6] = jnp.zeros_like(g_ref[0])
    o_ref[7] = jnp.zeros_like(g_ref[0])


def _gate_prep(gq, rb):
    _, r, seq = gq.shape
    return pl.pallas_call(
        _gate_prep_kernel,
        grid=(r // rb,),
        in_specs=[pl.BlockSpec((4, rb, seq), lambda i: (0, i, 0))],
        out_specs=pl.BlockSpec((SUBLANES, rb, seq), lambda i: (0, i, 0)),
        out_shape=jax.ShapeDtypeStruct((SUBLANES, r, seq), F32),
        compiler_params=_cparams(("parallel",), 24 * 1024 * 1024),
        name="mlstm_gate_prep",
    )(gq)


def _mlstm_kernel(*refs, seq, heads, zero_init):
    q_ref, k_ref, v_ref, o_ref, z_ref, gate_ref = refs[0:6]
    refs = refs[6:]
    if not zero_init:
        c0_ref, n0_ref, m0_ref = refs[0:3]
        refs = refs[3:]
    y_ref, c_out_ref, n_out_ref, m_out_ref, hs_ref, cols_ref = refs
    t = CHUNK
    nchunks = seq // t
    half = nchunks // 2
    assert nchunks == 2 * half
    streams = [(hh, dr) for hh in range(heads) for dr in range(2)]

    def head_lanes(hh, width):
        return slice(hh * width, (hh + 1) * width)
    row_i = lax.broadcasted_iota(jnp.int32, (t, t), 0)
    col_i = lax.broadcasted_iota(jnp.int32, (t, t), 1)
    ones_blk = jnp.ones((t, LANES), BF16)

    sel_r = lax.broadcasted_iota(jnp.int32, (4 * SUBLANES, 4 * LANES), 0)
    sel_c = lax.broadcasted_iota(jnp.int32, (4 * SUBLANES, 4 * LANES), 1) // LANES
    wanted = jnp.where(sel_c == 0, 2, jnp.where(sel_c == 1, 1, jnp.where(sel_c == 2, 5, 4)))
    selector = jnp.where(((sel_r & (SUBLANES - 1)) == wanted) & (sel_r < 3 * SUBLANES), 1.0, 0.0).astype(BF16)

    piece = min(seq, 4 * t)
    for hh in range(heads):
        for c in range(seq // piece):
            g = gate_ref[0, hh, :, c * piece:(c + 1) * piece]
            hi = g.astype(BF16).astype(F32)
            mid = (g - hi).astype(BF16).astype(F32)
            lo = g - hi - mid
            stack = jnp.concatenate([hi, mid, lo, jnp.zeros_like(g)], axis=0).astype(BF16)
            cols_ref[hh, c * piece:(c + 1) * piece, :] = lax.dot_general(
                stack, selector, (((0,), (0,)), ((), ())), preferred_element_type=F32)

    def chunk_of(step, direction):
        return (nchunks - 1 - step) if direction == 1 else step

    def gate_terms(rows, hh, direction):
        g_rows = gate_ref[0, hh, :, rows]
        last = 0 if direction == 1 else t - 1
        a_row = g_rows[3 * direction:3 * direction + 1, :]
        b_end = g_rows[3 * direction + 1:3 * direction + 2, last:last + 1]
        a_end = g_rows[3 * direction + 2:3 * direction + 3, last:last + 1]
        amax_rep = cols_ref[hh, rows, 2 * direction * LANES:(2 * direction + 1) * LANES]
        return a_row, b_end, a_end, amax_rep

    def scores(step, stream, m_rep):
        hh, direction = stream
        rows = pl.ds(pl.multiple_of(chunk_of(step, direction) * t, t), t)
        a_row, b_end, a_end, amax_rep = gate_terms(rows, hh, direction)
        m_prev = m_rep[:, 0:1]
        mx = jnp.maximum(m_prev, amax_rep)
        mask = (col_i >= row_i) if direction == 1 else (col_i <= row_i)
        e = jnp.where(mask, jnp.exp(a_row - mx), 0.0)
        s = lax.dot_general(q_ref[0, rows, head_lanes(hh, BLK)], k_ref[0, rows, head_lanes(hh, BLK)],
                            (((1,), (1,)), ((), ())), preferred_element_type=F32) * e
        m_next = jnp.broadcast_to(b_end + jnp.maximum(m_prev, a_end), (1, t))
        return (s.astype(BF16), jnp.sum(s, axis=-1, keepdims=True), m_rep), m_next

    def outputs(step, stream, memory, pending):
        hh, direction = stream
        c_mat, n_rep = memory
        s_b, s_sum, m_rep = pending
        rows = pl.ds(pl.multiple_of(chunk_of(step, direction) * t, t), t)
        qc = q_ref[0, rows, head_lanes(hh, BLK)]
        kc = k_ref[0, rows, head_lanes(hh, BLK)]
        vc = v_ref[0, rows, head_lanes(hh, DV)]
        a_row, _, a_end, amax_rep = gate_terms(rows, hh, direction)
        b_rep = cols_ref[hh, rows, (2 * direction + 1) * LANES:(2 * direction + 2) * LANES]
        m_prev = m_rep[:, 0:1]
        mx_end = jnp.maximum(m_prev, a_end)
        mx = jnp.maximum(m_prev, amax_rep)
        sv = jnp.dot(s_b, vc, preferred_element_type=F32)
        qcn = jnp.dot(qc, jnp.concatenate([c_mat.astype(BF16), n_rep.astype(BF16)], axis=1),
                      preferred_element_type=F32)
        wkk_t = (kc.astype(F32).T * jnp.exp(a_row - mx_end)).astype(BF16)
        upd = jnp.dot(wkk_t, jnp.concatenate([vc, ones_blk], axis=1), preferred_element_type=F32)
        decay = jnp.exp(m_prev - mx_end)
        w_inter = jnp.exp(m_prev - mx)
        den = s_sum + w_inter * qcn[:, DV:DV + LANES]
        inv = 1.0 / jnp.maximum(jnp.abs(den), jnp.exp(-(b_rep + mx)))
        hc = jnp.concatenate(
            [(sv[:, kb * LANES:(kb + 1) * LANES] + w_inter * qcn[:, kb * LANES:(kb + 1) * LANES]) * inv
             for kb in range(DV // LANES)], axis=1)
        return rows, hc, (decay * c_mat + upd[:, 0:DV], decay * n_rep + upd[:, DV:DV + LANES])

    def n_to_columns(n_row):
        return jnp.broadcast_to(n_row, (LANES, BLK)).T

    def n_to_row(n_rep):
        return n_rep.T[0:1, :]

    def emit(hh, rows, hc, first_visit):
        if first_visit:
            hs_ref[hh, rows, :] = hc
        else:
            hsum = hs_ref[hh, rows, :] + hc
            lanes = head_lanes(hh, DV)
            y_ref[0, rows, lanes] = (jax.nn.sigmoid(o_ref[0, rows, lanes]) * hsum
                                     * _silu(z_ref[0, rows, lanes])).astype(y_ref.dtype)

    def pipelined(step, carry, first_visit, with_next):
        memories, pendings, m_reps = carry
        nxt = [scores(step + 1, sm, m_reps[i]) for i, sm in enumerate(streams)] if with_next else None
        done = [outputs(step, sm, memories[i], pendings[i]) for i, sm in enumerate(streams)]
        for (hh, _), (rows, hc, _) in zip(streams, done):
            emit(hh, rows, hc, first_visit)
        new_mem = tuple(dn[2] for dn in done)
        if with_next:
            return new_mem, tuple(n[0] for n in nxt), tuple(n[1] for n in nxt)
        return new_mem, pendings, m_reps

    if zero_init:
        memories = tuple((jnp.zeros((BLK, DV), F32), jnp.zeros((BLK, LANES), F32)) for _ in streams)
        first = [scores(0, sm, jnp.zeros((1, t), F32)) for sm in streams]
    else:
        memories = tuple((c0_ref[0, dr, hh], n_to_columns(n0_ref[0, dr, hh])) for hh, dr in streams)
        first = [scores(0, (hh, dr), m0_ref[0, dr, hh]) for hh, dr in streams]
    carry = (memories, tuple(f[0] for f in first), tuple(f[1] for f in first))
    if nchunks <= 2:
        for step in range(nchunks - 1):
            carry = pipelined(step, carry, first_visit=step < half, with_next=True)
    else:
        unroll = 2 if half % 2 == 0 else 1
        carry = lax.fori_loop(0, half, functools.partial(pipelined, first_visit=True, with_next=True), carry,
                              unroll=unroll)
        carry = lax.fori_loop(half, nchunks - 1, functools.partial(pipelined, first_visit=False, with_next=True),
                              carry, unroll=unroll)
    memories, _, m_reps = pipelined(nchunks - 1, carry, first_visit=False, with_next=False)
    for i, (hh, dr) in enumerate(streams):
        c_out_ref[0, dr, hh] = memories[i][0]
        n_out_ref[0, dr, hh] = n_to_row(memories[i][1])
        m_out_ref[0, dr, hh] = m_reps[i]


def _mlstm(q, k, v, o, z, gates, states, heads):
    b, seq, _ = q.shape
    qk_blk = lambda i, h: (i, 0, h)
    st5 = lambda i, h: (i, 0, h, 0, 0)
    state_specs = [pl.BlockSpec((1, 2, heads, BLK, DV), st5),
                   pl.BlockSpec((1, 2, heads, 1, BLK), st5),
                   pl.BlockSpec((1, 2, heads, 1, CHUNK), st5)]
    vmem = heads * (2 * (2 * seq * BLK * 2 + seq * DV * 2 * 2 + 2 * seq * DV * 4 + 2 * 2 * BLK * DV * 4)
                    + seq * DV * 4 + seq * 4 * LANES * 4) + 16 * 1024 * 1024
    return pl.pallas_call(
        functools.partial(_mlstm_kernel, seq=seq, heads=heads, zero_init=states is None),
        grid=(b, N_HEADS // heads),
        in_specs=[pl.BlockSpec((1, seq, heads * BLK), qk_blk),
                  pl.BlockSpec((1, seq, heads * BLK), qk_blk),
                  pl.BlockSpec((1, seq, heads * DV), qk_blk),
                  pl.BlockSpec((1, seq, heads * DV), qk_blk),
                  pl.BlockSpec((1, seq, heads * DV), qk_blk),
                  pl.BlockSpec((1, heads, SUBLANES, seq), lambda i, h: (i, h, 0, 0))]
        + ([] if states is None else state_specs),
        out_specs=[pl.BlockSpec((1, seq, heads * DV), qk_blk)] + state_specs,
        out_shape=[jax.ShapeDtypeStruct((b, seq, N_HEADS * DV), BF16),
                   jax.ShapeDtypeStruct((b, 2, N_HEADS, BLK, DV), F32),
                   jax.ShapeDtypeStruct((b, 2, N_HEADS, 1, BLK), F32),
                   jax.ShapeDtypeStruct((b, 2, N_HEADS, 1, CHUNK), F32)],
        scratch_shapes=[pltpu.VMEM((heads, seq, DV), F32), pltpu.VMEM((heads, seq, 4 * LANES), F32)],
        compiler_params=_cparams(("parallel", "parallel"), vmem),
        name="mlstm_chunkwise",
    )(q, k, v, o, z, gates, *(() if states is None else states))


def _dft_tables(seq):
    n = np.arange(BLK)
    ang = 2.0 * np.pi * np.outer(n, n) / BLK
    dft_c = np.concatenate([np.cos(ang), -np.sin(ang)], axis=1) / np.sqrt(BLK)
    l1 = seq // FFT_L2
    n2 = np.arange(FFT_L2)
    ang2 = 2.0 * np.pi * np.outer(n2, n2) / FFT_L2
    w_pos = np.concatenate([np.cos(ang2), np.sin(ang2)], axis=1) / np.sqrt(seq)
    angt = 2.0 * np.pi * np.outer(np.arange(l1), n2) / seq
    twc = np.broadcast_to(np.cos(angt)[:, :, None], (l1, FFT_L2, LANES))
    tws = np.broadcast_to(np.sin(angt)[:, :, None], (l1, FFT_L2, LANES))
    return (jnp.asarray(dft_c, F32).astype(BF16), jnp.asarray(w_pos, F32).astype(BF16),
            jnp.asarray(twc, F32), jnp.asarray(tws, F32))


def _run_trunk(x, mod, row_off, per_batch_mod, lru0, mlstm0, p, cfg):
    b, seq, d = x.shape
    m = b * seq
    rows_per_mod = seq if per_batch_mod else m
    x2d = x.reshape(m, d)
    dft_c, w_pos, twc, tws = _dft_tables(seq)

    mod0 = mod[0].reshape(MOD_ROWS, 1, 3 * d)
    xa, za, zb, xc, xs = _inproj_e(x2d, mod0, p["g_pre"][0:1], p["w_in_e"], dft_c, row_off, rows_per_mod,
                                   cfg["tm_in"])
    r3 = lambda a: a.reshape(b, seq, a.shape[-1])
    ya, lru_f = _rglru(r3(xa), r3(za), lru0, p["conv_w"], p["conv_b"], p["wg"], p["bg"], p["lam"], cfg["tc_lru"],
                       cfg["bb_lru"])
    yb = _fourier(r3(xc), r3(xs), r3(zb), twc, tws, w_pos, p["w_four"], p["b_four"], cfg["w_four"], cfg["tr_four"])
    x1 = _outproj([ya.reshape(m, d), yb.reshape(m, d)], [p["w_out_e"][0:d], p["w_out_e"][d:2 * d]],
                  x2d, mod0, p["g_post"][0:1], row_off, rows_per_mod, cfg["tm_out"])

    mod1 = mod[1].reshape(MOD_ROWS, 1, 3 * d)
    q, k, v, o, z, gpt = _inproj_o(x1, mod1, p["g_pre"][1:2], p["w_in_o"], p["w_gate"], p["gate_bias"],
                                   row_off, rows_per_mod, cfg["tm_in_o"])
    gq = gpt[0:4 * N_HEADS].reshape(4, N_HEADS, b, seq).transpose(0, 2, 1, 3).reshape(4, b * N_HEADS, seq)
    gates = _gate_prep(gq, cfg["rb_gate"]).reshape(SUBLANES, b, N_HEADS, seq).transpose(1, 2, 0, 3)
    y, c_f, n_f, m_f = _mlstm(r3(q), r3(k), r3(v), r3(o), r3(z), gates, mlstm0, cfg["heads_mlstm"])
    x2 = _outproj([y.reshape(m, 2 * d)], [p["w_out_o"]], x1, mod1, p["g_post"][1:2], row_off, rows_per_mod,
                  cfg["tm_out"])
    return x2.reshape(b, seq, d), lru_f, c_f, n_f, m_f


def _prepare_params(conv_w, conv_b, w_rg, b_rg, w_ig, b_ig, lru_lambda, w_four, b_four, w_in_e, w_out_e,
                    w_in_o, b_if, w_out_o, g_pre, g_post):
    d = D_MODEL
    wg = jnp.concatenate([w_rg[0], w_ig[0]], axis=-1).astype(BF16)
    bg = jnp.concatenate([b_rg[0].reshape(2, N_HEADS, 1, BLK), b_ig[0].reshape(2, N_HEADS, 1, BLK)], axis=-1)
    n_gate = 4 * N_HEADS
    w_gate = jnp.zeros((d, LANES), F32).at[:, 0:n_gate].set(w_in_o[0][:, 8 * d:8 * d + n_gate]).astype(BF16)
    bias = jnp.zeros((LANES,), F32).at[0:n_gate].set(b_if[0].reshape(n_gate))
    gate_bias = jnp.broadcast_to(bias[:, None], (LANES, LANES))
    return dict(
        g_pre=g_pre, g_post=g_post,
        w_in_e=w_in_e[0].astype(BF16), w_out_e=w_out_e[0].astype(BF16),
        conv_w=conv_w[0], conv_b=conv_b[0].reshape(1, d), wg=wg, bg=bg, lam=lru_lambda[0].reshape(2, 1, d),
        w_four=w_four[0].astype(BF16), b_four=b_four[0].reshape(1, d),
        w_in_o=w_in_o[0].astype(BF16), w_gate=w_gate, gate_bias=gate_bias,
        w_out_o=w_out_o[0].astype(BF16))


def kernel(x_prompt, x_sample, c, state_lru, state_mlstm_C, state_mlstm_n, state_mlstm_m, c_ctx, w_mod, b_mod,
           g_pre, g_post, w_in_e, conv_w, conv_b, w_rg, b_rg, w_ig, b_ig, lru_lambda, w_four, b_four, w_out_e,
           w_in_o, b_if, w_out_o):
    d = D_MODEL
    bp = x_prompt.shape[0]
    bs = x_sample.shape[0]
    p = _prepare_params(conv_w, conv_b, w_rg, b_rg, w_ig, b_ig, lru_lambda, w_four, b_four, w_in_e, w_out_e,
                        w_in_o, b_if, w_out_o, g_pre, g_post)
    cond = jnp.concatenate([c_ctx[None, :], c, jnp.zeros((MOD_ROWS - 1 - bs, d), F32)], axis=0)
    mod = _modulation(cond, w_mod, b_mod)

    cfg_p = dict(tm_in=512, tm_in_o=512, tm_out=1024, tc_lru=128, bb_lru=4 if bp % 4 == 0 else 1, w_four=d,
                 tr_four=256, rb_gate=min(bp * N_HEADS, 128), heads_mlstm=4)
    y_prompt, lru_f, c_f, n_f, m_f = _run_trunk(x_prompt, mod, 0, False, jnp.zeros((bp, 2, d), F32), None, p, cfg_p)

    cfg_s = dict(tm_in=512, tm_in_o=512, tm_out=1024, tc_lru=128, bb_lru=1, w_four=BLK, tr_four=256,
                 rb_gate=SUBLANES, heads_mlstm=1)
    mlstm0 = (state_mlstm_C[:, 0], state_mlstm_n[:, 0].reshape(bs, 2, N_HEADS, 1, BLK),
              jnp.broadcast_to(state_mlstm_m[:, 0].reshape(bs, 2, N_HEADS, 1, 1), (bs, 2, N_HEADS, 1, CHUNK)))
    y_sample, _, _, _, _ = _run_trunk(x_sample, mod, 1, True, state_lru[:, 0], mlstm0, p, cfg_s)

    return (y_prompt, y_sample, lru_f[:, None], c_f[:, None], n_f[:, None, :, :, 0, :], m_f[:, None, :, :, 0, 0])
```

```python
import functools
import math

import numpy as np
import jax
import jax.numpy as jnp
from jax import lax
from jax.experimental import pallas as pl
from jax.experimental.pallas import tpu as pltpu

F32 = jnp.float32
BF16 = jnp.bfloat16

D_MODEL = 1024
DEPTH = 2
EPS = 1e-6
LRU_C = 8.0
CONV_W = 4
N_HEADS = 8
BLK = D_MODEL // N_HEADS
DV = 2 * BLK
CHUNK = 128
FFT_L2 = 256
MOD_ROWS = 8

LANES = 128
SUBLANES = 8
MIB = 1024 * 1024
VMEM_LIMIT_CAP = 56 * MIB
VMEM_LIMIT_FLOOR = 16 * MIB
VMEM_TEMPORARIES = 16 * MIB


def _cparams(sem, vmem_bytes):
    return pltpu.CompilerParams(dimension_semantics=sem,
                                vmem_limit_bytes=int(min(max(vmem_bytes, VMEM_LIMIT_FLOOR), VMEM_LIMIT_CAP)))


def _silu(x):
    return x * jax.nn.sigmoid(x)


def _mod_kernel(cond_ref, w_ref, b_ref, o_ref):
    s = _silu(cond_ref[...]).astype(BF16)
    o_ref[0] = jnp.dot(s, w_ref[0].astype(BF16), preferred_element_type=F32) + b_ref[0]


def _modulation(cond, w_mod, b_mod):
    d = D_MODEL
    return pl.pallas_call(
        _mod_kernel,
        grid=(DEPTH, 3),
        in_specs=[pl.BlockSpec((MOD_ROWS, d), lambda l, j: (0, 0)),
                  pl.BlockSpec((1, d, d), lambda l, j: (l, 0, j)),
                  pl.BlockSpec((1, 1, d), lambda l, j: (l, 0, j))],
        out_specs=pl.BlockSpec((1, MOD_ROWS, d), lambda l, j: (l, 0, j)),
        out_shape=jax.ShapeDtypeStruct((DEPTH, MOD_ROWS, 3 * d), F32),
        compiler_params=_cparams(("arbitrary", "arbitrary"), 2 * d * d * 4 + VMEM_TEMPORARIES),
        name="adaln_mod",
    )(cond, w_mod, b_mod.reshape(DEPTH, 1, 3 * d))


def _normed_input(x_ref, mod_ref, g_ref):
    d = D_MODEL
    x = x_ref[...]
    y = x * lax.rsqrt(jnp.mean(x * x, axis=-1, keepdims=True) + EPS) * g_ref[...]
    shift = mod_ref[0, :, 0:d]
    scale = mod_ref[0, :, d:2 * d]
    return (y * (1.0 + scale) + shift).astype(BF16)


def _mod_index_map(row_off, tiles_per_mod):
    return lambda i: (row_off + i // tiles_per_mod, 0, 0)


def _inproj_e_kernel(x_ref, mod_ref, g_ref, w_ref, dft_ref, xa_ref, za_ref, zb_ref, xc_ref, xs_ref):
    d = D_MODEL
    hb = _normed_input(x_ref, mod_ref, g_ref)
    xa_ref[...] = jnp.dot(hb, w_ref[:, 0:d], preferred_element_type=F32)
    za_ref[...] = jnp.dot(hb, w_ref[:, d:2 * d], preferred_element_type=F32)
    zb_ref[...] = jnp.dot(hb, w_ref[:, 3 * d:4 * d], preferred_element_type=F32)
    xb = jnp.dot(hb, w_ref[:, 2 * d:3 * d], preferred_element_type=F32).astype(BF16)
    for g in range(N_HEADS):
        cs = jnp.dot(xb[:, g * BLK:(g + 1) * BLK], dft_ref[...], preferred_element_type=F32)
        xc_ref[:, g * BLK:(g + 1) * BLK] = cs[:, 0:BLK]
        xs_ref[:, g * BLK:(g + 1) * BLK] = cs[:, BLK:2 * BLK]


def _inproj_e(x2d, mod_l, g_pre, w_in, dft_c, row_off, rows_per_mod, tm):
    m, d = x2d.shape
    row = lambda i: (i, 0)
    const = lambda i: (0, 0)
    out = jax.ShapeDtypeStruct((m, d), F32)
    vmem = 2 * (w_in.size * 2 + tm * d * 4 * 6) + 8 * tm * d * 4
    return pl.pallas_call(
        _inproj_e_kernel,
        grid=(m // tm,),
        in_specs=[pl.BlockSpec((tm, d), row),
                  pl.BlockSpec((1, 1, 3 * d), _mod_index_map(row_off, rows_per_mod // tm)),
                  pl.BlockSpec((1, d), const),
                  pl.BlockSpec(w_in.shape, const),
                  pl.BlockSpec(dft_c.shape, const)],
        out_specs=[pl.BlockSpec((tm, d), row)] * 5,
        out_shape=[out] * 5,
        compiler_params=_cparams(("parallel",), vmem),
        name="inproj_even",
    )(x2d, mod_l, g_pre, w_in, dft_c)


def _scan_sublanes(a, u, reverse):
    t = a.shape[0]
    assert t == SUBLANES
    row = lax.broadcasted_iota(jnp.int32, a.shape, 0)
    d = 1
    while d < t:
        shift = (t - d) if reverse else d
        valid = (row < t - d) if reverse else (row >= d)
        a_sh = jnp.where(valid, pltpu.roll(a, shift, axis=0), 1.0)
        u_sh = jnp.where(valid, pltpu.roll(u, shift, axis=0), 0.0)
        u = a * u_sh + u
        a = a * a_sh
        d *= 2
    return a, u


def _rglru_kernel(xa_ref, za_ref, cw_ref, cb_ref, wg_ref, bg_ref, lam_ref, h0_ref,
                  y_ref, hT_ref, hf_ref, hb_ref, xc_ref, *, seq, tc, bb):
    nchunks = seq // tc
    groups = tc // SUBLANES
    sub = lax.broadcasted_iota(jnp.int32, (SUBLANES, LANES), 0)

    def row_bcast(block, r):
        return jnp.broadcast_to(block[r:r + 1, :], (SUBLANES, LANES))

    def conv_chunk(bi, c):
        t0 = pl.multiple_of(c * tc, tc)
        xs = [xa_ref[bi, pl.ds(t0 + g, SUBLANES, stride=groups), :] for g in range(groups)]
        prev = xa_ref[bi, pl.ds(pl.multiple_of(jnp.maximum(t0 - SUBLANES, 0), SUBLANES), SUBLANES), :]
        nxt = xa_ref[bi, pl.ds(pl.multiple_of(jnp.minimum(t0 + tc, seq - SUBLANES), SUBLANES), SUBLANES), :]
        prev = jnp.where(c > 0, prev, 0.0)
        nxt = jnp.where(c < nchunks - 1, nxt, 0.0)
        before2 = jnp.where(sub == 0, row_bcast(prev, SUBLANES - 2), pltpu.roll(xs[groups - 2], 1, axis=0))
        before1 = jnp.where(sub == 0, row_bcast(prev, SUBLANES - 1), pltpu.roll(xs[groups - 1], 1, axis=0))
        after1 = jnp.where(sub == SUBLANES - 1, row_bcast(nxt, 0), pltpu.roll(xs[0], SUBLANES - 1, axis=0))
        ext = [before2, before1] + xs + [after1]
        out = []
        for g in range(groups):
            acc = cb_ref[...] + cw_ref[0:1, :] * ext[g]
            for j in range(1, CONV_W):
                acc = acc + cw_ref[j:j + 1, :] * ext[g + j]
            out.append(acc)
        return t0, jnp.concatenate(out, axis=0)

    def gates(xc, direction):
        g = jnp.dot(xc.astype(BF16), wg_ref[direction, 0], preferred_element_type=F32) + bg_ref[direction, 0]
        r = jax.nn.sigmoid(g[:, 0:BLK])
        i = jax.nn.sigmoid(g[:, BLK:2 * BLK])
        lam = lam_ref[direction]
        softplus_neg = jnp.maximum(-lam, 0.0) + jnp.log1p(jnp.exp(-jnp.abs(lam)))
        neg_log_a = LRU_C * r * softplus_neg
        a = jnp.exp(-neg_log_a)
        om = jnp.tanh(neg_log_a) * (1.0 + a * a)
        root = jnp.where(om > 0.0, om * lax.rsqrt(om), 0.0)
        return a, root * (i * xc)

    def scan_dir(bi, c, direction, carry, out_ref, first_visit):
        reverse = direction == 1
        if first_visit:
            t0, xc = conv_chunk(bi, c)
            xc_ref[bi, pl.ds(t0, tc), :] = xc
        else:
            t0 = pl.multiple_of(c * tc, tc)
            xc = xc_ref[bi, pl.ds(t0, tc), :]
        a, u = gates(xc, direction)
        order = range(groups - 1, -1, -1) if reverse else range(groups)
        p_g, s_g = [None] * groups, [None] * groups
        p_run = s_run = None
        for g in order:
            a_v = a[g * SUBLANES:(g + 1) * SUBLANES]
            u_v = u[g * SUBLANES:(g + 1) * SUBLANES]
            if p_run is None:
                p_run, s_run = a_v, u_v
            else:
                s_run = a_v * s_run + u_v
                p_run = a_v * p_run
            p_g[g], s_g[g] = p_run, s_run
        pp, ss = _scan_sublanes(p_run, s_run, reverse)
        carry_b = jnp.broadcast_to(carry, (SUBLANES, LANES))
        seg_out = pp * carry_b + ss
        if reverse:
            h_in = jnp.where(sub == SUBLANES - 1, carry_b, pltpu.roll(seg_out, SUBLANES - 1, axis=0))
            new_carry = seg_out[0:1, :]
        else:
            h_in = jnp.where(sub == 0, carry_b, pltpu.roll(seg_out, 1, axis=0))
            new_carry = seg_out[SUBLANES - 1:SUBLANES, :]
        for g in range(groups):
            out_ref[bi, pl.ds(t0 + g, SUBLANES, stride=groups), :] = p_g[g] * h_in + s_g[g]
        return new_carry

    def body(j, carries, first_visit):
        return tuple((scan_dir(bi, j, 0, carries[bi][0], hf_ref, first_visit),
                      scan_dir(bi, nchunks - 1 - j, 1, carries[bi][1], hb_ref, first_visit)) for bi in range(bb))

    half = nchunks // 2
    assert nchunks == 2 * half
    init = tuple((h0_ref[bi, 0:1, :], h0_ref[bi, 1:2, :]) for bi in range(bb))
    if nchunks <= 2:
        finals = init
        for j in range(nchunks):
            finals = body(j, finals, j < half)
    else:
        unroll = 4 if half % 4 == 0 else 1
        finals = lax.fori_loop(0, half, functools.partial(body, first_visit=True), init, unroll=unroll)
        finals = lax.fori_loop(half, nchunks, functools.partial(body, first_visit=False), finals, unroll=unroll)
    for bi in range(bb):
        hT_ref[bi, 0:1, :] = finals[bi][0]
        hT_ref[bi, 1:2, :] = finals[bi][1]

    def gate_out(c, carry):
        rows = pl.ds(pl.multiple_of(c * tc, tc), tc)
        for bi in range(bb):
            y_ref[bi, rows, :] = ((hf_ref[bi, rows, :] + hb_ref[bi, rows, :])
                                  * _silu(za_ref[bi, rows, :])).astype(y_ref.dtype)
        return carry

    lax.fori_loop(0, nchunks, gate_out, 0, unroll=2 if nchunks > 2 else 1)


def _rglru(xa, za, h0, conv_w, conv_b, wg, bg, lam, tc, bb):
    b, seq, d = xa.shape
    blk = lambda i, j: (i, 0, j)
    vmem = bb * (2 * (2 * seq * BLK * 4 + seq * BLK * 2) + 3 * seq * BLK * 4) + VMEM_TEMPORARIES
    return pl.pallas_call(
        functools.partial(_rglru_kernel, seq=seq, tc=tc, bb=bb),
        grid=(b // bb, d // BLK),
        in_specs=[pl.BlockSpec((bb, seq, BLK), blk),
                  pl.BlockSpec((bb, seq, BLK), blk),
                  pl.BlockSpec((CONV_W, BLK), lambda i, j: (0, j)),
                  pl.BlockSpec((1, BLK), lambda i, j: (0, j)),
                  pl.BlockSpec((2, 1, BLK, 2 * BLK), lambda i, j: (0, j, 0, 0)),
                  pl.BlockSpec((2, 1, 1, 2 * BLK), lambda i, j: (0, j, 0, 0)),
                  pl.BlockSpec((2, 1, BLK), lambda i, j: (0, 0, j)),
                  pl.BlockSpec((bb, 2, BLK), blk)],
        out_specs=[pl.BlockSpec((bb, seq, BLK), blk),
                   pl.BlockSpec((bb, 2, BLK), blk)],
        out_shape=[jax.ShapeDtypeStruct((b, seq, d), BF16),
                   jax.ShapeDtypeStruct((b, 2, d), F32)],
        scratch_shapes=[pltpu.VMEM((bb, seq, BLK), F32)] * 3,
        compiler_params=_cparams(("parallel", "parallel"), vmem),
        name="rglru_scan",
    )(xa, za, conv_w, conv_b, wg, bg, lam, h0)


def _fft_list(xs):
    n = len(xs)
    if n == 1:
        return xs
    even = _fft_list(xs[0::2])
    odd = _fft_list(xs[1::2])
    out = [None] * n
    for k in range(n // 2):
        o_re, o_im = odd[k]
        if k == 0:
            t_re, t_im = o_re, o_im
        elif 4 * k == n:
            t_re, t_im = o_im, -o_re
        else:
            ang = -2.0 * math.pi * k / n
            wr, wi = math.cos(ang), math.sin(ang)
            t_re = o_re * wr - o_im * wi
            t_im = o_re * wi + o_im * wr
        e_re, e_im = even[k]
        out[k] = (e_re + t_re, e_im + t_im)
        out[k + n // 2] = (e_re - t_re, e_im - t_im)
    return out


def _fourier_kernel(xc_ref, xs_ref, zb_ref, twc_ref, tws_ref, wpos_ref, wf_ref, bf_ref, y_ref, *scratch,
                    seq, tr):
    w = xc_ref.shape[2]
    nblk = w // BLK
    l1 = seq // FFT_L2

    def epilogue(fr, rows):
        parts = []
        for kb in range(nblk):
            yb = jnp.dot(fr[:, kb * BLK:(kb + 1) * BLK].astype(BF16), wf_ref[kb], preferred_element_type=F32)
            parts.append(yb + bf_ref[:, kb * BLK:(kb + 1) * BLK])
        yb = parts[0] if nblk == 1 else jnp.concatenate(parts, axis=1)
        y_ref[0, rows, :] = (yb * _silu(zb_ref[0, rows, :])).astype(y_ref.dtype)

    if l1 == 1:
        fr = (jnp.dot(wpos_ref[:, 0:FFT_L2], xc_ref[0].astype(BF16), preferred_element_type=F32)
              + jnp.dot(wpos_ref[:, FFT_L2:2 * FFT_L2], xs_ref[0].astype(BF16), preferred_element_type=F32))
        epilogue(fr, pl.ds(0, seq))
        return

    b_ref, fr_ref = scratch
    assert w == LANES

    def butterfly(r, carry):
        r0 = pl.multiple_of(r * SUBLANES, SUBLANES)
        zs = [(xc_ref[0, pl.ds(n1 * FFT_L2 + r0, SUBLANES), :], xs_ref[0, pl.ds(n1 * FFT_L2 + r0, SUBLANES), :])
              for n1 in range(l1)]
        for k1, (a_re, a_im) in enumerate(_fft_list(zs)):
            if k1 == 0:
                b_re, b_im = a_re, a_im
            else:
                tc_ = twc_ref[k1, pl.ds(r0, SUBLANES), :]
                ts_ = tws_ref[k1, pl.ds(r0, SUBLANES), :]
                b_re = a_re * tc_ + a_im * ts_
                b_im = a_im * tc_ - a_re * ts_
            b_ref[pl.ds(r0, SUBLANES), k1 * LANES:(k1 + 1) * LANES] = b_re
            b_ref[pl.ds(FFT_L2 + r0, SUBLANES), k1 * LANES:(k1 + 1) * LANES] = b_im
        return carry

    lax.fori_loop(0, FFT_L2 // SUBLANES, butterfly, 0, unroll=2)

    per_dot = min(4, l1)
    for nb in range(l1 // per_dot):
        cols = slice(nb * per_dot * LANES, (nb + 1) * per_dot * LANES)
        fr = jnp.dot(wpos_ref[...], b_ref[:, cols].astype(BF16), preferred_element_type=F32)
        for kk in range(per_dot):
            fr_ref[pl.ds(nb * per_dot + kk, FFT_L2, stride=l1), :] = fr[:, kk * LANES:(kk + 1) * LANES]

    def finish(i, carry):
        rows = pl.ds(pl.multiple_of(i * tr, tr), tr)
        epilogue(fr_ref[rows, :], rows)
        return carry

    lax.fori_loop(0, seq // tr, finish, 0, unroll=4)


def _fourier(xc, xs, zb, twc, tws, w_pos, w_four, b_four, w_blk, tr):
    b, seq, d = xc.shape
    nblk = w_blk // BLK
    l1 = seq // FFT_L2
    blk = lambda i, j: (i, 0, j)
    scratch = []
    if l1 > 1:
        scratch = [pltpu.VMEM((2 * FFT_L2, l1 * w_blk), F32), pltpu.VMEM((seq, w_blk), F32)]
    vmem = (2 * (3 * seq * w_blk * 4 + seq * w_blk * 2 + 2 * twc.size * 4) + 3 * seq * w_blk * 4
            + VMEM_TEMPORARIES)
    return pl.pallas_call(
        functools.partial(_fourier_kernel, seq=seq, tr=tr),
        grid=(b, d // w_blk),
        in_specs=[pl.BlockSpec((1, seq, w_blk), blk),
                  pl.BlockSpec((1, seq, w_blk), blk),
                  pl.BlockSpec((1, seq, w_blk), blk),
                  pl.BlockSpec(twc.shape, lambda i, j: (0, 0, 0)),
                  pl.BlockSpec(tws.shape, lambda i, j: (0, 0, 0)),
                  pl.BlockSpec(w_pos.shape, lambda i, j: (0, 0)),
                  pl.BlockSpec((nblk, BLK, BLK), lambda i, j: (j, 0, 0)),
                  pl.BlockSpec((1, w_blk), lambda i, j: (0, j))],
        out_specs=pl.BlockSpec((1, seq, w_blk), blk),
        out_shape=jax.ShapeDtypeStruct((b, seq, d), BF16),
        scratch_shapes=scratch,
        compiler_params=_cparams(("parallel", "parallel"), vmem),
        name="fourier_mix",
    )(xc, xs, zb, twc, tws, w_pos, w_four, b_four)


def _outproj_kernel(*refs, n_in):
    y_refs = refs[:n_in]
    w_refs = refs[n_in:2 * n_in]
    x_ref, mod_ref, g_ref, o_ref = refs[2 * n_in:]
    d = D_MODEL
    y = jnp.dot(y_refs[0][...], w_refs[0][...], preferred_element_type=F32)
    for k in range(1, n_in):
        y = y + jnp.dot(y_refs[k][...], w_refs[k][...], preferred_element_type=F32)
    yn = y * lax.rsqrt(jnp.mean(y * y, axis=-1, keepdims=True) + EPS) * g_ref[...]
    o_ref[...] = x_ref[...] + mod_ref[0, :, 2 * d:3 * d] * yn


def _outproj(ys, ws, x2d, mod_l, g_post, row_off, rows_per_mod, tm):
    m, d = x2d.shape
    n_in = len(ys)
    row = lambda i: (i, 0)
    const = lambda i: (0, 0)
    vmem = 2 * (sum(wk.size for wk in ws) * 2 + sum(tm * yk.shape[1] for yk in ys) * 2 + 2 * tm * d * 4) + 4 * tm * d * 4
    return pl.pallas_call(
        functools.partial(_outproj_kernel, n_in=n_in),
        grid=(m // tm,),
        in_specs=([pl.BlockSpec((tm, yk.shape[1]), row) for yk in ys]
                  + [pl.BlockSpec(wk.shape, const) for wk in ws]
                  + [pl.BlockSpec((tm, d), row),
                     pl.BlockSpec((1, 1, 3 * d), _mod_index_map(row_off, rows_per_mod // tm)),
                     pl.BlockSpec((1, d), const)]),
        out_specs=pl.BlockSpec((tm, d), row),
        out_shape=jax.ShapeDtypeStruct((m, d), F32),
        compiler_params=_cparams(("parallel",), vmem),
        name="outproj_residual",
    )(*ys, *ws, x2d, mod_l, g_post)


def _inproj_o_kernel(x_ref, mod_ref, g_ref, w_ref, wgt_ref, gb_ref, q_ref, k_ref, v_ref, o_ref, z_ref, gpt_ref,
                     gp_ref):
    d = D_MODEL
    hb = _normed_input(x_ref, mod_ref, g_ref)
    q = jnp.dot(hb, w_ref[:, 0:d], preferred_element_type=F32)
    q_ref[...] = (q * (BLK ** -0.5)).astype(q_ref.dtype)
    k_ref[...] = jnp.dot(hb, w_ref[:, d:2 * d], preferred_element_type=F32).astype(k_ref.dtype)
    for half in range(2):
        cols = slice(half * d, (half + 1) * d)
        v_ref[:, cols] = jnp.dot(hb, w_ref[:, 2 * d + half * d:3 * d + half * d],
                                 preferred_element_type=F32).astype(v_ref.dtype)
        o_ref[:, cols] = jnp.dot(hb, w_ref[:, 4 * d + half * d:5 * d + half * d], preferred_element_type=F32)
        z_ref[:, cols] = jnp.dot(hb, w_ref[:, 6 * d + half * d:7 * d + half * d], preferred_element_type=F32)
    gp_ref[...] = jnp.dot(hb, wgt_ref[...], preferred_element_type=F32)
    gpt_ref[...] = gp_ref[...].T + gb_ref[:, 0:1]


def _inproj_o(x2d, mod_l, g_pre, w_main, w_gate, gate_bias, row_off, rows_per_mod, tm):
    m, d = x2d.shape
    row = lambda i: (i, 0)
    const = lambda i: (0, 0)
    ng = w_gate.shape[1]
    vmem = w_main.size * 2 + 2 * (tm * d * 4 + tm * d * 2 * 4 + tm * 2 * d * 4 * 2) + 6 * tm * d * 4
    return pl.pallas_call(
        _inproj_o_kernel,
        grid=(m // tm,),
        in_specs=[pl.BlockSpec((tm, d), row),
                  pl.BlockSpec((1, 1, 3 * d), _mod_index_map(row_off, rows_per_mod // tm)),
                  pl.BlockSpec((1, d), const),
                  pl.BlockSpec(w_main.shape, const, pipeline_mode=pl.Buffered(1)),
                  pl.BlockSpec(w_gate.shape, const),
                  pl.BlockSpec(gate_bias.shape, const)],
        out_specs=[pl.BlockSpec((tm, d), row), pl.BlockSpec((tm, d), row),
                   pl.BlockSpec((tm, 2 * d), row), pl.BlockSpec((tm, 2 * d), row), pl.BlockSpec((tm, 2 * d), row),
                   pl.BlockSpec((ng, tm), lambda i: (0, i))],
        out_shape=[jax.ShapeDtypeStruct((m, d), BF16), jax.ShapeDtypeStruct((m, d), BF16),
                   jax.ShapeDtypeStruct((m, 2 * d), BF16), jax.ShapeDtypeStruct((m, 2 * d), F32),
                   jax.ShapeDtypeStruct((m, 2 * d), F32), jax.ShapeDtypeStruct((ng, m), F32)],
        scratch_shapes=[pltpu.VMEM((tm, ng), F32)],
        compiler_params=_cparams(("parallel",), vmem),
        name="inproj_odd",
    )(x2d, mod_l, g_pre, w_main, w_gate, gate_bias)


def _chunk_scan_lanes(v, op, identity, reverse):
    n = v.shape[-1]
    pos = lax.broadcasted_iota(jnp.int32, v.shape, 1) & (CHUNK - 1)
    d = 1
    while d < CHUNK:
        if reverse:
            shifted = jnp.where(pos < CHUNK - d, pltpu.roll(v, n - d, axis=1), identity)
        else:
            shifted = jnp.where(pos >= d, pltpu.roll(v, d, axis=1), identity)
        v = op(v, shifted)
        d *= 2
    return v


def _gate_prep_kernel(g_ref, o_ref):
    for direction in range(2):
        reverse = direction == 1
        i_pre = g_ref[2 * direction]
        b = _chunk_scan_lanes(jax.nn.log_sigmoid(g_ref[2 * direction + 1]), jnp.add, 0.0, reverse)
        a = i_pre - b
        o_ref[3 * direction] = a
        o_ref[3 * direction + 1] = b
        o_ref[3 * direction + 2] = _chunk_scan_lanes(a, jnp.maximum, -jnp.inf, reverse)
    o_ref[6] = jnp.zeros_like(g_ref[0])
    o_ref[7] = jnp.zeros_like(g_ref[0])


def _gate_prep(gq, rb):
    _, r, seq = gq.shape
    return pl.pallas_call(
        _gate_prep_kernel,
        grid=(r // rb,),
        in_specs=[pl.BlockSpec((4, rb, seq), lambda i: (0, i, 0))],
        out_specs=pl.BlockSpec((SUBLANES, rb, seq), lambda i: (0, i, 0)),
        out_shape=jax.ShapeDtypeStruct((SUBLANES, r, seq), F32),
        compiler_params=_cparams(("parallel",), 2 * (4 + SUBLANES) * rb * seq * 4 + VMEM_TEMPORARIES),
        name="mlstm_gate_prep",
    )(gq)


def _mlstm_kernel(*refs, seq, heads, zero_init):
    q_ref, k_ref, v_ref, o_ref, z_ref, gate_ref = refs[0:6]
    refs = refs[6:]
    if not zero_init:
        c0_ref, n0_ref, m0_ref = refs[0:3]
        refs = refs[3:]
    y_ref, c_out_ref, n_out_ref, m_out_ref, hs_ref, cols_ref = refs
    t = CHUNK
    nchunks = seq // t
    half = nchunks // 2
    assert nchunks == 2 * half
    streams = [(hh, dr) for hh in range(heads) for dr in range(2)]

    def head_lanes(hh, width):
        return slice(hh * width, (hh + 1) * width)
    row_i = lax.broadcasted_iota(jnp.int32, (t, t), 0)
    col_i = lax.broadcasted_iota(jnp.int32, (t, t), 1)
    ones_blk = jnp.ones((t, LANES), BF16)

    sel_r = lax.broadcasted_iota(jnp.int32, (4 * SUBLANES, 4 * LANES), 0)
    sel_c = lax.broadcasted_iota(jnp.int32, (4 * SUBLANES, 4 * LANES), 1) // LANES
    wanted = jnp.where(sel_c == 0, 2, jnp.where(sel_c == 1, 1, jnp.where(sel_c == 2, 5, 4)))
    selector = jnp.where(((sel_r & (SUBLANES - 1)) == wanted) & (sel_r < 3 * SUBLANES), 1.0, 0.0).astype(BF16)

    piece = min(seq, 4 * t)
    for hh in range(heads):
        for c in range(seq // piece):
            g = gate_ref[0, hh, :, c * piece:(c + 1) * piece]
            hi = g.astype(BF16).astype(F32)
            mid = (g - hi).astype(BF16).astype(F32)
            lo = g - hi - mid
            stack = jnp.concatenate([hi, mid, lo, jnp.zeros_like(g)], axis=0).astype(BF16)
            cols_ref[hh, c * piece:(c + 1) * piece, :] = lax.dot_general(
                stack, selector, (((0,), (0,)), ((), ())), preferred_element_type=F32)

    def chunk_of(step, direction):
        return (nchunks - 1 - step) if direction == 1 else step

    def gate_terms(rows, hh, direction):
        g_rows = gate_ref[0, hh, :, rows]
        last = 0 if direction == 1 else t - 1
        a_row = g_rows[3 * direction:3 * direction + 1, :]
        b_end = g_rows[3 * direction + 1:3 * direction + 2, last:last + 1]
        a_end = g_rows[3 * direction + 2:3 * direction + 3, last:last + 1]
        amax_rep = cols_ref[hh, rows, 2 * direction * LANES:(2 * direction + 1) * LANES]
        return a_row, b_end, a_end, amax_rep

    def scores(step, stream, m_rep):
        hh, direction = stream
        rows = pl.ds(pl.multiple_of(chunk_of(step, direction) * t, t), t)
        a_row, b_end, a_end, amax_rep = gate_terms(rows, hh, direction)
        m_prev = m_rep[:, 0:1]
        mx = jnp.maximum(m_prev, amax_rep)
        mask = (col_i >= row_i) if direction == 1 else (col_i <= row_i)
        e = jnp.where(mask, jnp.exp(a_row - mx), 0.0)
        s = lax.dot_general(q_ref[0, rows, head_lanes(hh, BLK)], k_ref[0, rows, head_lanes(hh, BLK)],
                            (((1,), (1,)), ((), ())), preferred_element_type=F32) * e
        m_next = jnp.broadcast_to(b_end + jnp.maximum(m_prev, a_end), (1, t))
        return (s.astype(BF16), jnp.sum(s, axis=-1, keepdims=True), m_rep), m_next

    def outputs(step, stream, memory, pending):
        hh, direction = stream
        c_mat, n_rep = memory
        s_b, s_sum, m_rep = pending
        rows = pl.ds(pl.multiple_of(chunk_of(step, direction) * t, t), t)
        qc = q_ref[0, rows, head_lanes(hh, BLK)]
        kc = k_ref[0, rows, head_lanes(hh, BLK)]
        vc = v_ref[0, rows, head_lanes(hh, DV)]
        a_row, _, a_end, amax_rep = gate_terms(rows, hh, direction)
        b_rep = cols_ref[hh, rows, (2 * direction + 1) * LANES:(2 * direction + 2) * LANES]
        m_prev = m_rep[:, 0:1]
        mx_end = jnp.maximum(m_prev, a_end)
        mx = jnp.maximum(m_prev, amax_rep)
        sv = jnp.dot(s_b, vc, preferred_element_type=F32)
        qcn = jnp.dot(qc, jnp.concatenate([c_mat.astype(BF16), n_rep.astype(BF16)], axis=1),
                      preferred_element_type=F32)
        wkk_t = (kc.astype(F32).T * jnp.exp(a_row - mx_end)).astype(BF16)
        upd = jnp.dot(wkk_t, jnp.concatenate([vc, ones_blk], axis=1), preferred_element_type=F32)
        decay = jnp.exp(m_prev - mx_end)
        w_inter = jnp.exp(m_prev - mx)
        den = s_sum + w_inter * qcn[:, DV:DV + LANES]
        inv = 1.0 / jnp.maximum(jnp.abs(den), jnp.exp(-(b_rep + mx)))
        hc = jnp.concatenate(
            [(sv[:, kb * LANES:(kb + 1) * LANES] + w_inter * qcn[:, kb * LANES:(kb + 1) * LANES]) * inv
             for kb in range(DV // LANES)], axis=1)
        return rows, hc, (decay * c_mat + upd[:, 0:DV], decay * n_rep + upd[:, DV:DV + LANES])

    def n_to_columns(n_row):
        return jnp.broadcast_to(n_row, (LANES, BLK)).T

    def n_to_row(n_rep):
        return n_rep.T[0:1, :]

    def emit(hh, rows, hc, first_visit):
        if first_visit:
            hs_ref[hh, rows, :] = hc
        else:
            hsum = hs_ref[hh, rows, :] + hc
            lanes = head_lanes(hh, DV)
            y_ref[0, rows, lanes] = (jax.nn.sigmoid(o_ref[0, rows, lanes]) * hsum
                                     * _silu(z_ref[0, rows, lanes])).astype(y_ref.dtype)

    def pipelined(step, carry, first_visit, with_next):
        memories, pendings, m_reps = carry
        nxt = [scores(step + 1, sm, m_reps[i]) for i, sm in enumerate(streams)] if with_next else None
        done = [outputs(step, sm, memories[i], pendings[i]) for i, sm in enumerate(streams)]
        for (hh, _), (rows, hc, _) in zip(streams, done):
            emit(hh, rows, hc, first_visit)
        new_mem = tuple(dn[2] for dn in done)
        if with_next:
            return new_mem, tuple(n[0] for n in nxt), tuple(n[1] for n in nxt)
        return new_mem, pendings, m_reps

    if zero_init:
        memories = tuple((jnp.zeros((BLK, DV), F32), jnp.zeros((BLK, LANES), F32)) for _ in streams)
        first = [scores(0, sm, jnp.zeros((1, t), F32)) for sm in streams]
    else:
        memories = tuple((c0_ref[0, dr, hh], n_to_columns(n0_ref[0, dr, hh])) for hh, dr in streams)
        first = [scores(0, (hh, dr), m0_ref[0, dr, hh]) for hh, dr in streams]
    carry = (memories, tuple(f[0] for f in first), tuple(f[1] for f in first))
    if nchunks <= 2:
        for step in range(nchunks - 1):
            carry = pipelined(step, carry, first_visit=step < half, with_next=True)
    else:
        unroll = 2 if half % 2 == 0 else 1
        carry = lax.fori_loop(0, half, functools.partial(pipelined, first_visit=True, with_next=True), carry,
                              unroll=unroll)
        carry = lax.fori_loop(half, nchunks - 1, functools.partial(pipelined, first_visit=False, with_next=True),
                              carry, unroll=unroll)
    memories, _, m_reps = pipelined(nchunks - 1, carry, first_visit=False, with_next=False)
    for i, (hh, dr) in enumerate(streams):
        c_out_ref[0, dr, hh] = memories[i][0]
        n_out_ref[0, dr, hh] = n_to_row(memories[i][1])
        m_out_ref[0, dr, hh] = m_reps[i]


def _mlstm(q, k, v, o, z, gates, states, heads):
    b, seq, _ = q.shape
    qk_blk = lambda i, h: (i, 0, h)
    st5 = lambda i, h: (i, 0, h, 0, 0)
    state_specs = [pl.BlockSpec((1, 2, heads, BLK, DV), st5),
                   pl.BlockSpec((1, 2, heads, 1, BLK), st5),
                   pl.BlockSpec((1, 2, heads, 1, CHUNK), st5)]
    vmem = heads * (2 * (2 * seq * BLK * 2 + seq * DV * 2 * 2 + 2 * seq * DV * 4 + 2 * 2 * BLK * DV * 4)
                    + seq * DV * 4 + seq * 4 * LANES * 4) + VMEM_TEMPORARIES
    return pl.pallas_call(
        functools.partial(_mlstm_kernel, seq=seq, heads=heads, zero_init=states is None),
        grid=(b, N_HEADS // heads),
        in_specs=[pl.BlockSpec((1, seq, heads * BLK), qk_blk),
                  pl.BlockSpec((1, seq, heads * BLK), qk_blk),
                  pl.BlockSpec((1, seq, heads * DV), qk_blk),
                  pl.BlockSpec((1, seq, heads * DV), qk_blk),
                  pl.BlockSpec((1, seq, heads * DV), qk_blk),
                  pl.BlockSpec((1, heads, SUBLANES, seq), lambda i, h: (i, h, 0, 0))]
        + ([] if states is None else state_specs),
        out_specs=[pl.BlockSpec((1, seq, heads * DV), qk_blk)] + state_specs,
        out_shape=[jax.ShapeDtypeStruct((b, seq, N_HEADS * DV), BF16),
                   jax.ShapeDtypeStruct((b, 2, N_HEADS, BLK, DV), F32),
                   jax.ShapeDtypeStruct((b, 2, N_HEADS, 1, BLK), F32),
                   jax.ShapeDtypeStruct((b, 2, N_HEADS, 1, CHUNK), F32)],
        scratch_shapes=[pltpu.VMEM((heads, seq, DV), F32), pltpu.VMEM((heads, seq, 4 * LANES), F32)],
        compiler_params=_cparams(("parallel", "parallel"), vmem),
        name="mlstm_chunkwise",
    )(q, k, v, o, z, gates, *(() if states is None else states))


def _dft_tables(seq):
    n = np.arange(BLK)
    ang = 2.0 * np.pi * np.outer(n, n) / BLK
    dft_c = np.concatenate([np.cos(ang), -np.sin(ang)], axis=1) / np.sqrt(BLK)
    l1 = seq // FFT_L2
    n2 = np.arange(FFT_L2)
    ang2 = 2.0 * np.pi * np.outer(n2, n2) / FFT_L2
    w_pos = np.concatenate([np.cos(ang2), np.sin(ang2)], axis=1) / np.sqrt(seq)
    angt = 2.0 * np.pi * np.outer(np.arange(l1), n2) / seq
    twc = np.broadcast_to(np.cos(angt)[:, :, None], (l1, FFT_L2, LANES))
    tws = np.broadcast_to(np.sin(angt)[:, :, None], (l1, FFT_L2, LANES))
    return (jnp.asarray(dft_c, F32).astype(BF16), jnp.asarray(w_pos, F32).astype(BF16),
            jnp.asarray(twc, F32), jnp.asarray(tws, F32))


def _run_trunk(x, mod, row_off, per_batch_mod, lru0, mlstm0, p, cfg):
    b, seq, d = x.shape
    m = b * seq
    rows_per_mod = seq if per_batch_mod else m
    x2d = x.reshape(m, d)
    dft_c, w_pos, twc, tws = _dft_tables(seq)

    mod0 = mod[0].reshape(MOD_ROWS, 1, 3 * d)
    xa, za, zb, xc, xs = _inproj_e(x2d, mod0, p["g_pre"][0:1], p["w_in_e"], dft_c, row_off, rows_per_mod,
                                   cfg["tm_in"])
    r3 = lambda a: a.reshape(b, seq, a.shape[-1])
    ya, lru_f = _rglru(r3(xa), r3(za), lru0, p["conv_w"], p["conv_b"], p["wg"], p["bg"], p["lam"], cfg["tc_lru"],
                       cfg["bb_lru"])
    yb = _fourier(r3(xc), r3(xs), r3(zb), twc, tws, w_pos, p["w_four"], p["b_four"], cfg["w_four"], cfg["tr_four"])
    x1 = _outproj([ya.reshape(m, d), yb.reshape(m, d)], [p["w_out_e"][0:d], p["w_out_e"][d:2 * d]],
                  x2d, mod0, p["g_post"][0:1], row_off, rows_per_mod, cfg["tm_out"])

    mod1 = mod[1].reshape(MOD_ROWS, 1, 3 * d)
    q, k, v, o, z, gpt = _inproj_o(x1, mod1, p["g_pre"][1:2], p["w_in_o"], p["w_gate"], p["gate_bias"],
                                   row_off, rows_per_mod, cfg["tm_in_o"])
    gq = gpt[0:4 * N_HEADS].reshape(4, N_HEADS, b, seq).transpose(0, 2, 1, 3).reshape(4, b * N_HEADS, seq)
    gates = _gate_prep(gq, cfg["rb_gate"]).reshape(SUBLANES, b, N_HEADS, seq).transpose(1, 2, 0, 3)
    y, c_f, n_f, m_f = _mlstm(r3(q), r3(k), r3(v), r3(o), r3(z), gates, mlstm0, cfg["heads_mlstm"])
    x2 = _outproj([y.reshape(m, 2 * d)], [p["w_out_o"]], x1, mod1, p["g_post"][1:2], row_off, rows_per_mod,
                  cfg["tm_out"])
    return x2.reshape(b, seq, d), lru_f, c_f, n_f, m_f


def _prepare_params(conv_w, conv_b, w_rg, b_rg, w_ig, b_ig, lru_lambda, w_four, b_four, w_in_e, w_out_e,
                    w_in_o, b_if, w_out_o, g_pre, g_post):
    d = D_MODEL
    wg = jnp.concatenate([w_rg[0], w_ig[0]], axis=-1).astype(BF16)
    bg = jnp.concatenate([b_rg[0].reshape(2, N_HEADS, 1, BLK), b_ig[0].reshape(2, N_HEADS, 1, BLK)], axis=-1)
    n_gate = 4 * N_HEADS
    w_gate = jnp.zeros((d, LANES), F32).at[:, 0:n_gate].set(w_in_o[0][:, 8 * d:8 * d + n_gate]).astype(BF16)
    bias = jnp.zeros((LANES,), F32).at[0:n_gate].set(b_if[0].reshape(n_gate))
    gate_bias = jnp.broadcast_to(bias[:, None], (LANES, LANES))
    return dict(
        g_pre=g_pre, g_post=g_post,
        w_in_e=w_in_e[0].astype(BF16), w_out_e=w_out_e[0].astype(BF16),
        conv_w=conv_w[0], conv_b=conv_b[0].reshape(1, d), wg=wg, bg=bg, lam=lru_lambda[0].reshape(2, 1, d),
        w_four=w_four[0].astype(BF16), b_four=b_four[0].reshape(1, d),
        w_in_o=w_in_o[0].astype(BF16), w_gate=w_gate, gate_bias=gate_bias,
        w_out_o=w_out_o[0].astype(BF16))


def _trunk_config(batch, seq):
    short = seq <= 4 * CHUNK
    rows = batch * seq
    return dict(
        tm_in=min(512, rows), tm_in_o=min(512, rows), tm_out=min(1024, rows),
        tc_lru=CHUNK,
        bb_lru=next(n for n in (4, 2, 1) if batch % n == 0) if short else 1,
        w_four=D_MODEL if short else BLK,
        tr_four=min(seq, 256 if short else 512),
        rb_gate=min(batch * N_HEADS, max(SUBLANES, (32 * 1024) // seq)),
        heads_mlstm=4 if short else 1)


def kernel(x_prompt, x_sample, c, state_lru, state_mlstm_C, state_mlstm_n, state_mlstm_m, c_ctx, w_mod, b_mod,
           g_pre, g_post, w_in_e, conv_w, conv_b, w_rg, b_rg, w_ig, b_ig, lru_lambda, w_four, b_four, w_out_e,
           w_in_o, b_if, w_out_o):
    d = D_MODEL
    bp = x_prompt.shape[0]
    bs = x_sample.shape[0]
    p = _prepare_params(conv_w, conv_b, w_rg, b_rg, w_ig, b_ig, lru_lambda, w_four, b_four, w_in_e, w_out_e,
                        w_in_o, b_if, w_out_o, g_pre, g_post)
    cond = jnp.concatenate([c_ctx[None, :], c, jnp.zeros((MOD_ROWS - 1 - bs, d), F32)], axis=0)
    mod = _modulation(cond, w_mod, b_mod)

    y_prompt, lru_f, c_f, n_f, m_f = _run_trunk(x_prompt, mod, 0, False, jnp.zeros((bp, 2, d), F32), None, p,
                                                _trunk_config(bp, x_prompt.shape[1]))

    mlstm0 = (state_mlstm_C[:, 0], state_mlstm_n[:, 0].reshape(bs, 2, N_HEADS, 1, BLK),
              jnp.broadcast_to(state_mlstm_m[:, 0].reshape(bs, 2, N_HEADS, 1, 1), (bs, 2, N_HEADS, 1, CHUNK)))
    y_sample, _, _, _, _ = _run_trunk(x_sample, mod, 1, True, state_lru[:, 0], mlstm0, p,
                                      _trunk_config(bs, x_sample.shape[1]))

    return (y_prompt, y_sample, lru_f[:, None], c_f[:, None], n_f[:, None, :, :, 0, :], m_f[:, None, :, :, 0, 0])
```

```python
import functools
import math

import numpy as np
import jax
import jax.numpy as jnp
from jax import lax
from jax.experimental import pallas as pl
from jax.experimental.pallas import tpu as pltpu

F32 = jnp.float32
BF16 = jnp.bfloat16

D_MODEL = 1024
DEPTH = 2
EPS = 1e-6
LRU_C = 8.0
CONV_W = 4
N_HEADS = 8
BLK = D_MODEL // N_HEADS
DV = 2 * BLK
CHUNK = 128
FFT_L2 = 256
MOD_ROWS = 8

LANES = 128
SUBLANES = 8
MIB = 1024 * 1024
VMEM_LIMIT_CAP = 56 * MIB
VMEM_LIMIT_FLOOR = 16 * MIB
VMEM_TEMPORARIES = 16 * MIB


def _cparams(sem, vmem_bytes):
    return pltpu.CompilerParams(dimension_semantics=sem,
                                vmem_limit_bytes=int(min(max(vmem_bytes, VMEM_LIMIT_FLOOR), VMEM_LIMIT_CAP)))


def _silu(x):
    return x * jax.nn.sigmoid(x)


def _mod_kernel(cond_ref, w_ref, b_ref, o_ref):
    s = _silu(cond_ref[...]).astype(BF16)
    o_ref[0] = jnp.dot(s, w_ref[0].astype(BF16), preferred_element_type=F32) + b_ref[0]


def _modulation(cond, w_mod, b_mod):
    d = D_MODEL
    return pl.pallas_call(
        _mod_kernel,
        grid=(DEPTH, 3),
        in_specs=[pl.BlockSpec((MOD_ROWS, d), lambda l, j: (0, 0)),
                  pl.BlockSpec((1, d, d), lambda l, j: (l, 0, j)),
                  pl.BlockSpec((1, 1, d), lambda l, j: (l, 0, j))],
        out_specs=pl.BlockSpec((1, MOD_ROWS, d), lambda l, j: (l, 0, j)),
        out_shape=jax.ShapeDtypeStruct((DEPTH, MOD_ROWS, 3 * d), F32),
        compiler_params=_cparams(("arbitrary", "arbitrary"), 2 * d * d * 4 + VMEM_TEMPORARIES),
        name="adaln_mod",
    )(cond, w_mod, b_mod.reshape(DEPTH, 1, 3 * d))


def _normed_input(x_ref, mod_ref, g_ref):
    d = D_MODEL
    x = x_ref[...]
    y = x * lax.rsqrt(jnp.mean(x * x, axis=-1, keepdims=True) + EPS) * g_ref[...]
    shift = mod_ref[0, :, 0:d]
    scale = mod_ref[0, :, d:2 * d]
    return (y * (1.0 + scale) + shift).astype(BF16)


def _mod_index_map(row_off, tiles_per_mod):
    return lambda i: (row_off + i // tiles_per_mod, 0, 0)


def _inproj_e_kernel(x_ref, mod_ref, g_ref, w_ref, dft_ref, xa_ref, za_ref, zb_ref, xc_ref, xs_ref):
    d = D_MODEL
    hb = _normed_input(x_ref, mod_ref, g_ref)
    xa_ref[...] = jnp.dot(hb, w_ref[:, 0:d], preferred_element_type=F32)
    za_ref[...] = jnp.dot(hb, w_ref[:, d:2 * d], preferred_element_type=F32)
    zb_ref[...] = jnp.dot(hb, w_ref[:, 3 * d:4 * d], preferred_element_type=F32)
    xb = jnp.dot(hb, w_ref[:, 2 * d:3 * d], preferred_element_type=F32).astype(BF16)
    for g in range(N_HEADS):
        cs = jnp.dot(xb[:, g * BLK:(g + 1) * BLK], dft_ref[...], preferred_element_type=F32)
        xc_ref[:, g * BLK:(g + 1) * BLK] = cs[:, 0:BLK]
        xs_ref[:, g * BLK:(g + 1) * BLK] = cs[:, BLK:2 * BLK]


def _inproj_e(x2d, mod_l, g_pre, w_in, dft_c, row_off, rows_per_mod, tm):
    m, d = x2d.shape
    row = lambda i: (i, 0)
    const = lambda i: (0, 0)
    out = jax.ShapeDtypeStruct((m, d), F32)
    vmem = 2 * (w_in.size * 2 + tm * d * 4 * 6) + 8 * tm * d * 4
    return pl.pallas_call(
        _inproj_e_kernel,
        grid=(m // tm,),
        in_specs=[pl.BlockSpec((tm, d), row),
                  pl.BlockSpec((1, 1, 3 * d), _mod_index_map(row_off, rows_per_mod // tm)),
                  pl.BlockSpec((1, d), const),
                  pl.BlockSpec(w_in.shape, const),
                  pl.BlockSpec(dft_c.shape, const)],
        out_specs=[pl.BlockSpec((tm, d), row)] * 5,
        out_shape=[out] * 5,
        compiler_params=_cparams(("parallel",), vmem),
        name="inproj_even",
    )(x2d, mod_l, g_pre, w_in, dft_c)


def _scan_sublanes(a, u, reverse):
    t = a.shape[0]
    assert t == SUBLANES
    row = lax.broadcasted_iota(jnp.int32, a.shape, 0)
    d = 1
    while d < t:
        shift = (t - d) if reverse else d
        valid = (row < t - d) if reverse else (row >= d)
        a_sh = jnp.where(valid, pltpu.roll(a, shift, axis=0), 1.0)
        u_sh = jnp.where(valid, pltpu.roll(u, shift, axis=0), 0.0)
        u = a * u_sh + u
        a = a * a_sh
        d *= 2
    return a, u


def _rglru_kernel(xa_ref, za_ref, cw_ref, cb_ref, wg_ref, bg_ref, lam_ref, h0_ref,
                  y_ref, hT_ref, hf_ref, hb_ref, xc_ref, *, seq, tc, bb):
    nchunks = seq // tc
    groups = tc // SUBLANES
    sub = lax.broadcasted_iota(jnp.int32, (SUBLANES, LANES), 0)

    def row_bcast(block, r):
        return jnp.broadcast_to(block[r:r + 1, :], (SUBLANES, LANES))

    def conv_chunk(bi, c):
        t0 = pl.multiple_of(c * tc, tc)
        xs = [xa_ref[bi, pl.ds(t0 + g, SUBLANES, stride=groups), :] for g in range(groups)]
        prev = xa_ref[bi, pl.ds(pl.multiple_of(jnp.maximum(t0 - SUBLANES, 0), SUBLANES), SUBLANES), :]
        nxt = xa_ref[bi, pl.ds(pl.multiple_of(jnp.minimum(t0 + tc, seq - SUBLANES), SUBLANES), SUBLANES), :]
        prev = jnp.where(c > 0, prev, 0.0)
        nxt = jnp.where(c < nchunks - 1, nxt, 0.0)
        before2 = jnp.where(sub == 0, row_bcast(prev, SUBLANES - 2), pltpu.roll(xs[groups - 2], 1, axis=0))
        before1 = jnp.where(sub == 0, row_bcast(prev, SUBLANES - 1), pltpu.roll(xs[groups - 1], 1, axis=0))
        after1 = jnp.where(sub == SUBLANES - 1, row_bcast(nxt, 0), pltpu.roll(xs[0], SUBLANES - 1, axis=0))
        ext = [before2, before1] + xs + [after1]
        out = []
        for g in range(groups):
            acc = cb_ref[...] + cw_ref[0:1, :] * ext[g]
            for j in range(1, CONV_W):
                acc = acc + cw_ref[j:j + 1, :] * ext[g + j]
            out.append(acc)
        return t0, jnp.concatenate(out, axis=0)

    def gates(xc, direction):
        g = jnp.dot(xc.astype(BF16), wg_ref[direction, 0], preferred_element_type=F32) + bg_ref[direction, 0]
        r = jax.nn.sigmoid(g[:, 0:BLK])
        i = jax.nn.sigmoid(g[:, BLK:2 * BLK])
        lam = lam_ref[direction]
        softplus_neg = jnp.maximum(-lam, 0.0) + jnp.log1p(jnp.exp(-jnp.abs(lam)))
        neg_log_a = LRU_C * r * softplus_neg
        a = jnp.exp(-neg_log_a)
        om = jnp.tanh(neg_log_a) * (1.0 + a * a)
        root = jnp.where(om > 0.0, om * lax.rsqrt(om), 0.0)
        return a, root * (i * xc)

    def scan_dir(bi, c, direction, carry, out_ref, first_visit):
        reverse = direction == 1
        if first_visit:
            t0, xc = conv_chunk(bi, c)
            xc_ref[bi, pl.ds(t0, tc), :] = xc
        else:
            t0 = pl.multiple_of(c * tc, tc)
            xc = xc_ref[bi, pl.ds(t0, tc), :]
        a, u = gates(xc, direction)
        order = range(groups - 1, -1, -1) if reverse else range(groups)
        p_g, s_g = [None] * groups, [None] * groups
        p_run = s_run = None
        for g in order:
            a_v = a[g * SUBLANES:(g + 1) * SUBLANES]
            u_v = u[g * SUBLANES:(g + 1) * SUBLANES]
            if p_run is None:
                p_run, s_run = a_v, u_v
            else:
                s_run = a_v * s_run + u_v
                p_run = a_v * p_run
            p_g[g], s_g[g] = p_run, s_run
        pp, ss = _scan_sublanes(p_run, s_run, reverse)
        carry_b = jnp.broadcast_to(carry, (SUBLANES, LANES))
        seg_out = pp * carry_b + ss
        if reverse:
            h_in = jnp.where(sub == SUBLANES - 1, carry_b, pltpu.roll(seg_out, SUBLANES - 1, axis=0))
            new_carry = seg_out[0:1, :]
        else:
            h_in = jnp.where(sub == 0, carry_b, pltpu.roll(seg_out, 1, axis=0))
            new_carry = seg_out[SUBLANES - 1:SUBLANES, :]
        for g in range(groups):
            out_ref[bi, pl.ds(t0 + g, SUBLANES, stride=groups), :] = p_g[g] * h_in + s_g[g]
        return new_carry

    def body(j, carries, first_visit):
        return tuple((scan_dir(bi, j, 0, carries[bi][0], hf_ref, first_visit),
                      scan_dir(bi, nchunks - 1 - j, 1, carries[bi][1], hb_ref, first_visit)) for bi in range(bb))

    half = nchunks // 2
    assert nchunks == 2 * half
    init = tuple((h0_ref[bi, 0:1, :], h0_ref[bi, 1:2, :]) for bi in range(bb))
    if nchunks <= 2:
        finals = init
        for j in range(nchunks):
            finals = body(j, finals, j < half)
    else:
        unroll_first = next(n for n in (4, 2, 1) if half % n == 0)
        unroll_second = next(n for n in (8, 4, 2, 1) if half % n == 0)
        finals = lax.fori_loop(0, half, functools.partial(body, first_visit=True), init, unroll=unroll_first)
        finals = lax.fori_loop(half, nchunks, functools.partial(body, first_visit=False), finals,
                               unroll=unroll_second)
    for bi in range(bb):
        hT_ref[bi, 0:1, :] = finals[bi][0]
        hT_ref[bi, 1:2, :] = finals[bi][1]

    def gate_out(c, carry):
        rows = pl.ds(pl.multiple_of(c * tc, tc), tc)
        for bi in range(bb):
            y_ref[bi, rows, :] = ((hf_ref[bi, rows, :] + hb_ref[bi, rows, :])
                                  * _silu(za_ref[bi, rows, :])).astype(y_ref.dtype)
        return carry

    lax.fori_loop(0, nchunks, gate_out, 0, unroll=2 if nchunks > 2 else 1)


def _rglru(xa, za, h0, conv_w, conv_b, wg, bg, lam, tc, bb):
    b, seq, d = xa.shape
    blk = lambda i, j: (i, 0, j)
    vmem = bb * (2 * (2 * seq * BLK * 4 + seq * BLK * 2) + 3 * seq * BLK * 4) + VMEM_TEMPORARIES
    return pl.pallas_call(
        functools.partial(_rglru_kernel, seq=seq, tc=tc, bb=bb),
        grid=(b // bb, d // BLK),
        in_specs=[pl.BlockSpec((bb, seq, BLK), blk),
                  pl.BlockSpec((bb, seq, BLK), blk),
                  pl.BlockSpec((CONV_W, BLK), lambda i, j: (0, j)),
                  pl.BlockSpec((1, BLK), lambda i, j: (0, j)),
                  pl.BlockSpec((2, 1, BLK, 2 * BLK), lambda i, j: (0, j, 0, 0)),
                  pl.BlockSpec((2, 1, 1, 2 * BLK), lambda i, j: (0, j, 0, 0)),
                  pl.BlockSpec((2, 1, BLK), lambda i, j: (0, 0, j)),
                  pl.BlockSpec((bb, 2, BLK), blk)],
        out_specs=[pl.BlockSpec((bb, seq, BLK), blk),
                   pl.BlockSpec((bb, 2, BLK), blk)],
        out_shape=[jax.ShapeDtypeStruct((b, seq, d), BF16),
                   jax.ShapeDtypeStruct((b, 2, d), F32)],
        scratch_shapes=[pltpu.VMEM((bb, seq, BLK), F32)] * 3,
        compiler_params=_cparams(("parallel", "parallel"), vmem),
        name="rglru_scan",
    )(xa, za, conv_w, conv_b, wg, bg, lam, h0)


def _fft_list(xs):
    n = len(xs)
    if n == 1:
        return xs
    even = _fft_list(xs[0::2])
    odd = _fft_list(xs[1::2])
    out = [None] * n
    for k in range(n // 2):
        o_re, o_im = odd[k]
        if k == 0:
            t_re, t_im = o_re, o_im
        elif 4 * k == n:
            t_re, t_im = o_im, -o_re
        else:
            ang = -2.0 * math.pi * k / n
            wr, wi = math.cos(ang), math.sin(ang)
            t_re = o_re * wr - o_im * wi
            t_im = o_re * wi + o_im * wr
        e_re, e_im = even[k]
        out[k] = (e_re + t_re, e_im + t_im)
        out[k + n // 2] = (e_re - t_re, e_im - t_im)
    return out


def _fourier_kernel(xc_ref, xs_ref, zb_ref, twc_ref, tws_ref, wpos_ref, wf_ref, bf_ref, y_ref, *scratch,
                    seq, tr):
    w = xc_ref.shape[2]
    nblk = w // BLK
    l1 = seq // FFT_L2

    def epilogue(fr, rows):
        parts = []
        for kb in range(nblk):
            yb = jnp.dot(fr[:, kb * BLK:(kb + 1) * BLK].astype(BF16), wf_ref[kb], preferred_element_type=F32)
            parts.append(yb + bf_ref[:, kb * BLK:(kb + 1) * BLK])
        yb = parts[0] if nblk == 1 else jnp.concatenate(parts, axis=1)
        y_ref[0, rows, :] = (yb * _silu(zb_ref[0, rows, :])).astype(y_ref.dtype)

    if l1 == 1:
        fr = (jnp.dot(wpos_ref[:, 0:FFT_L2], xc_ref[0].astype(BF16), preferred_element_type=F32)
              + jnp.dot(wpos_ref[:, FFT_L2:2 * FFT_L2], xs_ref[0].astype(BF16), preferred_element_type=F32))
        epilogue(fr, pl.ds(0, seq))
        return

    b_ref, fr_ref = scratch
    assert w == LANES

    def butterfly(r, carry):
        r0 = pl.multiple_of(r * SUBLANES, SUBLANES)
        zs = [(xc_ref[0, pl.ds(n1 * FFT_L2 + r0, SUBLANES), :], xs_ref[0, pl.ds(n1 * FFT_L2 + r0, SUBLANES), :])
              for n1 in range(l1)]
        for k1, (a_re, a_im) in enumerate(_fft_list(zs)):
            if k1 == 0:
                b_re, b_im = a_re, a_im
            else:
                tc_ = twc_ref[k1, pl.ds(r0, SUBLANES), :]
                ts_ = tws_ref[k1, pl.ds(r0, SUBLANES), :]
                b_re = a_re * tc_ + a_im * ts_
                b_im = a_im * tc_ - a_re * ts_
            b_ref[pl.ds(r0, SUBLANES), k1 * LANES:(k1 + 1) * LANES] = b_re
            b_ref[pl.ds(FFT_L2 + r0, SUBLANES), k1 * LANES:(k1 + 1) * LANES] = b_im
        return carry

    lax.fori_loop(0, FFT_L2 // SUBLANES, butterfly, 0, unroll=4)

    per_dot = min(4, l1)
    for nb in range(l1 // per_dot):
        cols = slice(nb * per_dot * LANES, (nb + 1) * per_dot * LANES)
        fr = jnp.dot(wpos_ref[...], b_ref[:, cols].astype(BF16), preferred_element_type=F32)
        for kk in range(per_dot):
            fr_ref[pl.ds(nb * per_dot + kk, FFT_L2, stride=l1), :] = fr[:, kk * LANES:(kk + 1) * LANES]

    def finish(i, carry):
        rows = pl.ds(pl.multiple_of(i * tr, tr), tr)
        epilogue(fr_ref[rows, :], rows)
        return carry

    lax.fori_loop(0, seq // tr, finish, 0, unroll=4)


def _fourier(xc, xs, zb, twc, tws, w_pos, w_four, b_four, w_blk, tr):
    b, seq, d = xc.shape
    nblk = w_blk // BLK
    l1 = seq // FFT_L2
    blk = lambda i, j: (i, 0, j)
    scratch = []
    if l1 > 1:
        scratch = [pltpu.VMEM((2 * FFT_L2, l1 * w_blk), F32), pltpu.VMEM((seq, w_blk), F32)]
    vmem = (2 * (3 * seq * w_blk * 4 + seq * w_blk * 2 + 2 * twc.size * 4) + 3 * seq * w_blk * 4
            + VMEM_TEMPORARIES)
    return pl.pallas_call(
        functools.partial(_fourier_kernel, seq=seq, tr=tr),
        grid=(b, d // w_blk),
        in_specs=[pl.BlockSpec((1, seq, w_blk), blk),
                  pl.BlockSpec((1, seq, w_blk), blk),
                  pl.BlockSpec((1, seq, w_blk), blk),
                  pl.BlockSpec(twc.shape, lambda i, j: (0, 0, 0)),
                  pl.BlockSpec(tws.shape, lambda i, j: (0, 0, 0)),
                  pl.BlockSpec(w_pos.shape, lambda i, j: (0, 0)),
                  pl.BlockSpec((nblk, BLK, BLK), lambda i, j: (j, 0, 0)),
                  pl.BlockSpec((1, w_blk), lambda i, j: (0, j))],
        out_specs=pl.BlockSpec((1, seq, w_blk), blk),
        out_shape=jax.ShapeDtypeStruct((b, seq, d), BF16),
        scratch_shapes=scratch,
        compiler_params=_cparams(("parallel", "parallel"), vmem),
        name="fourier_mix",
    )(xc, xs, zb, twc, tws, w_pos, w_four, b_four)


def _outproj_kernel(*refs, n_in):
    y_refs = refs[:n_in]
    w_refs = refs[n_in:2 * n_in]
    x_ref, mod_ref, g_ref, o_ref = refs[2 * n_in:]
    d = D_MODEL
    y = jnp.dot(y_refs[0][...], w_refs[0][...], preferred_element_type=F32)
    for k in range(1, n_in):
        y = y + jnp.dot(y_refs[k][...], w_refs[k][...], preferred_element_type=F32)
    yn = y * lax.rsqrt(jnp.mean(y * y, axis=-1, keepdims=True) + EPS) * g_ref[...]
    o_ref[...] = x_ref[...] + mod_ref[0, :, 2 * d:3 * d] * yn


def _outproj(ys, ws, x2d, mod_l, g_post, row_off, rows_per_mod, tm):
    m, d = x2d.shape
    n_in = len(ys)
    row = lambda i: (i, 0)
    const = lambda i: (0, 0)
    vmem = 2 * (sum(wk.size for wk in ws) * 2 + sum(tm * yk.shape[1] for yk in ys) * 2 + 2 * tm * d * 4) + 4 * tm * d * 4
    return pl.pallas_call(
        functools.partial(_outproj_kernel, n_in=n_in),
        grid=(m // tm,),
        in_specs=([pl.BlockSpec((tm, yk.shape[1]), row) for yk in ys]
                  + [pl.BlockSpec(wk.shape, const) for wk in ws]
                  + [pl.BlockSpec((tm, d), row),
                     pl.BlockSpec((1, 1, 3 * d), _mod_index_map(row_off, rows_per_mod // tm)),
                     pl.BlockSpec((1, d), const)]),
        out_specs=pl.BlockSpec((tm, d), row),
        out_shape=jax.ShapeDtypeStruct((m, d), F32),
        compiler_params=_cparams(("parallel",), vmem),
        name="outproj_residual",
    )(*ys, *ws, x2d, mod_l, g_post)


def _inproj_o_kernel(x_ref, mod_ref, g_ref, w_ref, wgt_ref, gb_ref, q_ref, k_ref, v_ref, o_ref, z_ref, gpt_ref,
                     gp_ref):
    d = D_MODEL
    hb = _normed_input(x_ref, mod_ref, g_ref)
    q = jnp.dot(hb, w_ref[:, 0:d], preferred_element_type=F32)
    q_ref[...] = (q * (BLK ** -0.5)).astype(q_ref.dtype)
    k_ref[...] = jnp.dot(hb, w_ref[:, d:2 * d], preferred_element_type=F32).astype(k_ref.dtype)
    for half in range(2):
        cols = slice(half * d, (half + 1) * d)
        v_ref[:, cols] = jnp.dot(hb, w_ref[:, 2 * d + half * d:3 * d + half * d],
                                 preferred_element_type=F32).astype(v_ref.dtype)
        o_ref[:, cols] = jnp.dot(hb, w_ref[:, 4 * d + half * d:5 * d + half * d], preferred_element_type=F32)
        z_ref[:, cols] = jnp.dot(hb, w_ref[:, 6 * d + half * d:7 * d + half * d], preferred_element_type=F32)
    gp_ref[...] = jnp.dot(hb, wgt_ref[...], preferred_element_type=F32)
    gpt_ref[...] = gp_ref[...].T + gb_ref[:, 0:1]


def _inproj_o(x2d, mod_l, g_pre, w_main, w_gate, gate_bias, row_off, rows_per_mod, tm):
    m, d = x2d.shape
    row = lambda i: (i, 0)
    const = lambda i: (0, 0)
    ng = w_gate.shape[1]
    vmem = w_main.size * 2 + 2 * (tm * d * 4 + tm * d * 2 * 4 + tm * 2 * d * 4 * 2) + 6 * tm * d * 4
    return pl.pallas_call(
        _inproj_o_kernel,
        grid=(m // tm,),
        in_specs=[pl.BlockSpec((tm, d), row),
                  pl.BlockSpec((1, 1, 3 * d), _mod_index_map(row_off, rows_per_mod // tm)),
                  pl.BlockSpec((1, d), const),
                  pl.BlockSpec(w_main.shape, const, pipeline_mode=pl.Buffered(1)),
                  pl.BlockSpec(w_gate.shape, const),
                  pl.BlockSpec(gate_bias.shape, const)],
        out_specs=[pl.BlockSpec((tm, d), row), pl.BlockSpec((tm, d), row),
                   pl.BlockSpec((tm, 2 * d), row), pl.BlockSpec((tm, 2 * d), row), pl.BlockSpec((tm, 2 * d), row),
                   pl.BlockSpec((ng, tm), lambda i: (0, i))],
        out_shape=[jax.ShapeDtypeStruct((m, d), BF16), jax.ShapeDtypeStruct((m, d), BF16),
                   jax.ShapeDtypeStruct((m, 2 * d), BF16), jax.ShapeDtypeStruct((m, 2 * d), F32),
                   jax.ShapeDtypeStruct((m, 2 * d), F32), jax.ShapeDtypeStruct((ng, m), F32)],
        scratch_shapes=[pltpu.VMEM((tm, ng), F32)],
        compiler_params=_cparams(("parallel",), vmem),
        name="inproj_odd",
    )(x2d, mod_l, g_pre, w_main, w_gate, gate_bias)


def _chunk_scan_lanes(v, op, identity, reverse):
    n = v.shape[-1]
    pos = lax.broadcasted_iota(jnp.int32, v.shape, 1) & (CHUNK - 1)
    d = 1
    while d < CHUNK:
        if reverse:
            shifted = jnp.where(pos < CHUNK - d, pltpu.roll(v, n - d, axis=1), identity)
        else:
            shifted = jnp.where(pos >= d, pltpu.roll(v, d, axis=1), identity)
        v = op(v, shifted)
        d *= 2
    return v


def _gate_prep_kernel(g_ref, o_ref):
    for direction in range(2):
        reverse = direction == 1
        i_pre = g_ref[2 * direction]
        b = _chunk_scan_lanes(jax.nn.log_sigmoid(g_ref[2 * direction + 1]), jnp.add, 0.0, reverse)
        a = i_pre - b
        o_ref[3 * direction] = a
        o_ref[3 * direction + 1] = b
        o_ref[3 * direction + 2] = _chunk_scan_lanes(a, jnp.maximum, -jnp.inf, reverse)
    o_ref[6] = jnp.zeros_like(g_ref[0])
    o_ref[7] = jnp.zeros_like(g_ref[0])


def _gate_prep(gq, rb):
    _, r, seq = gq.shape
    return pl.pallas_call(
        _gate_prep_kernel,
        grid=(r // rb,),
        in_specs=[pl.BlockSpec((4, rb, seq), lambda i: (0, i, 0))],
        out_specs=pl.BlockSpec((SUBLANES, rb, seq), lambda i: (0, i, 0)),
        out_shape=jax.ShapeDtypeStruct((SUBLANES, r, seq), F32),
        compiler_params=_cparams(("parallel",), 2 * (4 + SUBLANES) * rb * seq * 4 + VMEM_TEMPORARIES),
        name="mlstm_gate_prep",
    )(gq)


def _mlstm_kernel(*refs, seq, heads, zero_init):
    q_ref, k_ref, v_ref, o_ref, z_ref, gate_ref = refs[0:6]
    refs = refs[6:]
    if not zero_init:
        c0_ref, n0_ref, m0_ref = refs[0:3]
        refs = refs[3:]
    y_ref, c_out_ref, n_out_ref, m_out_ref, hs_ref, cols_ref = refs
    t = CHUNK
    nchunks = seq // t
    half = nchunks // 2
    assert nchunks == 2 * half
    streams = [(hh, dr) for hh in range(heads) for dr in range(2)]

    def head_lanes(hh, width):
        return slice(hh * width, (hh + 1) * width)
    row_i = lax.broadcasted_iota(jnp.int32, (t, t), 0)
    col_i = lax.broadcasted_iota(jnp.int32, (t, t), 1)
    ones_blk = jnp.ones((t, LANES), BF16)

    sel_r = lax.broadcasted_iota(jnp.int32, (4 * SUBLANES, 4 * LANES), 0)
    sel_c = lax.broadcasted_iota(jnp.int32, (4 * SUBLANES, 4 * LANES), 1) // LANES
    wanted = jnp.where(sel_c == 0, 2, jnp.where(sel_c == 1, 1, jnp.where(sel_c == 2, 5, 4)))
    selector = jnp.where(((sel_r & (SUBLANES - 1)) == wanted) & (sel_r < 3 * SUBLANES), 1.0, 0.0).astype(BF16)

    piece = min(seq, 4 * t)
    for hh in range(heads):
        for c in range(seq // piece):
            g = gate_ref[0, hh, :, c * piece:(c + 1) * piece]
            hi = g.astype(BF16).astype(F32)
            mid = (g - hi).astype(BF16).astype(F32)
            lo = g - hi - mid
            stack = jnp.concatenate([hi, mid, lo, jnp.zeros_like(g)], axis=0).astype(BF16)
            cols_ref[hh, c * piece:(c + 1) * piece, :] = lax.dot_general(
                stack, selector, (((0,), (0,)), ((), ())), preferred_element_type=F32)

    def chunk_of(step, direction):
        return (nchunks - 1 - step) if direction == 1 else step

    def gate_terms(rows, hh, direction):
        g_rows = gate_ref[0, hh, :, rows]
        last = 0 if direction == 1 else t - 1
        a_row = g_rows[3 * direction:3 * direction + 1, :]
        b_end = g_rows[3 * direction + 1:3 * direction + 2, last:last + 1]
        a_end = g_rows[3 * direction + 2:3 * direction + 3, last:last + 1]
        amax_rep = cols_ref[hh, rows, 2 * direction * LANES:(2 * direction + 1) * LANES]
        return a_row, b_end, a_end, amax_rep

    def scores(step, stream, m_rep):
        hh, direction = stream
        rows = pl.ds(pl.multiple_of(chunk_of(step, direction) * t, t), t)
        a_row, b_end, a_end, amax_rep = gate_terms(rows, hh, direction)
        m_prev = m_rep[:, 0:1]
        mx = jnp.maximum(m_prev, amax_rep)
        mask = (col_i >= row_i) if direction == 1 else (col_i <= row_i)
        e = jnp.where(mask, jnp.exp(a_row - mx), 0.0)
        s = lax.dot_general(q_ref[0, rows, head_lanes(hh, BLK)], k_ref[0, rows, head_lanes(hh, BLK)],
                            (((1,), (1,)), ((), ())), preferred_element_type=F32) * e
        m_next = jnp.broadcast_to(b_end + jnp.maximum(m_prev, a_end), (1, t))
        return (s.astype(BF16), jnp.sum(s, axis=-1, keepdims=True), m_rep), m_next

    def outputs(step, stream, memory, pending):
        hh, direction = stream
        c_mat, n_rep = memory
        s_b, s_sum, m_rep = pending
        rows = pl.ds(pl.multiple_of(chunk_of(step, direction) * t, t), t)
        qc = q_ref[0, rows, head_lanes(hh, BLK)]
        kc = k_ref[0, rows, head_lanes(hh, BLK)]
        vc = v_ref[0, rows, head_lanes(hh, DV)]
        a_row, _, a_end, amax_rep = gate_terms(rows, hh, direction)
        b_rep = cols_ref[hh, rows, (2 * direction + 1) * LANES:(2 * direction + 2) * LANES]
        m_prev = m_rep[:, 0:1]
        mx_end = jnp.maximum(m_prev, a_end)
        mx = jnp.maximum(m_prev, amax_rep)
        sv = jnp.dot(s_b, vc, preferred_element_type=F32)
        qcn = jnp.dot(qc, jnp.concatenate([c_mat.astype(BF16), n_rep.astype(BF16)], axis=1),
                      preferred_element_type=F32)
        wkk_t = (kc.astype(F32).T * jnp.exp(a_row - mx_end)).astype(BF16)
        upd = jnp.dot(wkk_t, jnp.concatenate([vc, ones_blk], axis=1), preferred_element_type=F32)
        decay = jnp.exp(m_prev - mx_end)
        w_inter = jnp.exp(m_prev - mx)
        den = s_sum + w_inter * qcn[:, DV:DV + LANES]
        inv = 1.0 / jnp.maximum(jnp.abs(den), jnp.exp(-(b_rep + mx)))
        hc = jnp.concatenate(
            [(sv[:, kb * LANES:(kb + 1) * LANES] + w_inter * qcn[:, kb * LANES:(kb + 1) * LANES]) * inv
             for kb in range(DV // LANES)], axis=1)
        return rows, hc, (decay * c_mat + upd[:, 0:DV], decay * n_rep + upd[:, DV:DV + LANES])

    def n_to_columns(n_row):
        return jnp.broadcast_to(n_row, (LANES, BLK)).T

    def n_to_row(n_rep):
        return n_rep.T[0:1, :]

    def emit(hh, rows, hc, first_visit):
        if first_visit:
            hs_ref[hh, rows, :] = hc
        else:
            hsum = hs_ref[hh, rows, :] + hc
            lanes = head_lanes(hh, DV)
            y_ref[0, rows, lanes] = (jax.nn.sigmoid(o_ref[0, rows, lanes]) * hsum
                                     * _silu(z_ref[0, rows, lanes])).astype(y_ref.dtype)

    def pipelined(step, carry, first_visit, with_next):
        memories, pendings, m_reps = carry
        nxt = [scores(step + 1, sm, m_reps[i]) for i, sm in enumerate(streams)] if with_next else None
        done = [outputs(step, sm, memories[i], pendings[i]) for i, sm in enumerate(streams)]
        for (hh, _), (rows, hc, _) in zip(streams, done):
            emit(hh, rows, hc, first_visit)
        new_mem = tuple(dn[2] for dn in done)
        if with_next:
            return new_mem, tuple(n[0] for n in nxt), tuple(n[1] for n in nxt)
        return new_mem, pendings, m_reps

    if zero_init:
        memories = tuple((jnp.zeros((BLK, DV), F32), jnp.zeros((BLK, LANES), F32)) for _ in streams)
        first = [scores(0, sm, jnp.zeros((1, t), F32)) for sm in streams]
    else:
        memories = tuple((c0_ref[0, dr, hh], n_to_columns(n0_ref[0, dr, hh])) for hh, dr in streams)
        first = [scores(0, (hh, dr), m0_ref[0, dr, hh]) for hh, dr in streams]
    carry = (memories, tuple(f[0] for f in first), tuple(f[1] for f in first))
    if nchunks <= 2:
        for step in range(nchunks - 1):
            carry = pipelined(step, carry, first_visit=step < half, with_next=True)
    else:
        unroll = 2 if half % 2 == 0 else 1
        carry = lax.fori_loop(0, half, functools.partial(pipelined, first_visit=True, with_next=True), carry,
                              unroll=unroll)
        carry = lax.fori_loop(half, nchunks - 1, functools.partial(pipelined, first_visit=False, with_next=True),
                              carry, unroll=unroll)
    memories, _, m_reps = pipelined(nchunks - 1, carry, first_visit=False, with_next=False)
    for i, (hh, dr) in enumerate(streams):
        c_out_ref[0, dr, hh] = memories[i][0]
        n_out_ref[0, dr, hh] = n_to_row(memories[i][1])
        m_out_ref[0, dr, hh] = m_reps[i]


def _mlstm(q, k, v, o, z, gates, states, heads):
    b, seq, _ = q.shape
    qk_blk = lambda i, h: (i, 0, h)
    st5 = lambda i, h: (i, 0, h, 0, 0)
    state_specs = [pl.BlockSpec((1, 2, heads, BLK, DV), st5),
                   pl.BlockSpec((1, 2, heads, 1, BLK), st5),
                   pl.BlockSpec((1, 2, heads, 1, CHUNK), st5)]
    vmem = heads * (2 * (2 * seq * BLK * 2 + seq * DV * 2 * 2 + 2 * seq * DV * 4 + 2 * 2 * BLK * DV * 4)
                    + seq * DV * 4 + seq * 4 * LANES * 4) + VMEM_TEMPORARIES
    return pl.pallas_call(
        functools.partial(_mlstm_kernel, seq=seq, heads=heads, zero_init=states is None),
        grid=(b, N_HEADS // heads),
        in_specs=[pl.BlockSpec((1, seq, heads * BLK), qk_blk),
                  pl.BlockSpec((1, seq, heads * BLK), qk_blk),
                  pl.BlockSpec((1, seq, heads * DV), qk_blk),
                  pl.BlockSpec((1, seq, heads * DV), qk_blk),
                  pl.BlockSpec((1, seq, heads * DV), qk_blk),
                  pl.BlockSpec((1, heads, SUBLANES, seq), lambda i, h: (i, h, 0, 0))]
        + ([] if states is None else state_specs),
        out_specs=[pl.BlockSpec((1, seq, heads * DV), qk_blk)] + state_specs,
        out_shape=[jax.ShapeDtypeStruct((b, seq, N_HEADS * DV), BF16),
                   jax.ShapeDtypeStruct((b, 2, N_HEADS, BLK, DV), F32),
                   jax.ShapeDtypeStruct((b, 2, N_HEADS, 1, BLK), F32),
                   jax.ShapeDtypeStruct((b, 2, N_HEADS, 1, CHUNK), F32)],
        scratch_shapes=[pltpu.VMEM((heads, seq, DV), F32), pltpu.VMEM((heads, seq, 4 * LANES), F32)],
        compiler_params=_cparams(("parallel", "parallel"), vmem),
        name="mlstm_chunkwise",
    )(q, k, v, o, z, gates, *(() if states is None else states))


def _dft_tables(seq):
    n = np.arange(BLK)
    ang = 2.0 * np.pi * np.outer(n, n) / BLK
    dft_c = np.concatenate([np.cos(ang), -np.sin(ang)], axis=1) / np.sqrt(BLK)
    l1 = seq // FFT_L2
    n2 = np.arange(FFT_L2)
    ang2 = 2.0 * np.pi * np.outer(n2, n2) / FFT_L2
    w_pos = np.concatenate([np.cos(ang2), np.sin(ang2)], axis=1) / np.sqrt(seq)
    angt = 2.0 * np.pi * np.outer(np.arange(l1), n2) / seq
    twc = np.broadcast_to(np.cos(angt)[:, :, None], (l1, FFT_L2, LANES))
    tws = np.broadcast_to(np.sin(angt)[:, :, None], (l1, FFT_L2, LANES))
    return (jnp.asarray(dft_c, F32).astype(BF16), jnp.asarray(w_pos, F32).astype(BF16),
            jnp.asarray(twc, F32), jnp.asarray(tws, F32))


def _run_trunk(x, mod, row_off, per_batch_mod, lru0, mlstm0, p, cfg):
    b, seq, d = x.shape
    m = b * seq
    rows_per_mod = seq if per_batch_mod else m
    x2d = x.reshape(m, d)
    dft_c, w_pos, twc, tws = _dft_tables(seq)

    mod0 = mod[0].reshape(MOD_ROWS, 1, 3 * d)
    xa, za, zb, xc, xs = _inproj_e(x2d, mod0, p["g_pre"][0:1], p["w_in_e"], dft_c, row_off, rows_per_mod,
                                   cfg["tm_in"])
    r3 = lambda a: a.reshape(b, seq, a.shape[-1])
    ya, lru_f = _rglru(r3(xa), r3(za), lru0, p["conv_w"], p["conv_b"], p["wg"], p["bg"], p["lam"], cfg["tc_lru"],
                       cfg["bb_lru"])
    yb = _fourier(r3(xc), r3(xs), r3(zb), twc, tws, w_pos, p["w_four"], p["b_four"], cfg["w_four"], cfg["tr_four"])
    x1 = _outproj([ya.reshape(m, d), yb.reshape(m, d)], [p["w_out_e"][0:d], p["w_out_e"][d:2 * d]],
                  x2d, mod0, p["g_post"][0:1], row_off, rows_per_mod, cfg["tm_out"])

    mod1 = mod[1].reshape(MOD_ROWS, 1, 3 * d)
    q, k, v, o, z, gpt = _inproj_o(x1, mod1, p["g_pre"][1:2], p["w_in_o"], p["w_gate"], p["gate_bias"],
                                   row_off, rows_per_mod, cfg["tm_in_o"])
    gq = gpt[0:4 * N_HEADS].reshape(4, N_HEADS, b, seq).transpose(0, 2, 1, 3).reshape(4, b * N_HEADS, seq)
    gates = _gate_prep(gq, cfg["rb_gate"]).reshape(SUBLANES, b, N_HEADS, seq).transpose(1, 2, 0, 3)
    y, c_f, n_f, m_f = _mlstm(r3(q), r3(k), r3(v), r3(o), r3(z), gates, mlstm0, cfg["heads_mlstm"])
    x2 = _outproj([y.reshape(m, 2 * d)], [p["w_out_o"]], x1, mod1, p["g_post"][1:2], row_off, rows_per_mod,
                  cfg["tm_out"])
    return x2.reshape(b, seq, d), lru_f, c_f, n_f, m_f


def _prepare_params(conv_w, conv_b, w_rg, b_rg, w_ig, b_ig, lru_lambda, w_four, b_four, w_in_e, w_out_e,
                    w_in_o, b_if, w_out_o, g_pre, g_post):
    d = D_MODEL
    wg = jnp.concatenate([w_rg[0], w_ig[0]], axis=-1).astype(BF16)
    bg = jnp.concatenate([b_rg[0].reshape(2, N_HEADS, 1, BLK), b_ig[0].reshape(2, N_HEADS, 1, BLK)], axis=-1)
    n_gate = 4 * N_HEADS
    w_gate = jnp.zeros((d, LANES), F32).at[:, 0:n_gate].set(w_in_o[0][:, 8 * d:8 * d + n_gate]).astype(BF16)
    bias = jnp.zeros((LANES,), F32).at[0:n_gate].set(b_if[0].reshape(n_gate))
    gate_bias = jnp.broadcast_to(bias[:, None], (LANES, LANES))
    return dict(
        g_pre=g_pre, g_post=g_post,
        w_in_e=w_in_e[0].astype(BF16), w_out_e=w_out_e[0].astype(BF16),
        conv_w=conv_w[0], conv_b=conv_b[0].reshape(1, d), wg=wg, bg=bg, lam=lru_lambda[0].reshape(2, 1, d),
        w_four=w_four[0].astype(BF16), b_four=b_four[0].reshape(1, d),
        w_in_o=w_in_o[0].astype(BF16), w_gate=w_gate, gate_bias=gate_bias,
        w_out_o=w_out_o[0].astype(BF16))


def _trunk_config(batch, seq):
    short = seq <= 4 * CHUNK
    rows = batch * seq
    return dict(
        tm_in=min(512, rows), tm_in_o=min(512, rows), tm_out=min(1024, rows),
        tc_lru=CHUNK,
        bb_lru=next(n for n in (4, 2, 1) if batch % n == 0) if short else 1,
        w_four=D_MODEL if short else BLK,
        tr_four=min(seq, 256 if short else 512),
        rb_gate=min(batch * N_HEADS, max(SUBLANES, (32 * 1024) // seq)),
        heads_mlstm=4 if short else 1)


def kernel(x_prompt, x_sample, c, state_lru, state_mlstm_C, state_mlstm_n, state_mlstm_m, c_ctx, w_mod, b_mod,
           g_pre, g_post, w_in_e, conv_w, conv_b, w_rg, b_rg, w_ig, b_ig, lru_lambda, w_four, b_four, w_out_e,
           w_in_o, b_if, w_out_o):
    d = D_MODEL
    bp = x_prompt.shape[0]
    bs = x_sample.shape[0]
    p = _prepare_params(conv_w, conv_b, w_rg, b_rg, w_ig, b_ig, lru_lambda, w_four, b_four, w_in_e, w_out_e,
                        w_in_o, b_if, w_out_o, g_pre, g_post)
    cond = jnp.concatenate([c_ctx[None, :], c, jnp.zeros((MOD_ROWS - 1 - bs, d), F32)], axis=0)
    mod = _modulation(cond, w_mod, b_mod)

    y_prompt, lru_f, c_f, n_f, m_f = _run_trunk(x_prompt, mod, 0, False, jnp.zeros((bp, 2, d), F32), None, p,
                                                _trunk_config(bp, x_prompt.shape[1]))

    mlstm0 = (state_mlstm_C[:, 0], state_mlstm_n[:, 0].reshape(bs, 2, N_HEADS, 1, BLK),
              jnp.broadcast_to(state_mlstm_m[:, 0].reshape(bs, 2, N_HEADS, 1, 1), (bs, 2, N_HEADS, 1, CHUNK)))
    y_sample, _, _, _, _ = _run_trunk(x_sample, mod, 1, True, state_lru[:, 0], mlstm0, p,
                                      _trunk_config(bs, x_sample.shape[1]))

    return (y_prompt, y_sample, lru_f[:, None], c_f[:, None], n_f[:, None, :, :, 0, :], m_f[:, None, :, :, 0, 0])
```

```python
import functools
import math

import numpy as np
import jax
import jax.numpy as jnp
from jax import lax
from jax.experimental import pallas as pl
from jax.experimental.pallas import tpu as pltpu

F32 = jnp.float32
BF16 = jnp.bfloat16

D_MODEL = 1024
DEPTH = 2
EPS = 1e-6
LRU_C = 8.0
CONV_W = 4
N_HEADS = 8
BLK = D_MODEL // N_HEADS
DV = 2 * BLK
CHUNK = 128
FFT_L2 = 256
MOD_ROWS = 8

LANES = 128
SUBLANES = 8
MIB = 1024 * 1024
VMEM_LIMIT_CAP = 56 * MIB
VMEM_LIMIT_FLOOR = 16 * MIB
VMEM_TEMPORARIES = 16 * MIB


def _cparams(sem, vmem_bytes):
    return pltpu.CompilerParams(dimension_semantics=sem,
                                vmem_limit_bytes=int(min(max(vmem_bytes, VMEM_LIMIT_FLOOR), VMEM_LIMIT_CAP)))


def _silu(x):
    return x * jax.nn.sigmoid(x)


def _mod_kernel(cond_ref, w_ref, b_ref, o_ref):
    s = _silu(cond_ref[...]).astype(BF16)
    o_ref[0] = jnp.dot(s, w_ref[0].astype(BF16), preferred_element_type=F32) + b_ref[0]


def _modulation(cond, w_mod, b_mod):
    d = D_MODEL
    return pl.pallas_call(
        _mod_kernel,
        grid=(DEPTH, 3),
        in_specs=[pl.BlockSpec((MOD_ROWS, d), lambda l, j: (0, 0)),
                  pl.BlockSpec((1, d, d), lambda l, j: (l, 0, j)),
                  pl.BlockSpec((1, 1, d), lambda l, j: (l, 0, j))],
        out_specs=pl.BlockSpec((1, MOD_ROWS, d), lambda l, j: (l, 0, j)),
        out_shape=jax.ShapeDtypeStruct((DEPTH, MOD_ROWS, 3 * d), F32),
        compiler_params=_cparams(("arbitrary", "arbitrary"), 2 * d * d * 4 + VMEM_TEMPORARIES),
        name="adaln_mod",
    )(cond, w_mod, b_mod.reshape(DEPTH, 1, 3 * d))


def _normed_input(x_ref, mod_ref, g_ref):
    d = D_MODEL
    x = x_ref[...]
    y = x * lax.rsqrt(jnp.mean(x * x, axis=-1, keepdims=True) + EPS) * g_ref[...]
    shift = mod_ref[0, :, 0:d]
    scale = mod_ref[0, :, d:2 * d]
    return (y * (1.0 + scale) + shift).astype(BF16)


def _mod_index_map(row_off, tiles_per_mod):
    return lambda i: (row_off + i // tiles_per_mod, 0, 0)


def _inproj_e_kernel(x_ref, mod_ref, g_ref, w_ref, dft_ref, xa_ref, za_ref, zb_ref, xc_ref, xs_ref):
    d = D_MODEL
    hb = _normed_input(x_ref, mod_ref, g_ref)
    xa_ref[...] = jnp.dot(hb, w_ref[:, 0:d], preferred_element_type=F32)
    za_ref[...] = jnp.dot(hb, w_ref[:, d:2 * d], preferred_element_type=F32)
    zb_ref[...] = jnp.dot(hb, w_ref[:, 3 * d:4 * d], preferred_element_type=F32)
    xb = jnp.dot(hb, w_ref[:, 2 * d:3 * d], preferred_element_type=F32).astype(BF16)
    for g in range(N_HEADS):
        cs = jnp.dot(xb[:, g * BLK:(g + 1) * BLK], dft_ref[...], preferred_element_type=F32)
        xc_ref[:, g * BLK:(g + 1) * BLK] = cs[:, 0:BLK]
        xs_ref[:, g * BLK:(g + 1) * BLK] = cs[:, BLK:2 * BLK]


def _inproj_e(x2d, mod_l, g_pre, w_in, dft_c, row_off, rows_per_mod, tm):
    m, d = x2d.shape
    row = lambda i: (i, 0)
    const = lambda i: (0, 0)
    out = jax.ShapeDtypeStruct((m, d), F32)
    vmem = 2 * (w_in.size * 2 + tm * d * 4 * 6) + 8 * tm * d * 4
    return pl.pallas_call(
        _inproj_e_kernel,
        grid=(m // tm,),
        in_specs=[pl.BlockSpec((tm, d), row),
                  pl.BlockSpec((1, 1, 3 * d), _mod_index_map(row_off, rows_per_mod // tm)),
                  pl.BlockSpec((1, d), const),
                  pl.BlockSpec(w_in.shape, const),
                  pl.BlockSpec(dft_c.shape, const)],
        out_specs=[pl.BlockSpec((tm, d), row)] * 5,
        out_shape=[out] * 5,
        compiler_params=_cparams(("parallel",), vmem),
        name="inproj_even",
    )(x2d, mod_l, g_pre, w_in, dft_c)


def _scan_sublanes(a, u, reverse):
    t = a.shape[0]
    assert t == SUBLANES
    row = lax.broadcasted_iota(jnp.int32, a.shape, 0)
    d = 1
    while d < t:
        shift = (t - d) if reverse else d
        valid = (row < t - d) if reverse else (row >= d)
        a_sh = jnp.where(valid, pltpu.roll(a, shift, axis=0), 1.0)
        u_sh = jnp.where(valid, pltpu.roll(u, shift, axis=0), 0.0)
        u = a * u_sh + u
        a = a * a_sh
        d *= 2
    return a, u


def _rglru_kernel(xa_ref, za_ref, cw_ref, cb_ref, wg_ref, bg_ref, lam_ref, h0_ref,
                  y_ref, hT_ref, hf_ref, hb_ref, xc_ref, *, seq, tc, bb):
    nchunks = seq // tc
    groups = tc // SUBLANES
    sub = lax.broadcasted_iota(jnp.int32, (SUBLANES, LANES), 0)

    def row_bcast(block, r):
        return jnp.broadcast_to(block[r:r + 1, :], (SUBLANES, LANES))

    def conv_chunk(bi, c):
        t0 = pl.multiple_of(c * tc, tc)
        xs = [xa_ref[bi, pl.ds(t0 + g, SUBLANES, stride=groups), :] for g in range(groups)]
        prev = xa_ref[bi, pl.ds(pl.multiple_of(jnp.maximum(t0 - SUBLANES, 0), SUBLANES), SUBLANES), :]
        nxt = xa_ref[bi, pl.ds(pl.multiple_of(jnp.minimum(t0 + tc, seq - SUBLANES), SUBLANES), SUBLANES), :]
        prev = jnp.where(c > 0, prev, 0.0)
        nxt = jnp.where(c < nchunks - 1, nxt, 0.0)
        before2 = jnp.where(sub == 0, row_bcast(prev, SUBLANES - 2), pltpu.roll(xs[groups - 2], 1, axis=0))
        before1 = jnp.where(sub == 0, row_bcast(prev, SUBLANES - 1), pltpu.roll(xs[groups - 1], 1, axis=0))
        after1 = jnp.where(sub == SUBLANES - 1, row_bcast(nxt, 0), pltpu.roll(xs[0], SUBLANES - 1, axis=0))
        ext = [before2, before1] + xs + [after1]
        out = []
        for g in range(groups):
            acc = cb_ref[...] + cw_ref[0:1, :] * ext[g]
            for j in range(1, CONV_W):
                acc = acc + cw_ref[j:j + 1, :] * ext[g + j]
            out.append(acc)
        return t0, jnp.concatenate(out, axis=0)

    def gates(xc, direction):
        g = jnp.dot(xc.astype(BF16), wg_ref[direction, 0], preferred_element_type=F32) + bg_ref[direction, 0]
        r = jax.nn.sigmoid(g[:, 0:BLK])
        i = jax.nn.sigmoid(g[:, BLK:2 * BLK])
        lam = lam_ref[direction]
        softplus_neg = jnp.maximum(-lam, 0.0) + jnp.log1p(jnp.exp(-jnp.abs(lam)))
        neg_log_a = LRU_C * r * softplus_neg
        a = jnp.exp(-neg_log_a)
        om = jnp.tanh(neg_log_a) * (1.0 + a * a)
        root = jnp.where(om > 0.0, om * lax.rsqrt(om), 0.0)
        return a, root * (i * xc)

    def scan_dir(bi, c, direction, carry, out_ref, first_visit):
        reverse = direction == 1
        if first_visit:
            t0, xc = conv_chunk(bi, c)
            xc_ref[bi, pl.ds(t0, tc), :] = xc
        else:
            t0 = pl.multiple_of(c * tc, tc)
            xc = xc_ref[bi, pl.ds(t0, tc), :]
        a, u = gates(xc, direction)
        order = range(groups - 1, -1, -1) if reverse else range(groups)
        p_g, s_g = [None] * groups, [None] * groups
        p_run = s_run = None
        for g in order:
            a_v = a[g * SUBLANES:(g + 1) * SUBLANES]
            u_v = u[g * SUBLANES:(g + 1) * SUBLANES]
            if p_run is None:
                p_run, s_run = a_v, u_v
            else:
                s_run = a_v * s_run + u_v
                p_run = a_v * p_run
            p_g[g], s_g[g] = p_run, s_run
        pp, ss = _scan_sublanes(p_run, s_run, reverse)
        carry_b = jnp.broadcast_to(carry, (SUBLANES, LANES))
        seg_out = pp * carry_b + ss
        if reverse:
            h_in = jnp.where(sub == SUBLANES - 1, carry_b, pltpu.roll(seg_out, SUBLANES - 1, axis=0))
            new_carry = seg_out[0:1, :]
        else:
            h_in = jnp.where(sub == 0, carry_b, pltpu.roll(seg_out, 1, axis=0))
            new_carry = seg_out[SUBLANES - 1:SUBLANES, :]
        for g in range(groups):
            out_ref[bi, pl.ds(t0 + g, SUBLANES, stride=groups), :] = p_g[g] * h_in + s_g[g]
        return new_carry

    def body(j, carries, first_visit):
        return tuple((scan_dir(bi, j, 0, carries[bi][0], hf_ref, first_visit),
                      scan_dir(bi, nchunks - 1 - j, 1, carries[bi][1], hb_ref, first_visit)) for bi in range(bb))

    half = nchunks // 2
    assert nchunks == 2 * half
    init = tuple((h0_ref[bi, 0:1, :], h0_ref[bi, 1:2, :]) for bi in range(bb))
    if nchunks <= 2:
        finals = init
        for j in range(nchunks):
            finals = body(j, finals, j < half)
    else:
        unroll_first = next(n for n in (4, 2, 1) if half % n == 0)
        unroll_second = next(n for n in (8, 4, 2, 1) if half % n == 0)
        finals = lax.fori_loop(0, half, functools.partial(body, first_visit=True), init, unroll=unroll_first)
        finals = lax.fori_loop(half, nchunks, functools.partial(body, first_visit=False), finals,
                               unroll=unroll_second)
    for bi in range(bb):
        hT_ref[bi, 0:1, :] = finals[bi][0]
        hT_ref[bi, 1:2, :] = finals[bi][1]

    def gate_out(c, carry):
        rows = pl.ds(pl.multiple_of(c * tc, tc), tc)
        for bi in range(bb):
            y_ref[bi, rows, :] = ((hf_ref[bi, rows, :] + hb_ref[bi, rows, :])
                                  * _silu(za_ref[bi, rows, :])).astype(y_ref.dtype)
        return carry

    lax.fori_loop(0, nchunks, gate_out, 0, unroll=2 if nchunks > 2 else 1)


def _rglru(xa, za, h0, conv_w, conv_b, wg, bg, lam, tc, bb):
    b, seq, d = xa.shape
    blk = lambda i, j: (i, 0, j)
    vmem = bb * (2 * (2 * seq * BLK * 4 + seq * BLK * 2) + 3 * seq * BLK * 4) + VMEM_TEMPORARIES
    return pl.pallas_call(
        functools.partial(_rglru_kernel, seq=seq, tc=tc, bb=bb),
        grid=(b // bb, d // BLK),
        in_specs=[pl.BlockSpec((bb, seq, BLK), blk),
                  pl.BlockSpec((bb, seq, BLK), blk),
                  pl.BlockSpec((CONV_W, BLK), lambda i, j: (0, j)),
                  pl.BlockSpec((1, BLK), lambda i, j: (0, j)),
                  pl.BlockSpec((2, 1, BLK, 2 * BLK), lambda i, j: (0, j, 0, 0)),
                  pl.BlockSpec((2, 1, 1, 2 * BLK), lambda i, j: (0, j, 0, 0)),
                  pl.BlockSpec((2, 1, BLK), lambda i, j: (0, 0, j)),
                  pl.BlockSpec((bb, 2, BLK), blk)],
        out_specs=[pl.BlockSpec((bb, seq, BLK), blk),
                   pl.BlockSpec((bb, 2, BLK), blk)],
        out_shape=[jax.ShapeDtypeStruct((b, seq, d), BF16),
                   jax.ShapeDtypeStruct((b, 2, d), F32)],
        scratch_shapes=[pltpu.VMEM((bb, seq, BLK), F32)] * 3,
        compiler_params=_cparams(("parallel", "parallel"), vmem),
        name="rglru_scan",
    )(xa, za, conv_w, conv_b, wg, bg, lam, h0)


def _fft_list(xs):
    n = len(xs)
    if n == 1:
        return xs
    even = _fft_list(xs[0::2])
    odd = _fft_list(xs[1::2])
    out = [None] * n
    for k in range(n // 2):
        o_re, o_im = odd[k]
        if k == 0:
            t_re, t_im = o_re, o_im
        elif 4 * k == n:
            t_re, t_im = o_im, -o_re
        else:
            ang = -2.0 * math.pi * k / n
            wr, wi = math.cos(ang), math.sin(ang)
            t_re = o_re * wr - o_im * wi
            t_im = o_re * wi + o_im * wr
        e_re, e_im = even[k]
        out[k] = (e_re + t_re, e_im + t_im)
        out[k + n // 2] = (e_re - t_re, e_im - t_im)
    return out


def _fourier_kernel(xc_ref, xs_ref, zb_ref, twc_ref, tws_ref, wpos_ref, wf_ref, bf_ref, y_ref, *scratch,
                    seq, tr):
    w = xc_ref.shape[2]
    nblk = w // BLK
    l1 = seq // FFT_L2

    def epilogue(fr, rows):
        parts = []
        for kb in range(nblk):
            yb = jnp.dot(fr[:, kb * BLK:(kb + 1) * BLK].astype(BF16), wf_ref[kb], preferred_element_type=F32)
            parts.append(yb + bf_ref[:, kb * BLK:(kb + 1) * BLK])
        yb = parts[0] if nblk == 1 else jnp.concatenate(parts, axis=1)
        y_ref[0, rows, :] = (yb * _silu(zb_ref[0, rows, :])).astype(y_ref.dtype)

    if l1 == 1:
        fr = (jnp.dot(wpos_ref[:, 0:FFT_L2], xc_ref[0].astype(BF16), preferred_element_type=F32)
              + jnp.dot(wpos_ref[:, FFT_L2:2 * FFT_L2], xs_ref[0].astype(BF16), preferred_element_type=F32))
        epilogue(fr, pl.ds(0, seq))
        return

    b_ref, fr_ref = scratch
    assert w == LANES

    def butterfly(r, carry):
        r0 = pl.multiple_of(r * SUBLANES, SUBLANES)
        zs = [(xc_ref[0, pl.ds(n1 * FFT_L2 + r0, SUBLANES), :], xs_ref[0, pl.ds(n1 * FFT_L2 + r0, SUBLANES), :])
              for n1 in range(l1)]
        for k1, (a_re, a_im) in enumerate(_fft_list(zs)):
            if k1 == 0:
                b_re, b_im = a_re, a_im
            else:
                tc_ = twc_ref[k1, pl.ds(r0, SUBLANES), :]
                ts_ = tws_ref[k1, pl.ds(r0, SUBLANES), :]
                b_re = a_re * tc_ + a_im * ts_
                b_im = a_im * tc_ - a_re * ts_
            b_ref[pl.ds(r0, SUBLANES), k1 * LANES:(k1 + 1) * LANES] = b_re
            b_ref[pl.ds(FFT_L2 + r0, SUBLANES), k1 * LANES:(k1 + 1) * LANES] = b_im
        return carry

    lax.fori_loop(0, FFT_L2 // SUBLANES, butterfly, 0, unroll=4)

    per_dot = min(4, l1)
    for nb in range(l1 // per_dot):
        cols = slice(nb * per_dot * LANES, (nb + 1) * per_dot * LANES)
        fr = jnp.dot(wpos_ref[...], b_ref[:, cols].astype(BF16), preferred_element_type=F32)
        for kk in range(per_dot):
            fr_ref[pl.ds(nb * per_dot + kk, FFT_L2, stride=l1), :] = fr[:, kk * LANES:(kk + 1) * LANES]

    def finish(i, carry):
        rows = pl.ds(pl.multiple_of(i * tr, tr), tr)
        epilogue(fr_ref[rows, :], rows)
        return carry

    lax.fori_loop(0, seq // tr, finish, 0, unroll=4)


def _fourier(xc, xs, zb, twc, tws, w_pos, w_four, b_four, w_blk, tr):
    b, seq, d = xc.shape
    nblk = w_blk // BLK
    l1 = seq // FFT_L2
    blk = lambda i, j: (i, 0, j)
    scratch = []
    if l1 > 1:
        scratch = [pltpu.VMEM((2 * FFT_L2, l1 * w_blk), F32), pltpu.VMEM((seq, w_blk), F32)]
    vmem = (2 * (3 * seq * w_blk * 4 + seq * w_blk * 2 + 2 * twc.size * 4) + 3 * seq * w_blk * 4
            + VMEM_TEMPORARIES)
    return pl.pallas_call(
        functools.partial(_fourier_kernel, seq=seq, tr=tr),
        grid=(b, d // w_blk),
        in_specs=[pl.BlockSpec((1, seq, w_blk), blk),
                  pl.BlockSpec((1, seq, w_blk), blk),
                  pl.BlockSpec((1, seq, w_blk), blk),
                  pl.BlockSpec(twc.shape, lambda i, j: (0, 0, 0)),
                  pl.BlockSpec(tws.shape, lambda i, j: (0, 0, 0)),
                  pl.BlockSpec(w_pos.shape, lambda i, j: (0, 0)),
                  pl.BlockSpec((nblk, BLK, BLK), lambda i, j: (j, 0, 0)),
                  pl.BlockSpec((1, w_blk), lambda i, j: (0, j))],
        out_specs=pl.BlockSpec((1, seq, w_blk), blk),
        out_shape=jax.ShapeDtypeStruct((b, seq, d), BF16),
        scratch_shapes=scratch,
        compiler_params=_cparams(("parallel", "parallel"), vmem),
        name="fourier_mix",
    )(xc, xs, zb, twc, tws, w_pos, w_four, b_four)


def _outproj_kernel(*refs, n_in):
    y_refs = refs[:n_in]
    w_refs = refs[n_in:2 * n_in]
    x_ref, mod_ref, g_ref, o_ref = refs[2 * n_in:]
    d = D_MODEL
    y = jnp.dot(y_refs[0][...], w_refs[0][...], preferred_element_type=F32)
    for k in range(1, n_in):
        y = y + jnp.dot(y_refs[k][...], w_refs[k][...], preferred_element_type=F32)
    yn = y * lax.rsqrt(jnp.mean(y * y, axis=-1, keepdims=True) + EPS) * g_ref[...]
    o_ref[...] = x_ref[...] + mod_ref[0, :, 2 * d:3 * d] * yn


def _outproj(ys, ws, x2d, mod_l, g_post, row_off, rows_per_mod, tm):
    m, d = x2d.shape
    n_in = len(ys)
    row = lambda i: (i, 0)
    const = lambda i: (0, 0)
    vmem = 2 * (sum(wk.size for wk in ws) * 2 + sum(tm * yk.shape[1] for yk in ys) * 2 + 2 * tm * d * 4) + 4 * tm * d * 4
    return pl.pallas_call(
        functools.partial(_outproj_kernel, n_in=n_in),
        grid=(m // tm,),
        in_specs=([pl.BlockSpec((tm, yk.shape[1]), row) for yk in ys]
                  + [pl.BlockSpec(wk.shape, const) for wk in ws]
                  + [pl.BlockSpec((tm, d), row),
                     pl.BlockSpec((1, 1, 3 * d), _mod_index_map(row_off, rows_per_mod // tm)),
                     pl.BlockSpec((1, d), const)]),
        out_specs=pl.BlockSpec((tm, d), row),
        out_shape=jax.ShapeDtypeStruct((m, d), F32),
        compiler_params=_cparams(("parallel",), vmem),
        name="outproj_residual",
    )(*ys, *ws, x2d, mod_l, g_post)


def _inproj_o_kernel(x_ref, mod_ref, g_ref, w_ref, wgt_ref, gb_ref, q_ref, k_ref, v_ref, o_ref, z_ref, gpt_ref,
                     gp_ref):
    d = D_MODEL
    hb = _normed_input(x_ref, mod_ref, g_ref)
    q = jnp.dot(hb, w_ref[:, 0:d], preferred_element_type=F32)
    q_ref[...] = (q * (BLK ** -0.5)).astype(q_ref.dtype)
    k_ref[...] = jnp.dot(hb, w_ref[:, d:2 * d], preferred_element_type=F32).astype(k_ref.dtype)
    for half in range(2):
        cols = slice(half * d, (half + 1) * d)
        v_ref[:, cols] = jnp.dot(hb, w_ref[:, 2 * d + half * d:3 * d + half * d],
                                 preferred_element_type=F32).astype(v_ref.dtype)
        o_ref[:, cols] = jnp.dot(hb, w_ref[:, 4 * d + half * d:5 * d + half * d], preferred_element_type=F32)
        z_ref[:, cols] = jnp.dot(hb, w_ref[:, 6 * d + half * d:7 * d + half * d], preferred_element_type=F32)
    gp_ref[...] = jnp.dot(hb, wgt_ref[...], preferred_element_type=F32)
    gpt_ref[...] = gp_ref[...].T + gb_ref[:, 0:1]


def _inproj_o(x2d, mod_l, g_pre, w_main, w_gate, gate_bias, row_off, rows_per_mod, tm):
    m, d = x2d.shape
    row = lambda i: (i, 0)
    const = lambda i: (0, 0)
    ng = w_gate.shape[1]
    vmem = w_main.size * 2 + 2 * (tm * d * 4 + tm * d * 2 * 4 + tm * 2 * d * 4 * 2) + 6 * tm * d * 4
    return pl.pallas_call(
        _inproj_o_kernel,
        grid=(m // tm,),
        in_specs=[pl.BlockSpec((tm, d), row),
                  pl.BlockSpec((1, 1, 3 * d), _mod_index_map(row_off, rows_per_mod // tm)),
                  pl.BlockSpec((1, d), const),
                  pl.BlockSpec(w_main.shape, const, pipeline_mode=pl.Buffered(1)),
                  pl.BlockSpec(w_gate.shape, const),
                  pl.BlockSpec(gate_bias.shape, const)],
        out_specs=[pl.BlockSpec((tm, d), row), pl.BlockSpec((tm, d), row),
                   pl.BlockSpec((tm, 2 * d), row), pl.BlockSpec((tm, 2 * d), row), pl.BlockSpec((tm, 2 * d), row),
                   pl.BlockSpec((ng, tm), lambda i: (0, i))],
        out_shape=[jax.ShapeDtypeStruct((m, d), BF16), jax.ShapeDtypeStruct((m, d), BF16),
                   jax.ShapeDtypeStruct((m, 2 * d), BF16), jax.ShapeDtypeStruct((m, 2 * d), F32),
                   jax.ShapeDtypeStruct((m, 2 * d), F32), jax.ShapeDtypeStruct((ng, m), F32)],
        scratch_shapes=[pltpu.VMEM((tm, ng), F32)],
        compiler_params=_cparams(("parallel",), vmem),
        name="inproj_odd",
    )(x2d, mod_l, g_pre, w_main, w_gate, gate_bias)


def _chunk_scan_lanes(v, op, identity, reverse):
    n = v.shape[-1]
    pos = lax.broadcasted_iota(jnp.int32, v.shape, 1) & (CHUNK - 1)
    d = 1
    while d < CHUNK:
        if reverse:
            shifted = jnp.where(pos < CHUNK - d, pltpu.roll(v, n - d, axis=1), identity)
        else:
            shifted = jnp.where(pos >= d, pltpu.roll(v, d, axis=1), identity)
        v = op(v, shifted)
        d *= 2
    return v


def _gate_prep_kernel(g_ref, o_ref):
    for direction in range(2):
        reverse = direction == 1
        i_pre = g_ref[2 * direction]
        b = _chunk_scan_lanes(jax.nn.log_sigmoid(g_ref[2 * direction + 1]), jnp.add, 0.0, reverse)
        a = i_pre - b
        o_ref[3 * direction] = a
        o_ref[3 * direction + 1] = b
        o_ref[3 * direction + 2] = _chunk_scan_lanes(a, jnp.maximum, -jnp.inf, reverse)
    o_ref[6] = jnp.zeros_like(g_ref[0])
    o_ref[7] = jnp.zeros_like(g_ref[0])


def _gate_prep(gq, rb):
    _, r, seq = gq.shape
    return pl.pallas_call(
        _gate_prep_kernel,
        grid=(r // rb,),
        in_specs=[pl.BlockSpec((4, rb, seq), lambda i: (0, i, 0))],
        out_specs=pl.BlockSpec((SUBLANES, rb, seq), lambda i: (0, i, 0)),
        out_shape=jax.ShapeDtypeStruct((SUBLANES, r, seq), F32),
        compiler_params=_cparams(("parallel",), 2 * (4 + SUBLANES) * rb * seq * 4 + VMEM_TEMPORARIES),
        name="mlstm_gate_prep",
    )(gq)


def _mlstm_kernel(*refs, seq, heads, zero_init):
    q_ref, k_ref, v_ref, o_ref, z_ref, gate_ref = refs[0:6]
    refs = refs[6:]
    if not zero_init:
        c0_ref, n0_ref, m0_ref = refs[0:3]
        refs = refs[3:]
    y_ref, c_out_ref, n_out_ref, m_out_ref, hs_ref, cols_ref = refs
    t = CHUNK
    nchunks = seq // t
    half = nchunks // 2
    assert nchunks == 2 * half
    streams = [(hh, dr) for hh in range(heads) for dr in range(2)]

    def head_lanes(hh, width):
        return slice(hh * width, (hh + 1) * width)
    row_i = lax.broadcasted_iota(jnp.int32, (t, t), 0)
    col_i = lax.broadcasted_iota(jnp.int32, (t, t), 1)
    ones_blk = jnp.ones((t, LANES), BF16)

    sel_r = lax.broadcasted_iota(jnp.int32, (4 * SUBLANES, 4 * LANES), 0)
    sel_c = lax.broadcasted_iota(jnp.int32, (4 * SUBLANES, 4 * LANES), 1) // LANES
    wanted = jnp.where(sel_c == 0, 2, jnp.where(sel_c == 1, 1, jnp.where(sel_c == 2, 5, 4)))
    selector = jnp.where(((sel_r & (SUBLANES - 1)) == wanted) & (sel_r < 3 * SUBLANES), 1.0, 0.0).astype(BF16)

    piece = min(seq, 4 * t)
    for hh in range(heads):
        for c in range(seq // piece):
            g = gate_ref[0, hh, :, c * piece:(c + 1) * piece]
            hi = g.astype(BF16).astype(F32)
            mid = (g - hi).astype(BF16).astype(F32)
            lo = g - hi - mid
            stack = jnp.concatenate([hi, mid, lo, jnp.zeros_like(g)], axis=0).astype(BF16)
            cols_ref[hh, c * piece:(c + 1) * piece, :] = lax.dot_general(
                stack, selector, (((0,), (0,)), ((), ())), preferred_element_type=F32)

    def chunk_of(step, direction):
        return (nchunks - 1 - step) if direction == 1 else step

    def gate_terms(rows, hh, direction):
        g_rows = gate_ref[0, hh, :, rows]
        last = 0 if direction == 1 else t - 1
        a_row = g_rows[3 * direction:3 * direction + 1, :]
        b_end = g_rows[3 * direction + 1:3 * direction + 2, last:last + 1]
        a_end = g_rows[3 * direction + 2:3 * direction + 3, last:last + 1]
        amax_rep = cols_ref[hh, rows, 2 * direction * LANES:(2 * direction + 1) * LANES]
        return a_row, b_end, a_end, amax_rep

    def scores(step, stream, m_rep):
        hh, direction = stream
        rows = pl.ds(pl.multiple_of(chunk_of(step, direction) * t, t), t)
        a_row, b_end, a_end, amax_rep = gate_terms(rows, hh, direction)
        m_prev = m_rep[:, 0:1]
        mx = jnp.maximum(m_prev, amax_rep)
        mask = (col_i >= row_i) if direction == 1 else (col_i <= row_i)
        e = jnp.where(mask, jnp.exp(a_row - mx), 0.0)
        s = lax.dot_general(q_ref[0, rows, head_lanes(hh, BLK)], k_ref[0, rows, head_lanes(hh, BLK)],
                            (((1,), (1,)), ((), ())), preferred_element_type=F32) * e
        m_next = jnp.broadcast_to(b_end + jnp.maximum(m_prev, a_end), (1, t))
        return (s.astype(BF16), jnp.sum(s, axis=-1, keepdims=True), m_rep), m_next

    def outputs(step, stream, memory, pending):
        hh, direction = stream
        c_mat, n_rep = memory
        s_b, s_sum, m_rep = pending
        rows = pl.ds(pl.multiple_of(chunk_of(step, direction) * t, t), t)
        qc = q_ref[0, rows, head_lanes(hh, BLK)]
        kc = k_ref[0, rows, head_lanes(hh, BLK)]
        vc = v_ref[0, rows, head_lanes(hh, DV)]
        a_row, _, a_end, amax_rep = gate_terms(rows, hh, direction)
        b_rep = cols_ref[hh, rows, (2 * direction + 1) * LANES:(2 * direction + 2) * LANES]
        m_prev = m_rep[:, 0:1]
        mx_end = jnp.maximum(m_prev, a_end)
        mx = jnp.maximum(m_prev, amax_rep)
        sv = jnp.dot(s_b, vc, preferred_element_type=F32)
        qcn = jnp.dot(qc, jnp.concatenate([c_mat.astype(BF16), n_rep.astype(BF16)], axis=1),
                      preferred_element_type=F32)
        wkk_t = (kc.astype(F32).T * jnp.exp(a_row - mx_end)).astype(BF16)
        upd = jnp.dot(wkk_t, jnp.concatenate([vc, ones_blk], axis=1), preferred_element_type=F32)
        decay = jnp.exp(m_prev - mx_end)
        w_inter = jnp.exp(m_prev - mx)
        den = s_sum + w_inter * qcn[:, DV:DV + LANES]
        inv = 1.0 / jnp.maximum(jnp.abs(den), jnp.exp(-(b_rep + mx)))
        hc = jnp.concatenate(
            [(sv[:, kb * LANES:(kb + 1) * LANES] + w_inter * qcn[:, kb * LANES:(kb + 1) * LANES]) * inv
             for kb in range(DV // LANES)], axis=1)
        return rows, hc, (decay * c_mat + upd[:, 0:DV], decay * n_rep + upd[:, DV:DV + LANES])

    def n_to_columns(n_row):
        return jnp.broadcast_to(n_row, (LANES, BLK)).T

    def n_to_row(n_rep):
        return n_rep.T[0:1, :]

    def emit(hh, rows, hc, first_visit):
        if first_visit:
            hs_ref[hh, rows, :] = hc
        else:
            hsum = hs_ref[hh, rows, :] + hc
            lanes = head_lanes(hh, DV)
            y_ref[0, rows, lanes] = (jax.nn.sigmoid(o_ref[0, rows, lanes]) * hsum
                                     * _silu(z_ref[0, rows, lanes])).astype(y_ref.dtype)

    def pipelined(step, carry, first_visit, with_next):
        memories, pendings, m_reps = carry
        nxt = [scores(step + 1, sm, m_reps[i]) for i, sm in enumerate(streams)] if with_next else None
        done = [outputs(step, sm, memories[i], pendings[i]) for i, sm in enumerate(streams)]
        for (hh, _), (rows, hc, _) in zip(streams, done):
            emit(hh, rows, hc, first_visit)
        new_mem = tuple(dn[2] for dn in done)
        if with_next:
            return new_mem, tuple(n[0] for n in nxt), tuple(n[1] for n in nxt)
        return new_mem, pendings, m_reps

    if zero_init:
        memories = tuple((jnp.zeros((BLK, DV), F32), jnp.zeros((BLK, LANES), F32)) for _ in streams)
        first = [scores(0, sm, jnp.zeros((1, t), F32)) for sm in streams]
    else:
        memories = tuple((c0_ref[0, dr, hh], n_to_columns(n0_ref[0, dr, hh])) for hh, dr in streams)
        first = [scores(0, (hh, dr), m0_ref[0, dr, hh]) for hh, dr in streams]
    carry = (memories, tuple(f[0] for f in first), tuple(f[1] for f in first))
    if nchunks <= 2:
        for step in range(nchunks - 1):
            carry = pipelined(step, carry, first_visit=step < half, with_next=True)
    else:
        unroll = 2 if half % 2 == 0 else 1
        carry = lax.fori_loop(0, half, functools.partial(pipelined, first_visit=True, with_next=True), carry,
                              unroll=unroll)
        carry = lax.fori_loop(half, nchunks - 1, functools.partial(pipelined, first_visit=False, with_next=True),
                              carry, unroll=unroll)
    memories, _, m_reps = pipelined(nchunks - 1, carry, first_visit=False, with_next=False)
    for i, (hh, dr) in enumerate(streams):
        c_out_ref[0, dr, hh] = memories[i][0]
        n_out_ref[0, dr, hh] = n_to_row(memories[i][1])
        m_out_ref[0, dr, hh] = m_reps[i]


def _mlstm(q, k, v, o, z, gates, states, heads):
    b, seq, _ = q.shape
    qk_blk = lambda i, h: (i, 0, h)
    st5 = lambda i, h: (i, 0, h, 0, 0)
    state_specs = [pl.BlockSpec((1, 2, heads, BLK, DV), st5),
                   pl.BlockSpec((1, 2, heads, 1, BLK), st5),
                   pl.BlockSpec((1, 2, heads, 1, CHUNK), st5)]
    vmem = heads * (2 * (2 * seq * BLK * 2 + seq * DV * 2 * 2 + 2 * seq * DV * 4 + 2 * 2 * BLK * DV * 4)
                    + seq * DV * 4 + seq * 4 * LANES * 4) + VMEM_TEMPORARIES
    return pl.pallas_call(
        functools.partial(_mlstm_kernel, seq=seq, heads=heads, zero_init=states is None),
        grid=(b, N_HEADS // heads),
        in_specs=[pl.BlockSpec((1, seq, heads * BLK), qk_blk),
                  pl.BlockSpec((1, seq, heads * BLK), qk_blk),
                  pl.BlockSpec((1, seq, heads * DV), qk_blk),
                  pl.BlockSpec((1, seq, heads * DV), qk_blk),
                  pl.BlockSpec((1, seq, heads * DV), qk_blk),
                  pl.BlockSpec((1, heads, SUBLANES, seq), lambda i, h: (i, h, 0, 0))]
        + ([] if states is None else state_specs),
        out_specs=[pl.BlockSpec((1, seq, heads * DV), qk_blk)] + state_specs,
        out_shape=[jax.ShapeDtypeStruct((b, seq, N_HEADS * DV), BF16),
                   jax.ShapeDtypeStruct((b, 2, N_HEADS, BLK, DV), F32),
                   jax.ShapeDtypeStruct((b, 2, N_HEADS, 1, BLK), F32),
                   jax.ShapeDtypeStruct((b, 2, N_HEADS, 1, CHUNK), F32)],
        scratch_shapes=[pltpu.VMEM((heads, seq, DV), F32), pltpu.VMEM((heads, seq, 4 * LANES), F32)],
        compiler_params=_cparams(("parallel", "parallel"), vmem),
        name="mlstm_chunkwise",
    )(q, k, v, o, z, gates, *(() if states is None else states))


def _dft_tables(seq):
    n = np.arange(BLK)
    ang = 2.0 * np.pi * np.outer(n, n) / BLK
    dft_c = np.concatenate([np.cos(ang), -np.sin(ang)], axis=1) / np.sqrt(BLK)
    l1 = seq // FFT_L2
    n2 = np.arange(FFT_L2)
    ang2 = 2.0 * np.pi * np.outer(n2, n2) / FFT_L2
    w_pos = np.concatenate([np.cos(ang2), np.sin(ang2)], axis=1) / np.sqrt(seq)
    angt = 2.0 * np.pi * np.outer(np.arange(l1), n2) / seq
    twc = np.broadcast_to(np.cos(angt)[:, :, None], (l1, FFT_L2, LANES))
    tws = np.broadcast_to(np.sin(angt)[:, :, None], (l1, FFT_L2, LANES))
    return (jnp.asarray(dft_c, F32).astype(BF16), jnp.asarray(w_pos, F32).astype(BF16),
            jnp.asarray(twc, F32), jnp.asarray(tws, F32))


def _run_trunk(x, mod, row_off, per_batch_mod, lru0, mlstm0, p, cfg):
    b, seq, d = x.shape
    m = b * seq
    rows_per_mod = seq if per_batch_mod else m
    x2d = x.reshape(m, d)
    dft_c, w_pos, twc, tws = _dft_tables(seq)

    mod0 = mod[0].reshape(MOD_ROWS, 1, 3 * d)
    xa, za, zb, xc, xs = _inproj_e(x2d, mod0, p["g_pre"][0:1], p["w_in_e"], dft_c, row_off, rows_per_mod,
                                   cfg["tm_in"])
    r3 = lambda a: a.reshape(b, seq, a.shape[-1])
    ya, lru_f = _rglru(r3(xa), r3(za), lru0, p["conv_w"], p["conv_b"], p["wg"], p["bg"], p["lam"], cfg["tc_lru"],
                       cfg["bb_lru"])
    yb = _fourier(r3(xc), r3(xs), r3(zb), twc, tws, w_pos, p["w_four"], p["b_four"], cfg["w_four"], cfg["tr_four"])
    x1 = _outproj([ya.reshape(m, d), yb.reshape(m, d)], [p["w_out_e"][0:d], p["w_out_e"][d:2 * d]],
                  x2d, mod0, p["g_post"][0:1], row_off, rows_per_mod, cfg["tm_out"])

    mod1 = mod[1].reshape(MOD_ROWS, 1, 3 * d)
    q, k, v, o, z, gpt = _inproj_o(x1, mod1, p["g_pre"][1:2], p["w_in_o"], p["w_gate"], p["gate_bias"],
                                   row_off, rows_per_mod, cfg["tm_in_o"])
    gq = gpt[0:4 * N_HEADS].reshape(4, N_HEADS, b, seq).transpose(0, 2, 1, 3).reshape(4, b * N_HEADS, seq)
    gates = _gate_prep(gq, cfg["rb_gate"]).reshape(SUBLANES, b, N_HEADS, seq).transpose(1, 2, 0, 3)
    y, c_f, n_f, m_f = _mlstm(r3(q), r3(k), r3(v), r3(o), r3(z), gates, mlstm0, cfg["heads_mlstm"])
    x2 = _outproj([y.reshape(m, 2 * d)], [p["w_out_o"]], x1, mod1, p["g_post"][1:2], row_off, rows_per_mod,
                  cfg["tm_out"])
    return x2.reshape(b, seq, d), lru_f, c_f, n_f, m_f


def _prepare_params(conv_w, conv_b, w_rg, b_rg, w_ig, b_ig, lru_lambda, w_four, b_four, w_in_e, w_out_e,
                    w_in_o, b_if, w_out_o, g_pre, g_post):
    d = D_MODEL
    wg = jnp.concatenate([w_rg[0], w_ig[0]], axis=-1).astype(BF16)
    bg = jnp.concatenate([b_rg[0].reshape(2, N_HEADS, 1, BLK), b_ig[0].reshape(2, N_HEADS, 1, BLK)], axis=-1)
    n_gate = 4 * N_HEADS
    w_gate = jnp.zeros((d, LANES), F32).at[:, 0:n_gate].set(w_in_o[0][:, 8 * d:8 * d + n_gate]).astype(BF16)
    bias = jnp.zeros((LANES,), F32).at[0:n_gate].set(b_if[0].reshape(n_gate))
    gate_bias = jnp.broadcast_to(bias[:, None], (LANES, LANES))
    return dict(
        g_pre=g_pre, g_post=g_post,
        w_in_e=w_in_e[0].astype(BF16), w_out_e=w_out_e[0].astype(BF16),
        conv_w=conv_w[0], conv_b=conv_b[0].reshape(1, d), wg=wg, bg=bg, lam=lru_lambda[0].reshape(2, 1, d),
        w_four=w_four[0].astype(BF16), b_four=b_four[0].reshape(1, d),
        w_in_o=w_in_o[0].astype(BF16), w_gate=w_gate, gate_bias=gate_bias,
        w_out_o=w_out_o[0].astype(BF16))


def _trunk_config(batch, seq):
    short = seq <= 4 * CHUNK
    rows = batch * seq
    return dict(
        tm_in=min(512, rows), tm_in_o=min(512, rows), tm_out=min(1024, rows),
        tc_lru=CHUNK,
        bb_lru=next(n for n in (8, 4, 2, 1) if batch % n == 0) if short else 1,
        w_four=D_MODEL if short else BLK,
        tr_four=min(seq, 256 if short else 512),
        rb_gate=min(batch * N_HEADS, max(SUBLANES, (32 * 1024) // seq)),
        heads_mlstm=4 if short else 1)


def kernel(x_prompt, x_sample, c, state_lru, state_mlstm_C, state_mlstm_n, state_mlstm_m, c_ctx, w_mod, b_mod,
           g_pre, g_post, w_in_e, conv_w, conv_b, w_rg, b_rg, w_ig, b_ig, lru_lambda, w_four, b_four, w_out_e,
           w_in_o, b_if, w_out_o):
    d = D_MODEL
    bp = x_prompt.shape[0]
    bs = x_sample.shape[0]
    p = _prepare_params(conv_w, conv_b, w_rg, b_rg, w_ig, b_ig, lru_lambda, w_four, b_four, w_in_e, w_out_e,
                        w_in_o, b_if, w_out_o, g_pre, g_post)
    cond = jnp.concatenate([c_ctx[None, :], c, jnp.zeros((MOD_ROWS - 1 - bs, d), F32)], axis=0)
    mod = _modulation(cond, w_mod, b_mod)

    y_prompt, lru_f, c_f, n_f, m_f = _run_trunk(x_prompt, mod, 0, False, jnp.zeros((bp, 2, d), F32), None, p,
                                                _trunk_config(bp, x_prompt.shape[1]))

    mlstm0 = (state_mlstm_C[:, 0], state_mlstm_n[:, 0].reshape(bs, 2, N_HEADS, 1, BLK),
              jnp.broadcast_to(state_mlstm_m[:, 0].reshape(bs, 2, N_HEADS, 1, 1), (bs, 2, N_HEADS, 1, CHUNK)))
    y_sample, _, _, _, _ = _run_trunk(x_sample, mod, 1, True, state_lru[:, 0], mlstm0, p,
                                      _trunk_config(bs, x_sample.shape[1]))

    return (y_prompt, y_sample, lru_f[:, None], c_f[:, None], n_f[:, None, :, :, 0, :], m_f[:, None, :, :, 0, 0])
```

```python
import functools
import math

import numpy as np
import jax
import jax.numpy as jnp
from jax import lax
from jax.experimental import pallas as pl
from jax.experimental.pallas import tpu as pltpu

F32 = jnp.float32
BF16 = jnp.bfloat16

D_MODEL = 1024
DEPTH = 2
EPS = 1e-6
LRU_C = 8.0
CONV_W = 4
N_HEADS = 8
BLK = D_MODEL // N_HEADS
DV = 2 * BLK
CHUNK = 128
FFT_L2 = 256
MOD_ROWS = 8

LANES = 128
SUBLANES = 8
MIB = 1024 * 1024
VMEM_LIMIT_CAP = 56 * MIB
VMEM_LIMIT_FLOOR = 16 * MIB
VMEM_TEMPORARIES = 16 * MIB


def _cparams(sem, vmem_bytes):
    return pltpu.CompilerParams(dimension_semantics=sem,
                                vmem_limit_bytes=int(min(max(vmem_bytes, VMEM_LIMIT_FLOOR), VMEM_LIMIT_CAP)))


def _silu(x):
    return x * jax.nn.sigmoid(x)


def _mod_kernel(cond_ref, w_ref, b_ref, o_ref):
    s = _silu(cond_ref[...]).astype(BF16)
    o_ref[0] = jnp.dot(s, w_ref[0].astype(BF16), preferred_element_type=F32) + b_ref[0]


def _modulation(cond, w_mod, b_mod):
    d = D_MODEL
    return pl.pallas_call(
        _mod_kernel,
        grid=(DEPTH, 3),
        in_specs=[pl.BlockSpec((MOD_ROWS, d), lambda l, j: (0, 0)),
                  pl.BlockSpec((1, d, d), lambda l, j: (l, 0, j)),
                  pl.BlockSpec((1, 1, d), lambda l, j: (l, 0, j))],
        out_specs=pl.BlockSpec((1, MOD_ROWS, d), lambda l, j: (l, 0, j)),
        out_shape=jax.ShapeDtypeStruct((DEPTH, MOD_ROWS, 3 * d), F32),
        compiler_params=_cparams(("arbitrary", "arbitrary"), 2 * d * d * 4 + VMEM_TEMPORARIES),
        name="adaln_mod",
    )(cond, w_mod, b_mod.reshape(DEPTH, 1, 3 * d))


def _normed_input(x_ref, mod_ref, g_ref):
    d = D_MODEL
    x = x_ref[...]
    y = x * lax.rsqrt(jnp.mean(x * x, axis=-1, keepdims=True) + EPS) * g_ref[...]
    shift = mod_ref[0, :, 0:d]
    scale = mod_ref[0, :, d:2 * d]
    return (y * (1.0 + scale) + shift).astype(BF16)


def _mod_index_map(row_off, tiles_per_mod):
    return lambda i: (row_off + i // tiles_per_mod, 0, 0)


def _inproj_e_kernel(x_ref, mod_ref, g_ref, w_ref, dft_ref, xa_ref, za_ref, zb_ref, xc_ref, xs_ref):
    d = D_MODEL
    hb = _normed_input(x_ref, mod_ref, g_ref)
    xa_ref[...] = jnp.dot(hb, w_ref[:, 0:d], preferred_element_type=F32)
    za_ref[...] = jnp.dot(hb, w_ref[:, d:2 * d], preferred_element_type=F32)
    zb_ref[...] = jnp.dot(hb, w_ref[:, 3 * d:4 * d], preferred_element_type=F32)
    xb = jnp.dot(hb, w_ref[:, 2 * d:3 * d], preferred_element_type=F32).astype(BF16)
    for g in range(N_HEADS):
        cs = jnp.dot(xb[:, g * BLK:(g + 1) * BLK], dft_ref[...], preferred_element_type=F32)
        xc_ref[:, g * BLK:(g + 1) * BLK] = cs[:, 0:BLK]
        xs_ref[:, g * BLK:(g + 1) * BLK] = cs[:, BLK:2 * BLK]


def _inproj_e(x2d, mod_l, g_pre, w_in, dft_c, row_off, rows_per_mod, tm):
    m, d = x2d.shape
    row = lambda i: (i, 0)
    const = lambda i: (0, 0)
    out = jax.ShapeDtypeStruct((m, d), F32)
    vmem = 2 * (w_in.size * 2 + tm * d * 4 * 6) + 8 * tm * d * 4
    return pl.pallas_call(
        _inproj_e_kernel,
        grid=(m // tm,),
        in_specs=[pl.BlockSpec((tm, d), row),
                  pl.BlockSpec((1, 1, 3 * d), _mod_index_map(row_off, rows_per_mod // tm)),
                  pl.BlockSpec((1, d), const),
                  pl.BlockSpec(w_in.shape, const),
                  pl.BlockSpec(dft_c.shape, const)],
        out_specs=[pl.BlockSpec((tm, d), row)] * 5,
        out_shape=[out] * 5,
        compiler_params=_cparams(("parallel",), vmem),
        name="inproj_even",
    )(x2d, mod_l, g_pre, w_in, dft_c)


def _scan_sublanes(a, u, reverse):
    t = a.shape[0]
    assert t == SUBLANES
    row = lax.broadcasted_iota(jnp.int32, a.shape, 0)
    d = 1
    while d < t:
        shift = (t - d) if reverse else d
        valid = (row < t - d) if reverse else (row >= d)
        a_sh = jnp.where(valid, pltpu.roll(a, shift, axis=0), 1.0)
        u_sh = jnp.where(valid, pltpu.roll(u, shift, axis=0), 0.0)
        u = a * u_sh + u
        a = a * a_sh
        d *= 2
    return a, u


def _rglru_kernel(xa_ref, za_ref, cw_ref, cb_ref, wg_ref, bg_ref, lam_ref, h0_ref,
                  y_ref, hT_ref, hf_ref, hb_ref, xc_ref, *, seq, tc, bb):
    nchunks = seq // tc
    groups = tc // SUBLANES
    sub = lax.broadcasted_iota(jnp.int32, (SUBLANES, LANES), 0)

    def row_bcast(block, r):
        return jnp.broadcast_to(block[r:r + 1, :], (SUBLANES, LANES))

    def conv_chunk(bi, c):
        t0 = pl.multiple_of(c * tc, tc)
        xs = [xa_ref[bi, pl.ds(t0 + g, SUBLANES, stride=groups), :] for g in range(groups)]
        prev = xa_ref[bi, pl.ds(pl.multiple_of(jnp.maximum(t0 - SUBLANES, 0), SUBLANES), SUBLANES), :]
        nxt = xa_ref[bi, pl.ds(pl.multiple_of(jnp.minimum(t0 + tc, seq - SUBLANES), SUBLANES), SUBLANES), :]
        prev = jnp.where(c > 0, prev, 0.0)
        nxt = jnp.where(c < nchunks - 1, nxt, 0.0)
        before2 = jnp.where(sub == 0, row_bcast(prev, SUBLANES - 2), pltpu.roll(xs[groups - 2], 1, axis=0))
        before1 = jnp.where(sub == 0, row_bcast(prev, SUBLANES - 1), pltpu.roll(xs[groups - 1], 1, axis=0))
        after1 = jnp.where(sub == SUBLANES - 1, row_bcast(nxt, 0), pltpu.roll(xs[0], SUBLANES - 1, axis=0))
        ext = [before2, before1] + xs + [after1]
        out = []
        for g in range(groups):
            acc = cb_ref[...] + cw_ref[0:1, :] * ext[g]
            for j in range(1, CONV_W):
                acc = acc + cw_ref[j:j + 1, :] * ext[g + j]
            out.append(acc)
        return t0, jnp.concatenate(out, axis=0)

    def gates(xc, direction):
        g = jnp.dot(xc.astype(BF16), wg_ref[direction, 0], preferred_element_type=F32) + bg_ref[direction, 0]
        r = jax.nn.sigmoid(g[:, 0:BLK])
        i = jax.nn.sigmoid(g[:, BLK:2 * BLK])
        lam = lam_ref[direction]
        softplus_neg = jnp.maximum(-lam, 0.0) + jnp.log1p(jnp.exp(-jnp.abs(lam)))
        neg_log_a = LRU_C * r * softplus_neg
        a = jnp.exp(-neg_log_a)
        om = jnp.tanh(neg_log_a) * (1.0 + a * a)
        root = jnp.where(om > 0.0, om * lax.rsqrt(om), 0.0)
        return a, root * (i * xc)

    def scan_dir(bi, c, direction, carry, out_ref, first_visit):
        reverse = direction == 1
        if first_visit:
            t0, xc = conv_chunk(bi, c)
            xc_ref[bi, pl.ds(t0, tc), :] = xc
        else:
            t0 = pl.multiple_of(c * tc, tc)
            xc = xc_ref[bi, pl.ds(t0, tc), :]
        a, u = gates(xc, direction)
        order = range(groups - 1, -1, -1) if reverse else range(groups)
        p_g, s_g = [None] * groups, [None] * groups
        p_run = s_run = None
        for g in order:
            a_v = a[g * SUBLANES:(g + 1) * SUBLANES]
            u_v = u[g * SUBLANES:(g + 1) * SUBLANES]
            if p_run is None:
                p_run, s_run = a_v, u_v
            else:
                s_run = a_v * s_run + u_v
                p_run = a_v * p_run
            p_g[g], s_g[g] = p_run, s_run
        pp, ss = _scan_sublanes(p_run, s_run, reverse)
        carry_b = jnp.broadcast_to(carry, (SUBLANES, LANES))
        seg_out = pp * carry_b + ss
        if reverse:
            h_in = jnp.where(sub == SUBLANES - 1, carry_b, pltpu.roll(seg_out, SUBLANES - 1, axis=0))
            new_carry = seg_out[0:1, :]
        else:
            h_in = jnp.where(sub == 0, carry_b, pltpu.roll(seg_out, 1, axis=0))
            new_carry = seg_out[SUBLANES - 1:SUBLANES, :]
        for g in range(groups):
            out_ref[bi, pl.ds(t0 + g, SUBLANES, stride=groups), :] = p_g[g] * h_in + s_g[g]
        return new_carry

    def body(j, carries, first_visit):
        return tuple((scan_dir(bi, j, 0, carries[bi][0], hf_ref, first_visit),
                      scan_dir(bi, nchunks - 1 - j, 1, carries[bi][1], hb_ref, first_visit)) for bi in range(bb))

    half = nchunks // 2
    assert nchunks == 2 * half
    init = tuple((h0_ref[bi, 0:1, :], h0_ref[bi, 1:2, :]) for bi in range(bb))
    if nchunks <= 2:
        finals = init
        for j in range(nchunks):
            finals = body(j, finals, j < half)
    else:
        unroll_first = next(n for n in (4, 2, 1) if half % n == 0)
        unroll_second = next(n for n in (8, 4, 2, 1) if half % n == 0)
        finals = lax.fori_loop(0, half, functools.partial(body, first_visit=True), init, unroll=unroll_first)
        finals = lax.fori_loop(half, nchunks, functools.partial(body, first_visit=False), finals,
                               unroll=unroll_second)
    for bi in range(bb):
        hT_ref[bi, 0:1, :] = finals[bi][0]
        hT_ref[bi, 1:2, :] = finals[bi][1]

    def gate_out(c, carry):
        rows = pl.ds(pl.multiple_of(c * tc, tc), tc)
        for bi in range(bb):
            y_ref[bi, rows, :] = ((hf_ref[bi, rows, :] + hb_ref[bi, rows, :])
                                  * _silu(za_ref[bi, rows, :])).astype(y_ref.dtype)
        return carry

    lax.fori_loop(0, nchunks, gate_out, 0, unroll=2 if nchunks > 2 else 1)


def _rglru(xa, za, h0, conv_w, conv_b, wg, bg, lam, tc, bb):
    b, seq, d = xa.shape
    blk = lambda i, j: (i, 0, j)
    vmem = bb * (2 * (2 * seq * BLK * 4 + seq * BLK * 2) + 3 * seq * BLK * 4) + VMEM_TEMPORARIES
    return pl.pallas_call(
        functools.partial(_rglru_kernel, seq=seq, tc=tc, bb=bb),
        grid=(b // bb, d // BLK),
        in_specs=[pl.BlockSpec((bb, seq, BLK), blk),
                  pl.BlockSpec((bb, seq, BLK), blk),
                  pl.BlockSpec((CONV_W, BLK), lambda i, j: (0, j)),
                  pl.BlockSpec((1, BLK), lambda i, j: (0, j)),
                  pl.BlockSpec((2, 1, BLK, 2 * BLK), lambda i, j: (0, j, 0, 0)),
                  pl.BlockSpec((2, 1, 1, 2 * BLK), lambda i, j: (0, j, 0, 0)),
                  pl.BlockSpec((2, 1, BLK), lambda i, j: (0, 0, j)),
                  pl.BlockSpec((bb, 2, BLK), blk)],
        out_specs=[pl.BlockSpec((bb, seq, BLK), blk),
                   pl.BlockSpec((bb, 2, BLK), blk)],
        out_shape=[jax.ShapeDtypeStruct((b, seq, d), BF16),
                   jax.ShapeDtypeStruct((b, 2, d), F32)],
        scratch_shapes=[pltpu.VMEM((bb, seq, BLK), F32)] * 3,
        compiler_params=_cparams(("parallel", "parallel"), vmem),
        name="rglru_scan",
    )(xa, za, conv_w, conv_b, wg, bg, lam, h0)


def _fft_list(xs):
    n = len(xs)
    if n == 1:
        return xs
    even = _fft_list(xs[0::2])
    odd = _fft_list(xs[1::2])
    out = [None] * n
    for k in range(n // 2):
        o_re, o_im = odd[k]
        if k == 0:
            t_re, t_im = o_re, o_im
        elif 4 * k == n:
            t_re, t_im = o_im, -o_re
        else:
            ang = -2.0 * math.pi * k / n
            wr, wi = math.cos(ang), math.sin(ang)
            t_re = o_re * wr - o_im * wi
            t_im = o_re * wi + o_im * wr
        e_re, e_im = even[k]
        out[k] = (e_re + t_re, e_im + t_im)
        out[k + n // 2] = (e_re - t_re, e_im - t_im)
    return out


def _fourier_kernel(xc_ref, xs_ref, zb_ref, twc_ref, tws_ref, wpos_ref, wf_ref, bf_ref, y_ref, *scratch,
                    seq, tr):
    w = xc_ref.shape[2]
    nblk = w // BLK
    l1 = seq // FFT_L2

    def epilogue(fr, rows):
        parts = []
        for kb in range(nblk):
            yb = jnp.dot(fr[:, kb * BLK:(kb + 1) * BLK].astype(BF16), wf_ref[kb], preferred_element_type=F32)
            parts.append(yb + bf_ref[:, kb * BLK:(kb + 1) * BLK])
        yb = parts[0] if nblk == 1 else jnp.concatenate(parts, axis=1)
        y_ref[0, rows, :] = (yb * _silu(zb_ref[0, rows, :])).astype(y_ref.dtype)

    if l1 == 1:
        fr = (jnp.dot(wpos_ref[:, 0:FFT_L2], xc_ref[0].astype(BF16), preferred_element_type=F32)
              + jnp.dot(wpos_ref[:, FFT_L2:2 * FFT_L2], xs_ref[0].astype(BF16), preferred_element_type=F32))
        epilogue(fr, pl.ds(0, seq))
        return

    b_ref, fr_ref = scratch
    assert w == LANES

    def butterfly(r, carry):
        r0 = pl.multiple_of(r * SUBLANES, SUBLANES)
        zs = [(xc_ref[0, pl.ds(n1 * FFT_L2 + r0, SUBLANES), :], xs_ref[0, pl.ds(n1 * FFT_L2 + r0, SUBLANES), :])
              for n1 in range(l1)]
        for k1, (a_re, a_im) in enumerate(_fft_list(zs)):
            if k1 == 0:
                b_re, b_im = a_re, a_im
            else:
                tc_ = twc_ref[k1, pl.ds(r0, SUBLANES), :]
                ts_ = tws_ref[k1, pl.ds(r0, SUBLANES), :]
                b_re = a_re * tc_ + a_im * ts_
                b_im = a_im * tc_ - a_re * ts_
            b_ref[pl.ds(r0, SUBLANES), k1 * LANES:(k1 + 1) * LANES] = b_re
            b_ref[pl.ds(FFT_L2 + r0, SUBLANES), k1 * LANES:(k1 + 1) * LANES] = b_im
        return carry

    lax.fori_loop(0, FFT_L2 // SUBLANES, butterfly, 0, unroll=4)

    per_dot = min(4, l1)
    for nb in range(l1 // per_dot):
        cols = slice(nb * per_dot * LANES, (nb + 1) * per_dot * LANES)
        fr = jnp.dot(wpos_ref[...], b_ref[:, cols].astype(BF16), preferred_element_type=F32)
        for kk in range(per_dot):
            fr_ref[pl.ds(nb * per_dot + kk, FFT_L2, stride=l1), :] = fr[:, kk * LANES:(kk + 1) * LANES]

    def finish(i, carry):
        rows = pl.ds(pl.multiple_of(i * tr, tr), tr)
        epilogue(fr_ref[rows, :], rows)
        return carry

    lax.fori_loop(0, seq // tr, finish, 0, unroll=4)


def _fourier(xc, xs, zb, twc, tws, w_pos, w_four, b_four, w_blk, tr):
    b, seq, d = xc.shape
    nblk = w_blk // BLK
    l1 = seq // FFT_L2
    blk = lambda i, j: (i, 0, j)
    scratch = []
    if l1 > 1:
        scratch = [pltpu.VMEM((2 * FFT_L2, l1 * w_blk), F32), pltpu.VMEM((seq, w_blk), F32)]
    vmem = (2 * (3 * seq * w_blk * 4 + seq * w_blk * 2 + 2 * twc.size * 4) + 3 * seq * w_blk * 4
            + VMEM_TEMPORARIES)
    return pl.pallas_call(
        functools.partial(_fourier_kernel, seq=seq, tr=tr),
        grid=(b, d // w_blk),
        in_specs=[pl.BlockSpec((1, seq, w_blk), blk),
                  pl.BlockSpec((1, seq, w_blk), blk),
                  pl.BlockSpec((1, seq, w_blk), blk),
                  pl.BlockSpec(twc.shape, lambda i, j: (0, 0, 0)),
                  pl.BlockSpec(tws.shape, lambda i, j: (0, 0, 0)),
                  pl.BlockSpec(w_pos.shape, lambda i, j: (0, 0)),
                  pl.BlockSpec((nblk, BLK, BLK), lambda i, j: (j, 0, 0)),
                  pl.BlockSpec((1, w_blk), lambda i, j: (0, j))],
        out_specs=pl.BlockSpec((1, seq, w_blk), blk),
        out_shape=jax.ShapeDtypeStruct((b, seq, d), BF16),
        scratch_shapes=scratch,
        compiler_params=_cparams(("parallel", "parallel"), vmem),
        name="fourier_mix",
    )(xc, xs, zb, twc, tws, w_pos, w_four, b_four)


def _outproj_kernel(*refs, n_in):
    y_refs = refs[:n_in]
    w_refs = refs[n_in:2 * n_in]
    x_ref, mod_ref, g_ref, o_ref = refs[2 * n_in:]
    d = D_MODEL
    y = jnp.dot(y_refs[0][...], w_refs[0][...], preferred_element_type=F32)
    for k in range(1, n_in):
        y = y + jnp.dot(y_refs[k][...], w_refs[k][...], preferred_element_type=F32)
    yn = y * lax.rsqrt(jnp.mean(y * y, axis=-1, keepdims=True) + EPS) * g_ref[...]
    o_ref[...] = x_ref[...] + mod_ref[0, :, 2 * d:3 * d] * yn


def _outproj(ys, ws, x2d, mod_l, g_post, row_off, rows_per_mod, tm):
    m, d = x2d.shape
    n_in = len(ys)
    row = lambda i: (i, 0)
    const = lambda i: (0, 0)
    vmem = 2 * (sum(wk.size for wk in ws) * 2 + sum(tm * yk.shape[1] for yk in ys) * 2 + 2 * tm * d * 4) + 4 * tm * d * 4
    return pl.pallas_call(
        functools.partial(_outproj_kernel, n_in=n_in),
        grid=(m // tm,),
        in_specs=([pl.BlockSpec((tm, yk.shape[1]), row) for yk in ys]
                  + [pl.BlockSpec(wk.shape, const) for wk in ws]
                  + [pl.BlockSpec((tm, d), row),
                     pl.BlockSpec((1, 1, 3 * d), _mod_index_map(row_off, rows_per_mod // tm)),
                     pl.BlockSpec((1, d), const)]),
        out_specs=pl.BlockSpec((tm, d), row),
        out_shape=jax.ShapeDtypeStruct((m, d), F32),
        compiler_params=_cparams(("parallel",), vmem),
        name="outproj_residual",
    )(*ys, *ws, x2d, mod_l, g_post)


def _inproj_o_kernel(x_ref, mod_ref, g_ref, w_ref, wgt_ref, gb_ref, q_ref, k_ref, v_ref, o_ref, z_ref, gpt_ref,
                     gp_ref):
    d = D_MODEL
    hb = _normed_input(x_ref, mod_ref, g_ref)
    q = jnp.dot(hb, w_ref[:, 0:d], preferred_element_type=F32)
    q_ref[...] = (q * (BLK ** -0.5)).astype(q_ref.dtype)
    k_ref[...] = jnp.dot(hb, w_ref[:, d:2 * d], preferred_element_type=F32).astype(k_ref.dtype)
    for half in range(2):
        cols = slice(half * d, (half + 1) * d)
        v_ref[:, cols] = jnp.dot(hb, w_ref[:, 2 * d + half * d:3 * d + half * d],
                                 preferred_element_type=F32).astype(v_ref.dtype)
        o_ref[:, cols] = jnp.dot(hb, w_ref[:, 4 * d + half * d:5 * d + half * d], preferred_element_type=F32)
        z_ref[:, cols] = jnp.dot(hb, w_ref[:, 6 * d + half * d:7 * d + half * d], preferred_element_type=F32)
    gp_ref[...] = jnp.dot(hb, wgt_ref[...], preferred_element_type=F32)
    gpt_ref[...] = gp_ref[...].T + gb_ref[:, 0:1]


def _inproj_o(x2d, mod_l, g_pre, w_main, w_gate, gate_bias, row_off, rows_per_mod, tm):
    m, d = x2d.shape
    row = lambda i: (i, 0)
    const = lambda i: (0, 0)
    ng = w_gate.shape[1]
    vmem = w_main.size * 2 + 2 * (tm * d * 4 + tm * d * 2 * 4 + tm * 2 * d * 4 * 2) + 6 * tm * d * 4
    return pl.pallas_call(
        _inproj_o_kernel,
        grid=(m // tm,),
        in_specs=[pl.BlockSpec((tm, d), row),
                  pl.BlockSpec((1, 1, 3 * d), _mod_index_map(row_off, rows_per_mod // tm)),
                  pl.BlockSpec((1, d), const),
                  pl.BlockSpec(w_main.shape, const, pipeline_mode=pl.Buffered(1)),
                  pl.BlockSpec(w_gate.shape, const),
                  pl.BlockSpec(gate_bias.shape, const)],
        out_specs=[pl.BlockSpec((tm, d), row), pl.BlockSpec((tm, d), row),
                   pl.BlockSpec((tm, 2 * d), row), pl.BlockSpec((tm, 2 * d), row), pl.BlockSpec((tm, 2 * d), row),
                   pl.BlockSpec((ng, tm), lambda i: (0, i))],
        out_shape=[jax.ShapeDtypeStruct((m, d), BF16), jax.ShapeDtypeStruct((m, d), BF16),
                   jax.ShapeDtypeStruct((m, 2 * d), BF16), jax.ShapeDtypeStruct((m, 2 * d), F32),
                   jax.ShapeDtypeStruct((m, 2 * d), F32), jax.ShapeDtypeStruct((ng, m), F32)],
        scratch_shapes=[pltpu.VMEM((tm, ng), F32)],
        compiler_params=_cparams(("parallel",), vmem),
        name="inproj_odd",
    )(x2d, mod_l, g_pre, w_main, w_gate, gate_bias)


def _chunk_scan_lanes(v, op, identity, reverse):
    n = v.shape[-1]
    pos = lax.broadcasted_iota(jnp.int32, v.shape, 1) & (CHUNK - 1)
    d = 1
    while d < CHUNK:
        if reverse:
            shifted = jnp.where(pos < CHUNK - d, pltpu.roll(v, n - d, axis=1), identity)
        else:
            shifted = jnp.where(pos >= d, pltpu.roll(v, d, axis=1), identity)
        v = op(v, shifted)
        d *= 2
    return v


def _gate_prep_kernel(g_ref, o_ref):
    for direction in range(2):
        reverse = direction == 1
        i_pre = g_ref[2 * direction]
        b = _chunk_scan_lanes(jax.nn.log_sigmoid(g_ref[2 * direction + 1]), jnp.add, 0.0, reverse)
        a = i_pre - b
        o_ref[3 * direction] = a
        o_ref[3 * direction + 1] = b
        o_ref[3 * direction + 2] = _chunk_scan_lanes(a, jnp.maximum, -jnp.inf, reverse)
    o_ref[6] = jnp.zeros_like(g_ref[0])
    o_ref[7] = jnp.zeros_like(g_ref[0])


def _gate_prep(gq, rb):
    _, r, seq = gq.shape
    return pl.pallas_call(
        _gate_prep_kernel,
        grid=(r // rb,),
        in_specs=[pl.BlockSpec((4, rb, seq), lambda i: (0, i, 0))],
        out_specs=pl.BlockSpec((SUBLANES, rb, seq), lambda i: (0, i, 0)),
        out_shape=jax.ShapeDtypeStruct((SUBLANES, r, seq), F32),
        compiler_params=_cparams(("parallel",), 2 * (4 + SUBLANES) * rb * seq * 4 + VMEM_TEMPORARIES),
        name="mlstm_gate_prep",
    )(gq)


def _mlstm_kernel(*refs, seq, heads, zero_init):
    q_ref, k_ref, v_ref, o_ref, z_ref, gate_ref = refs[0:6]
    refs = refs[6:]
    if not zero_init:
        c0_ref, n0_ref, m0_ref = refs[0:3]
        refs = refs[3:]
    y_ref, c_out_ref, n_out_ref, m_out_ref, hs_ref, cols_ref = refs
    t = CHUNK
    nchunks = seq // t
    half = nchunks // 2
    assert nchunks == 2 * half
    streams = [(hh, dr) for hh in range(heads) for dr in range(2)]

    def head_lanes(hh, width):
        return slice(hh * width, (hh + 1) * width)
    row_i = lax.broadcasted_iota(jnp.int32, (t, t), 0)
    col_i = lax.broadcasted_iota(jnp.int32, (t, t), 1)
    ones_blk = jnp.ones((t, LANES), BF16)

    sel_r = lax.broadcasted_iota(jnp.int32, (4 * SUBLANES, 4 * LANES), 0)
    sel_c = lax.broadcasted_iota(jnp.int32, (4 * SUBLANES, 4 * LANES), 1) // LANES
    wanted = jnp.where(sel_c == 0, 2, jnp.where(sel_c == 1, 1, jnp.where(sel_c == 2, 5, 4)))
    selector = jnp.where(((sel_r & (SUBLANES - 1)) == wanted) & (sel_r < 3 * SUBLANES), 1.0, 0.0).astype(BF16)

    piece = min(seq, 4 * t)
    for hh in range(heads):
        for c in range(seq // piece):
            g = gate_ref[0, hh, :, c * piece:(c + 1) * piece]
            hi = g.astype(BF16).astype(F32)
            mid = (g - hi).astype(BF16).astype(F32)
            lo = g - hi - mid
            stack = jnp.concatenate([hi, mid, lo, jnp.zeros_like(g)], axis=0).astype(BF16)
            cols_ref[hh, c * piece:(c + 1) * piece, :] = lax.dot_general(
                stack, selector, (((0,), (0,)), ((), ())), preferred_element_type=F32)

    def chunk_of(step, direction):
        return (nchunks - 1 - step) if direction == 1 else step

    def gate_terms(rows, hh, direction):
        g_rows = gate_ref[0, hh, :, rows]
        last = 0 if direction == 1 else t - 1
        a_row = g_rows[3 * direction:3 * direction + 1, :]
        b_end = g_rows[3 * direction + 1:3 * direction + 2, last:last + 1]
        a_end = g_rows[3 * direction + 2:3 * direction + 3, last:last + 1]
        amax_rep = cols_ref[hh, rows, 2 * direction * LANES:(2 * direction + 1) * LANES]
        return a_row, b_end, a_end, amax_rep

    def scores(step, stream, m_rep):
        hh, direction = stream
        rows = pl.ds(pl.multiple_of(chunk_of(step, direction) * t, t), t)
        a_row, b_end, a_end, amax_rep = gate_terms(rows, hh, direction)
        m_prev = m_rep[:, 0:1]
        mx = jnp.maximum(m_prev, amax_rep)
        mask = (col_i >= row_i) if direction == 1 else (col_i <= row_i)
        e = jnp.where(mask, jnp.exp(a_row - mx), 0.0)
        s = lax.dot_general(q_ref[0, rows, head_lanes(hh, BLK)], k_ref[0, rows, head_lanes(hh, BLK)],
                            (((1,), (1,)), ((), ())), preferred_element_type=F32) * e
        m_next = jnp.broadcast_to(b_end + jnp.maximum(m_prev, a_end), (1, t))
        return (s.astype(BF16), jnp.sum(s, axis=-1, keepdims=True), m_rep), m_next

    def outputs(step, stream, memory, pending):
        hh, direction = stream
        c_mat, n_rep = memory
        s_b, s_sum, m_rep = pending
        rows = pl.ds(pl.multiple_of(chunk_of(step, direction) * t, t), t)
        qc = q_ref[0, rows, head_lanes(hh, BLK)]
        kc = k_ref[0, rows, head_lanes(hh, BLK)]
        vc = v_ref[0, rows, head_lanes(hh, DV)]
        a_row, _, a_end, amax_rep = gate_terms(rows, hh, direction)
        b_rep = cols_ref[hh, rows, (2 * direction + 1) * LANES:(2 * direction + 2) * LANES]
        m_prev = m_rep[:, 0:1]
        mx_end = jnp.maximum(m_prev, a_end)
        mx = jnp.maximum(m_prev, amax_rep)
        sv = jnp.dot(s_b, vc, preferred_element_type=F32)
        qcn = jnp.dot(qc, jnp.concatenate([c_mat.astype(BF16), n_rep.astype(BF16)], axis=1),
                      preferred_element_type=F32)
        wkk_t = (kc.astype(F32).T * jnp.exp(a_row - mx_end)).astype(BF16)
        upd = jnp.dot(wkk_t, jnp.concatenate([vc, ones_blk], axis=1), preferred_element_type=F32)
        decay = jnp.exp(m_prev - mx_end)
        w_inter = jnp.exp(m_prev - mx)
        den = s_sum + w_inter * qcn[:, DV:DV + LANES]
        inv = 1.0 / jnp.maximum(jnp.abs(den), jnp.exp(-(b_rep + mx)))
        hc = jnp.concatenate(
            [(sv[:, kb * LANES:(kb + 1) * LANES] + w_inter * qcn[:, kb * LANES:(kb + 1) * LANES]) * inv
             for kb in range(DV // LANES)], axis=1)
        return rows, hc, (decay * c_mat + upd[:, 0:DV], decay * n_rep + upd[:, DV:DV + LANES])

    def n_to_columns(n_row):
        return jnp.broadcast_to(n_row, (LANES, BLK)).T

    def n_to_row(n_rep):
        return n_rep.T[0:1, :]

    def emit(hh, rows, hc, first_visit):
        if first_visit:
            hs_ref[hh, rows, :] = hc
        else:
            hsum = hs_ref[hh, rows, :] + hc
            lanes = head_lanes(hh, DV)
            y_ref[0, rows, lanes] = (jax.nn.sigmoid(o_ref[0, rows, lanes]) * hsum
                                     * _silu(z_ref[0, rows, lanes])).astype(y_ref.dtype)

    def pipelined(step, carry, first_visit, with_next):
        memories, pendings, m_reps = carry
        nxt = [scores(step + 1, sm, m_reps[i]) for i, sm in enumerate(streams)] if with_next else None
        done = [outputs(step, sm, memories[i], pendings[i]) for i, sm in enumerate(streams)]
        for (hh, _), (rows, hc, _) in zip(streams, done):
            emit(hh, rows, hc, first_visit)
        new_mem = tuple(dn[2] for dn in done)
        if with_next:
            return new_mem, tuple(n[0] for n in nxt), tuple(n[1] for n in nxt)
        return new_mem, pendings, m_reps

    if zero_init:
        memories = tuple((jnp.zeros((BLK, DV), F32), jnp.zeros((BLK, LANES), F32)) for _ in streams)
        first = [scores(0, sm, jnp.zeros((1, t), F32)) for sm in streams]
    else:
        memories = tuple((c0_ref[0, dr, hh], n_to_columns(n0_ref[0, dr, hh])) for hh, dr in streams)
        first = [scores(0, (hh, dr), m0_ref[0, dr, hh]) for hh, dr in streams]
    carry = (memories, tuple(f[0] for f in first), tuple(f[1] for f in first))
    if nchunks <= 2:
        for step in range(nchunks - 1):
            carry = pipelined(step, carry, first_visit=step < half, with_next=True)
    else:
        unroll = 2 if half % 2 == 0 else 1
        carry = lax.fori_loop(0, half, functools.partial(pipelined, first_visit=True, with_next=True), carry,
                              unroll=2 * unroll if half % (2 * unroll) == 0 else unroll)
        carry = lax.fori_loop(half, nchunks - 1, functools.partial(pipelined, first_visit=False, with_next=True),
                              carry, unroll=3 if (half - 1) % 3 == 0 else unroll)
    memories, _, m_reps = pipelined(nchunks - 1, carry, first_visit=False, with_next=False)
    for i, (hh, dr) in enumerate(streams):
        c_out_ref[0, dr, hh] = memories[i][0]
        n_out_ref[0, dr, hh] = n_to_row(memories[i][1])
        m_out_ref[0, dr, hh] = m_reps[i]


def _mlstm(q, k, v, o, z, gates, states, heads):
    b, seq, _ = q.shape
    qk_blk = lambda i, h: (i, 0, h)
    st5 = lambda i, h: (i, 0, h, 0, 0)
    state_specs = [pl.BlockSpec((1, 2, heads, BLK, DV), st5),
                   pl.BlockSpec((1, 2, heads, 1, BLK), st5),
                   pl.BlockSpec((1, 2, heads, 1, CHUNK), st5)]
    vmem = heads * (2 * (2 * seq * BLK * 2 + seq * DV * 2 * 2 + 2 * seq * DV * 4 + 2 * 2 * BLK * DV * 4)
                    + seq * DV * 4 + seq * 4 * LANES * 4) + VMEM_TEMPORARIES
    return pl.pallas_call(
        functools.partial(_mlstm_kernel, seq=seq, heads=heads, zero_init=states is None),
        grid=(b, N_HEADS // heads),
        in_specs=[pl.BlockSpec((1, seq, heads * BLK), qk_blk),
                  pl.BlockSpec((1, seq, heads * BLK), qk_blk),
                  pl.BlockSpec((1, seq, heads * DV), qk_blk),
                  pl.BlockSpec((1, seq, heads * DV), qk_blk),
                  pl.BlockSpec((1, seq, heads * DV), qk_blk),
                  pl.BlockSpec((1, heads, SUBLANES, seq), lambda i, h: (i, h, 0, 0))]
        + ([] if states is None else state_specs),
        out_specs=[pl.BlockSpec((1, seq, heads * DV), qk_blk)] + state_specs,
        out_shape=[jax.ShapeDtypeStruct((b, seq, N_HEADS * DV), BF16),
                   jax.ShapeDtypeStruct((b, 2, N_HEADS, BLK, DV), F32),
                   jax.ShapeDtypeStruct((b, 2, N_HEADS, 1, BLK), F32),
                   jax.ShapeDtypeStruct((b, 2, N_HEADS, 1, CHUNK), F32)],
        scratch_shapes=[pltpu.VMEM((heads, seq, DV), F32), pltpu.VMEM((heads, seq, 4 * LANES), F32)],
        compiler_params=_cparams(("parallel", "parallel"), vmem),
        name="mlstm_chunkwise",
    )(q, k, v, o, z, gates, *(() if states is None else states))


def _dft_tables(seq):
    n = np.arange(BLK)
    ang = 2.0 * np.pi * np.outer(n, n) / BLK
    dft_c = np.concatenate([np.cos(ang), -np.sin(ang)], axis=1) / np.sqrt(BLK)
    l1 = seq // FFT_L2
    n2 = np.arange(FFT_L2)
    ang2 = 2.0 * np.pi * np.outer(n2, n2) / FFT_L2
    w_pos = np.concatenate([np.cos(ang2), np.sin(ang2)], axis=1) / np.sqrt(seq)
    angt = 2.0 * np.pi * np.outer(np.arange(l1), n2) / seq
    twc = np.broadcast_to(np.cos(angt)[:, :, None], (l1, FFT_L2, LANES))
    tws = np.broadcast_to(np.sin(angt)[:, :, None], (l1, FFT_L2, LANES))
    return (jnp.asarray(dft_c, F32).astype(BF16), jnp.asarray(w_pos, F32).astype(BF16),
            jnp.asarray(twc, F32), jnp.asarray(tws, F32))


def _run_trunk(x, mod, row_off, per_batch_mod, lru0, mlstm0, p, cfg):
    b, seq, d = x.shape
    m = b * seq
    rows_per_mod = seq if per_batch_mod else m
    x2d = x.reshape(m, d)
    dft_c, w_pos, twc, tws = _dft_tables(seq)

    mod0 = mod[0].reshape(MOD_ROWS, 1, 3 * d)
    xa, za, zb, xc, xs = _inproj_e(x2d, mod0, p["g_pre"][0:1], p["w_in_e"], dft_c, row_off, rows_per_mod,
                                   cfg["tm_in"])
    r3 = lambda a: a.reshape(b, seq, a.shape[-1])
    ya, lru_f = _rglru(r3(xa), r3(za), lru0, p["conv_w"], p["conv_b"], p["wg"], p["bg"], p["lam"], cfg["tc_lru"],
                       cfg["bb_lru"])
    yb = _fourier(r3(xc), r3(xs), r3(zb), twc, tws, w_pos, p["w_four"], p["b_four"], cfg["w_four"], cfg["tr_four"])
    x1 = _outproj([ya.reshape(m, d), yb.reshape(m, d)], [p["w_out_e"][0:d], p["w_out_e"][d:2 * d]],
                  x2d, mod0, p["g_post"][0:1], row_off, rows_per_mod, cfg["tm_out"])

    mod1 = mod[1].reshape(MOD_ROWS, 1, 3 * d)
    q, k, v, o, z, gpt = _inproj_o(x1, mod1, p["g_pre"][1:2], p["w_in_o"], p["w_gate"], p["gate_bias"],
                                   row_off, rows_per_mod, cfg["tm_in_o"])
    gq = gpt[0:4 * N_HEADS].reshape(4, N_HEADS, b, seq).transpose(0, 2, 1, 3).reshape(4, b * N_HEADS, seq)
    gates = _gate_prep(gq, cfg["rb_gate"]).reshape(SUBLANES, b, N_HEADS, seq).transpose(1, 2, 0, 3)
    y, c_f, n_f, m_f = _mlstm(r3(q), r3(k), r3(v), r3(o), r3(z), gates, mlstm0, cfg["heads_mlstm"])
    x2 = _outproj([y.reshape(m, 2 * d)], [p["w_out_o"]], x1, mod1, p["g_post"][1:2], row_off, rows_per_mod,
                  cfg["tm_out"])
    return x2.reshape(b, seq, d), lru_f, c_f, n_f, m_f


def _prepare_params(conv_w, conv_b, w_rg, b_rg, w_ig, b_ig, lru_lambda, w_four, b_four, w_in_e, w_out_e,
                    w_in_o, b_if, w_out_o, g_pre, g_post):
    d = D_MODEL
    wg = jnp.concatenate([w_rg[0], w_ig[0]], axis=-1).astype(BF16)
    bg = jnp.concatenate([b_rg[0].reshape(2, N_HEADS, 1, BLK), b_ig[0].reshape(2, N_HEADS, 1, BLK)], axis=-1)
    n_gate = 4 * N_HEADS
    w_gate = jnp.zeros((d, LANES), F32).at[:, 0:n_gate].set(w_in_o[0][:, 8 * d:8 * d + n_gate]).astype(BF16)
    bias = jnp.zeros((LANES,), F32).at[0:n_gate].set(b_if[0].reshape(n_gate))
    gate_bias = jnp.broadcast_to(bias[:, None], (LANES, LANES))
    return dict(
        g_pre=g_pre, g_post=g_post,
        w_in_e=w_in_e[0].astype(BF16), w_out_e=w_out_e[0].astype(BF16),
        conv_w=conv_w[0], conv_b=conv_b[0].reshape(1, d), wg=wg, bg=bg, lam=lru_lambda[0].reshape(2, 1, d),
        w_four=w_four[0].astype(BF16), b_four=b_four[0].reshape(1, d),
        w_in_o=w_in_o[0].astype(BF16), w_gate=w_gate, gate_bias=gate_bias,
        w_out_o=w_out_o[0].astype(BF16))


def _trunk_config(batch, seq):
    short = seq <= 4 * CHUNK
    rows = batch * seq
    return dict(
        tm_in=min(512, rows), tm_in_o=min(512, rows), tm_out=min(1024, rows),
        tc_lru=CHUNK,
        bb_lru=next(n for n in (8, 4, 2, 1) if batch % n == 0) if short else 1,
        w_four=D_MODEL if short else BLK,
        tr_four=min(seq, 256 if short else 512),
        rb_gate=min(batch * N_HEADS, max(SUBLANES, (32 * 1024) // seq)),
        heads_mlstm=4 if short else 1)


def kernel(x_prompt, x_sample, c, state_lru, state_mlstm_C, state_mlstm_n, state_mlstm_m, c_ctx, w_mod, b_mod,
           g_pre, g_post, w_in_e, conv_w, conv_b, w_rg, b_rg, w_ig, b_ig, lru_lambda, w_four, b_four, w_out_e,
           w_in_o, b_if, w_out_o):
    d = D_MODEL
    bp = x_prompt.shape[0]
    bs = x_sample.shape[0]
    p = _prepare_params(conv_w, conv_b, w_rg, b_rg, w_ig, b_ig, lru_lambda, w_four, b_four, w_in_e, w_out_e,
                        w_in_o, b_if, w_out_o, g_pre, g_post)
    cond = jnp.concatenate([c_ctx[None, :], c, jnp.zeros((MOD_ROWS - 1 - bs, d), F32)], axis=0)
    mod = _modulation(cond, w_mod, b_mod)

    y_prompt, lru_f, c_f, n_f, m_f = _run_trunk(x_prompt, mod, 0, False, jnp.zeros((bp, 2, d), F32), None, p,
                                                _trunk_config(bp, x_prompt.shape[1]))

    mlstm0 = (state_mlstm_C[:, 0], state_mlstm_n[:, 0].reshape(bs, 2, N_HEADS, 1, BLK),
              jnp.broadcast_to(state_mlstm_m[:, 0].reshape(bs, 2, N_HEADS, 1, 1), (bs, 2, N_HEADS, 1, CHUNK)))
    y_sample, _, _, _, _ = _run_trunk(x_sample, mod, 1, True, state_lru[:, 0], mlstm0, p,
                                      _trunk_config(bs, x_sample.shape[1]))

    return (y_prompt, y_sample, lru_f[:, None], c_f[:, None], n_f[:, None, :, :, 0, :], m_f[:, None, :, :, 0, 0])
```

```python
import functools
import math

import numpy as np
import jax
import jax.numpy as jnp
from jax import lax
from jax.experimental import pallas as pl
from jax.experimental.pallas import tpu as pltpu

F32 = jnp.float32
BF16 = jnp.bfloat16

D_MODEL = 1024
DEPTH = 2
EPS = 1e-6
LRU_C = 8.0
CONV_W = 4
N_HEADS = 8
BLK = D_MODEL // N_HEADS
DV = 2 * BLK
CHUNK = 128
FFT_L2 = 256
MOD_ROWS = 8

LANES = 128
SUBLANES = 8
MIB = 1024 * 1024
VMEM_LIMIT_CAP = 56 * MIB
VMEM_LIMIT_FLOOR = 16 * MIB
VMEM_TEMPORARIES = 16 * MIB


def _cparams(sem, vmem_bytes):
    return pltpu.CompilerParams(dimension_semantics=sem,
                                vmem_limit_bytes=int(min(max(vmem_bytes, VMEM_LIMIT_FLOOR), VMEM_LIMIT_CAP)))


def _silu(x):
    return x * jax.nn.sigmoid(x)


def _mod_kernel(cond_ref, w_ref, b_ref, o_ref):
    s = _silu(cond_ref[...]).astype(BF16)
    o_ref[0] = jnp.dot(s, w_ref[0].astype(BF16), preferred_element_type=F32) + b_ref[0]


def _modulation(cond, w_mod, b_mod):
    d = D_MODEL
    return pl.pallas_call(
        _mod_kernel,
        grid=(DEPTH, 3),
        in_specs=[pl.BlockSpec((MOD_ROWS, d), lambda l, j: (0, 0)),
                  pl.BlockSpec((1, d, d), lambda l, j: (l, 0, j)),
                  pl.BlockSpec((1, 1, d), lambda l, j: (l, 0, j))],
        out_specs=pl.BlockSpec((1, MOD_ROWS, d), lambda l, j: (l, 0, j)),
        out_shape=jax.ShapeDtypeStruct((DEPTH, MOD_ROWS, 3 * d), F32),
        compiler_params=_cparams(("arbitrary", "arbitrary"), 2 * d * d * 4 + VMEM_TEMPORARIES),
        name="adaln_mod",
    )(cond, w_mod, b_mod.reshape(DEPTH, 1, 3 * d))


def _normed_input(x_ref, mod_ref, g_ref):
    d = D_MODEL
    x = x_ref[...]
    y = x * lax.rsqrt(jnp.mean(x * x, axis=-1, keepdims=True) + EPS) * g_ref[...]
    shift = mod_ref[0, :, 0:d]
    scale = mod_ref[0, :, d:2 * d]
    return (y * (1.0 + scale) + shift).astype(BF16)


def _mod_index_map(row_off, tiles_per_mod):
    return lambda i: (row_off + i // tiles_per_mod, 0, 0)


def _inproj_e_kernel(x_ref, mod_ref, g_ref, w_ref, dft_ref, xa_ref, za_ref, zb_ref, xc_ref, xs_ref):
    d = D_MODEL
    hb = _normed_input(x_ref, mod_ref, g_ref)
    xa_ref[...] = jnp.dot(hb, w_ref[:, 0:d], preferred_element_type=F32)
    za_ref[...] = jnp.dot(hb, w_ref[:, d:2 * d], preferred_element_type=F32)
    zb_ref[...] = jnp.dot(hb, w_ref[:, 3 * d:4 * d], preferred_element_type=F32)
    xb = jnp.dot(hb, w_ref[:, 2 * d:3 * d], preferred_element_type=F32).astype(BF16)
    for g in range(N_HEADS):
        cs = jnp.dot(xb[:, g * BLK:(g + 1) * BLK], dft_ref[...], preferred_element_type=F32)
        xc_ref[:, g * BLK:(g + 1) * BLK] = cs[:, 0:BLK]
        xs_ref[:, g * BLK:(g + 1) * BLK] = cs[:, BLK:2 * BLK]


def _inproj_e(x2d, mod_l, g_pre, w_in, dft_c, row_off, rows_per_mod, tm):
    m, d = x2d.shape
    row = lambda i: (i, 0)
    const = lambda i: (0, 0)
    out = jax.ShapeDtypeStruct((m, d), F32)
    vmem = 2 * (w_in.size * 2 + tm * d * 4 * 6) + 8 * tm * d * 4
    return pl.pallas_call(
        _inproj_e_kernel,
        grid=(m // tm,),
        in_specs=[pl.BlockSpec((tm, d), row),
                  pl.BlockSpec((1, 1, 3 * d), _mod_index_map(row_off, rows_per_mod // tm)),
                  pl.BlockSpec((1, d), const),
                  pl.BlockSpec(w_in.shape, const),
                  pl.BlockSpec(dft_c.shape, const)],
        out_specs=[pl.BlockSpec((tm, d), row)] * 5,
        out_shape=[out] * 5,
        compiler_params=_cparams(("parallel",), vmem),
        name="inproj_even",
    )(x2d, mod_l, g_pre, w_in, dft_c)


def _scan_sublanes(a, u, reverse):
    t = a.shape[0]
    assert t == SUBLANES
    row = lax.broadcasted_iota(jnp.int32, a.shape, 0)
    d = 1
    while d < t:
        shift = (t - d) if reverse else d
        valid = (row < t - d) if reverse else (row >= d)
        a_sh = jnp.where(valid, pltpu.roll(a, shift, axis=0), 1.0)
        u_sh = jnp.where(valid, pltpu.roll(u, shift, axis=0), 0.0)
        u = a * u_sh + u
        a = a * a_sh
        d *= 2
    return a, u


def _rglru_kernel(xa_ref, za_ref, cw_ref, cb_ref, wg_ref, bg_ref, lam_ref, h0_ref,
                  y_ref, hT_ref, hf_ref, hb_ref, xc_ref, *, seq, tc, bb):
    nchunks = seq // tc
    groups = tc // SUBLANES
    sub = lax.broadcasted_iota(jnp.int32, (SUBLANES, LANES), 0)

    def row_bcast(block, r):
        return jnp.broadcast_to(block[r:r + 1, :], (SUBLANES, LANES))

    def conv_chunk(bi, c):
        t0 = pl.multiple_of(c * tc, tc)
        xs = [xa_ref[bi, pl.ds(t0 + g, SUBLANES, stride=groups), :] for g in range(groups)]
        prev = xa_ref[bi, pl.ds(pl.multiple_of(jnp.maximum(t0 - SUBLANES, 0), SUBLANES), SUBLANES), :]
        nxt = xa_ref[bi, pl.ds(pl.multiple_of(jnp.minimum(t0 + tc, seq - SUBLANES), SUBLANES), SUBLANES), :]
        prev = jnp.where(c > 0, prev, 0.0)
        nxt = jnp.where(c < nchunks - 1, nxt, 0.0)
        before2 = jnp.where(sub == 0, row_bcast(prev, SUBLANES - 2), pltpu.roll(xs[groups - 2], 1, axis=0))
        before1 = jnp.where(sub == 0, row_bcast(prev, SUBLANES - 1), pltpu.roll(xs[groups - 1], 1, axis=0))
        after1 = jnp.where(sub == SUBLANES - 1, row_bcast(nxt, 0), pltpu.roll(xs[0], SUBLANES - 1, axis=0))
        ext = [before2, before1] + xs + [after1]
        out = []
        for g in range(groups):
            acc = cb_ref[...] + cw_ref[0:1, :] * ext[g]
            for j in range(1, CONV_W):
                acc = acc + cw_ref[j:j + 1, :] * ext[g + j]
            out.append(acc)
        return t0, jnp.concatenate(out, axis=0)

    def gates(xc, direction):
        g = jnp.dot(xc.astype(BF16), wg_ref[direction, 0], preferred_element_type=F32) + bg_ref[direction, 0]
        r = jax.nn.sigmoid(g[:, 0:BLK])
        i = jax.nn.sigmoid(g[:, BLK:2 * BLK])
        lam = lam_ref[direction]
        softplus_neg = jnp.maximum(-lam, 0.0) + jnp.log1p(jnp.exp(-jnp.abs(lam)))
        neg_log_a = LRU_C * r * softplus_neg
        a = jnp.exp(-neg_log_a)
        om = jnp.tanh(neg_log_a) * (1.0 + a * a)
        root = jnp.where(om > 0.0, om * lax.rsqrt(om), 0.0)
        return a, root * (i * xc)

    def scan_dir(bi, c, direction, carry, out_ref, first_visit):
        reverse = direction == 1
        if first_visit:
            t0, xc = conv_chunk(bi, c)
            xc_ref[bi, pl.ds(t0, tc), :] = xc
        else:
            t0 = pl.multiple_of(c * tc, tc)
            xc = xc_ref[bi, pl.ds(t0, tc), :]
        a, u = gates(xc, direction)
        order = range(groups - 1, -1, -1) if reverse else range(groups)
        p_g, s_g = [None] * groups, [None] * groups
        p_run = s_run = None
        for g in order:
            a_v = a[g * SUBLANES:(g + 1) * SUBLANES]
            u_v = u[g * SUBLANES:(g + 1) * SUBLANES]
            if p_run is None:
                p_run, s_run = a_v, u_v
            else:
                s_run = a_v * s_run + u_v
                p_run = a_v * p_run
            p_g[g], s_g[g] = p_run, s_run
        pp, ss = _scan_sublanes(p_run, s_run, reverse)
        carry_b = jnp.broadcast_to(carry, (SUBLANES, LANES))
        seg_out = pp * carry_b + ss
        if reverse:
            h_in = jnp.where(sub == SUBLANES - 1, carry_b, pltpu.roll(seg_out, SUBLANES - 1, axis=0))
            new_carry = seg_out[0:1, :]
        else:
            h_in = jnp.where(sub == 0, carry_b, pltpu.roll(seg_out, 1, axis=0))
            new_carry = seg_out[SUBLANES - 1:SUBLANES, :]
        for g in range(groups):
            out_ref[bi, pl.ds(t0 + g, SUBLANES, stride=groups), :] = p_g[g] * h_in + s_g[g]
        return new_carry

    def body(j, carries, first_visit):
        return tuple((scan_dir(bi, j, 0, carries[bi][0], hf_ref, first_visit),
                      scan_dir(bi, nchunks - 1 - j, 1, carries[bi][1], hb_ref, first_visit)) for bi in range(bb))

    half = nchunks // 2
    assert nchunks == 2 * half
    init = tuple((h0_ref[bi, 0:1, :], h0_ref[bi, 1:2, :]) for bi in range(bb))
    if nchunks <= 2:
        finals = init
        for j in range(nchunks):
            finals = body(j, finals, j < half)
    else:
        unroll_first = next(n for n in (4, 2, 1) if half % n == 0)
        unroll_second = next(n for n in (8, 4, 2, 1) if half % n == 0)
        finals = lax.fori_loop(0, half, functools.partial(body, first_visit=True), init, unroll=unroll_first)
        finals = lax.fori_loop(half, nchunks, functools.partial(body, first_visit=False), finals,
                               unroll=unroll_second)
    for bi in range(bb):
        hT_ref[bi, 0:1, :] = finals[bi][0]
        hT_ref[bi, 1:2, :] = finals[bi][1]

    def gate_out(c, carry):
        rows = pl.ds(pl.multiple_of(c * tc, tc), tc)
        for bi in range(bb):
            y_ref[bi, rows, :] = ((hf_ref[bi, rows, :] + hb_ref[bi, rows, :])
                                  * _silu(za_ref[bi, rows, :])).astype(y_ref.dtype)
        return carry

    lax.fori_loop(0, nchunks, gate_out, 0, unroll=2 if nchunks > 2 else 1)


def _rglru(xa, za, h0, conv_w, conv_b, wg, bg, lam, tc, bb):
    b, seq, d = xa.shape
    blk = lambda i, j: (i, 0, j)
    vmem = bb * (2 * (2 * seq * BLK * 4 + seq * BLK * 2) + 3 * seq * BLK * 4) + VMEM_TEMPORARIES
    return pl.pallas_call(
        functools.partial(_rglru_kernel, seq=seq, tc=tc, bb=bb),
        grid=(b // bb, d // BLK),
        in_specs=[pl.BlockSpec((bb, seq, BLK), blk),
                  pl.BlockSpec((bb, seq, BLK), blk),
                  pl.BlockSpec((CONV_W, BLK), lambda i, j: (0, j)),
                  pl.BlockSpec((1, BLK), lambda i, j: (0, j)),
                  pl.BlockSpec((2, 1, BLK, 2 * BLK), lambda i, j: (0, j, 0, 0)),
                  pl.BlockSpec((2, 1, 1, 2 * BLK), lambda i, j: (0, j, 0, 0)),
                  pl.BlockSpec((2, 1, BLK), lambda i, j: (0, 0, j)),
                  pl.BlockSpec((bb, 2, BLK), blk)],
        out_specs=[pl.BlockSpec((bb, seq, BLK), blk),
                   pl.BlockSpec((bb, 2, BLK), blk)],
        out_shape=[jax.ShapeDtypeStruct((b, seq, d), BF16),
                   jax.ShapeDtypeStruct((b, 2, d), F32)],
        scratch_shapes=[pltpu.VMEM((bb, seq, BLK), F32)] * 3,
        compiler_params=_cparams(("parallel", "parallel"), vmem),
        name="rglru_scan",
    )(xa, za, conv_w, conv_b, wg, bg, lam, h0)


def _fft_list(xs):
    n = len(xs)
    if n == 1:
        return xs
    even = _fft_list(xs[0::2])
    odd = _fft_list(xs[1::2])
    out = [None] * n
    for k in range(n // 2):
        o_re, o_im = odd[k]
        if k == 0:
            t_re, t_im = o_re, o_im
        elif 4 * k == n:
            t_re, t_im = o_im, -o_re
        else:
            ang = -2.0 * math.pi * k / n
            wr, wi = math.cos(ang), math.sin(ang)
            t_re = o_re * wr - o_im * wi
            t_im = o_re * wi + o_im * wr
        e_re, e_im = even[k]
        out[k] = (e_re + t_re, e_im + t_im)
        out[k + n // 2] = (e_re - t_re, e_im - t_im)
    return out


def _fourier_kernel(xc_ref, xs_ref, zb_ref, twc_ref, tws_ref, wpos_ref, wf_ref, bf_ref, y_ref, *scratch,
                    seq, tr):
    w = xc_ref.shape[2]
    nblk = w // BLK
    l1 = seq // FFT_L2

    def epilogue(fr, rows):
        parts = []
        for kb in range(nblk):
            yb = jnp.dot(fr[:, kb * BLK:(kb + 1) * BLK].astype(BF16), wf_ref[kb], preferred_element_type=F32)
            parts.append(yb + bf_ref[:, kb * BLK:(kb + 1) * BLK])
        yb = parts[0] if nblk == 1 else jnp.concatenate(parts, axis=1)
        y_ref[0, rows, :] = (yb * _silu(zb_ref[0, rows, :])).astype(y_ref.dtype)

    if l1 == 1:
        fr = (jnp.dot(wpos_ref[:, 0:FFT_L2], xc_ref[0].astype(BF16), preferred_element_type=F32)
              + jnp.dot(wpos_ref[:, FFT_L2:2 * FFT_L2], xs_ref[0].astype(BF16), preferred_element_type=F32))
        epilogue(fr, pl.ds(0, seq))
        return

    b_ref, fr_ref = scratch
    assert w == LANES

    def butterfly(r, carry):
        r0 = pl.multiple_of(r * SUBLANES, SUBLANES)
        zs = [(xc_ref[0, pl.ds(n1 * FFT_L2 + r0, SUBLANES), :], xs_ref[0, pl.ds(n1 * FFT_L2 + r0, SUBLANES), :])
              for n1 in range(l1)]
        for k1, (a_re, a_im) in enumerate(_fft_list(zs)):
            if k1 == 0:
                b_re, b_im = a_re, a_im
            else:
                tc_ = twc_ref[k1, pl.ds(r0, SUBLANES), :]
                ts_ = tws_ref[k1, pl.ds(r0, SUBLANES), :]
                b_re = a_re * tc_ + a_im * ts_
                b_im = a_im * tc_ - a_re * ts_
            b_ref[pl.ds(r0, SUBLANES), k1 * LANES:(k1 + 1) * LANES] = b_re
            b_ref[pl.ds(FFT_L2 + r0, SUBLANES), k1 * LANES:(k1 + 1) * LANES] = b_im
        return carry

    lax.fori_loop(0, FFT_L2 // SUBLANES, butterfly, 0, unroll=4)

    per_dot = min(4, l1)
    for nb in range(l1 // per_dot):
        cols = slice(nb * per_dot * LANES, (nb + 1) * per_dot * LANES)
        fr = jnp.dot(wpos_ref[...], b_ref[:, cols].astype(BF16), preferred_element_type=F32)
        for kk in range(per_dot):
            fr_ref[pl.ds(nb * per_dot + kk, FFT_L2, stride=l1), :] = fr[:, kk * LANES:(kk + 1) * LANES]

    def finish(i, carry):
        rows = pl.ds(pl.multiple_of(i * tr, tr), tr)
        epilogue(fr_ref[rows, :], rows)
        return carry

    lax.fori_loop(0, seq // tr, finish, 0, unroll=4)


def _fourier(xc, xs, zb, twc, tws, w_pos, w_four, b_four, w_blk, tr):
    b, seq, d = xc.shape
    nblk = w_blk // BLK
    l1 = seq // FFT_L2
    blk = lambda i, j: (i, 0, j)
    scratch = []
    if l1 > 1:
        scratch = [pltpu.VMEM((2 * FFT_L2, l1 * w_blk), F32), pltpu.VMEM((seq, w_blk), F32)]
    vmem = (2 * (3 * seq * w_blk * 4 + seq * w_blk * 2 + 2 * twc.size * 4) + 3 * seq * w_blk * 4
            + VMEM_TEMPORARIES)
    return pl.pallas_call(
        functools.partial(_fourier_kernel, seq=seq, tr=tr),
        grid=(b, d // w_blk),
        in_specs=[pl.BlockSpec((1, seq, w_blk), blk),
                  pl.BlockSpec((1, seq, w_blk), blk),
                  pl.BlockSpec((1, seq, w_blk), blk),
                  pl.BlockSpec(twc.shape, lambda i, j: (0, 0, 0)),
                  pl.BlockSpec(tws.shape, lambda i, j: (0, 0, 0)),
                  pl.BlockSpec(w_pos.shape, lambda i, j: (0, 0)),
                  pl.BlockSpec((nblk, BLK, BLK), lambda i, j: (j, 0, 0)),
                  pl.BlockSpec((1, w_blk), lambda i, j: (0, j))],
        out_specs=pl.BlockSpec((1, seq, w_blk), blk),
        out_shape=jax.ShapeDtypeStruct((b, seq, d), BF16),
        scratch_shapes=scratch,
        compiler_params=_cparams(("parallel", "parallel"), vmem),
        name="fourier_mix",
    )(xc, xs, zb, twc, tws, w_pos, w_four, b_four)


def _outproj_kernel(*refs, n_in):
    y_refs = refs[:n_in]
    w_refs = refs[n_in:2 * n_in]
    x_ref, mod_ref, g_ref, o_ref = refs[2 * n_in:]
    d = D_MODEL
    y = jnp.dot(y_refs[0][...], w_refs[0][...], preferred_element_type=F32)
    for k in range(1, n_in):
        y = y + jnp.dot(y_refs[k][...], w_refs[k][...], preferred_element_type=F32)
    yn = y * lax.rsqrt(jnp.mean(y * y, axis=-1, keepdims=True) + EPS) * g_ref[...]
    o_ref[...] = x_ref[...] + mod_ref[0, :, 2 * d:3 * d] * yn


def _outproj(ys, ws, x2d, mod_l, g_post, row_off, rows_per_mod, tm):
    m, d = x2d.shape
    n_in = len(ys)
    row = lambda i: (i, 0)
    const = lambda i: (0, 0)
    vmem = 2 * (sum(wk.size for wk in ws) * 2 + sum(tm * yk.shape[1] for yk in ys) * 2 + 2 * tm * d * 4) + 4 * tm * d * 4
    return pl.pallas_call(
        functools.partial(_outproj_kernel, n_in=n_in),
        grid=(m // tm,),
        in_specs=([pl.BlockSpec((tm, yk.shape[1]), row) for yk in ys]
                  + [pl.BlockSpec(wk.shape, const) for wk in ws]
                  + [pl.BlockSpec((tm, d), row),
                     pl.BlockSpec((1, 1, 3 * d), _mod_index_map(row_off, rows_per_mod // tm)),
                     pl.BlockSpec((1, d), const)]),
        out_specs=pl.BlockSpec((tm, d), row),
        out_shape=jax.ShapeDtypeStruct((m, d), F32),
        compiler_params=_cparams(("parallel",), vmem),
        name="outproj_residual",
    )(*ys, *ws, x2d, mod_l, g_post)


def _inproj_o_kernel(x_ref, mod_ref, g_ref, w_ref, wgt_ref, gb_ref, q_ref, k_ref, v_ref, o_ref, z_ref, gpt_ref,
                     gp_ref):
    d = D_MODEL
    hb = _normed_input(x_ref, mod_ref, g_ref)
    q = jnp.dot(hb, w_ref[:, 0:d], preferred_element_type=F32)
    q_ref[...] = (q * (BLK ** -0.5)).astype(q_ref.dtype)
    k_ref[...] = jnp.dot(hb, w_ref[:, d:2 * d], preferred_element_type=F32).astype(k_ref.dtype)
    for half in range(2):
        cols = slice(half * d, (half + 1) * d)
        v_ref[:, cols] = jnp.dot(hb, w_ref[:, 2 * d + half * d:3 * d + half * d],
                                 preferred_element_type=F32).astype(v_ref.dtype)
        o_ref[:, cols] = jnp.dot(hb, w_ref[:, 4 * d + half * d:5 * d + half * d], preferred_element_type=F32)
        z_ref[:, cols] = jnp.dot(hb, w_ref[:, 6 * d + half * d:7 * d + half * d], preferred_element_type=F32)
    gp_ref[...] = jnp.dot(hb, wgt_ref[...], preferred_element_type=F32)
    gpt_ref[...] = gp_ref[...].T + gb_ref[:, 0:1]


def _inproj_o(x2d, mod_l, g_pre, w_main, w_gate, gate_bias, row_off, rows_per_mod, tm):
    m, d = x2d.shape
    row = lambda i: (i, 0)
    const = lambda i: (0, 0)
    ng = w_gate.shape[1]
    vmem = w_main.size * 2 + 2 * (tm * d * 4 + tm * d * 2 * 4 + tm * 2 * d * 4 * 2) + 6 * tm * d * 4
    return pl.pallas_call(
        _inproj_o_kernel,
        grid=(m // tm,),
        in_specs=[pl.BlockSpec((tm, d), row),
                  pl.BlockSpec((1, 1, 3 * d), _mod_index_map(row_off, rows_per_mod // tm)),
                  pl.BlockSpec((1, d), const),
                  pl.BlockSpec(w_main.shape, const, pipeline_mode=pl.Buffered(1)),
                  pl.BlockSpec(w_gate.shape, const),
                  pl.BlockSpec(gate_bias.shape, const)],
        out_specs=[pl.BlockSpec((tm, d), row), pl.BlockSpec((tm, d), row),
                   pl.BlockSpec((tm, 2 * d), row), pl.BlockSpec((tm, 2 * d), row), pl.BlockSpec((tm, 2 * d), row),
                   pl.BlockSpec((ng, tm), lambda i: (0, i))],
        out_shape=[jax.ShapeDtypeStruct((m, d), BF16), jax.ShapeDtypeStruct((m, d), BF16),
                   jax.ShapeDtypeStruct((m, 2 * d), BF16), jax.ShapeDtypeStruct((m, 2 * d), F32),
                   jax.ShapeDtypeStruct((m, 2 * d), F32), jax.ShapeDtypeStruct((ng, m), F32)],
        scratch_shapes=[pltpu.VMEM((tm, ng), F32)],
        compiler_params=_cparams(("parallel",), vmem),
        name="inproj_odd",
    )(x2d, mod_l, g_pre, w_main, w_gate, gate_bias)


def _chunk_scan_lanes(v, op, identity, reverse):
    n = v.shape[-1]
    pos = lax.broadcasted_iota(jnp.int32, v.shape, 1) & (CHUNK - 1)
    d = 1
    while d < CHUNK:
        if reverse:
            shifted = jnp.where(pos < CHUNK - d, pltpu.roll(v, n - d, axis=1), identity)
        else:
            shifted = jnp.where(pos >= d, pltpu.roll(v, d, axis=1), identity)
        v = op(v, shifted)
        d *= 2
    return v


def _gate_prep_kernel(g_ref, o_ref):
    for direction in range(2):
        reverse = direction == 1
        i_pre = g_ref[2 * direction]
        b = _chunk_scan_lanes(jax.nn.log_sigmoid(g_ref[2 * direction + 1]), jnp.add, 0.0, reverse)
        a = i_pre - b
        o_ref[3 * direction] = a
        o_ref[3 * direction + 1] = b
        o_ref[3 * direction + 2] = _chunk_scan_lanes(a, jnp.maximum, -jnp.inf, reverse)
    o_ref[6] = jnp.zeros_like(g_ref[0])
    o_ref[7] = jnp.zeros_like(g_ref[0])


def _gate_prep(gq, rb):
    _, r, seq = gq.shape
    return pl.pallas_call(
        _gate_prep_kernel,
        grid=(r // rb,),
        in_specs=[pl.BlockSpec((4, rb, seq), lambda i: (0, i, 0))],
        out_specs=pl.BlockSpec((SUBLANES, rb, seq), lambda i: (0, i, 0)),
        out_shape=jax.ShapeDtypeStruct((SUBLANES, r, seq), F32),
        compiler_params=_cparams(("parallel",), 2 * (4 + SUBLANES) * rb * seq * 4 + VMEM_TEMPORARIES),
        name="mlstm_gate_prep",
    )(gq)


def _mlstm_kernel(*refs, seq, heads, zero_init):
    q_ref, k_ref, v_ref, o_ref, z_ref, gate_ref = refs[0:6]
    refs = refs[6:]
    if not zero_init:
        c0_ref, n0_ref, m0_ref = refs[0:3]
        refs = refs[3:]
    y_ref, c_out_ref, n_out_ref, m_out_ref, hs_ref, cols_ref = refs
    t = CHUNK
    nchunks = seq // t
    half = nchunks // 2
    assert nchunks == 2 * half
    streams = [(hh, dr) for hh in range(heads) for dr in range(2)]

    def head_lanes(hh, width):
        return slice(hh * width, (hh + 1) * width)
    row_i = lax.broadcasted_iota(jnp.int32, (t, t), 0)
    col_i = lax.broadcasted_iota(jnp.int32, (t, t), 1)
    ones_blk = jnp.ones((t, LANES), BF16)

    sel_r = lax.broadcasted_iota(jnp.int32, (4 * SUBLANES, 4 * LANES), 0)
    sel_c = lax.broadcasted_iota(jnp.int32, (4 * SUBLANES, 4 * LANES), 1) // LANES
    wanted = jnp.where(sel_c == 0, 2, jnp.where(sel_c == 1, 1, jnp.where(sel_c == 2, 5, 4)))
    selector = jnp.where(((sel_r & (SUBLANES - 1)) == wanted) & (sel_r < 3 * SUBLANES), 1.0, 0.0).astype(BF16)

    piece = min(seq, 4 * t)
    for hh in range(heads):
        for c in range(seq // piece):
            g = gate_ref[0, hh, :, c * piece:(c + 1) * piece]
            hi = g.astype(BF16).astype(F32)
            mid = (g - hi).astype(BF16).astype(F32)
            lo = g - hi - mid
            stack = jnp.concatenate([hi, mid, lo, jnp.zeros_like(g)], axis=0).astype(BF16)
            cols_ref[hh, c * piece:(c + 1) * piece, :] = lax.dot_general(
                stack, selector, (((0,), (0,)), ((), ())), preferred_element_type=F32)

    def chunk_of(step, direction):
        return (nchunks - 1 - step) if direction == 1 else step

    def gate_terms(rows, hh, direction):
        g_rows = gate_ref[0, hh, :, rows]
        last = 0 if direction == 1 else t - 1
        a_row = g_rows[3 * direction:3 * direction + 1, :]
        b_end = g_rows[3 * direction + 1:3 * direction + 2, last:last + 1]
        a_end = g_rows[3 * direction + 2:3 * direction + 3, last:last + 1]
        amax_rep = cols_ref[hh, rows, 2 * direction * LANES:(2 * direction + 1) * LANES]
        return a_row, b_end, a_end, amax_rep

    def scores(step, stream, m_rep):
        hh, direction = stream
        rows = pl.ds(pl.multiple_of(chunk_of(step, direction) * t, t), t)
        a_row, b_end, a_end, amax_rep = gate_terms(rows, hh, direction)
        m_prev = m_rep[:, 0:1]
        mx = jnp.maximum(m_prev, amax_rep)
        mask = (col_i >= row_i) if direction == 1 else (col_i <= row_i)
        e = jnp.where(mask, jnp.exp(a_row - mx), 0.0)
        s = lax.dot_general(q_ref[0, rows, head_lanes(hh, BLK)], k_ref[0, rows, head_lanes(hh, BLK)],
                            (((1,), (1,)), ((), ())), preferred_element_type=F32) * e
        m_next = jnp.broadcast_to(b_end + jnp.maximum(m_prev, a_end), (1, t))
        return (s.astype(BF16), jnp.sum(s, axis=-1, keepdims=True), m_rep), m_next

    def outputs(step, stream, memory, pending):
        hh, direction = stream
        c_mat, n_rep = memory
        s_b, s_sum, m_rep = pending
        rows = pl.ds(pl.multiple_of(chunk_of(step, direction) * t, t), t)
        qc = q_ref[0, rows, head_lanes(hh, BLK)]
        kc = k_ref[0, rows, head_lanes(hh, BLK)]
        vc = v_ref[0, rows, head_lanes(hh, DV)]
        a_row, _, a_end, amax_rep = gate_terms(rows, hh, direction)
        b_rep = cols_ref[hh, rows, (2 * direction + 1) * LANES:(2 * direction + 2) * LANES]
        m_prev = m_rep[:, 0:1]
        mx_end = jnp.maximum(m_prev, a_end)
        mx = jnp.maximum(m_prev, amax_rep)
        sv = jnp.dot(s_b, vc, preferred_element_type=F32)
        qcn = jnp.dot(qc, jnp.concatenate([c_mat.astype(BF16), n_rep.astype(BF16)], axis=1),
                      preferred_element_type=F32)
        wkk_t = (kc.astype(F32).T * jnp.exp(a_row - mx_end)).astype(BF16)
        upd = jnp.dot(wkk_t, jnp.concatenate([vc, ones_blk], axis=1), preferred_element_type=F32)
        decay = jnp.exp(m_prev - mx_end)
        w_inter = jnp.exp(m_prev - mx)
        den = s_sum + w_inter * qcn[:, DV:DV + LANES]
        inv = 1.0 / jnp.maximum(jnp.abs(den), jnp.exp(-(b_rep + mx)))
        hc = jnp.concatenate(
            [(sv[:, kb * LANES:(kb + 1) * LANES] + w_inter * qcn[:, kb * LANES:(kb + 1) * LANES]) * inv
             for kb in range(DV // LANES)], axis=1)
        return rows, hc, (decay * c_mat + upd[:, 0:DV], decay * n_rep + upd[:, DV:DV + LANES])

    def n_to_columns(n_row):
        return jnp.broadcast_to(n_row, (LANES, BLK)).T

    def n_to_row(n_rep):
        return n_rep.T[0:1, :]

    def emit(hh, rows, hc, first_visit):
        if first_visit:
            hs_ref[hh, rows, :] = hc
        else:
            hsum = hs_ref[hh, rows, :] + hc
            lanes = head_lanes(hh, DV)
            y_ref[0, rows, lanes] = (jax.nn.sigmoid(o_ref[0, rows, lanes]) * hsum
                                     * _silu(z_ref[0, rows, lanes])).astype(y_ref.dtype)

    def pipelined(step, carry, first_visit, with_next):
        memories, pendings, m_reps = carry
        nxt = [scores(step + 1, sm, m_reps[i]) for i, sm in enumerate(streams)] if with_next else None
        done = [outputs(step, sm, memories[i], pendings[i]) for i, sm in enumerate(streams)]
        for (hh, _), (rows, hc, _) in zip(streams, done):
            emit(hh, rows, hc, first_visit)
        new_mem = tuple(dn[2] for dn in done)
        if with_next:
            return new_mem, tuple(n[0] for n in nxt), tuple(n[1] for n in nxt)
        return new_mem, pendings, m_reps

    if zero_init:
        memories = tuple((jnp.zeros((BLK, DV), F32), jnp.zeros((BLK, LANES), F32)) for _ in streams)
        first = [scores(0, sm, jnp.zeros((1, t), F32)) for sm in streams]
    else:
        memories = tuple((c0_ref[0, dr, hh], n_to_columns(n0_ref[0, dr, hh])) for hh, dr in streams)
        first = [scores(0, (hh, dr), m0_ref[0, dr, hh]) for hh, dr in streams]
    carry = (memories, tuple(f[0] for f in first), tuple(f[1] for f in first))
    if nchunks <= 2:
        for step in range(nchunks - 1):
            carry = pipelined(step, carry, first_visit=step < half, with_next=True)
    else:
        unroll = 2 if half % 2 == 0 else 1
        carry = lax.fori_loop(0, half, functools.partial(pipelined, first_visit=True, with_next=True), carry,
                              unroll=4 * unroll if half % (4 * unroll) == 0 else unroll)
        carry = lax.fori_loop(half, nchunks - 1, functools.partial(pipelined, first_visit=False, with_next=True),
                              carry, unroll=5 if (half - 1) % 5 == 0 else unroll)
    memories, _, m_reps = pipelined(nchunks - 1, carry, first_visit=False, with_next=False)
    for i, (hh, dr) in enumerate(streams):
        c_out_ref[0, dr, hh] = memories[i][0]
        n_out_ref[0, dr, hh] = n_to_row(memories[i][1])
        m_out_ref[0, dr, hh] = m_reps[i]


def _mlstm(q, k, v, o, z, gates, states, heads):
    b, seq, _ = q.shape
    qk_blk = lambda i, h: (i, 0, h)
    st5 = lambda i, h: (i, 0, h, 0, 0)
    state_specs = [pl.BlockSpec((1, 2, heads, BLK, DV), st5),
                   pl.BlockSpec((1, 2, heads, 1, BLK), st5),
                   pl.BlockSpec((1, 2, heads, 1, CHUNK), st5)]
    vmem = heads * (2 * (2 * seq * BLK * 2 + seq * DV * 2 * 2 + 2 * seq * DV * 4 + 2 * 2 * BLK * DV * 4)
                    + seq * DV * 4 + seq * 4 * LANES * 4) + VMEM_TEMPORARIES
    return pl.pallas_call(
        functools.partial(_mlstm_kernel, seq=seq, heads=heads, zero_init=states is None),
        grid=(b, N_HEADS // heads),
        in_specs=[pl.BlockSpec((1, seq, heads * BLK), qk_blk),
                  pl.BlockSpec((1, seq, heads * BLK), qk_blk),
                  pl.BlockSpec((1, seq, heads * DV), qk_blk),
                  pl.BlockSpec((1, seq, heads * DV), qk_blk),
                  pl.BlockSpec((1, seq, heads * DV), qk_blk),
                  pl.BlockSpec((1, heads, SUBLANES, seq), lambda i, h: (i, h, 0, 0))]
        + ([] if states is None else state_specs),
        out_specs=[pl.BlockSpec((1, seq, heads * DV), qk_blk)] + state_specs,
        out_shape=[jax.ShapeDtypeStruct((b, seq, N_HEADS * DV), BF16),
                   jax.ShapeDtypeStruct((b, 2, N_HEADS, BLK, DV), F32),
                   jax.ShapeDtypeStruct((b, 2, N_HEADS, 1, BLK), F32),
                   jax.ShapeDtypeStruct((b, 2, N_HEADS, 1, CHUNK), F32)],
        scratch_shapes=[pltpu.VMEM((heads, seq, DV), F32), pltpu.VMEM((heads, seq, 4 * LANES), F32)],
        compiler_params=_cparams(("parallel", "parallel"), vmem),
        name="mlstm_chunkwise",
    )(q, k, v, o, z, gates, *(() if states is None else states))


def _dft_tables(seq):
    n = np.arange(BLK)
    ang = 2.0 * np.pi * np.outer(n, n) / BLK
    dft_c = np.concatenate([np.cos(ang), -np.sin(ang)], axis=1) / np.sqrt(BLK)
    l1 = seq // FFT_L2
    n2 = np.arange(FFT_L2)
    ang2 = 2.0 * np.pi * np.outer(n2, n2) / FFT_L2
    w_pos = np.concatenate([np.cos(ang2), np.sin(ang2)], axis=1) / np.sqrt(seq)
    angt = 2.0 * np.pi * np.outer(np.arange(l1), n2) / seq
    twc = np.broadcast_to(np.cos(angt)[:, :, None], (l1, FFT_L2, LANES))
    tws = np.broadcast_to(np.sin(angt)[:, :, None], (l1, FFT_L2, LANES))
    return (jnp.asarray(dft_c, F32).astype(BF16), jnp.asarray(w_pos, F32).astype(BF16),
            jnp.asarray(twc, F32), jnp.asarray(tws, F32))


def _run_trunk(x, mod, row_off, per_batch_mod, lru0, mlstm0, p, cfg):
    b, seq, d = x.shape
    m = b * seq
    rows_per_mod = seq if per_batch_mod else m
    x2d = x.reshape(m, d)
    dft_c, w_pos, twc, tws = _dft_tables(seq)

    mod0 = mod[0].reshape(MOD_ROWS, 1, 3 * d)
    xa, za, zb, xc, xs = _inproj_e(x2d, mod0, p["g_pre"][0:1], p["w_in_e"], dft_c, row_off, rows_per_mod,
                                   cfg["tm_in"])
    r3 = lambda a: a.reshape(b, seq, a.shape[-1])
    ya, lru_f = _rglru(r3(xa), r3(za), lru0, p["conv_w"], p["conv_b"], p["wg"], p["bg"], p["lam"], cfg["tc_lru"],
                       cfg["bb_lru"])
    yb = _fourier(r3(xc), r3(xs), r3(zb), twc, tws, w_pos, p["w_four"], p["b_four"], cfg["w_four"], cfg["tr_four"])
    x1 = _outproj([ya.reshape(m, d), yb.reshape(m, d)], [p["w_out_e"][0:d], p["w_out_e"][d:2 * d]],
                  x2d, mod0, p["g_post"][0:1], row_off, rows_per_mod, cfg["tm_out"])

    mod1 = mod[1].reshape(MOD_ROWS, 1, 3 * d)
    q, k, v, o, z, gpt = _inproj_o(x1, mod1, p["g_pre"][1:2], p["w_in_o"], p["w_gate"], p["gate_bias"],
                                   row_off, rows_per_mod, cfg["tm_in_o"])
    gq = gpt[0:4 * N_HEADS].reshape(4, N_HEADS, b, seq).transpose(0, 2, 1, 3).reshape(4, b * N_HEADS, seq)
    gates = _gate_prep(gq, cfg["rb_gate"]).reshape(SUBLANES, b, N_HEADS, seq).transpose(1, 2, 0, 3)
    y, c_f, n_f, m_f = _mlstm(r3(q), r3(k), r3(v), r3(o), r3(z), gates, mlstm0, cfg["heads_mlstm"])
    x2 = _outproj([y.reshape(m, 2 * d)], [p["w_out_o"]], x1, mod1, p["g_post"][1:2], row_off, rows_per_mod,
                  cfg["tm_out"])
    return x2.reshape(b, seq, d), lru_f, c_f, n_f, m_f


def _prepare_params(conv_w, conv_b, w_rg, b_rg, w_ig, b_ig, lru_lambda, w_four, b_four, w_in_e, w_out_e,
                    w_in_o, b_if, w_out_o, g_pre, g_post):
    d = D_MODEL
    wg = jnp.concatenate([w_rg[0], w_ig[0]], axis=-1).astype(BF16)
    bg = jnp.concatenate([b_rg[0].reshape(2, N_HEADS, 1, BLK), b_ig[0].reshape(2, N_HEADS, 1, BLK)], axis=-1)
    n_gate = 4 * N_HEADS
    w_gate = jnp.zeros((d, LANES), F32).at[:, 0:n_gate].set(w_in_o[0][:, 8 * d:8 * d + n_gate]).astype(BF16)
    bias = jnp.zeros((LANES,), F32).at[0:n_gate].set(b_if[0].reshape(n_gate))
    gate_bias = jnp.broadcast_to(bias[:, None], (LANES, LANES))
    return dict(
        g_pre=g_pre, g_post=g_post,
        w_in_e=w_in_e[0].astype(BF16), w_out_e=w_out_e[0].astype(BF16),
        conv_w=conv_w[0], conv_b=conv_b[0].reshape(1, d), wg=wg, bg=bg, lam=lru_lambda[0].reshape(2, 1, d),
        w_four=w_four[0].astype(BF16), b_four=b_four[0].reshape(1, d),
        w_in_o=w_in_o[0].astype(BF16), w_gate=w_gate, gate_bias=gate_bias,
        w_out_o=w_out_o[0].astype(BF16))


def _trunk_config(batch, seq):
    short = seq <= 4 * CHUNK
    rows = batch * seq
    return dict(
        tm_in=min(512, rows), tm_in_o=min(512, rows), tm_out=min(1024, rows),
        tc_lru=CHUNK,
        bb_lru=next(n for n in (8, 4, 2, 1) if batch % n == 0) if short else 1,
        w_four=D_MODEL if short else BLK,
        tr_four=min(seq, 256 if short else 512),
        rb_gate=min(batch * N_HEADS, max(SUBLANES, (32 * 1024) // seq)),
        heads_mlstm=4 if short else 1)


def kernel(x_prompt, x_sample, c, state_lru, state_mlstm_C, state_mlstm_n, state_mlstm_m, c_ctx, w_mod, b_mod,
           g_pre, g_post, w_in_e, conv_w, conv_b, w_rg, b_rg, w_ig, b_ig, lru_lambda, w_four, b_four, w_out_e,
           w_in_o, b_if, w_out_o):
    d = D_MODEL
    bp = x_prompt.shape[0]
    bs = x_sample.shape[0]
    p = _prepare_params(conv_w, conv_b, w_rg, b_rg, w_ig, b_ig, lru_lambda, w_four, b_four, w_in_e, w_out_e,
                        w_in_o, b_if, w_out_o, g_pre, g_post)
    cond = jnp.concatenate([c_ctx[None, :], c, jnp.zeros((MOD_ROWS - 1 - bs, d), F32)], axis=0)
    mod = _modulation(cond, w_mod, b_mod)

    y_prompt, lru_f, c_f, n_f, m_f = _run_trunk(x_prompt, mod, 0, False, jnp.zeros((bp, 2, d), F32), None, p,
                                                _trunk_config(bp, x_prompt.shape[1]))

    mlstm0 = (state_mlstm_C[:, 0], state_mlstm_n[:, 0].reshape(bs, 2, N_HEADS, 1, BLK),
              jnp.broadcast_to(state_mlstm_m[:, 0].reshape(bs, 2, N_HEADS, 1, 1), (bs, 2, N_HEADS, 1, CHUNK)))
    y_sample, _, _, _, _ = _run_trunk(x_sample, mod, 1, True, state_lru[:, 0], mlstm0, p,
                                      _trunk_config(bs, x_sample.shape[1]))

    return (y_prompt, y_sample, lru_f[:, None], c_f[:, None], n_f[:, None, :, :, 0, :], m_f[:, None, :, :, 0, 0])
```
